```python
import math
import jax, jax.numpy as jnp
from jax import lax
import numpy as np

D_MODEL = 1024
BATCH = 16
SEQ = 256
DEPTH = 2
DEC_BATCH = 8
DEC_SEQ = 4096
PAST_LEN = 512

GRID_W = 64
EPS = 1e-6
D_A = D_MODEL // 4
NH_A = 4
HD_A = D_A // NH_A
CHUNK = 128
D_B = D_MODEL // 2
HD_B = 64
NH_B = D_B // HD_B
NA_WIN_R = 8
NA_WIN_C = 16
ATTN_SCALE = HD_B ** -0.5
D_C = D_MODEL - D_A - D_B
SSM_CH = 16
SSM_G = D_C // SSM_CH
SSM_P = 64
DT_MIN = 0.001
DT_MAX = 0.1
D_IN = 2 * D_A + 3 * D_B + D_C
IN_SPLITS = (2 * D_A, 2 * D_A + D_B, 2 * D_A + 2 * D_B, 2 * D_A + 3 * D_B)
N_EXPERTS = 64
TOP_K = 8
N_EXP_GROUPS = 8
TOPK_GROUPS = 4
D_EXPERT = 256
D_SHARED = 256
ROUTE_SCALE = 2.5
MOE_BLOCK = 128

kernel_name = 'hybrid_prefix_diffusion_step'


def _rmsnorm(x, g):
    x32 = x.astype(jnp.float32)
    y = x32 * lax.rsqrt(jnp.mean(x32 * x32, axis=-1, keepdims=True) + EPS)
    return (y * g.astype(jnp.float32)).astype(x.dtype)


def _chunk_mlp(pa, sgu_g, w_sp, b_sp):
    b_, l_, _ = pa.shape
    z = jax.nn.gelu(pa)
    u, v = z[..., :D_A], z[..., D_A:]
    v32 = v.astype(jnp.float32)
    mu = jnp.mean(v32, axis=-1, keepdims=True)
    var = jnp.mean(jnp.square(v32 - mu), axis=-1, keepdims=True)
    v = ((v32 - mu) * lax.rsqrt(var + EPS) * sgu_g.astype(jnp.float32)).astype(pa.dtype)
    v = v.reshape(b_, l_ // CHUNK, CHUNK, NH_A, HD_A)
    s = jnp.einsum('hij,bnjhd->bnihd', w_sp, v) + b_sp.T[:, :, None]
    return u * s.reshape(b_, l_, D_A)


def _ctx_attn(q, k, v):
    b_, l_ = q.shape[:2]
    qb = q.reshape(b_, l_ // CHUNK, CHUNK, NH_B, HD_B).transpose(1, 0, 2, 3, 4)

    def block(qi):
        s = jnp.einsum('bqhd,bkhd->bhqk', qi, k).astype(jnp.float32) * ATTN_SCALE
        pr = jax.nn.softmax(s, axis=-1).astype(v.dtype)
        return jnp.einsum('bhqk,bkhd->bqhd', pr, v)

    o = lax.map(block, qb)
    return o.transpose(1, 0, 2, 3, 4).reshape(b_, l_, D_B)


def _nat_attn(q, k, v, ck, cv, rpb):
    b_, l_ = q.shape[:2]
    rows = l_ // GRID_W
    wr = min(NA_WIN_R, rows)
    qg = q.reshape(b_, rows, GRID_W, NH_B, HD_B)
    kg = k.reshape(b_, rows, GRID_W, NH_B, HD_B)
    vg = v.reshape(b_, rows, GRID_W, NH_B, HD_B)
    cols = jnp.arange(GRID_W)
    col_start = jnp.clip(cols - NA_WIN_C // 2, 0, GRID_W - NA_WIN_C)
    col_idx = col_start[:, None] + jnp.arange(NA_WIN_C)[None, :]
    col_off = col_idx - cols[:, None] + (NA_WIN_C - 1)
    n_win = wr * NA_WIN_C

    def row_fn(r):
        rs = jnp.clip(r - wr // 2, 0, rows - wr)
        q_r = lax.dynamic_index_in_dim(qg, r, axis=1, keepdims=False)
        k_win = lax.dynamic_slice_in_dim(kg, rs, wr, axis=1)[:, :, col_idx]
        v_win = lax.dynamic_slice_in_dim(vg, rs, wr, axis=1)[:, :, col_idx]
        row_off = rs + jnp.arange(wr) - r + (NA_WIN_R - 1)
        bias = rpb[:, row_off][:, :, col_off].transpose(0, 2, 1, 3)
        s_win = (jnp.einsum('bchd,bwcjhd->bhcwj', q_r, k_win).astype(jnp.float32) * ATTN_SCALE
                 + bias.astype(jnp.float32)[None])
        s_ctx = jnp.einsum('bchd,blhd->bhcl', q_r, ck).astype(jnp.float32) * ATTN_SCALE
        s = jnp.concatenate([s_win.reshape(b_, NH_B, GRID_W, n_win), s_ctx], axis=-1)
        pr = jax.nn.softmax(s, axis=-1).astype(v.dtype)
        p_win = pr[..., :n_win].reshape(b_, NH_B, GRID_W, wr, NA_WIN_C)
        p_ctx = pr[..., n_win:]
        return (jnp.einsum('bhcwj,bwcjhd->bchd', p_win, v_win)
                + jnp.einsum('bhcl,blhd->bchd', p_ctx, cv))

    o = lax.map(row_fn, jnp.arange(rows))
    return o.transpose(1, 0, 2, 3, 4).reshape(b_, l_, D_B)


def _lin_combine(e1, e2):
    a1, b1 = e1
    a2, b2 = e2
    return a1 * a2, a2 * b1 + b2


def _diag_scan(a, bu, h0):
    bu = bu.at[:, 0].add(a * h0)
    a_full = jnp.broadcast_to(a, bu.shape)
    _, h = lax.associative_scan(_lin_combine, (a_full, bu), axis=1)
    return h


def _s5(pc, p, h0):
    b_, l_, _ = pc.shape
    f32 = jnp.float32
    u = pc.astype(f32).reshape(b_, l_, SSM_G, SSM_CH)
    lam = lax.complex(p['ssm_a_re'].astype(f32), p['ssm_a_im'].astype(f32))
    dt = jnp.exp(p['ssm_log_dt'].astype(f32))[..., None]
    lam_bar = jnp.exp(lam * dt)
    b_bar = ((lam_bar - 1) / lam)[..., None] * lax.complex(
        p['ssm_b_re'].astype(f32), p['ssm_b_im'].astype(f32))
    c_mat = lax.complex(p['ssm_c_re'].astype(f32), p['ssm_c_im'].astype(f32))
    if h0 is None:
        h0 = jnp.zeros((b_, 2, SSM_G, SSM_P), jnp.complex64)
    uc = u.astype(jnp.complex64)
    h_f = _diag_scan(lam_bar[0], jnp.einsum('blgc,gpc->blgp', uc, b_bar[0]), h0[:, 0])
    h_b = jnp.flip(_diag_scan(lam_bar[1], jnp.flip(jnp.einsum('blgc,gpc->blgp', uc, b_bar[1]), axis=1),
                              h0[:, 1]), axis=1)
    y = (jnp.real(jnp.einsum('blgp,gcp->blgc', h_f, c_mat[0]))
         + jnp.real(jnp.einsum('blgp,gcp->blgc', h_b, c_mat[1]))
         + p['ssm_d'].astype(f32).reshape(SSM_G, SSM_CH) * u)
    y = jax.nn.gelu(y.reshape(b_, l_, D_C)).astype(pc.dtype)
    out = y * jax.nn.sigmoid(y @ p['w_glu'] + p['b_glu'])
    h_last = jnp.stack([h_f[:, -1], h_b[:, 0]], axis=1)
    return out, h_last


def _moe(h, p):
    b_, l_, d_ = h.shape
    f32 = jnp.float32
    xf = h.reshape(-1, d_)
    t_ = xf.shape[0]
    scores = jax.nn.sigmoid(xf.astype(f32) @ p['w_router'].astype(f32))
    sel = scores + p['b_router'].astype(f32)
    grp = lax.top_k(sel.reshape(t_, N_EXP_GROUPS, N_EXPERTS // N_EXP_GROUPS), 2)[0].sum(-1)
    gidx = lax.top_k(grp, TOPK_GROUPS)[1]
    gmask = jax.nn.one_hot(gidx, N_EXP_GROUPS, dtype=f32).sum(1) > 0
    emask = jnp.repeat(gmask, N_EXPERTS // N_EXP_GROUPS, axis=1)
    _, idx = lax.top_k(jnp.where(emask, sel, -jnp.inf), TOP_K)
    wts = jnp.take_along_axis(scores, idx, axis=1)
    wts = wts / jnp.sum(wts, axis=-1, keepdims=True) * ROUTE_SCALE
    e_flat = idx.reshape(-1)
    tok_flat = jnp.repeat(jnp.arange(t_, dtype=jnp.int32), TOP_K)
    g_flat = wts.reshape(-1)
    order = jnp.argsort(e_flat)
    e_s, tok_s, g_s = e_flat[order], tok_flat[order], g_flat[order]
    counts = jnp.bincount(e_flat, length=N_EXPERTS)
    starts = jnp.cumsum(counts) - counts
    pcounts = (counts + MOE_BLOCK - 1) // MOE_BLOCK * MOE_BLOCK
    pends = jnp.cumsum(pcounts)
    pstarts = pends - pcounts
    dest = pstarts[e_s] + (jnp.arange(t_ * TOP_K) - starts[e_s])
    n_blocks = (t_ * TOP_K + N_EXPERTS * (MOE_BLOCK - 1)) // MOE_BLOCK + 1
    n_pad = n_blocks * MOE_BLOCK
    row_tok = jnp.full((n_pad,), t_, jnp.int32).at[dest].set(tok_s)
    row_gate = jnp.zeros((n_pad,), f32).at[dest].set(g_s)
    blk_exp = jnp.minimum(jnp.searchsorted(pends, jnp.arange(n_blocks) * MOE_BLOCK, side='right'),
                          N_EXPERTS - 1)
    x_pad = jnp.concatenate([xf, jnp.zeros((1, d_), xf.dtype)], axis=0)

    def step(acc, blk):
        toks, gates, e = blk
        xb = x_pad[toks]
        hb = jax.nn.silu(xb @ p['w_e_gate'][e]) * (xb @ p['w_e_up'][e])
        yb = (hb @ p['w_e_down'][e]) * gates[:, None].astype(xb.dtype)
        return acc.at[toks].add(yb), None

    acc, _ = lax.scan(step, jnp.zeros((t_ + 1, d_), xf.dtype),
                      (row_tok.reshape(n_blocks, MOE_BLOCK), row_gate.reshape(n_blocks, MOE_BLOCK), blk_exp))
    shared = (jax.nn.silu(xf @ p['w_s_gate']) * (xf @ p['w_s_up'])) @ p['w_s_down']
    return (acc[:t_] + shared).reshape(b_, l_, d_)


def _layer(x, mod, p, ctx):
    sh1, sc1, g1, sh2, sc2, g2 = jnp.split(mod, 6, axis=-1)
    b_, l_, _ = x.shape
    h = _rmsnorm(x, p['g_pre_mix']) * (1 + sc1) + sh1
    pa, pq, pk, pv, pc = jnp.split(h @ p['w_in'], IN_SPLITS, axis=-1)
    q = pq.reshape(b_, l_, NH_B, HD_B)
    k = pk.reshape(b_, l_, NH_B, HD_B)
    v = pv.reshape(b_, l_, NH_B, HD_B)
    ya = _chunk_mlp(pa, p['sgu_g'], p['w_sp'], p['b_sp'])
    if ctx is None:
        yb = _ctx_attn(q, k, v)
        yc, h_last = _s5(pc, p, None)
        state = (k, v, jnp.real(h_last), jnp.imag(h_last))
    else:
        ck, cv, s_re, s_im = ctx
        yb = _nat_attn(q, k, v, ck, cv, p['rpb'])
        yc, _ = _s5(pc, p, lax.complex(s_re.astype(jnp.float32), s_im.astype(jnp.float32)))
        state = None
    mixed = jnp.concatenate([ya, yb, yc], axis=-1) @ p['w_out']
    x = x + g1 * _rmsnorm(mixed, p['g_post_mix'])
    h = _rmsnorm(x, p['g_pre_ffn']) * (1 + sc2) + sh2
    x = x + g2 * _rmsnorm(_moe(h, p), p['g_post_ffn'])
    return x, state


def setup_inputs(seed: int = 0) -> dict:
    key = jax.random.key(seed)
    ks = iter(jax.random.split(key, 48))
    f32 = jnp.float32

    def nrm(shape, scale):
        return jax.random.normal(next(ks), shape, f32) * scale

    L = DEPTH
    return {
        'x_prompt': nrm((BATCH, SEQ, D_MODEL), 1.0),
        'x_sample': nrm((DEC_BATCH, DEC_SEQ, D_MODEL), 1.0),
        'cache_k': nrm((DEC_BATCH, DEPTH, PAST_LEN, NH_B, HD_B), 1.0),
        'cache_v': nrm((DEC_BATCH, DEPTH, PAST_LEN, NH_B, HD_B), 1.0),
        'state_ssm_re': nrm((DEC_BATCH, DEPTH, 2, SSM_G, SSM_P), 0.1),
        'state_ssm_im': nrm((DEC_BATCH, DEPTH, 2, SSM_G, SSM_P), 0.1),
        'c': nrm((DEC_BATCH, D_MODEL), 1.0),
        'c_ctx': nrm((D_MODEL,), 1.0),
        'g_pre_mix': 1.0 + nrm((L, D_MODEL), 0.02),
        'g_post_mix': 1.0 + nrm((L, D_MODEL), 0.02),
        'g_pre_ffn': 1.0 + nrm((L, D_MODEL), 0.02),
        'g_post_ffn': 1.0 + nrm((L, D_MODEL), 0.02),
        'w_ada': nrm((L, D_MODEL, 6 * D_MODEL), 0.5 * D_MODEL ** -0.5),
        'b_ada': nrm((L, 6 * D_MODEL), 0.02),
        'w_in': nrm((L, D_MODEL, D_IN), D_MODEL ** -0.5),
        'w_out': nrm((L, D_A + D_B + D_C, D_MODEL), (D_A + D_B + D_C) ** -0.5),
        'sgu_g': 1.0 + nrm((L, D_A), 0.02),
        'w_sp': nrm((L, NH_A, CHUNK, CHUNK), CHUNK ** -0.5),
        'b_sp': 1.0 + nrm((L, NH_A, CHUNK), 0.02),
        'rpb': nrm((L, NH_B, 2 * NA_WIN_R - 1, 2 * NA_WIN_C - 1), 0.1),
        'ssm_a_re': -0.5 + nrm((L, 2, SSM_G, SSM_P), 0.01),
        'ssm_a_im': math.pi * jnp.arange(SSM_P, dtype=f32) + nrm((L, 2, SSM_G, SSM_P), 0.01),
        'ssm_log_dt': jax.random.uniform(next(ks), (L, 2, SSM_G), f32, math.log(DT_MIN), math.log(DT_MAX)),
        'ssm_b_re': nrm((L, 2, SSM_G, SSM_P, SSM_CH), (2 * SSM_CH) ** -0.5),
        'ssm_b_im': nrm((L, 2, SSM_G, SSM_P, SSM_CH), (2 * SSM_CH) ** -0.5),
        'ssm_c_re': nrm((L, 2, SSM_G, SSM_CH, SSM_P), SSM_P ** -0.5),
        'ssm_c_im': nrm((L, 2, SSM_G, SSM_CH, SSM_P), SSM_P ** -0.5),
        'ssm_d': nrm((L, D_C), 1.0),
        'w_glu': nrm((L, D_C, D_C), D_C ** -0.5),
        'b_glu': nrm((L, D_C), 0.02),
        'w_router': nrm((L, D_MODEL, N_EXPERTS), D_MODEL ** -0.5),
        'b_router': nrm((L, N_EXPERTS), 0.01),
        'w_e_gate': nrm((L, N_EXPERTS, D_MODEL, D_EXPERT), D_MODEL ** -0.5),
        'w_e_up': nrm((L, N_EXPERTS, D_MODEL, D_EXPERT), D_MODEL ** -0.5),
        'w_e_down': nrm((L, N_EXPERTS, D_EXPERT, D_MODEL), D_EXPERT ** -0.5),
        'w_s_gate': nrm((L, D_MODEL, D_SHARED), D_MODEL ** -0.5),
        'w_s_up': nrm((L, D_MODEL, D_SHARED), D_MODEL ** -0.5),
        'w_s_down': nrm((L, D_SHARED, D_MODEL), D_SHARED ** -0.5),
    }


def reference(x_prompt, x_sample, cache_k, cache_v, state_ssm_re, state_ssm_im, c, c_ctx,
              g_pre_mix, g_post_mix, g_pre_ffn, g_post_ffn, w_ada, b_ada, w_in, w_out,
              sgu_g, w_sp, b_sp, rpb, ssm_a_re, ssm_a_im, ssm_log_dt, ssm_b_re, ssm_b_im,
              ssm_c_re, ssm_c_im, ssm_d, w_glu, b_glu, w_router, b_router,
              w_e_gate, w_e_up, w_e_down, w_s_gate, w_s_up, w_s_down):
    xp = x_prompt
    xs = x_sample
    new_k, new_v, new_re, new_im = [], [], [], []
    for l in range(DEPTH):
        p = {
            'g_pre_mix': g_pre_mix[l], 'g_post_mix': g_post_mix[l],
            'g_pre_ffn': g_pre_ffn[l], 'g_post_ffn': g_post_ffn[l],
            'w_in': w_in[l], 'w_out': w_out[l],
            'sgu_g': sgu_g[l], 'w_sp': w_sp[l], 'b_sp': b_sp[l], 'rpb': rpb[l],
            'ssm_a_re': ssm_a_re[l], 'ssm_a_im': ssm_a_im[l], 'ssm_log_dt': ssm_log_dt[l],
            'ssm_b_re': ssm_b_re[l], 'ssm_b_im': ssm_b_im[l],
            'ssm_c_re': ssm_c_re[l], 'ssm_c_im': ssm_c_im[l], 'ssm_d': ssm_d[l],
            'w_glu': w_glu[l], 'b_glu': b_glu[l],
            'w_router': w_router[l], 'b_router': b_router[l],
            'w_e_gate': w_e_gate[l], 'w_e_up': w_e_up[l], 'w_e_down': w_e_down[l],
            'w_s_gate': w_s_gate[l], 'w_s_up': w_s_up[l], 'w_s_down': w_s_down[l],
        }
        mod_ctx = (jax.nn.silu(c_ctx) @ w_ada[l] + b_ada[l])[None, None, :]
        mod_lat = (jax.nn.silu(c) @ w_ada[l] + b_ada[l])[:, None, :]
        xp, (k_l, v_l, s_re_l, s_im_l) = _layer(xp, mod_ctx, p, None)
        xs, _ = _layer(xs, mod_lat, p, (cache_k[:, l], cache_v[:, l], state_ssm_re[:, l], state_ssm_im[:, l]))
        new_k.append(k_l)
        new_v.append(v_l)
        new_re.append(s_re_l)
        new_im.append(s_im_l)
    return (xp, xs, jnp.stack(new_k, axis=1), jnp.stack(new_v, axis=1),
            jnp.stack(new_re, axis=1), jnp.stack(new_im, axis=1))
```

```python
import functools
import math

import jax
import jax.numpy as jnp
from jax import lax
from jax.experimental import pallas as pl
from jax.experimental.pallas import tpu as pltpu

F32 = jnp.float32
BF16 = jnp.bfloat16
I32 = jnp.int32

GRID_W = 64
EPS = 1e-6
NH_A = 4
CHUNK = 128
HD_B = 64
NA_WIN_R = 8
NA_WIN_C = 16
ATTN_SCALE = HD_B ** -0.5
SSM_CH = 16
SSM_P = 64
N_EXPERTS = 64
TOP_K = 8
N_EXP_GROUPS = 8
TOPK_GROUPS = 4
ROUTE_SCALE = 2.5

LANES = 128
TOK_TILE = 256
S5_CHUNK = 16
MOE_ROWS = 256
COMBINE_TILE = 128
DISPATCH_TILE = 512
NEG_BIG = -1e30
VMEM_LIMIT = 48 * 1024 * 1024


def _cparams(*sem):
    return pltpu.CompilerParams(dimension_semantics=sem, vmem_limit_bytes=VMEM_LIMIT)


def _dot(a, b):
    return jnp.dot(a, b, preferred_element_type=F32)


def _dot_nt(a, b):
    return lax.dot_general(a, b, (((1,), (1,)), ((), ())), preferred_element_type=F32)


def _rms(x, g):
    return x * lax.rsqrt(jnp.mean(x * x, axis=-1, keepdims=True) + EPS) * g


def _ada_body(c_ref, w_ref, b_ref, o_ref):
    s = jax.nn.silu(c_ref[...]).astype(BF16)
    o_ref[0] = _dot(s, w_ref[0].astype(BF16)) + b_ref[0]


def _ada(c_all, w_ada, b_ada):
    n_layers, d, n = w_ada.shape
    rows = c_all.shape[0]
    tn = 1536
    return pl.pallas_call(
        _ada_body,
        grid=(n_layers, n // tn),
        in_specs=[pl.BlockSpec((rows, d), lambda l, j: (0, 0)),
                  pl.BlockSpec((1, d, tn), lambda l, j: (l, 0, j)),
                  pl.BlockSpec((1, 1, tn), lambda l, j: (l, 0, j))],
        out_specs=pl.BlockSpec((1, rows, tn), lambda l, j: (l, 0, j)),
        out_shape=jax.ShapeDtypeStruct((n_layers, rows, n), F32),
        compiler_params=_cparams("arbitrary", "arbitrary"),
    )(c_all, w_ada, b_ada.reshape(n_layers, 1, n))


def _premix_body(x_ref, mod_ref, g_ref, w_ref, pa_ref, q_ref, k_ref, v_ref, kb_ref, vb_ref, pc_ref):
    x = x_ref[...]
    d = x.shape[1]
    h = _rms(x, g_ref[...]) * (1 + mod_ref[0, :, d:2 * d]) + mod_ref[0, :, 0:d]
    p = _dot(h.astype(BF16), w_ref[...])
    d_a2 = pa_ref.shape[1]
    d_b = q_ref.shape[1]
    o = d_a2
    pa_ref[...] = p[:, 0:o]
    q_ref[...] = p[:, o:o + d_b].astype(BF16)
    k = p[:, o + d_b:o + 2 * d_b]
    v = p[:, o + 2 * d_b:o + 3 * d_b]
    k_ref[...] = k
    v_ref[...] = v
    kb_ref[...] = k.astype(BF16)
    vb_ref[...] = v.astype(BF16)
    pc_ref[...] = p[:, o + 3 * d_b:]


def _premix(x, mod, g, w_in_b, row_map, d_a, d_b, d_c):
    t, d = x.shape
    tm = TOK_TILE
    d_in = w_in_b.shape[1]
    tok = lambda n: pl.BlockSpec((tm, n), lambda i: (i, 0))
    return pl.pallas_call(
        _premix_body,
        grid=(t // tm,),
        in_specs=[tok(d),
                  pl.BlockSpec((1, 1, mod.shape[2]), lambda i: (row_map(i), 0, 0)),
                  pl.BlockSpec((1, d), lambda i: (0, 0)),
                  pl.BlockSpec((d, d_in), lambda i: (0, 0))],
        out_specs=[tok(2 * d_a), tok(d_b), tok(d_b), tok(d_b), tok(d_b), tok(d_b), tok(d_c)],
        out_shape=[jax.ShapeDtypeStruct((t, 2 * d_a), F32),
                   jax.ShapeDtypeStruct((t, d_b), BF16),
                   jax.ShapeDtypeStruct((t, d_b), F32),
                   jax.ShapeDtypeStruct((t, d_b), F32),
                   jax.ShapeDtypeStruct((t, d_b), BF16),
                   jax.ShapeDtypeStruct((t, d_b), BF16),
                   jax.ShapeDtypeStruct((t, d_c), F32)],
        compiler_params=_cparams("arbitrary"),
    )(x, mod, g, w_in_b)


def _chunk_body(pa_ref, g_ref, w_ref, b_ref, o_ref):
    z = jax.nn.gelu(pa_ref[...])
    d_a = o_ref.shape[1]
    hd = d_a // NH_A
    u = z[:, :d_a]
    v = z[:, d_a:]
    mu = jnp.mean(v, axis=-1, keepdims=True)
    var = jnp.mean(jnp.square(v - mu), axis=-1, keepdims=True)
    vb = ((v - mu) * lax.rsqrt(var + EPS) * g_ref[...]).astype(BF16)
    head = lax.broadcasted_iota(I32, (CHUNK, d_a), 1) // hd
    for ch in range(pa_ref.shape[0] // CHUNK):
        rows = slice(ch * CHUNK, (ch + 1) * CHUNK)
        sf = _dot(w_ref[...], vb[rows])
        s = b_ref[...]
        for h in range(NH_A):
            s = s + jnp.where(head == h, sf[h * CHUNK:(h + 1) * CHUNK], 0.0)
        o_ref[rows, :] = (u[rows] * s).astype(BF16)


def _chunk_mlp(pa, sgu_g, w_sp_b, bias):
    t, d2 = pa.shape
    d_a = d2 // 2
    tm = TOK_TILE
    return pl.pallas_call(
        _chunk_body,
        grid=(t // tm,),
        in_specs=[pl.BlockSpec((tm, d2), lambda i: (i, 0)),
                  pl.BlockSpec((1, d_a), lambda i: (0, 0)),
                  pl.BlockSpec(w_sp_b.shape, lambda i: (0, 0)),
                  pl.BlockSpec(bias.shape, lambda i: (0, 0))],
        out_specs=pl.BlockSpec((tm, d_a), lambda i: (i, 0)),
        out_shape=jax.ShapeDtypeStruct((t, d_a), BF16),
        compiler_params=_cparams("arbitrary"),
    )(pa, sgu_g, w_sp_b, bias)


def _stack_pair(qg):
    lane = lax.broadcasted_iota(I32, qg.shape, 1)
    zero = jnp.zeros_like(qg)
    return jnp.concatenate([jnp.where(lane < HD_B, qg, zero), jnp.where(lane >= HD_B, qg, zero)], axis=0)


def _unstack_pair(o2):
    n = o2.shape[0] // 2
    lane = lax.broadcasted_iota(I32, (n, o2.shape[1]), 1)
    return jnp.where(lane < HD_B, o2[:n], o2[n:])


def _ctx_attn_body(q_ref, k_ref, v_ref, o_ref):
    for g in range(q_ref.shape[1] // LANES):
        cols = slice(g * LANES, (g + 1) * LANES)
        q2 = _stack_pair(q_ref[:, cols])
        s = _dot_nt(q2, k_ref[:, cols].astype(BF16)) * ATTN_SCALE
        e = jnp.exp(s - jnp.max(s, axis=-1, keepdims=True))
        p = e / jnp.sum(e, axis=-1, keepdims=True)
        o2 = _dot(p.astype(BF16), v_ref[:, cols].astype(BF16))
        o_ref[:, cols] = _unstack_pair(o2).astype(BF16)


def _ctx_attn(q, k, v, n_batch, seq):
    d_b = q.shape[1]
    blk = pl.BlockSpec((seq, d_b), lambda b: (b, 0))
    return pl.pallas_call(
        _ctx_attn_body,
        grid=(n_batch,),
        in_specs=[blk, blk, blk],
        out_specs=blk,
        out_shape=jax.ShapeDtypeStruct((n_batch * seq, d_b), BF16),
        compiler_params=_cparams("arbitrary"),
    )(q, k, v)


def _nat_body(q_ref, k_ref, v_ref, ck_ref, cv_ref, bias_ref, o_ref, *, rows):
    r = pl.program_id(1)
    start = pl.multiple_of(jnp.clip(r - NA_WIN_R // 2, 0, rows - NA_WIN_R) * GRID_W, GRID_W)
    n_win = NA_WIN_R * GRID_W
    for g in range(q_ref.shape[1] // LANES):
        cols = slice(g * LANES, (g + 1) * LANES)
        q2 = _stack_pair(q_ref[:, cols])
        s_win = _dot_nt(q2, k_ref[pl.ds(start, n_win), cols]) * ATTN_SCALE + bias_ref[0, g]
        s_ctx = _dot_nt(q2, ck_ref[0, :, cols]) * ATTN_SCALE
        m = jnp.maximum(jnp.max(s_win, axis=-1, keepdims=True), jnp.max(s_ctx, axis=-1, keepdims=True))
        e_win = jnp.exp(s_win - m)
        e_ctx = jnp.exp(s_ctx - m)
        inv = 1.0 / (jnp.sum(e_win, axis=-1, keepdims=True) + jnp.sum(e_ctx, axis=-1, keepdims=True))
        o2 = (_dot((e_win * inv).astype(BF16), v_ref[pl.ds(start, n_win), cols])
              + _dot((e_ctx * inv).astype(BF16), cv_ref[0, :, cols]))
        o_ref[:, cols] = _unstack_pair(o2).astype(BF16)


def _nat_attn(q, kb, vb, ck, cv, bias, n_batch, seq, tok0):
    d_b = q.shape[1]
    rows = seq // GRID_W
    lc = ck.shape[1]
    q0 = tok0 // GRID_W
    i0 = tok0 // seq

    def case(r):
        return r - jnp.clip(r - NA_WIN_R // 2, 0, rows - NA_WIN_R)

    img = pl.BlockSpec((seq, d_b), lambda b, r: (i0 + b, 0))
    ctx = pl.BlockSpec((1, lc, d_b), lambda b, r: (b, 0, 0))
    return pl.pallas_call(
        functools.partial(_nat_body, rows=rows),
        grid=(n_batch, rows),
        in_specs=[pl.BlockSpec((GRID_W, d_b), lambda b, r: (q0 + b * rows + r, 0)),
                  img, img, ctx, ctx,
                  pl.BlockSpec((1,) + bias.shape[1:], lambda b, r: (case(r), 0, 0, 0))],
        out_specs=pl.BlockSpec((GRID_W, d_b), lambda b, r: (b * rows + r, 0)),
        out_shape=jax.ShapeDtypeStruct((n_batch * seq, d_b), BF16),
        compiler_params=_cparams("arbitrary", "arbitrary"),
    )(q, kb, vb, ck, cv, bias)


def _nat_bias(rpb):
    n_heads = rpb.shape[0]
    cols = jnp.arange(GRID_W)
    col_start = jnp.clip(cols - NA_WIN_C // 2, 0, GRID_W - NA_WIN_C)
    j = jnp.arange(GRID_W)
    valid = (j[None, :] >= col_start[:, None]) & (j[None, :] < col_start[:, None] + NA_WIN_C)
    col_off = jnp.clip(j[None, :] - cols[:, None] + (NA_WIN_C - 1), 0, 2 * NA_WIN_C - 2)
    out = []
    for delta in range(NA_WIN_R):
        row_off = jnp.arange(NA_WIN_R) - delta + (NA_WIN_R - 1)
        b = rpb[:, row_off][:, :, col_off]
        b = jnp.where(valid[None, None], b.astype(F32), NEG_BIG)
        b = b.transpose(0, 2, 1, 3).reshape(n_heads // 2, 2 * GRID_W, NA_WIN_R * GRID_W)
        out.append(b)
    return jnp.stack(out)


def _s5_weights(a_re, a_im, log_dt, b_re, b_im, c_re, c_im, ssm_d):
    n_g = a_re.shape[1]
    c = S5_CHUNK
    lam = lax.complex(a_re.astype(F32), a_im.astype(F32))
    ldt = lam * jnp.exp(log_dt.astype(F32))[..., None]
    lam_bar = jnp.exp(ldt)
    b_bar = ((lam_bar - 1) / lam)[..., None] * lax.complex(b_re.astype(F32), b_im.astype(F32))
    c_mat = lax.complex(c_re.astype(F32), c_im.astype(F32))
    pw = jnp.exp(ldt[None] * jnp.arange(c + 1, dtype=F32)[:, None, None, None])
    kern = jnp.real(jnp.einsum('dgcp,kdgp,dgpe->dgkce', c_mat, pw[:c], b_bar))
    i = jnp.arange(c)
    lag = i[None, :] - i[:, None]
    tf = jnp.where((lag >= 0)[None, :, :, None, None], kern[0][:, jnp.clip(lag, 0, c - 1)], 0.0)
    tb = jnp.where((lag <= 0)[None, :, :, None, None], kern[1][:, jnp.clip(-lag, 0, c - 1)], 0.0)
    t_mat = (tf + tb).transpose(0, 1, 4, 2, 3).reshape(n_g, c * SSM_CH, c * SSM_CH)
    mf = pw[:c][::-1, 0][:, :, :, None] * b_bar[0][None]
    mb = pw[:c, 1][:, :, :, None] * b_bar[1][None]
    mf = mf.transpose(1, 0, 3, 2).reshape(n_g, c * SSM_CH, SSM_P)
    mb = mb.transpose(1, 0, 3, 2).reshape(n_g, c * SSM_CH, SSM_P)
    mq = jnp.stack([jnp.real(mf), jnp.imag(mf), jnp.real(mb), jnp.imag(mb)], axis=2)
    zf = c_mat[0][:, None] * pw[1:c + 1, 0][:, :, None, :].transpose(1, 0, 2, 3)
    zb = c_mat[1][:, None] * pw[1:c + 1, 1][::-1][:, :, None, :].transpose(1, 0, 2, 3)
    zf = zf.transpose(0, 3, 1, 2).reshape(n_g, SSM_P, c * SSM_CH)
    zb = zb.transpose(0, 3, 1, 2).reshape(n_g, SSM_P, c * SSM_CH)
    wq = jnp.stack([jnp.real(zf), -jnp.imag(zf), jnp.real(zb), -jnp.imag(zb)], axis=1)
    n_pr = n_g // 2
    eye2 = jnp.eye(2, dtype=F32)
    mq = mq.reshape(n_pr, 2, c * SSM_CH, 4, SSM_P)
    m_pair = jnp.einsum('rsiqp,st->rsiqtp', mq, eye2).reshape(n_pr, 2 * c * SSM_CH, 4 * 2 * SSM_P)
    t_pair = jnp.einsum('rsij,st->rsitj', t_mat.reshape(n_pr, 2, c * SSM_CH, c * SSM_CH), eye2)
    t_pair = t_pair.reshape(n_pr, 2 * c * SSM_CH, 2 * c * SSM_CH)
    wq = wq.reshape(n_pr, 2, 4, SSM_P, c * SSM_CH)
    w_pair = jnp.einsum('rsqpj,st->rqsptj', wq, eye2).reshape(n_pr, 4 * 2 * SSM_P, 2 * c * SSM_CH)
    tw_pair = jnp.concatenate([t_pair, w_pair], axis=1).astype(BF16)
    a_c = pw[c]
    a16 = jnp.stack([jnp.real(a_c[0]), jnp.imag(a_c[0]), jnp.real(a_c[1]), jnp.imag(a_c[1])])
    a16 = a16.reshape(4, 1, n_g * SSM_P)
    dvec = jnp.tile(ssm_d.astype(F32).reshape(n_g, 1, SSM_CH), (1, c, 1)).reshape(n_pr, 1, 2 * c * SSM_CH)
    return m_pair.astype(BF16), tw_pair, a16, dvec


def _s5_state_body(u_ref, m_ref, fre_ref, fim_ref, bre_ref, bim_ref):
    r = _dot(u_ref[...].astype(BF16), m_ref[0])
    fre_ref[...] = r[:, 0 * LANES:1 * LANES]
    fim_ref[...] = r[:, 1 * LANES:2 * LANES]
    bre_ref[...] = r[:, 2 * LANES:3 * LANES]
    bim_ref[...] = r[:, 3 * LANES:4 * LANES]


def _s5_states(u2, m_pair):
    rows, width = u2.shape
    n_pr, kdim, _ = m_pair.shape
    tr = min(rows, 256)
    out = pl.BlockSpec((tr, LANES), lambda i, p: (i, p))
    return pl.pallas_call(
        _s5_state_body,
        grid=(rows // tr, n_pr),
        in_specs=[pl.BlockSpec((tr, kdim), lambda i, p: (i, p)),
                  pl.BlockSpec((1, kdim, 4 * LANES), lambda i, p: (p, 0, 0))],
        out_specs=[out] * 4,
        out_shape=[jax.ShapeDtypeStruct((rows, n_pr * LANES), F32)] * 4,
        compiler_params=_cparams("arbitrary", "arbitrary"),
    )(u2, m_pair)


def _s5_scan_body(sfr_ref, sfi_ref, sbr_ref, sbi_ref, a_ref, h0_ref,
                  hfr_ref, hfi_ref, hbr_ref, hbi_ref, fin_ref, st_ref):
    @pl.when(pl.program_id(0) == 0)
    def _():
        st_ref[...] = h0_ref[...]

    ks = sfr_ref.shape[0]
    afr, afi, abr, abi = a_ref[0], a_ref[1], a_ref[2], a_ref[3]

    def step(s, carry):
        fr, fi, br, bi = carry
        sb = ks - 1 - s
        hfr_ref[s] = fr
        hfi_ref[s] = fi
        hbr_ref[sb] = br
        hbi_ref[sb] = bi
        nfr = afr * fr - afi * fi + sfr_ref[s]
        nfi = afr * fi + afi * fr + sfi_ref[s]
        nbr = abr * br - abi * bi + sbr_ref[sb]
        nbi = abr * bi + abi * br + sbi_ref[sb]
        return nfr, nfi, nbr, nbi

    carry = lax.fori_loop(0, ks, step, (st_ref[0], st_ref[1], st_ref[2], st_ref[3]))
    for q in range(4):
        st_ref[q] = carry[q]
        fin_ref[q] = carry[q]


def _s5_scan(s4, a16, h0, n_chunks, n_batch):
    width = s4[0].shape[1]
    ks = min(n_chunks, 32)
    nb = n_chunks // ks
    s3 = [s.reshape(n_chunks, n_batch, width) for s in s4]
    fwd = pl.BlockSpec((ks, n_batch, width), lambda i: (i, 0, 0))
    bwd = pl.BlockSpec((ks, n_batch, width), lambda i: (nb - 1 - i, 0, 0))
    small = lambda shape: pl.BlockSpec(shape, lambda i: (0, 0, 0))
    outs = pl.pallas_call(
        _s5_scan_body,
        grid=(nb,),
        in_specs=[fwd, fwd, bwd, bwd, small(a16.shape), small(h0.shape)],
        out_specs=[fwd, fwd, bwd, bwd, small(h0.shape)],
        out_shape=[jax.ShapeDtypeStruct((n_chunks, n_batch, width), F32)] * 4
        + [jax.ShapeDtypeStruct(h0.shape, F32)],
        scratch_shapes=[pltpu.VMEM(h0.shape, F32)],
        compiler_params=_cparams("arbitrary"),
    )(*s3, a16, h0)
    return [o.reshape(n_chunks * n_batch, width) for o in outs[:4]], outs[4]


def _s5_out_body(u_ref, hfr_ref, hfi_ref, hbr_ref, hbi_ref, tw_ref, d_ref, y_ref):
    u = u_ref[...]
    ku = u.shape[1]
    y = _dot(u.astype(BF16), tw_ref[0, 0:ku, :])
    for q, h_ref in enumerate((hfr_ref, hfi_ref, hbr_ref, hbi_ref)):
        y = y + _dot(h_ref[...].astype(BF16), tw_ref[0, ku + q * LANES:ku + (q + 1) * LANES, :])
    y_ref[...] = jax.nn.gelu(y + d_ref[0] * u)


def _s5_out(u2, h4, tw_pair, dvec):
    rows, width = u2.shape
    n_pr, kdim, ku = tw_pair.shape
    tr = min(rows, 256)
    hb = pl.BlockSpec((tr, LANES), lambda i, p: (i, p))
    return pl.pallas_call(
        _s5_out_body,
        grid=(rows // tr, n_pr),
        in_specs=[pl.BlockSpec((tr, ku), lambda i, p: (i, p)), hb, hb, hb, hb,
                  pl.BlockSpec((1, kdim, ku), lambda i, p: (p, 0, 0)),
                  pl.BlockSpec((1, 1, ku), lambda i, p: (p, 0, 0))],
        out_specs=pl.BlockSpec((tr, ku), lambda i, p: (i, p)),
        out_shape=jax.ShapeDtypeStruct((rows, width), F32),
        compiler_params=_cparams("arbitrary", "arbitrary"),
    )(u2, *h4, tw_pair, dvec)


def _s5(pc, n_batch, seq, weights, h0):
    m_pair, tw_pair, a16, dvec = weights
    d_c = pc.shape[1]
    n_g = d_c // SSM_CH
    n_chunks = seq // S5_CHUNK
    u2 = pc.reshape(n_batch, n_chunks, S5_CHUNK, n_g, SSM_CH).transpose(1, 0, 3, 2, 4)
    u2 = u2.reshape(n_chunks * n_batch, n_g * S5_CHUNK * SSM_CH)
    s4 = _s5_states(u2, m_pair)
    h4, fin = _s5_scan(s4, a16, h0, n_chunks, n_batch)
    y2 = _s5_out(u2, h4, tw_pair, dvec)
    y = y2.reshape(n_chunks, n_batch, n_g, S5_CHUNK, SSM_CH).transpose(1, 0, 3, 2, 4)
    return y.reshape(n_batch * seq, d_c), fin


def _post_body(x_ref, ya_ref, yb_ref, yc_ref, mod_ref, gpost_ref, gffn_ref, wglu_ref, bglu_ref, wo_ref,
               x1_ref, h2_ref):
    x = x_ref[...]
    d = x.shape[1]
    d_a = ya_ref.shape[1]
    d_b = yb_ref.shape[1]
    y = yc_ref[...]
    glu = y * jax.nn.sigmoid(_dot(y.astype(BF16), wglu_ref[...]) + bglu_ref[...])
    mixed = (_dot(ya_ref[...], wo_ref[0:d_a, :]) + _dot(yb_ref[...], wo_ref[d_a:d_a + d_b, :])
             + _dot(glu.astype(BF16), wo_ref[d_a + d_b:, :]))
    x1 = x + mod_ref[0, :, 2 * d:3 * d] * _rms(mixed, gpost_ref[...])
    x1_ref[...] = x1
    h2_ref[...] = _rms(x1, gffn_ref[...]) * (1 + mod_ref[0, :, 4 * d:5 * d]) + mod_ref[0, :, 3 * d:4 * d]


def _post(x, ya, yb, yc, mod, g_post, g_ffn, w_glu_b, b_glu, w_out_b, row_map):
    t, d = x.shape
    tm = TOK_TILE
    tok = lambda n: pl.BlockSpec((tm, n), lambda i: (i, 0))
    full = lambda a: pl.BlockSpec(a.shape, lambda i: (0,) * a.ndim)
    return pl.pallas_call(
        _post_body,
        grid=(t // tm,),
        in_specs=[tok(d), tok(ya.shape[1]), tok(yb.shape[1]), tok(yc.shape[1]),
                  pl.BlockSpec((1, 1, mod.shape[2]), lambda i: (row_map(i), 0, 0)),
                  full(g_post), full(g_ffn), full(w_glu_b), full(b_glu), full(w_out_b)],
        out_specs=[tok(d), tok(d)],
        out_shape=[jax.ShapeDtypeStruct((t, d), F32)] * 2,
        compiler_params=_cparams("arbitrary"),
    )(x, ya, yb, yc, mod, g_post, g_ffn, w_glu_b, b_glu, w_out_b)


def _route_body(h_ref, whi_ref, wlo_ref, b_ref, idx_ref, gate_ref, pos_ref, cnt_ref, carry_ref):
    @pl.when(pl.program_id(0) == 0)
    def _():
        carry_ref[...] = jnp.zeros_like(carry_ref)

    h = h_ref[...]
    tm = h.shape[0]
    hi = h.astype(BF16)
    lo = (h - hi.astype(F32)).astype(BF16)
    logits = _dot_nt(whi_ref[...], hi) + (_dot_nt(whi_ref[...], lo) + _dot_nt(wlo_ref[...], hi))
    scores = jax.nn.sigmoid(logits)
    sel = scores + b_ref[...]
    gsz = N_EXPERTS // N_EXP_GROUPS
    within = lax.broadcasted_iota(I32, (gsz, tm), 0).astype(F32)
    grp = []
    for g in range(N_EXP_GROUPS):
        blk = sel[g * gsz:(g + 1) * gsz]
        m1 = jnp.max(blk, axis=0, keepdims=True)
        first = jnp.min(jnp.where(blk == m1, within, float(gsz)), axis=0, keepdims=True)
        m2 = jnp.max(jnp.where(within == first, -jnp.inf, blk), axis=0, keepdims=True)
        grp.append(m1 + m2)
    blocks = []
    for g in range(N_EXP_GROUPS):
        ahead = jnp.zeros((1, tm), F32)
        for o in range(N_EXP_GROUPS):
            if o == g:
                continue
            beats = (grp[o] >= grp[g]) if o < g else (grp[o] > grp[g])
            ahead = ahead + jnp.where(beats, 1.0, 0.0)
        ahead = jnp.broadcast_to(ahead, (gsz, tm))
        blocks.append(jnp.where(ahead < TOPK_GROUPS, sel[g * gsz:(g + 1) * gsz], -jnp.inf))
    v = jnp.concatenate(blocks, axis=0)
    eidx = lax.broadcasted_iota(I32, (N_EXPERTS, tm), 0)
    rnk = jnp.zeros((N_EXPERTS, tm), F32)
    for e in range(N_EXPERTS):
        row = v[e:e + 1]
        rnk = rnk + jnp.where(eidx > e, jnp.where(row >= v, 1.0, 0.0), jnp.where(row > v, 1.0, 0.0))
    chosen = rnk < TOP_K
    w = jnp.where(chosen, scores, 0.0)
    wn = w / jnp.sum(w, axis=0, keepdims=True) * ROUTE_SCALE
    tri = (lax.broadcasted_iota(I32, (tm, tm), 0) < lax.broadcasted_iota(I32, (tm, tm), 1))
    chosen_f = jnp.where(chosen, 1.0, 0.0)
    prefix = _dot(chosen_f.astype(BF16), jnp.where(tri, 1.0, 0.0).astype(BF16)) + carry_ref[:, 0:1]
    total = carry_ref[...] + jnp.sum(chosen_f, axis=1, keepdims=True)
    carry_ref[...] = total
    cnt_ref[...] = total
    eidx_f = eidx.astype(F32)
    for k in range(TOP_K):
        one = rnk == k
        idx_ref[k:k + 1, :] = jnp.sum(jnp.where(one, eidx_f, 0.0), axis=0, keepdims=True).astype(I32)
        gate_ref[k:k + 1, :] = jnp.sum(jnp.where(one, wn, 0.0), axis=0, keepdims=True)
        pos_ref[k:k + 1, :] = jnp.sum(jnp.where(one, prefix, 0.0), axis=0, keepdims=True).astype(I32)


def _route(h2, w_hi, w_lo, b_router):
    t, d = h2.shape
    tm = TOK_TILE
    out = pl.BlockSpec((TOP_K, tm), lambda i: (0, i))
    full = lambda a: pl.BlockSpec(a.shape, lambda i: (0, 0))
    return pl.pallas_call(
        _route_body,
        grid=(t // tm,),
        in_specs=[pl.BlockSpec((tm, d), lambda i: (i, 0)), full(w_hi), full(w_lo), full(b_router)],
        out_specs=[out, out, out, pl.BlockSpec((N_EXPERTS, LANES), lambda i: (0, 0))],
        out_shape=[jax.ShapeDtypeStruct((TOP_K, t), I32), jax.ShapeDtypeStruct((TOP_K, t), F32),
                   jax.ShapeDtypeStruct((TOP_K, t), I32), jax.ShapeDtypeStruct((N_EXPERTS, LANES), F32)],
        scratch_shapes=[pltpu.VMEM((N_EXPERTS, LANES), F32)],
        compiler_params=_cparams("arbitrary"),
    )(h2, w_hi, w_lo, b_router)


def _dispatch_body(dest_ref, h_hbm, xs_in, xs_out, sem):
    del xs_in
    tile = dest_ref.shape[1]
    base = pl.program_id(0) * tile

    def row_copy(t, k):
        return pltpu.make_async_copy(h_hbm.at[pl.ds(base + t, 1)], xs_out.at[pl.ds(dest_ref[k, t], 1)], sem)

    def start(t, c):
        for k in range(TOP_K):
            row_copy(t, k).start()
        return c

    def wait(t, c):
        for k in range(TOP_K):
            row_copy(t, k).wait()
        return c

    lax.fori_loop(0, tile, start, 0)
    lax.fori_loop(0, tile, wait, 0)


def _dispatch(dest, h2, xs_zero):
    t = h2.shape[0]
    tile = DISPATCH_TILE
    return pl.pallas_call(
        _dispatch_body,
        grid=(t // tile,),
        in_specs=[pl.BlockSpec((TOP_K, tile), lambda i: (0, i), memory_space=pltpu.SMEM),
                  pl.BlockSpec(memory_space=pl.ANY),
                  pl.BlockSpec(memory_space=pl.ANY)],
        out_specs=pl.BlockSpec(memory_space=pl.ANY),
        out_shape=jax.ShapeDtypeStruct(xs_zero.shape, xs_zero.dtype),
        scratch_shapes=[pltpu.SemaphoreType.DMA(())],
        input_output_aliases={2: 0},
        compiler_params=_cparams("arbitrary"),
    )(dest, h2, xs_zero)


def _experts_body(be_ref, x_ref, wg_ref, wu_ref, wd_ref, y_ref):
    del be_ref
    xb = x_ref[...].astype(BF16)
    hb = jax.nn.silu(_dot(xb, wg_ref[0])) * _dot(xb, wu_ref[0])
    y_ref[...] = _dot(hb.astype(BF16), wd_ref[0])


def _experts(blk_exp, xs, wg, wu, wd):
    n_rows, d = xs.shape
    de = wg.shape[2]
    bm = MOE_ROWS
    grid_spec = pltpu.PrefetchScalarGridSpec(
        num_scalar_prefetch=1,
        grid=(n_rows // bm,),
        in_specs=[pl.BlockSpec((bm, d), lambda i, be: (i, 0)),
                  pl.BlockSpec((1, d, de), lambda i, be: (be[i], 0, 0)),
                  pl.BlockSpec((1, d, de), lambda i, be: (be[i], 0, 0)),
                  pl.BlockSpec((1, de, d), lambda i, be: (be[i], 0, 0))],
        out_specs=pl.BlockSpec((bm, d), lambda i, be: (i, 0)),
    )
    return pl.pallas_call(
        _experts_body,
        grid_spec=grid_spec,
        out_shape=jax.ShapeDtypeStruct((n_rows, d), F32),
        compiler_params=_cparams("arbitrary"),
    )(blk_exp, xs, wg, wu, wd)


def _combine_body(dest_ref, gate_ref, h_ref, x_ref, mod_ref, g_ref, wsg_ref, wsu_ref, wsd_ref, ys_hbm,
                  o_ref, buf_ref, sem):
    tile = h_ref.shape[0]
    d = h_ref.shape[1]

    def row_copy(t, k):
        return pltpu.make_async_copy(ys_hbm.at[pl.ds(dest_ref[k, t], 1)], buf_ref.at[k, pl.ds(t, 1)], sem)

    def start(t, c):
        for k in range(TOP_K):
            row_copy(t, k).start()
        return c

    def wait(t, c):
        for k in range(TOP_K):
            row_copy(t, k).wait()
        return c

    lax.fori_loop(0, tile, start, 0)
    hb = h_ref[...].astype(BF16)
    acc = _dot((jax.nn.silu(_dot(hb, wsg_ref[...])) * _dot(hb, wsu_ref[...])).astype(BF16), wsd_ref[...])
    lax.fori_loop(0, tile, wait, 0)
    gates = gate_ref[...]
    moe = gates[:, 0:1] * buf_ref[0]
    for k in range(1, TOP_K):
        moe = moe + gates[:, k:k + 1] * buf_ref[k]
    o_ref[...] = x_ref[...] + mod_ref[0, :, 5 * d:6 * d] * _rms(moe + acc, g_ref[...])


def _combine(dest, gates, h2, x1, mod, g_post, wsg, wsu, wsd, ys, row_map):
    t, d = h2.shape
    tile = COMBINE_TILE
    per_tok = TOK_TILE // tile
    tok = lambda n: pl.BlockSpec((tile, n), lambda i: (i, 0))
    full = lambda a: pl.BlockSpec(a.shape, lambda i: (0,) * a.ndim)
    return pl.pallas_call(
        _combine_body,
        grid=(t // tile,),
        in_specs=[pl.BlockSpec((TOP_K, tile), lambda i: (0, i), memory_space=pltpu.SMEM),
                  tok(TOP_K), tok(d), tok(d),
                  pl.BlockSpec((1, 1, mod.shape[2]), lambda i: (row_map(i // per_tok), 0, 0)),
                  full(g_post), full(wsg), full(wsu), full(wsd),
                  pl.BlockSpec(memory_space=pl.ANY)],
        out_specs=tok(d),
        out_shape=jax.ShapeDtypeStruct((t, d), F32),
        scratch_shapes=[pltpu.VMEM((TOP_K, tile, d), F32), pltpu.SemaphoreType.DMA(())],
        compiler_params=_cparams("arbitrary"),
    )(dest, gates, h2, x1, mod, g_post, wsg, wsu, wsd, ys)


def _moe(h2, x1, mod, p, row_map):
    t, d = h2.shape
    idx, gate, pos, cnt = _route(h2, p['wr_hi'], p['wr_lo'], p['b_router'])
    counts = cnt[:, 0].astype(I32)
    padded = (counts + MOE_ROWS - 1) // MOE_ROWS * MOE_ROWS
    ends = jnp.cumsum(padded)
    starts = ends - padded
    dest = starts[idx] + pos
    n_blocks = (t * TOP_K + N_EXPERTS * (MOE_ROWS - 1)) // MOE_ROWS + 1
    blk_exp = jnp.minimum(jnp.searchsorted(ends, jnp.arange(n_blocks, dtype=I32) * MOE_ROWS, side='right'),
                          N_EXPERTS - 1).astype(I32)
    xs = _dispatch(dest, h2, jnp.zeros((n_blocks * MOE_ROWS, d), F32))
    ys = _experts(blk_exp, xs, p['w_e_gate'], p['w_e_up'], p['w_e_down'])
    return _combine(dest, gate.T, h2, x1, mod, p['g_post_ffn'], p['w_s_gate'], p['w_s_up'], p['w_s_down'],
                    ys, row_map)


def kernel(x_prompt, x_sample, cache_k, cache_v, state_ssm_re, state_ssm_im, c, c_ctx,
           g_pre_mix, g_post_mix, g_pre_ffn, g_post_ffn, w_ada, b_ada, w_in, w_out,
           sgu_g, w_sp, b_sp, rpb, ssm_a_re, ssm_a_im, ssm_log_dt, ssm_b_re, ssm_b_im,
           ssm_c_re, ssm_c_im, ssm_d, w_glu, b_glu, w_router, b_router,
           w_e_gate, w_e_up, w_e_down, w_s_gate, w_s_up, w_s_down):
    n_pb, p_seq, d = x_prompt.shape
    n_sb, s_seq, _ = x_sample.shape
    depth = w_in.shape[0]
    d_a = sgu_g.shape[1]
    d_c = w_glu.shape[1]
    d_b = d - d_a - d_c
    n_heads = d_b // HD_B
    n_g = d_c // SSM_CH
    t_p = n_pb * p_seq
    assert p_seq % TOK_TILE == 0 and s_seq % TOK_TILE == 0 and t_p % s_seq == 0
    assert s_seq % GRID_W == 0 and s_seq // GRID_W >= NA_WIN_R

    mod_rows = -(-(n_sb + 1) // 8) * 8
    c_all = jnp.zeros((mod_rows, d), F32).at[:n_sb].set(c).at[n_sb].set(c_ctx)
    mod_all = _ada(c_all, w_ada, b_ada)
    p_tiles = t_p // TOK_TILE
    s_tiles = s_seq // TOK_TILE

    def row_map(i):
        return jnp.where(i < p_tiles, n_sb, (i - p_tiles) // s_tiles)

    x = jnp.concatenate([x_prompt.reshape(t_p, d), x_sample.reshape(n_sb * s_seq, d)], axis=0)
    new_k, new_v, new_re, new_im = [], [], [], []
    for l in range(depth):
        mod = mod_all[l][:, None, :]
        row = lambda a: a[l][None, :].astype(F32)
        pa, q, k32, v32, kb, vb, pc = _premix(x, mod, row(g_pre_mix), w_in[l].astype(BF16), row_map,
                                              d_a, d_b, d_c)
        bias_a = jnp.repeat(b_sp[l].T.astype(F32), d_a // NH_A, axis=1)
        ya = _chunk_mlp(pa, row(sgu_g), w_sp[l].reshape(NH_A * CHUNK, CHUNK).astype(BF16), bias_a)
        yb_p = _ctx_attn(q, k32, v32, n_pb, p_seq)
        ck = cache_k[:, l].reshape(n_sb, -1, d_b).astype(BF16)
        cv = cache_v[:, l].reshape(n_sb, -1, d_b).astype(BF16)
        yb_s = _nat_attn(q, kb, vb, ck, cv, _nat_bias(rpb[l]), n_sb, s_seq, t_p)
        s5w = _s5_weights(ssm_a_re[l], ssm_a_im[l], ssm_log_dt[l], ssm_b_re[l], ssm_b_im[l],
                          ssm_c_re[l], ssm_c_im[l], ssm_d[l])
        yc_p, fin_p = _s5(pc[:t_p], n_pb, p_seq, s5w, jnp.zeros((4, n_pb, n_g * SSM_P), F32))
        sre = state_ssm_re[:, l].astype(F32).reshape(n_sb, 2, n_g * SSM_P)
        sim = state_ssm_im[:, l].astype(F32).reshape(n_sb, 2, n_g * SSM_P)
        h0_s = jnp.stack([sre[:, 0], sim[:, 0], sre[:, 1], sim[:, 1]])
        yc_s, _ = _s5(pc[t_p:], n_sb, s_seq, s5w, h0_s)
        x1, h2 = _post(x, ya, jnp.concatenate([yb_p, yb_s], axis=0), jnp.concatenate([yc_p, yc_s], axis=0),
                       mod, row(g_post_mix), row(g_pre_ffn), w_glu[l].astype(BF16), row(b_glu),
                       w_out[l].astype(BF16), row_map)
        wr = w_router[l].astype(F32).T
        wr_hi = wr.astype(BF16)
        moe_p = {
            'wr_hi': wr_hi, 'wr_lo': (wr - wr_hi.astype(F32)).astype(BF16),
            'b_router': b_router[l].astype(F32)[:, None],
            'w_e_gate': w_e_gate[l].astype(BF16), 'w_e_up': w_e_up[l].astype(BF16),
            'w_e_down': w_e_down[l].astype(BF16),
            'w_s_gate': w_s_gate[l].astype(BF16), 'w_s_up': w_s_up[l].astype(BF16),
            'w_s_down': w_s_down[l].astype(BF16), 'g_post_ffn': row(g_post_ffn),
        }
        x = _moe(h2, x1, mod, moe_p, row_map)
        new_k.append(k32[:t_p].reshape(n_pb, p_seq, n_heads, HD_B))
        new_v.append(v32[:t_p].reshape(n_pb, p_seq, n_heads, HD_B))
        fin_p = fin_p.reshape(4, n_pb, n_g, SSM_P)
        new_re.append(jnp.stack([fin_p[0], fin_p[2]], axis=1))
        new_im.append(jnp.stack([fin_p[1], fin_p[3]], axis=1))
    return (x[:t_p].reshape(n_pb, p_seq, d), x[t_p:].reshape(n_sb, s_seq, d),
            jnp.stack(new_k, axis=1), jnp.stack(new_v, axis=1),
            jnp.stack(new_re, axis=1), jnp.stack(new_im, axis=1))
```

```python
import functools
import math

import jax
import jax.numpy as jnp
from jax import lax
from jax.experimental import pallas as pl
from jax.experimental.pallas import tpu as pltpu

F32 = jnp.float32
BF16 = jnp.bfloat16
I32 = jnp.int32

GRID_W = 64
EPS = 1e-6
NH_A = 4
CHUNK = 128
HD_B = 64
NA_WIN_R = 8
NA_WIN_C = 16
ATTN_SCALE = HD_B ** -0.5
SSM_CH = 16
SSM_P = 64
N_EXPERTS = 64
TOP_K = 8
N_EXP_GROUPS = 8
TOPK_GROUPS = 4
ROUTE_SCALE = 2.5

LANES = 128
TOK_TILE = 256
S5_CHUNK = 16
MOE_ROWS = 256
COMBINE_TILE = 128
DISPATCH_TILE = 512
NEG_BIG = -1e30
VMEM_LIMIT = 48 * 1024 * 1024


def _cparams(*sem):
    return pltpu.CompilerParams(dimension_semantics=sem, vmem_limit_bytes=VMEM_LIMIT)


def _dot(a, b):
    return jnp.dot(a, b, preferred_element_type=F32)


def _dot_nt(a, b):
    return lax.dot_general(a, b, (((1,), (1,)), ((), ())), preferred_element_type=F32)


def _rms(x, g):
    return x * lax.rsqrt(jnp.mean(x * x, axis=-1, keepdims=True) + EPS) * g


def _load_tiled(ref, n_rows):
    nc = ref.shape[0] // n_rows
    return jnp.concatenate([ref[pl.ds(j, n_rows, stride=nc), :] for j in range(nc)], axis=1)


def _store_tiled(ref, val):
    n_rows = val.shape[0]
    nc = ref.shape[0] // n_rows
    for j in range(nc):
        ref[pl.ds(j, n_rows, stride=nc), :] = val[:, j * LANES:(j + 1) * LANES]


def _ada_body(c_ref, w_ref, b_ref, o_ref):
    s = jax.nn.silu(c_ref[...]).astype(BF16)
    o_ref[0] = _dot(s, w_ref[0].astype(BF16)) + b_ref[0]


def _ada(c_all, w_ada, b_ada):
    n_layers, d, n = w_ada.shape
    rows = c_all.shape[0]
    tn = 1536
    return pl.pallas_call(
        _ada_body,
        name="ada",
        grid=(n_layers, n // tn),
        in_specs=[pl.BlockSpec((rows, d), lambda l, j: (0, 0)),
                  pl.BlockSpec((1, d, tn), lambda l, j: (l, 0, j)),
                  pl.BlockSpec((1, 1, tn), lambda l, j: (l, 0, j))],
        out_specs=pl.BlockSpec((1, rows, tn), lambda l, j: (l, 0, j)),
        out_shape=jax.ShapeDtypeStruct((n_layers, rows, n), F32),
        compiler_params=_cparams("arbitrary", "arbitrary"),
    )(c_all, w_ada, b_ada.reshape(n_layers, 1, n))


def _premix_body(x_ref, mod_ref, g_ref, w_ref, pa_ref, q_ref, k_ref, v_ref, kb_ref, vb_ref, pc_ref):
    x = x_ref[...]
    d = x.shape[1]
    h = _rms(x, g_ref[...]) * (1 + mod_ref[0, :, d:2 * d]) + mod_ref[0, :, 0:d]
    p = _dot(h.astype(BF16), w_ref[...])
    d_a2 = pa_ref.shape[1]
    d_b = q_ref.shape[1]
    o = d_a2
    pa_ref[...] = p[:, 0:o]
    q_ref[...] = p[:, o:o + d_b].astype(BF16)
    k = p[:, o + d_b:o + 2 * d_b]
    v = p[:, o + 2 * d_b:o + 3 * d_b]
    k_ref[...] = k
    v_ref[...] = v
    kb_ref[...] = k.astype(BF16)
    vb_ref[...] = v.astype(BF16)
    pc_ref[...] = p[:, o + 3 * d_b:]


def _premix(x, mod, g, w_in_b, row_map, d_a, d_b, d_c):
    t, d = x.shape
    tm = TOK_TILE
    d_in = w_in_b.shape[1]
    tok = lambda n: pl.BlockSpec((tm, n), lambda i: (i, 0))
    return pl.pallas_call(
        _premix_body,
        name="premix",
        grid=(t // tm,),
        in_specs=[tok(d),
                  pl.BlockSpec((1, 1, mod.shape[2]), lambda i: (row_map(i), 0, 0)),
                  pl.BlockSpec((1, d), lambda i: (0, 0)),
                  pl.BlockSpec((d, d_in), lambda i: (0, 0))],
        out_specs=[tok(2 * d_a), tok(d_b), tok(d_b), tok(d_b), tok(d_b), tok(d_b), tok(d_c)],
        out_shape=[jax.ShapeDtypeStruct((t, 2 * d_a), F32),
                   jax.ShapeDtypeStruct((t, d_b), BF16),
                   jax.ShapeDtypeStruct((t, d_b), F32),
                   jax.ShapeDtypeStruct((t, d_b), F32),
                   jax.ShapeDtypeStruct((t, d_b), BF16),
                   jax.ShapeDtypeStruct((t, d_b), BF16),
                   jax.ShapeDtypeStruct((t, d_c), F32)],
        compiler_params=_cparams("arbitrary"),
    )(x, mod, g, w_in_b)


def _chunk_body(pa_ref, g_ref, w_ref, b_ref, o_ref):
    z = jax.nn.gelu(pa_ref[...])
    d_a = o_ref.shape[1]
    hd = d_a // NH_A
    u = z[:, :d_a]
    v = z[:, d_a:]
    mu = jnp.mean(v, axis=-1, keepdims=True)
    var = jnp.mean(jnp.square(v - mu), axis=-1, keepdims=True)
    vb = ((v - mu) * lax.rsqrt(var + EPS) * g_ref[...]).astype(BF16)
    head = lax.broadcasted_iota(I32, (CHUNK, d_a), 1) // hd
    for ch in range(pa_ref.shape[0] // CHUNK):
        rows = slice(ch * CHUNK, (ch + 1) * CHUNK)
        sf = _dot(w_ref[...], vb[rows])
        s = b_ref[...]
        for h in range(NH_A):
            s = s + jnp.where(head == h, sf[h * CHUNK:(h + 1) * CHUNK], 0.0)
        o_ref[rows, :] = (u[rows] * s).astype(BF16)


def _chunk_mlp(pa, sgu_g, w_sp_b, bias):
    t, d2 = pa.shape
    d_a = d2 // 2
    tm = TOK_TILE
    return pl.pallas_call(
        _chunk_body,
        name="chunk_mlp",
        grid=(t // tm,),
        in_specs=[pl.BlockSpec((tm, d2), lambda i: (i, 0)),
                  pl.BlockSpec((1, d_a), lambda i: (0, 0)),
                  pl.BlockSpec(w_sp_b.shape, lambda i: (0, 0)),
                  pl.BlockSpec(bias.shape, lambda i: (0, 0))],
        out_specs=pl.BlockSpec((tm, d_a), lambda i: (i, 0)),
        out_shape=jax.ShapeDtypeStruct((t, d_a), BF16),
        compiler_params=_cparams("arbitrary"),
    )(pa, sgu_g, w_sp_b, bias)


def _stack_pair(qg):
    lane = lax.broadcasted_iota(I32, qg.shape, 1)
    zero = jnp.zeros_like(qg)
    return jnp.concatenate([jnp.where(lane < HD_B, qg, zero), jnp.where(lane >= HD_B, qg, zero)], axis=0)


def _unstack_pair(o2):
    n = o2.shape[0] // 2
    lane = lax.broadcasted_iota(I32, (n, o2.shape[1]), 1)
    return jnp.where(lane < HD_B, o2[:n], o2[n:])


def _ctx_attn_body(q_ref, k_ref, v_ref, o_ref):
    for g in range(q_ref.shape[1] // LANES):
        cols = slice(g * LANES, (g + 1) * LANES)
        q2 = _stack_pair(q_ref[:, cols])
        s = _dot_nt(q2, k_ref[:, cols].astype(BF16)) * ATTN_SCALE
        e = jnp.exp(s - jnp.max(s, axis=-1, keepdims=True))
        p = e / jnp.sum(e, axis=-1, keepdims=True)
        o2 = _dot(p.astype(BF16), v_ref[:, cols].astype(BF16))
        o_ref[:, cols] = _unstack_pair(o2).astype(BF16)


def _ctx_attn(q, k, v, n_batch, seq):
    d_b = q.shape[1]
    blk = pl.BlockSpec((seq, d_b), lambda b: (b, 0))
    return pl.pallas_call(
        _ctx_attn_body,
        name="ctx_attn",
        grid=(n_batch,),
        in_specs=[blk, blk, blk],
        out_specs=blk,
        out_shape=jax.ShapeDtypeStruct((n_batch * seq, d_b), BF16),
        compiler_params=_cparams("arbitrary"),
    )(q, k, v)


def _nat_body(q_ref, k_ref, v_ref, ck_ref, cv_ref, bias_ref, o_ref, *, rows):
    r = pl.program_id(1)
    start = pl.multiple_of(jnp.clip(r - NA_WIN_R // 2, 0, rows - NA_WIN_R) * GRID_W, GRID_W)
    n_win = NA_WIN_R * GRID_W
    for g in range(q_ref.shape[1] // LANES):
        cols = slice(g * LANES, (g + 1) * LANES)
        q2 = _stack_pair(q_ref[:, cols])
        s_win = _dot_nt(q2, k_ref[pl.ds(start, n_win), cols]) * ATTN_SCALE + bias_ref[0, g]
        s_ctx = _dot_nt(q2, ck_ref[0, :, cols]) * ATTN_SCALE
        m = jnp.maximum(jnp.max(s_win, axis=-1, keepdims=True), jnp.max(s_ctx, axis=-1, keepdims=True))
        e_win = jnp.exp(s_win - m)
        e_ctx = jnp.exp(s_ctx - m)
        inv = 1.0 / (jnp.sum(e_win, axis=-1, keepdims=True) + jnp.sum(e_ctx, axis=-1, keepdims=True))
        o2 = (_dot((e_win * inv).astype(BF16), v_ref[pl.ds(start, n_win), cols])
              + _dot((e_ctx * inv).astype(BF16), cv_ref[0, :, cols]))
        o_ref[:, cols] = _unstack_pair(o2).astype(BF16)


def _nat_attn(q, kb, vb, ck, cv, bias, n_batch, seq, tok0):
    d_b = q.shape[1]
    rows = seq // GRID_W
    lc = ck.shape[1]
    q0 = tok0 // GRID_W
    i0 = tok0 // seq

    def case(r):
        return r - jnp.clip(r - NA_WIN_R // 2, 0, rows - NA_WIN_R)

    img = pl.BlockSpec((seq, d_b), lambda b, r: (i0 + b, 0))
    ctx = pl.BlockSpec((1, lc, d_b), lambda b, r: (b, 0, 0))
    return pl.pallas_call(
        functools.partial(_nat_body, rows=rows),
        name="nat_attn",
        grid=(n_batch, rows),
        in_specs=[pl.BlockSpec((GRID_W, d_b), lambda b, r: (q0 + b * rows + r, 0)),
                  img, img, ctx, ctx,
                  pl.BlockSpec((1,) + bias.shape[1:], lambda b, r: (case(r), 0, 0, 0))],
        out_specs=pl.BlockSpec((GRID_W, d_b), lambda b, r: (b * rows + r, 0)),
        out_shape=jax.ShapeDtypeStruct((n_batch * seq, d_b), BF16),
        compiler_params=_cparams("arbitrary", "arbitrary"),
    )(q, kb, vb, ck, cv, bias)


def _nat_bias(rpb):
    n_heads = rpb.shape[0]
    cols = jnp.arange(GRID_W)
    col_start = jnp.clip(cols - NA_WIN_C // 2, 0, GRID_W - NA_WIN_C)
    j = jnp.arange(GRID_W)
    valid = (j[None, :] >= col_start[:, None]) & (j[None, :] < col_start[:, None] + NA_WIN_C)
    col_off = jnp.clip(j[None, :] - cols[:, None] + (NA_WIN_C - 1), 0, 2 * NA_WIN_C - 2)
    out = []
    for delta in range(NA_WIN_R):
        row_off = jnp.arange(NA_WIN_R) - delta + (NA_WIN_R - 1)
        b = rpb[:, row_off][:, :, col_off]
        b = jnp.where(valid[None, None], b.astype(F32), NEG_BIG)
        b = b.transpose(0, 2, 1, 3).reshape(n_heads // 2, 2 * GRID_W, NA_WIN_R * GRID_W)
        out.append(b)
    return jnp.stack(out)


def _s5_weights(a_re, a_im, log_dt, b_re, b_im, c_re, c_im, ssm_d):
    n_g = a_re.shape[1]
    c = S5_CHUNK
    lam = lax.complex(a_re.astype(F32), a_im.astype(F32))
    ldt = lam * jnp.exp(log_dt.astype(F32))[..., None]
    lam_bar = jnp.exp(ldt)
    b_bar = ((lam_bar - 1) / lam)[..., None] * lax.complex(b_re.astype(F32), b_im.astype(F32))
    c_mat = lax.complex(c_re.astype(F32), c_im.astype(F32))
    pw = jnp.exp(ldt[None] * jnp.arange(c + 1, dtype=F32)[:, None, None, None])
    kern = jnp.real(jnp.einsum('dgcp,kdgp,dgpe->dgkce', c_mat, pw[:c], b_bar))
    i = jnp.arange(c)
    lag = i[None, :] - i[:, None]
    tf = jnp.where((lag >= 0)[None, :, :, None, None], kern[0][:, jnp.clip(lag, 0, c - 1)], 0.0)
    tb = jnp.where((lag <= 0)[None, :, :, None, None], kern[1][:, jnp.clip(-lag, 0, c - 1)], 0.0)
    t_mat = (tf + tb).transpose(0, 1, 4, 2, 3).reshape(n_g, c * SSM_CH, c * SSM_CH)
    mf = pw[:c][::-1, 0][:, :, :, None] * b_bar[0][None]
    mb = pw[:c, 1][:, :, :, None] * b_bar[1][None]
    mf = mf.transpose(1, 0, 3, 2).reshape(n_g, c * SSM_CH, SSM_P)
    mb = mb.transpose(1, 0, 3, 2).reshape(n_g, c * SSM_CH, SSM_P)
    mq = jnp.stack([jnp.real(mf), jnp.imag(mf), jnp.real(mb), jnp.imag(mb)], axis=2)
    zf = c_mat[0][:, None] * pw[1:c + 1, 0][:, :, None, :].transpose(1, 0, 2, 3)
    zb = c_mat[1][:, None] * pw[1:c + 1, 1][::-1][:, :, None, :].transpose(1, 0, 2, 3)
    zf = zf.transpose(0, 3, 1, 2).reshape(n_g, SSM_P, c * SSM_CH)
    zb = zb.transpose(0, 3, 1, 2).reshape(n_g, SSM_P, c * SSM_CH)
    wq = jnp.stack([jnp.real(zf), -jnp.imag(zf), jnp.real(zb), -jnp.imag(zb)], axis=1)
    n_pr = n_g // 2
    eye2 = jnp.eye(2, dtype=F32)
    mq = mq.reshape(n_pr, 2, c * SSM_CH, 4, SSM_P)
    m_pair = jnp.einsum('rsiqp,st->rsiqtp', mq, eye2).reshape(n_pr, 2 * c * SSM_CH, 4 * 2 * SSM_P)
    t_pair = jnp.einsum('rsij,st->rsitj', t_mat.reshape(n_pr, 2, c * SSM_CH, c * SSM_CH), eye2)
    t_pair = t_pair.reshape(n_pr, 2 * c * SSM_CH, 2 * c * SSM_CH)
    wq = wq.reshape(n_pr, 2, 4, SSM_P, c * SSM_CH)
    w_pair = jnp.einsum('rsqpj,st->rqsptj', wq, eye2).reshape(n_pr, 4 * 2 * SSM_P, 2 * c * SSM_CH)
    tw_pair = jnp.concatenate([t_pair, w_pair], axis=1).astype(BF16)
    a_c = pw[c]
    a16 = jnp.stack([jnp.real(a_c[0]), jnp.imag(a_c[0]), jnp.real(a_c[1]), jnp.imag(a_c[1])])
    a16 = a16.reshape(4, 1, n_g * SSM_P)
    dvec = jnp.tile(ssm_d.astype(F32).reshape(n_g, 1, SSM_CH), (1, c, 1)).reshape(n_pr, 1, 2 * c * SSM_CH)
    return m_pair.astype(BF16), tw_pair, a16, dvec


def _s5_state_body(u_ref, m_ref, fre_ref, fim_ref, bre_ref, bim_ref):
    r = _dot(u_ref[...].astype(BF16), m_ref[0])
    fre_ref[...] = r[:, 0 * LANES:1 * LANES]
    fim_ref[...] = r[:, 1 * LANES:2 * LANES]
    bre_ref[...] = r[:, 2 * LANES:3 * LANES]
    bim_ref[...] = r[:, 3 * LANES:4 * LANES]


def _s5_states(u2, m_pair):
    rows, width = u2.shape
    n_pr, kdim, _ = m_pair.shape
    tr = min(rows, 256)
    out = pl.BlockSpec((tr, LANES), lambda i, p: (i, p))
    return pl.pallas_call(
        _s5_state_body,
        name="s5_states",
        grid=(rows // tr, n_pr),
        in_specs=[pl.BlockSpec((tr, kdim), lambda i, p: (i, p)),
                  pl.BlockSpec((1, kdim, 4 * LANES), lambda i, p: (p, 0, 0))],
        out_specs=[out] * 4,
        out_shape=[jax.ShapeDtypeStruct((rows, n_pr * LANES), F32)] * 4,
        compiler_params=_cparams("arbitrary", "arbitrary"),
    )(u2, m_pair)


def _s5_scan_body(sfr_ref, sfi_ref, sbr_ref, sbi_ref, a_ref, h0_ref,
                  hfr_ref, hfi_ref, hbr_ref, hbi_ref, fin_ref, st_ref):
    @pl.when(pl.program_id(0) == 0)
    def _():
        st_ref[...] = h0_ref[...]

    ks = sfr_ref.shape[0]
    afr, afi, abr, abi = a_ref[0], a_ref[1], a_ref[2], a_ref[3]

    def step(s, carry):
        fr, fi, br, bi = carry
        sb = ks - 1 - s
        hfr_ref[s] = fr
        hfi_ref[s] = fi
        hbr_ref[sb] = br
        hbi_ref[sb] = bi
        nfr = afr * fr - afi * fi + sfr_ref[s]
        nfi = afr * fi + afi * fr + sfi_ref[s]
        nbr = abr * br - abi * bi + sbr_ref[sb]
        nbi = abr * bi + abi * br + sbi_ref[sb]
        return nfr, nfi, nbr, nbi

    carry = lax.fori_loop(0, ks, step, (st_ref[0], st_ref[1], st_ref[2], st_ref[3]))
    for q in range(4):
        st_ref[q] = carry[q]
        fin_ref[q] = carry[q]


def _s5_scan(s4, a16, h0, n_chunks, n_batch):
    width = s4[0].shape[1]
    ks = min(n_chunks, 32)
    nb = n_chunks // ks
    s3 = [s.reshape(n_chunks, n_batch, width) for s in s4]
    fwd = pl.BlockSpec((ks, n_batch, width), lambda i: (i, 0, 0))
    bwd = pl.BlockSpec((ks, n_batch, width), lambda i: (nb - 1 - i, 0, 0))
    small = lambda shape: pl.BlockSpec(shape, lambda i: (0, 0, 0))
    outs = pl.pallas_call(
        _s5_scan_body,
        name="s5_scan",
        grid=(nb,),
        in_specs=[fwd, fwd, bwd, bwd, small(a16.shape), small(h0.shape)],
        out_specs=[fwd, fwd, bwd, bwd, small(h0.shape)],
        out_shape=[jax.ShapeDtypeStruct((n_chunks, n_batch, width), F32)] * 4
        + [jax.ShapeDtypeStruct(h0.shape, F32)],
        scratch_shapes=[pltpu.VMEM(h0.shape, F32)],
        compiler_params=_cparams("arbitrary"),
    )(*s3, a16, h0)
    return [o.reshape(n_chunks * n_batch, width) for o in outs[:4]], outs[4]


def _s5_out_body(u_ref, hfr_ref, hfi_ref, hbr_ref, hbi_ref, tw_ref, d_ref, y_ref):
    u = u_ref[...]
    ku = u.shape[1]
    y = _dot(u.astype(BF16), tw_ref[0, 0:ku, :])
    for q, h_ref in enumerate((hfr_ref, hfi_ref, hbr_ref, hbi_ref)):
        y = y + _dot(h_ref[...].astype(BF16), tw_ref[0, ku + q * LANES:ku + (q + 1) * LANES, :])
    y_ref[...] = jax.nn.gelu(y + d_ref[0] * u)


def _s5_out(u2, h4, tw_pair, dvec):
    rows, width = u2.shape
    n_pr, kdim, ku = tw_pair.shape
    tr = min(rows, 256)
    hb = pl.BlockSpec((tr, LANES), lambda i, p: (i, p))
    return pl.pallas_call(
        _s5_out_body,
        name="s5_out",
        grid=(rows // tr, n_pr),
        in_specs=[pl.BlockSpec((tr, ku), lambda i, p: (i, p)), hb, hb, hb, hb,
                  pl.BlockSpec((1, kdim, ku), lambda i, p: (p, 0, 0)),
                  pl.BlockSpec((1, 1, ku), lambda i, p: (p, 0, 0))],
        out_specs=pl.BlockSpec((tr, ku), lambda i, p: (i, p)),
        out_shape=jax.ShapeDtypeStruct((rows, width), F32),
        compiler_params=_cparams("arbitrary", "arbitrary"),
    )(u2, *h4, tw_pair, dvec)


def _s5(pc, n_batch, seq, weights, h0):
    m_pair, tw_pair, a16, dvec = weights
    d_c = pc.shape[1]
    n_g = d_c // SSM_CH
    n_chunks = seq // S5_CHUNK
    u2 = pc.reshape(n_batch, n_chunks, S5_CHUNK, n_g, SSM_CH).transpose(1, 0, 3, 2, 4)
    u2 = u2.reshape(n_chunks * n_batch, n_g * S5_CHUNK * SSM_CH)
    s4 = _s5_states(u2, m_pair)
    h4, fin = _s5_scan(s4, a16, h0, n_chunks, n_batch)
    y2 = _s5_out(u2, h4, tw_pair, dvec)
    y = y2.reshape(n_chunks, n_batch, n_g, S5_CHUNK, SSM_CH).transpose(1, 0, 3, 2, 4)
    return y.reshape(n_batch * seq, d_c), fin


def _post_body(x_ref, ya_ref, yb_ref, yc_ref, mod_ref, gpost_ref, gffn_ref, wglu_ref, bglu_ref, wo_ref,
               x1_ref, h2_ref):
    x = x_ref[...]
    d = x.shape[1]
    d_a = ya_ref.shape[1]
    d_b = yb_ref.shape[1]
    y = yc_ref[...]
    glu = y * jax.nn.sigmoid(_dot(y.astype(BF16), wglu_ref[...]) + bglu_ref[...])
    mixed = (_dot(ya_ref[...], wo_ref[0:d_a, :]) + _dot(yb_ref[...], wo_ref[d_a:d_a + d_b, :])
             + _dot(glu.astype(BF16), wo_ref[d_a + d_b:, :]))
    x1 = x + mod_ref[0, :, 2 * d:3 * d] * _rms(mixed, gpost_ref[...])
    x1_ref[...] = x1
    h2 = _rms(x1, gffn_ref[...]) * (1 + mod_ref[0, :, 4 * d:5 * d]) + mod_ref[0, :, 3 * d:4 * d]
    _store_tiled(h2_ref, h2)


def _post(x, ya, yb, yc, mod, g_post, g_ffn, w_glu_b, b_glu, w_out_b, row_map):
    t, d = x.shape
    tm = TOK_TILE
    nc = d // LANES
    tok = lambda n: pl.BlockSpec((tm, n), lambda i: (i, 0))
    full = lambda a: pl.BlockSpec(a.shape, lambda i: (0,) * a.ndim)
    return pl.pallas_call(
        _post_body,
        name="post_mix",
        grid=(t // tm,),
        in_specs=[tok(d), tok(ya.shape[1]), tok(yb.shape[1]), tok(yc.shape[1]),
                  pl.BlockSpec((1, 1, mod.shape[2]), lambda i: (row_map(i), 0, 0)),
                  full(g_post), full(g_ffn), full(w_glu_b), full(b_glu), full(w_out_b)],
        out_specs=[tok(d), pl.BlockSpec((tm * nc, LANES), lambda i: (i, 0))],
        out_shape=[jax.ShapeDtypeStruct((t, d), F32), jax.ShapeDtypeStruct((t * nc, LANES), F32)],
        compiler_params=_cparams("arbitrary"),
    )(x, ya, yb, yc, mod, g_post, g_ffn, w_glu_b, b_glu, w_out_b)


def _route_body(h_ref, whi_ref, wlo_ref, b_ref, idx_ref, gate_ref, pos_ref, cnt_ref, carry_ref):
    @pl.when(pl.program_id(0) == 0)
    def _():
        carry_ref[...] = jnp.zeros_like(carry_ref)

    tm = idx_ref.shape[1]
    h = _load_tiled(h_ref, tm)
    hi = h.astype(BF16)
    lo = (h - hi.astype(F32)).astype(BF16)
    logits = _dot_nt(whi_ref[...], hi) + (_dot_nt(whi_ref[...], lo) + _dot_nt(wlo_ref[...], hi))
    scores = jax.nn.sigmoid(logits)
    sel = scores + b_ref[...]
    gsz = N_EXPERTS // N_EXP_GROUPS
    within = lax.broadcasted_iota(I32, (gsz, tm), 0).astype(F32)
    grp = []
    for g in range(N_EXP_GROUPS):
        blk = sel[g * gsz:(g + 1) * gsz]
        m1 = jnp.max(blk, axis=0, keepdims=True)
        first = jnp.min(jnp.where(blk == m1, within, float(gsz)), axis=0, keepdims=True)
        m2 = jnp.max(jnp.where(within == first, -jnp.inf, blk), axis=0, keepdims=True)
        grp.append(m1 + m2)
    blocks = []
    for g in range(N_EXP_GROUPS):
        ahead = jnp.zeros((1, tm), F32)
        for o in range(N_EXP_GROUPS):
            if o == g:
                continue
            beats = (grp[o] >= grp[g]) if o < g else (grp[o] > grp[g])
            ahead = ahead + jnp.where(beats, 1.0, 0.0)
        ahead = jnp.broadcast_to(ahead, (gsz, tm))
        blocks.append(jnp.where(ahead < TOPK_GROUPS, sel[g * gsz:(g + 1) * gsz], -jnp.inf))
    v = jnp.concatenate(blocks, axis=0)
    eidx = lax.broadcasted_iota(I32, (N_EXPERTS, tm), 0)
    rnk = jnp.zeros((N_EXPERTS, tm), F32)
    for e in range(N_EXPERTS):
        row = v[e:e + 1]
        rnk = rnk + jnp.where(eidx > e, jnp.where(row >= v, 1.0, 0.0), jnp.where(row > v, 1.0, 0.0))
    chosen = rnk < TOP_K
    w = jnp.where(chosen, scores, 0.0)
    wn = w / jnp.sum(w, axis=0, keepdims=True) * ROUTE_SCALE
    tri = (lax.broadcasted_iota(I32, (tm, tm), 0) < lax.broadcasted_iota(I32, (tm, tm), 1))
    chosen_f = jnp.where(chosen, 1.0, 0.0)
    prefix = _dot(chosen_f.astype(BF16), jnp.where(tri, 1.0, 0.0).astype(BF16)) + carry_ref[:, 0:1]
    total = carry_ref[...] + jnp.sum(chosen_f, axis=1, keepdims=True)
    carry_ref[...] = total
    cnt_ref[...] = total
    eidx_f = eidx.astype(F32)
    for k in range(TOP_K):
        one = rnk == k
        idx_ref[k:k + 1, :] = jnp.sum(jnp.where(one, eidx_f, 0.0), axis=0, keepdims=True).astype(I32)
        gate_ref[k:k + 1, :] = jnp.sum(jnp.where(one, wn, 0.0), axis=0, keepdims=True)
        pos_ref[k:k + 1, :] = jnp.sum(jnp.where(one, prefix, 0.0), axis=0, keepdims=True).astype(I32)


def _route(h2t, w_hi, w_lo, b_router):
    nc = w_hi.shape[1] // LANES
    t = h2t.shape[0] // nc
    tm = TOK_TILE
    out = pl.BlockSpec((TOP_K, tm), lambda i: (0, i))
    full = lambda a: pl.BlockSpec(a.shape, lambda i: (0, 0))
    return pl.pallas_call(
        _route_body,
        name="route",
        grid=(t // tm,),
        in_specs=[pl.BlockSpec((tm * nc, LANES), lambda i: (i, 0)), full(w_hi), full(w_lo), full(b_router)],
        out_specs=[out, out, out, pl.BlockSpec((N_EXPERTS, LANES), lambda i: (0, 0))],
        out_shape=[jax.ShapeDtypeStruct((TOP_K, t), I32), jax.ShapeDtypeStruct((TOP_K, t), F32),
                   jax.ShapeDtypeStruct((TOP_K, t), I32), jax.ShapeDtypeStruct((N_EXPERTS, LANES), F32)],
        scratch_shapes=[pltpu.VMEM((N_EXPERTS, LANES), F32)],
        compiler_params=_cparams("arbitrary"),
    )(h2t, w_hi, w_lo, b_router)


def _dest_body(starts_ref, idx_ref, pos_ref, dest_ref):
    idx = idx_ref[...]
    acc = pos_ref[...]
    for e in range(N_EXPERTS):
        acc = acc + jnp.where(idx == e, starts_ref[e], 0)
    dest_ref[...] = acc


def _dest(starts, idx, pos):
    t = idx.shape[1]
    tile = math.gcd(t, 4096)
    blk = pl.BlockSpec((TOP_K, tile), lambda i, st: (0, i))
    return pl.pallas_call(
        _dest_body,
        name="moe_dest",
        grid_spec=pltpu.PrefetchScalarGridSpec(num_scalar_prefetch=1, grid=(t // tile,),
                                               in_specs=[blk, blk], out_specs=blk),
        out_shape=jax.ShapeDtypeStruct(idx.shape, I32),
        compiler_params=_cparams("arbitrary"),
    )(starts, idx, pos)


def _dispatch_body(pad_lo_ref, pad_hi_ref, dest_ref, h_ref, xs_ref, zero_ref, sem, zsem, *, n_pad_rows):
    tile = dest_ref.shape[1]

    def zero_copy(r):
        return pltpu.make_async_copy(zero_ref, xs_ref.at[r], zsem)

    @pl.when(pl.program_id(0) == 0)
    def _():
        zero_ref[...] = jnp.zeros_like(zero_ref)

        def per_expert(e, c):
            def one(r, c2):
                zero_copy(r).start()
                return c2
            return lax.fori_loop(pad_lo_ref[e], pad_hi_ref[e], one, c)

        lax.fori_loop(0, N_EXPERTS, per_expert, 0)

    def row_copy(t, k):
        return pltpu.make_async_copy(h_ref.at[t], xs_ref.at[dest_ref[k, t]], sem)

    def start(t, c):
        for k in range(TOP_K):
            row_copy(t, k).start()
        return c

    def wait(t, c):
        for k in range(TOP_K):
            row_copy(t, k).wait()
        return c

    lax.fori_loop(0, tile, start, 0)
    lax.fori_loop(0, tile, wait, 0)

    @pl.when(pl.program_id(0) == 0)
    def _():
        def one(r, c):
            zero_copy(0).wait()
            return c
        lax.fori_loop(0, n_pad_rows, one, 0)


def _dispatch(pad_lo, pad_hi, dest, h2t, n_rows):
    t, nc, _ = h2t.shape
    tile = DISPATCH_TILE
    grid_spec = pltpu.PrefetchScalarGridSpec(
        num_scalar_prefetch=2,
        grid=(t // tile,),
        in_specs=[pl.BlockSpec((TOP_K, tile), lambda i, lo, hi: (0, i), memory_space=pltpu.SMEM),
                  pl.BlockSpec((tile, nc, LANES), lambda i, lo, hi: (i, 0, 0))],
        out_specs=pl.BlockSpec(memory_space=pl.ANY),
        scratch_shapes=[pltpu.VMEM((nc, LANES), F32), pltpu.SemaphoreType.DMA(()),
                        pltpu.SemaphoreType.DMA(())],
    )
    return pl.pallas_call(
        functools.partial(_dispatch_body, n_pad_rows=n_rows - t * TOP_K),
        name="moe_dispatch",
        grid_spec=grid_spec,
        out_shape=jax.ShapeDtypeStruct((n_rows, nc, LANES), F32),
        compiler_params=_cparams("arbitrary"),
    )(pad_lo, pad_hi, dest, h2t)


def _experts_body(be_ref, x_ref, wg_ref, wu_ref, wd_ref, y_ref):
    del be_ref
    xb = _load_tiled(x_ref, MOE_ROWS).astype(BF16)
    hb = jax.nn.silu(_dot(xb, wg_ref[0])) * _dot(xb, wu_ref[0])
    _store_tiled(y_ref, _dot(hb.astype(BF16), wd_ref[0]))


def _experts(blk_exp, xs, wg, wu, wd):
    d, de = wg.shape[1], wg.shape[2]
    nc = d // LANES
    bm = MOE_ROWS
    tiles = pl.BlockSpec((bm * nc, LANES), lambda i, be: (i, 0))
    grid_spec = pltpu.PrefetchScalarGridSpec(
        num_scalar_prefetch=1,
        grid=(xs.shape[0] // (bm * nc),),
        in_specs=[tiles,
                  pl.BlockSpec((1, d, de), lambda i, be: (be[i], 0, 0)),
                  pl.BlockSpec((1, d, de), lambda i, be: (be[i], 0, 0)),
                  pl.BlockSpec((1, de, d), lambda i, be: (be[i], 0, 0))],
        out_specs=tiles,
    )
    return pl.pallas_call(
        _experts_body,
        name="moe_experts",
        grid_spec=grid_spec,
        out_shape=jax.ShapeDtypeStruct(xs.shape, F32),
        compiler_params=_cparams("arbitrary"),
    )(blk_exp, xs, wg, wu, wd)


def _combine_body(dest_ref, gate_ref, h_ref, x_ref, mod_ref, g_ref, wsg_ref, wsu_ref, wsd_ref, ys_hbm,
                  o_ref, buf_ref, sem):
    tile, d = x_ref.shape
    nc = d // LANES

    def row_copy(t, k):
        return pltpu.make_async_copy(ys_hbm.at[dest_ref[k, t]],
                                     buf_ref.at[k, pl.ds(pl.multiple_of(t * nc, nc), nc)], sem)

    def start(t, c):
        for k in range(TOP_K):
            row_copy(t, k).start()
        return c

    def wait(t, c):
        for k in range(TOP_K):
            row_copy(t, k).wait()
        return c

    lax.fori_loop(0, tile, start, 0)
    hb = _load_tiled(h_ref, tile).astype(BF16)
    acc = _dot((jax.nn.silu(_dot(hb, wsg_ref[...])) * _dot(hb, wsu_ref[...])).astype(BF16), wsd_ref[...])
    lax.fori_loop(0, tile, wait, 0)
    gates = gate_ref[...]
    moe = gates[:, 0:1] * _load_tiled(buf_ref.at[0], tile)
    for k in range(1, TOP_K):
        moe = moe + gates[:, k:k + 1] * _load_tiled(buf_ref.at[k], tile)
    o_ref[...] = x_ref[...] + mod_ref[0, :, 5 * d:6 * d] * _rms(moe + acc, g_ref[...])


def _combine(dest, gates, h2t, x1, mod, g_post, wsg, wsu, wsd, ys, row_map):
    t, d = x1.shape
    nc = d // LANES
    tile = COMBINE_TILE
    per_tok = TOK_TILE // tile
    tok = lambda n: pl.BlockSpec((tile, n), lambda i: (i, 0))
    full = lambda a: pl.BlockSpec(a.shape, lambda i: (0,) * a.ndim)
    return pl.pallas_call(
        _combine_body,
        name="moe_combine",
        grid=(t // tile,),
        in_specs=[pl.BlockSpec((TOP_K, tile), lambda i: (0, i), memory_space=pltpu.SMEM),
                  tok(TOP_K), pl.BlockSpec((tile * nc, LANES), lambda i: (i, 0)), tok(d),
                  pl.BlockSpec((1, 1, mod.shape[2]), lambda i: (row_map(i // per_tok), 0, 0)),
                  full(g_post), full(wsg), full(wsu), full(wsd),
                  pl.BlockSpec(memory_space=pl.ANY)],
        out_specs=tok(d),
        out_shape=jax.ShapeDtypeStruct((t, d), F32),
        scratch_shapes=[pltpu.VMEM((TOP_K, tile * nc, LANES), F32), pltpu.SemaphoreType.DMA(())],
        compiler_params=_cparams("arbitrary"),
    )(dest, gates, h2t, x1, mod, g_post, wsg, wsu, wsd, ys)


def _moe(h2t, x1, mod, p, row_map):
    t, d = x1.shape
    nc = d // LANES
    idx, gate, pos, cnt = _route(h2t, p['wr_hi'], p['wr_lo'], p['b_router'])
    counts = cnt[:, 0].astype(I32)
    padded = (counts + MOE_ROWS - 1) // MOE_ROWS * MOE_ROWS
    ends = jnp.cumsum(padded)
    starts = ends - padded
    n_blocks = (t * TOP_K + N_EXPERTS * (MOE_ROWS - 1)) // MOE_ROWS + 1
    n_rows = n_blocks * MOE_ROWS
    blk_start = jnp.arange(n_blocks, dtype=I32) * MOE_ROWS
    blk_exp = jnp.minimum(jnp.sum((ends[None, :] <= blk_start[:, None]).astype(I32), axis=1), N_EXPERTS - 1)
    pad_hi = ends.at[N_EXPERTS - 1].set(n_rows)
    dest = _dest(starts, idx, pos)
    xs = _dispatch(starts + counts, pad_hi, dest, h2t.reshape(t, nc, LANES), n_rows)
    ys = _experts(blk_exp, xs.reshape(n_rows * nc, LANES), p['w_e_gate'], p['w_e_up'], p['w_e_down'])
    return _combine(dest, gate.T, h2t, x1, mod, p['g_post_ffn'], p['w_s_gate'], p['w_s_up'], p['w_s_down'],
                    ys.reshape(n_rows, nc, LANES), row_map)


def kernel(x_prompt, x_sample, cache_k, cache_v, state_ssm_re, state_ssm_im, c, c_ctx,
           g_pre_mix, g_post_mix, g_pre_ffn, g_post_ffn, w_ada, b_ada, w_in, w_out,
           sgu_g, w_sp, b_sp, rpb, ssm_a_re, ssm_a_im, ssm_log_dt, ssm_b_re, ssm_b_im,
           ssm_c_re, ssm_c_im, ssm_d, w_glu, b_glu, w_router, b_router,
           w_e_gate, w_e_up, w_e_down, w_s_gate, w_s_up, w_s_down):
    n_pb, p_seq, d = x_prompt.shape
    n_sb, s_seq, _ = x_sample.shape
    depth = w_in.shape[0]
    d_a = sgu_g.shape[1]
    d_c = w_glu.shape[1]
    d_b = d - d_a - d_c
    n_heads = d_b // HD_B
    n_g = d_c // SSM_CH
    t_p = n_pb * p_seq
    assert p_seq % TOK_TILE == 0 and s_seq % TOK_TILE == 0 and t_p % s_seq == 0
    assert s_seq % GRID_W == 0 and s_seq // GRID_W >= NA_WIN_R

    mod_rows = -(-(n_sb + 1) // 8) * 8
    c_all = jnp.zeros((mod_rows, d), F32).at[:n_sb].set(c).at[n_sb].set(c_ctx)
    mod_all = _ada(c_all, w_ada, b_ada)
    p_tiles = t_p // TOK_TILE
    s_tiles = s_seq // TOK_TILE

    def row_map(i):
        return jnp.where(i < p_tiles, n_sb, (i - p_tiles) // s_tiles)

    x = jnp.concatenate([x_prompt.reshape(t_p, d), x_sample.reshape(n_sb * s_seq, d)], axis=0)
    new_k, new_v, new_re, new_im = [], [], [], []
    for l in range(depth):
        mod = mod_all[l][:, None, :]
        row = lambda a: a[l][None, :].astype(F32)
        pa, q, k32, v32, kb, vb, pc = _premix(x, mod, row(g_pre_mix), w_in[l].astype(BF16), row_map,
                                              d_a, d_b, d_c)
        bias_a = jnp.repeat(b_sp[l].T.astype(F32), d_a // NH_A, axis=1)
        ya = _chunk_mlp(pa, row(sgu_g), w_sp[l].reshape(NH_A * CHUNK, CHUNK).astype(BF16), bias_a)
        yb_p = _ctx_attn(q, k32, v32, n_pb, p_seq)
        ck = cache_k[:, l].reshape(n_sb, -1, d_b).astype(BF16)
        cv = cache_v[:, l].reshape(n_sb, -1, d_b).astype(BF16)
        yb_s = _nat_attn(q, kb, vb, ck, cv, _nat_bias(rpb[l]), n_sb, s_seq, t_p)
        s5w = _s5_weights(ssm_a_re[l], ssm_a_im[l], ssm_log_dt[l], ssm_b_re[l], ssm_b_im[l],
                          ssm_c_re[l], ssm_c_im[l], ssm_d[l])
        yc_p, fin_p = _s5(pc[:t_p], n_pb, p_seq, s5w, jnp.zeros((4, n_pb, n_g * SSM_P), F32))
        sre = state_ssm_re[:, l].astype(F32).reshape(n_sb, 2, n_g * SSM_P)
        sim = state_ssm_im[:, l].astype(F32).reshape(n_sb, 2, n_g * SSM_P)
        h0_s = jnp.stack([sre[:, 0], sim[:, 0], sre[:, 1], sim[:, 1]])
        yc_s, _ = _s5(pc[t_p:], n_sb, s_seq, s5w, h0_s)
        x1, h2 = _post(x, ya, jnp.concatenate([yb_p, yb_s], axis=0), jnp.concatenate([yc_p, yc_s], axis=0),
                       mod, row(g_post_mix), row(g_pre_ffn), w_glu[l].astype(BF16), row(b_glu),
                       w_out[l].astype(BF16), row_map)
        wr = w_router[l].astype(F32).T
        wr_hi = wr.astype(BF16)
        moe_p = {
            'wr_hi': wr_hi, 'wr_lo': (wr - wr_hi.astype(F32)).astype(BF16),
            'b_router': b_router[l].astype(F32)[:, None],
            'w_e_gate': w_e_gate[l].astype(BF16), 'w_e_up': w_e_up[l].astype(BF16),
            'w_e_down': w_e_down[l].astype(BF16),
            'w_s_gate': w_s_gate[l].astype(BF16), 'w_s_up': w_s_up[l].astype(BF16),
            'w_s_down': w_s_down[l].astype(BF16), 'g_post_ffn': row(g_post_ffn),
        }
        x = _moe(h2, x1, mod, moe_p, row_map)
        new_k.append(k32[:t_p].reshape(n_pb, p_seq, n_heads, HD_B))
        new_v.append(v32[:t_p].reshape(n_pb, p_seq, n_heads, HD_B))
        fin_p = fin_p.reshape(4, n_pb, n_g, SSM_P)
        new_re.append(jnp.stack([fin_p[0], fin_p[2]], axis=1))
        new_im.append(jnp.stack([fin_p[1], fin_p[3]], axis=1))
    return (x[:t_p].reshape(n_pb, p_seq, d), x[t_p:].reshape(n_sb, s_seq, d),
            jnp.stack(new_k, axis=1), jnp.stack(new_v, axis=1),
            jnp.stack(new_re, axis=1), jnp.stack(new_im, axis=1))
```

```python
import functools
import math

import jax
import jax.numpy as jnp
from jax import lax
from jax.experimental import pallas as pl
from jax.experimental.pallas import tpu as pltpu

F32 = jnp.float32
BF16 = jnp.bfloat16
I32 = jnp.int32

GRID_W = 64
EPS = 1e-6
NH_A = 4
CHUNK = 128
HD_B = 64
NA_WIN_R = 8
NA_WIN_C = 16
ATTN_SCALE = HD_B ** -0.5
SSM_CH = 16
SSM_P = 64
N_EXPERTS = 64
TOP_K = 8
N_EXP_GROUPS = 8
TOPK_GROUPS = 4
ROUTE_SCALE = 2.5

LANES = 128
TOK_TILE = 512
S5_CHUNK = 16
MOE_ROWS = 512
COMBINE_TILE = 128
DISPATCH_TILE = 512
DMA_THREADS = 2
NEG_BIG = -1e30
VMEM_LIMIT = 48 * 1024 * 1024


def _cparams(*sem):
    return pltpu.CompilerParams(dimension_semantics=sem, vmem_limit_bytes=VMEM_LIMIT)


def _dot(a, b):
    return jnp.dot(a, b, preferred_element_type=F32)


def _dot_nt(a, b):
    return lax.dot_general(a, b, (((1,), (1,)), ((), ())), preferred_element_type=F32)


def _rms(x, g):
    return x * lax.rsqrt(jnp.mean(x * x, axis=-1, keepdims=True) + EPS) * g


def _load_tiled(ref, n_rows):
    nc = ref.shape[0] // n_rows
    return jnp.concatenate([ref[pl.ds(j, n_rows, stride=nc), :] for j in range(nc)], axis=1)


def _store_tiled(ref, val):
    n_rows = val.shape[0]
    nc = ref.shape[0] // n_rows
    for j in range(nc):
        ref[pl.ds(j, n_rows, stride=nc), :] = val[:, j * LANES:(j + 1) * LANES]


def _ada_body(c_ref, w_ref, b_ref, o_ref):
    s = jax.nn.silu(c_ref[...]).astype(BF16)
    o_ref[0] = _dot(s, w_ref[0].astype(BF16)) + b_ref[0]


def _ada(c_all, w_ada, b_ada):
    n_layers, d, n = w_ada.shape
    rows = c_all.shape[0]
    tn = 1536
    return pl.pallas_call(
        _ada_body,
        name="ada",
        grid=(n_layers, n // tn),
        in_specs=[pl.BlockSpec((rows, d), lambda l, j: (0, 0)),
                  pl.BlockSpec((1, d, tn), lambda l, j: (l, 0, j)),
                  pl.BlockSpec((1, 1, tn), lambda l, j: (l, 0, j))],
        out_specs=pl.BlockSpec((1, rows, tn), lambda l, j: (l, 0, j)),
        out_shape=jax.ShapeDtypeStruct((n_layers, rows, n), F32),
        compiler_params=_cparams("arbitrary", "arbitrary"),
    )(c_all, w_ada, b_ada.reshape(n_layers, 1, n))


def _premix_body(x_ref, mod_ref, g_ref, w_ref, pa_ref, q_ref, k_ref, v_ref, kb_ref, vb_ref, pc_ref):
    x = x_ref[...]
    d = x.shape[1]
    h = _rms(x, g_ref[...]) * (1 + mod_ref[0, :, d:2 * d]) + mod_ref[0, :, 0:d]
    p = _dot(h.astype(BF16), w_ref[...])
    d_a2 = pa_ref.shape[1]
    d_b = q_ref.shape[1]
    o = d_a2
    pa_ref[...] = p[:, 0:o]
    q_ref[...] = p[:, o:o + d_b].astype(BF16)
    k = p[:, o + d_b:o + 2 * d_b]
    v = p[:, o + 2 * d_b:o + 3 * d_b]
    k_ref[...] = k
    v_ref[...] = v
    kb_ref[...] = k.astype(BF16)
    vb_ref[...] = v.astype(BF16)
    pc_ref[...] = p[:, o + 3 * d_b:]


def _premix(x, mod, g, w_in_b, row_map, d_a, d_b, d_c):
    t, d = x.shape
    tm = TOK_TILE
    d_in = w_in_b.shape[1]
    tok = lambda n: pl.BlockSpec((tm, n), lambda i: (i, 0))
    return pl.pallas_call(
        _premix_body,
        name="premix",
        grid=(t // tm,),
        in_specs=[tok(d),
                  pl.BlockSpec((1, 1, mod.shape[2]), lambda i: (row_map(i), 0, 0)),
                  pl.BlockSpec((1, d), lambda i: (0, 0)),
                  pl.BlockSpec((d, d_in), lambda i: (0, 0))],
        out_specs=[tok(2 * d_a), tok(d_b), tok(d_b), tok(d_b), tok(d_b), tok(d_b), tok(d_c)],
        out_shape=[jax.ShapeDtypeStruct((t, 2 * d_a), F32),
                   jax.ShapeDtypeStruct((t, d_b), BF16),
                   jax.ShapeDtypeStruct((t, d_b), F32),
                   jax.ShapeDtypeStruct((t, d_b), F32),
                   jax.ShapeDtypeStruct((t, d_b), BF16),
                   jax.ShapeDtypeStruct((t, d_b), BF16),
                   jax.ShapeDtypeStruct((t, d_c), F32)],
        compiler_params=_cparams("arbitrary"),
    )(x, mod, g, w_in_b)


def _chunk_body(pa_ref, g_ref, w_ref, b_ref, o_ref):
    z = jax.nn.gelu(pa_ref[...])
    d_a = o_ref.shape[1]
    hd = d_a // NH_A
    u = z[:, :d_a]
    v = z[:, d_a:]
    mu = jnp.mean(v, axis=-1, keepdims=True)
    var = jnp.mean(jnp.square(v - mu), axis=-1, keepdims=True)
    vb = ((v - mu) * lax.rsqrt(var + EPS) * g_ref[...]).astype(BF16)
    head = lax.broadcasted_iota(I32, (CHUNK, d_a), 1) // hd
    for ch in range(pa_ref.shape[0] // CHUNK):
        rows = slice(ch * CHUNK, (ch + 1) * CHUNK)
        sf = _dot(w_ref[...], vb[rows])
        s = b_ref[...]
        for h in range(NH_A):
            s = s + jnp.where(head == h, sf[h * CHUNK:(h + 1) * CHUNK], 0.0)
        o_ref[rows, :] = (u[rows] * s).astype(BF16)


def _chunk_mlp(pa, sgu_g, w_sp_b, bias):
    t, d2 = pa.shape
    d_a = d2 // 2
    tm = TOK_TILE
    return pl.pallas_call(
        _chunk_body,
        name="chunk_mlp",
        grid=(t // tm,),
        in_specs=[pl.BlockSpec((tm, d2), lambda i: (i, 0)),
                  pl.BlockSpec((1, d_a), lambda i: (0, 0)),
                  pl.BlockSpec(w_sp_b.shape, lambda i: (0, 0)),
                  pl.BlockSpec(bias.shape, lambda i: (0, 0))],
        out_specs=pl.BlockSpec((tm, d_a), lambda i: (i, 0)),
        out_shape=jax.ShapeDtypeStruct((t, d_a), BF16),
        compiler_params=_cparams("arbitrary"),
    )(pa, sgu_g, w_sp_b, bias)


def _stack_pair(qg):
    lane = lax.broadcasted_iota(I32, qg.shape, 1)
    zero = jnp.zeros_like(qg)
    return jnp.concatenate([jnp.where(lane < HD_B, qg, zero), jnp.where(lane >= HD_B, qg, zero)], axis=0)


def _unstack_pair(o2):
    n = o2.shape[0] // 2
    lane = lax.broadcasted_iota(I32, (n, o2.shape[1]), 1)
    return jnp.where(lane < HD_B, o2[:n], o2[n:])


def _ctx_attn_body(q_ref, k_ref, v_ref, o_ref):
    for g in range(q_ref.shape[1] // LANES):
        cols = slice(g * LANES, (g + 1) * LANES)
        q2 = _stack_pair(q_ref[:, cols])
        s = _dot_nt(q2, k_ref[:, cols].astype(BF16)) * ATTN_SCALE
        e = jnp.exp(s - jnp.max(s, axis=-1, keepdims=True))
        p = e / jnp.sum(e, axis=-1, keepdims=True)
        o2 = _dot(p.astype(BF16), v_ref[:, cols].astype(BF16))
        o_ref[:, cols] = _unstack_pair(o2).astype(BF16)


def _ctx_attn(q, k, v, n_batch, seq):
    d_b = q.shape[1]
    blk = pl.BlockSpec((seq, d_b), lambda b: (b, 0))
    return pl.pallas_call(
        _ctx_attn_body,
        name="ctx_attn",
        grid=(n_batch,),
        in_specs=[blk, blk, blk],
        out_specs=blk,
        out_shape=jax.ShapeDtypeStruct((n_batch * seq, d_b), BF16),
        compiler_params=_cparams("arbitrary"),
    )(q, k, v)


def _nat_body(q_ref, k_ref, v_ref, ck_ref, cv_ref, bias_ref, o_ref, *, rows):
    r = pl.program_id(1)
    start = pl.multiple_of(jnp.clip(r - NA_WIN_R // 2, 0, rows - NA_WIN_R) * GRID_W, GRID_W)
    n_win = NA_WIN_R * GRID_W
    for g in range(q_ref.shape[1] // LANES):
        cols = slice(g * LANES, (g + 1) * LANES)
        q2 = _stack_pair(q_ref[:, cols])
        s_win = _dot_nt(q2, k_ref[pl.ds(start, n_win), cols]) * ATTN_SCALE + bias_ref[0, g]
        s_ctx = _dot_nt(q2, ck_ref[0, :, cols]) * ATTN_SCALE
        m = jnp.maximum(jnp.max(s_win, axis=-1, keepdims=True), jnp.max(s_ctx, axis=-1, keepdims=True))
        e_win = jnp.exp(s_win - m)
        e_ctx = jnp.exp(s_ctx - m)
        inv = 1.0 / (jnp.sum(e_win, axis=-1, keepdims=True) + jnp.sum(e_ctx, axis=-1, keepdims=True))
        o2 = (_dot((e_win * inv).astype(BF16), v_ref[pl.ds(start, n_win), cols])
              + _dot((e_ctx * inv).astype(BF16), cv_ref[0, :, cols]))
        o_ref[:, cols] = _unstack_pair(o2).astype(BF16)


def _nat_attn(q, kb, vb, ck, cv, bias, n_batch, seq, tok0):
    d_b = q.shape[1]
    rows = seq // GRID_W
    lc = ck.shape[1]
    q0 = tok0 // GRID_W
    i0 = tok0 // seq

    def case(r):
        return r - jnp.clip(r - NA_WIN_R // 2, 0, rows - NA_WIN_R)

    img = pl.BlockSpec((seq, d_b), lambda b, r: (i0 + b, 0))
    ctx = pl.BlockSpec((1, lc, d_b), lambda b, r: (b, 0, 0))
    return pl.pallas_call(
        functools.partial(_nat_body, rows=rows),
        name="nat_attn",
        grid=(n_batch, rows),
        in_specs=[pl.BlockSpec((GRID_W, d_b), lambda b, r: (q0 + b * rows + r, 0)),
                  img, img, ctx, ctx,
                  pl.BlockSpec((1,) + bias.shape[1:], lambda b, r: (case(r), 0, 0, 0))],
        out_specs=pl.BlockSpec((GRID_W, d_b), lambda b, r: (b * rows + r, 0)),
        out_shape=jax.ShapeDtypeStruct((n_batch * seq, d_b), BF16),
        compiler_params=_cparams("arbitrary", "arbitrary"),
    )(q, kb, vb, ck, cv, bias)


def _nat_bias(rpb):
    n_heads = rpb.shape[0]
    cols = jnp.arange(GRID_W)
    col_start = jnp.clip(cols - NA_WIN_C // 2, 0, GRID_W - NA_WIN_C)
    j = jnp.arange(GRID_W)
    valid = (j[None, :] >= col_start[:, None]) & (j[None, :] < col_start[:, None] + NA_WIN_C)
    col_off = jnp.clip(j[None, :] - cols[:, None] + (NA_WIN_C - 1), 0, 2 * NA_WIN_C - 2)
    toe = jnp.where(valid[None, None], rpb.astype(F32)[:, :, col_off], NEG_BIG)
    cases = jnp.stack([toe[:, NA_WIN_R - 1 - delta:2 * NA_WIN_R - 1 - delta] for delta in range(NA_WIN_R)])
    return cases.transpose(0, 1, 3, 2, 4).reshape(NA_WIN_R, n_heads // 2, 2 * GRID_W, NA_WIN_R * GRID_W)


def _s5_weights(a_re, a_im, log_dt, b_re, b_im, c_re, c_im, ssm_d):
    n_g = a_re.shape[1]
    c = S5_CHUNK
    lam = lax.complex(a_re.astype(F32), a_im.astype(F32))
    ldt = lam * jnp.exp(log_dt.astype(F32))[..., None]
    lam_bar = jnp.exp(ldt)
    b_bar = ((lam_bar - 1) / lam)[..., None] * lax.complex(b_re.astype(F32), b_im.astype(F32))
    c_mat = lax.complex(c_re.astype(F32), c_im.astype(F32))
    pw = jnp.exp(ldt[None] * jnp.arange(c + 1, dtype=F32)[:, None, None, None])
    kern = jnp.real(jnp.einsum('dgcp,kdgp,dgpe->dgkce', c_mat, pw[:c], b_bar))
    i = jnp.arange(c)
    lag = i[None, :] - i[:, None]
    tf = jnp.where((lag >= 0)[None, :, :, None, None], kern[0][:, jnp.clip(lag, 0, c - 1)], 0.0)
    tb = jnp.where((lag <= 0)[None, :, :, None, None], kern[1][:, jnp.clip(-lag, 0, c - 1)], 0.0)
    t_mat = (tf + tb).transpose(0, 1, 4, 2, 3).reshape(n_g, c * SSM_CH, c * SSM_CH)
    mf = pw[:c][::-1, 0][:, :, :, None] * b_bar[0][None]
    mb = pw[:c, 1][:, :, :, None] * b_bar[1][None]
    mf = mf.transpose(1, 0, 3, 2).reshape(n_g, c * SSM_CH, SSM_P)
    mb = mb.transpose(1, 0, 3, 2).reshape(n_g, c * SSM_CH, SSM_P)
    mq = jnp.stack([jnp.real(mf), jnp.imag(mf), jnp.real(mb), jnp.imag(mb)], axis=2)
    zf = c_mat[0][:, None] * pw[1:c + 1, 0][:, :, None, :].transpose(1, 0, 2, 3)
    zb = c_mat[1][:, None] * pw[1:c + 1, 1][::-1][:, :, None, :].transpose(1, 0, 2, 3)
    zf = zf.transpose(0, 3, 1, 2).reshape(n_g, SSM_P, c * SSM_CH)
    zb = zb.transpose(0, 3, 1, 2).reshape(n_g, SSM_P, c * SSM_CH)
    wq = jnp.stack([jnp.real(zf), -jnp.imag(zf), jnp.real(zb), -jnp.imag(zb)], axis=1)
    n_pr = n_g // 2
    eye2 = jnp.eye(2, dtype=F32)
    mq = mq.reshape(n_pr, 2, c * SSM_CH, 4, SSM_P)
    m_pair = jnp.einsum('rsiqp,st->rsiqtp', mq, eye2).reshape(n_pr, 2 * c * SSM_CH, 4 * 2 * SSM_P)
    t_pair = jnp.einsum('rsij,st->rsitj', t_mat.reshape(n_pr, 2, c * SSM_CH, c * SSM_CH), eye2)
    t_pair = t_pair.reshape(n_pr, 2 * c * SSM_CH, 2 * c * SSM_CH)
    wq = wq.reshape(n_pr, 2, 4, SSM_P, c * SSM_CH)
    w_pair = jnp.einsum('rsqpj,st->rqsptj', wq, eye2).reshape(n_pr, 4 * 2 * SSM_P, 2 * c * SSM_CH)
    tw_pair = jnp.concatenate([t_pair, w_pair], axis=1).astype(BF16)
    a_c = pw[c]
    a16 = jnp.stack([jnp.real(a_c[0]), jnp.imag(a_c[0]), jnp.real(a_c[1]), jnp.imag(a_c[1])])
    a16 = a16.reshape(4, 1, n_g * SSM_P)
    dvec = jnp.tile(ssm_d.astype(F32).reshape(n_g, 1, SSM_CH), (1, c, 1)).reshape(n_pr, 1, 2 * c * SSM_CH)
    return m_pair.astype(BF16), tw_pair, a16, dvec


def _s5_state_body(u_ref, m_ref, fre_ref, fim_ref, bre_ref, bim_ref):
    r = _dot(u_ref[...].astype(BF16), m_ref[0])
    fre_ref[...] = r[:, 0 * LANES:1 * LANES]
    fim_ref[...] = r[:, 1 * LANES:2 * LANES]
    bre_ref[...] = r[:, 2 * LANES:3 * LANES]
    bim_ref[...] = r[:, 3 * LANES:4 * LANES]


def _s5_states(u2, m_pair):
    rows, width = u2.shape
    n_pr, kdim, _ = m_pair.shape
    tr = min(rows, 256)
    out = pl.BlockSpec((tr, LANES), lambda i, p: (i, p))
    return pl.pallas_call(
        _s5_state_body,
        name="s5_states",
        grid=(rows // tr, n_pr),
        in_specs=[pl.BlockSpec((tr, kdim), lambda i, p: (i, p)),
                  pl.BlockSpec((1, kdim, 4 * LANES), lambda i, p: (p, 0, 0))],
        out_specs=[out] * 4,
        out_shape=[jax.ShapeDtypeStruct((rows, n_pr * LANES), F32)] * 4,
        compiler_params=_cparams("arbitrary", "arbitrary"),
    )(u2, m_pair)


def _s5_scan_body(sfr_ref, sfi_ref, sbr_ref, sbi_ref, a_ref, h0_ref,
                  hfr_ref, hfi_ref, hbr_ref, hbi_ref, fin_ref, st_ref):
    @pl.when(pl.program_id(0) == 0)
    def _():
        st_ref[...] = h0_ref[...]

    ks = sfr_ref.shape[0]
    afr, afi, abr, abi = a_ref[0], a_ref[1], a_ref[2], a_ref[3]

    def step(s, carry):
        fr, fi, br, bi = carry
        sb = ks - 1 - s
        hfr_ref[s] = fr
        hfi_ref[s] = fi
        hbr_ref[sb] = br
        hbi_ref[sb] = bi
        nfr = afr * fr - afi * fi + sfr_ref[s]
        nfi = afr * fi + afi * fr + sfi_ref[s]
        nbr = abr * br - abi * bi + sbr_ref[sb]
        nbi = abr * bi + abi * br + sbi_ref[sb]
        return nfr, nfi, nbr, nbi

    carry = lax.fori_loop(0, ks, step, (st_ref[0], st_ref[1], st_ref[2], st_ref[3]))
    for q in range(4):
        st_ref[q] = carry[q]
        fin_ref[q] = carry[q]


def _s5_scan(s4, a16, h0, n_chunks, n_batch):
    width = s4[0].shape[1]
    ks = min(n_chunks, 32)
    nb = n_chunks // ks
    s3 = [s.reshape(n_chunks, n_batch, width) for s in s4]
    fwd = pl.BlockSpec((ks, n_batch, width), lambda i: (i, 0, 0))
    bwd = pl.BlockSpec((ks, n_batch, width), lambda i: (nb - 1 - i, 0, 0))
    small = lambda shape: pl.BlockSpec(shape, lambda i: (0, 0, 0))
    outs = pl.pallas_call(
        _s5_scan_body,
        name="s5_scan",
        grid=(nb,),
        in_specs=[fwd, fwd, bwd, bwd, small(a16.shape), small(h0.shape)],
        out_specs=[fwd, fwd, bwd, bwd, small(h0.shape)],
        out_shape=[jax.ShapeDtypeStruct((n_chunks, n_batch, width), F32)] * 4
        + [jax.ShapeDtypeStruct(h0.shape, F32)],
        scratch_shapes=[pltpu.VMEM(h0.shape, F32)],
        compiler_params=_cparams("arbitrary"),
    )(*s3, a16, h0)
    return [o.reshape(n_chunks * n_batch, width) for o in outs[:4]], outs[4]


def _s5_out_body(u_ref, hfr_ref, hfi_ref, hbr_ref, hbi_ref, tw_ref, d_ref, y_ref):
    u = u_ref[...]
    ku = u.shape[1]
    y = _dot(u.astype(BF16), tw_ref[0, 0:ku, :])
    for q, h_ref in enumerate((hfr_ref, hfi_ref, hbr_ref, hbi_ref)):
        y = y + _dot(h_ref[...].astype(BF16), tw_ref[0, ku + q * LANES:ku + (q + 1) * LANES, :])
    y_ref[...] = jax.nn.gelu(y + d_ref[0] * u)


def _s5_out(u2, h4, tw_pair, dvec):
    rows, width = u2.shape
    n_pr, kdim, ku = tw_pair.shape
    tr = min(rows, 256)
    hb = pl.BlockSpec((tr, LANES), lambda i, p: (i, p))
    return pl.pallas_call(
        _s5_out_body,
        name="s5_out",
        grid=(rows // tr, n_pr),
        in_specs=[pl.BlockSpec((tr, ku), lambda i, p: (i, p)), hb, hb, hb, hb,
                  pl.BlockSpec((1, kdim, ku), lambda i, p: (p, 0, 0)),
                  pl.BlockSpec((1, 1, ku), lambda i, p: (p, 0, 0))],
        out_specs=pl.BlockSpec((tr, ku), lambda i, p: (i, p)),
        out_shape=jax.ShapeDtypeStruct((rows, width), F32),
        compiler_params=_cparams("arbitrary", "arbitrary"),
    )(u2, *h4, tw_pair, dvec)


def _s5(pc, n_batch, seq, weights, h0):
    m_pair, tw_pair, a16, dvec = weights
    d_c = pc.shape[1]
    n_g = d_c // SSM_CH
    n_chunks = seq // S5_CHUNK
    u2 = pc.reshape(n_batch, n_chunks, S5_CHUNK, n_g, SSM_CH).transpose(1, 0, 3, 2, 4)
    u2 = u2.reshape(n_chunks * n_batch, n_g * S5_CHUNK * SSM_CH)
    s4 = _s5_states(u2, m_pair)
    h4, fin = _s5_scan(s4, a16, h0, n_chunks, n_batch)
    y2 = _s5_out(u2, h4, tw_pair, dvec)
    y = y2.reshape(n_chunks, n_batch, n_g, S5_CHUNK, SSM_CH).transpose(1, 0, 3, 2, 4)
    return y.reshape(n_batch * seq, d_c), fin


def _post_body(x_ref, ya_ref, yb_ref, yc_ref, mod_ref, gpost_ref, gffn_ref, wglu_ref, bglu_ref, wo_ref,
               x1_ref, h2_ref):
    x = x_ref[...]
    d = x.shape[1]
    d_a = ya_ref.shape[1]
    d_b = yb_ref.shape[1]
    y = yc_ref[...]
    glu = y * jax.nn.sigmoid(_dot(y.astype(BF16), wglu_ref[...]) + bglu_ref[...])
    mixed = (_dot(ya_ref[...], wo_ref[0:d_a, :]) + _dot(yb_ref[...], wo_ref[d_a:d_a + d_b, :])
             + _dot(glu.astype(BF16), wo_ref[d_a + d_b:, :]))
    x1 = x + mod_ref[0, :, 2 * d:3 * d] * _rms(mixed, gpost_ref[...])
    x1_ref[...] = x1
    h2 = _rms(x1, gffn_ref[...]) * (1 + mod_ref[0, :, 4 * d:5 * d]) + mod_ref[0, :, 3 * d:4 * d]
    _store_tiled(h2_ref, h2)


def _post(x, ya, yb, yc, mod, g_post, g_ffn, w_glu_b, b_glu, w_out_b, row_map):
    t, d = x.shape
    tm = TOK_TILE
    nc = d // LANES
    tok = lambda n: pl.BlockSpec((tm, n), lambda i: (i, 0))
    full = lambda a: pl.BlockSpec(a.shape, lambda i: (0,) * a.ndim)
    return pl.pallas_call(
        _post_body,
        name="post_mix",
        grid=(t // tm,),
        in_specs=[tok(d), tok(ya.shape[1]), tok(yb.shape[1]), tok(yc.shape[1]),
                  pl.BlockSpec((1, 1, mod.shape[2]), lambda i: (row_map(i), 0, 0)),
                  full(g_post), full(g_ffn), full(w_glu_b), full(b_glu), full(w_out_b)],
        out_specs=[tok(d), pl.BlockSpec((tm * nc, LANES), lambda i: (i, 0))],
        out_shape=[jax.ShapeDtypeStruct((t, d), F32), jax.ShapeDtypeStruct((t * nc, LANES), F32)],
        compiler_params=_cparams("arbitrary"),
    )(x, ya, yb, yc, mod, g_post, g_ffn, w_glu_b, b_glu, w_out_b)


def _route_body(h_ref, whi_ref, wlo_ref, b_ref, idx_ref, gate_ref, pos_ref, cnt_ref, carry_ref):
    @pl.when(pl.program_id(0) == 0)
    def _():
        carry_ref[...] = jnp.zeros_like(carry_ref)

    tm = idx_ref.shape[1]
    h = _load_tiled(h_ref, tm)
    hi = h.astype(BF16)
    lo = (h - hi.astype(F32)).astype(BF16)
    logits = _dot_nt(whi_ref[...], hi) + (_dot_nt(whi_ref[...], lo) + _dot_nt(wlo_ref[...], hi))
    scores = jax.nn.sigmoid(logits)
    sel = scores + b_ref[...]
    gsz = N_EXPERTS // N_EXP_GROUPS
    within = lax.broadcasted_iota(I32, (gsz, tm), 0).astype(F32)
    grp = []
    for g in range(N_EXP_GROUPS):
        blk = sel[g * gsz:(g + 1) * gsz]
        m1 = jnp.max(blk, axis=0, keepdims=True)
        first = jnp.min(jnp.where(blk == m1, within, float(gsz)), axis=0, keepdims=True)
        m2 = jnp.max(jnp.where(within == first, -jnp.inf, blk), axis=0, keepdims=True)
        grp.append(m1 + m2)
    blocks = []
    for g in range(N_EXP_GROUPS):
        ahead = jnp.zeros((1, tm), F32)
        for o in range(N_EXP_GROUPS):
            if o == g:
                continue
            beats = (grp[o] >= grp[g]) if o < g else (grp[o] > grp[g])
            ahead = ahead + jnp.where(beats, 1.0, 0.0)
        ahead = jnp.broadcast_to(ahead, (gsz, tm))
        blocks.append(jnp.where(ahead < TOPK_GROUPS, sel[g * gsz:(g + 1) * gsz], -jnp.inf))
    v = jnp.concatenate(blocks, axis=0)
    eidx = lax.broadcasted_iota(I32, (N_EXPERTS, tm), 0)
    rnk = jnp.zeros((N_EXPERTS, tm), F32)
    for e in range(N_EXPERTS):
        row = v[e:e + 1]
        rnk = rnk + jnp.where(eidx > e, jnp.where(row >= v, 1.0, 0.0), jnp.where(row > v, 1.0, 0.0))
    chosen = rnk < TOP_K
    w = jnp.where(chosen, scores, 0.0)
    wn = w / jnp.sum(w, axis=0, keepdims=True) * ROUTE_SCALE
    tri = (lax.broadcasted_iota(I32, (tm, tm), 0) < lax.broadcasted_iota(I32, (tm, tm), 1))
    chosen_f = jnp.where(chosen, 1.0, 0.0)
    prefix = _dot(chosen_f.astype(BF16), jnp.where(tri, 1.0, 0.0).astype(BF16)) + carry_ref[:, 0:1]
    total = carry_ref[...] + jnp.sum(chosen_f, axis=1, keepdims=True)
    carry_ref[...] = total
    cnt_ref[...] = total
    eidx_f = eidx.astype(F32)
    for k in range(TOP_K):
        one = rnk == k
        idx_ref[k:k + 1, :] = jnp.sum(jnp.where(one, eidx_f, 0.0), axis=0, keepdims=True).astype(I32)
        gate_ref[k:k + 1, :] = jnp.sum(jnp.where(one, wn, 0.0), axis=0, keepdims=True)
        pos_ref[k:k + 1, :] = jnp.sum(jnp.where(one, prefix, 0.0), axis=0, keepdims=True).astype(I32)


def _route(h2t, w_hi, w_lo, b_router):
    nc = w_hi.shape[1] // LANES
    t = h2t.shape[0] // nc
    tm = TOK_TILE
    out = pl.BlockSpec((TOP_K, tm), lambda i: (0, i))
    full = lambda a: pl.BlockSpec(a.shape, lambda i: (0, 0))
    return pl.pallas_call(
        _route_body,
        name="route",
        grid=(t // tm,),
        in_specs=[pl.BlockSpec((tm * nc, LANES), lambda i: (i, 0)), full(w_hi), full(w_lo), full(b_router)],
        out_specs=[out, out, out, pl.BlockSpec((N_EXPERTS, LANES), lambda i: (0, 0))],
        out_shape=[jax.ShapeDtypeStruct((TOP_K, t), I32), jax.ShapeDtypeStruct((TOP_K, t), F32),
                   jax.ShapeDtypeStruct((TOP_K, t), I32), jax.ShapeDtypeStruct((N_EXPERTS, LANES), F32)],
        scratch_shapes=[pltpu.VMEM((N_EXPERTS, LANES), F32)],
        compiler_params=_cparams("arbitrary"),
    )(h2t, w_hi, w_lo, b_router)


def _dest_body(starts_ref, idx_ref, pos_ref, dest_ref):
    idx = idx_ref[...]
    acc = pos_ref[...]
    for e in range(N_EXPERTS):
        acc = acc + jnp.where(idx == e, starts_ref[e], 0)
    dest_ref[...] = acc


def _dest(starts, idx, pos):
    t = idx.shape[1]
    tile = math.gcd(t, 4096)
    blk = pl.BlockSpec((TOP_K, tile), lambda i, st: (0, i))
    return pl.pallas_call(
        _dest_body,
        name="moe_dest",
        grid_spec=pltpu.PrefetchScalarGridSpec(num_scalar_prefetch=1, grid=(t // tile,),
                                               in_specs=[blk, blk], out_specs=blk),
        out_shape=jax.ShapeDtypeStruct(idx.shape, I32),
        compiler_params=_cparams("arbitrary"),
    )(starts, idx, pos)


def _dispatch_body(pad_lo_ref, pad_hi_ref, dest_ref, h_ref, xs_ref, zero_ref, sem, zsem, *, n_pad_rows):
    tile = dest_ref.shape[1]

    def zero_copy(r):
        return pltpu.make_async_copy(zero_ref, xs_ref.at[r], zsem)

    @pl.when(pl.program_id(0) == 0)
    def _():
        zero_ref[...] = jnp.zeros_like(zero_ref)

        def per_expert(e, c):
            def one(r, c2):
                zero_copy(r).start()
                return c2
            return lax.fori_loop(pad_lo_ref[e], pad_hi_ref[e], one, c)

        lax.fori_loop(0, N_EXPERTS, per_expert, 0)

    def row_copy(t, k):
        return pltpu.make_async_copy(h_ref.at[t], xs_ref.at[dest_ref[k, t]], sem)

    def start(t, c):
        for k in range(TOP_K):
            row_copy(t, k).start(priority=k % DMA_THREADS)
        return c

    def wait(t, c):
        for k in range(TOP_K):
            row_copy(t, k).wait()
        return c

    lax.fori_loop(0, tile, start, 0)
    lax.fori_loop(0, tile, wait, 0)

    @pl.when(pl.program_id(0) == 0)
    def _():
        def one(r, c):
            zero_copy(0).wait()
            return c
        lax.fori_loop(0, n_pad_rows, one, 0)


def _dispatch(pad_lo, pad_hi, dest, h2t, n_rows):
    t, nc, _ = h2t.shape
    tile = DISPATCH_TILE
    grid_spec = pltpu.PrefetchScalarGridSpec(
        num_scalar_prefetch=2,
        grid=(t // tile,),
        in_specs=[pl.BlockSpec((TOP_K, tile), lambda i, lo, hi: (0, i), memory_space=pltpu.SMEM),
                  pl.BlockSpec((tile, nc, LANES), lambda i, lo, hi: (i, 0, 0))],
        out_specs=pl.BlockSpec(memory_space=pl.ANY),
        scratch_shapes=[pltpu.VMEM((nc, LANES), F32), pltpu.SemaphoreType.DMA(()),
                        pltpu.SemaphoreType.DMA(())],
    )
    return pl.pallas_call(
        functools.partial(_dispatch_body, n_pad_rows=n_rows - t * TOP_K),
        name="moe_dispatch",
        grid_spec=grid_spec,
        out_shape=jax.ShapeDtypeStruct((n_rows, nc, LANES), F32),
        compiler_params=_cparams("arbitrary"),
    )(pad_lo, pad_hi, dest, h2t)


def _experts_body(be_ref, x_ref, wg_ref, wu_ref, wd_ref, y_ref):
    del be_ref
    xb = _load_tiled(x_ref, MOE_ROWS).astype(BF16)
    hb = jax.nn.silu(_dot(xb, wg_ref[0])) * _dot(xb, wu_ref[0])
    _store_tiled(y_ref, _dot(hb.astype(BF16), wd_ref[0]))


def _experts(blk_exp, xs, wg, wu, wd):
    d, de = wg.shape[1], wg.shape[2]
    nc = d // LANES
    bm = MOE_ROWS
    tiles = pl.BlockSpec((bm * nc, LANES), lambda i, be: (i, 0))
    grid_spec = pltpu.PrefetchScalarGridSpec(
        num_scalar_prefetch=1,
        grid=(xs.shape[0] // (bm * nc),),
        in_specs=[tiles,
                  pl.BlockSpec((1, d, de), lambda i, be: (be[i], 0, 0)),
                  pl.BlockSpec((1, d, de), lambda i, be: (be[i], 0, 0)),
                  pl.BlockSpec((1, de, d), lambda i, be: (be[i], 0, 0))],
        out_specs=tiles,
    )
    return pl.pallas_call(
        _experts_body,
        name="moe_experts",
        grid_spec=grid_spec,
        out_shape=jax.ShapeDtypeStruct(xs.shape, F32),
        compiler_params=_cparams("arbitrary"),
    )(blk_exp, xs, wg, wu, wd)


def _combine_body(dest_ref, next_ref, gate_ref, h_ref, x_ref, mod_ref, g_ref, wsg_ref, wsu_ref, wsd_ref, ys_hbm,
                  o_ref, buf_ref, sem):
    tile, d = x_ref.shape
    nc = d // LANES
    i = pl.program_id(0)
    slot = i % 2

    def row_copy(d_ref, s, t, k):
        return pltpu.make_async_copy(ys_hbm.at[d_ref[k, t]],
                                     buf_ref.at[s, k, pl.ds(pl.multiple_of(t * nc, nc), nc)], sem.at[s])

    def gather(d_ref, s):
        def start(t, c):
            for k in range(TOP_K):
                row_copy(d_ref, s, t, k).start(priority=k % DMA_THREADS)
            return c
        lax.fori_loop(0, tile, start, 0)

    @pl.when(i == 0)
    def _():
        gather(dest_ref, 0)

    @pl.when(i + 1 < pl.num_programs(0))
    def _():
        gather(next_ref, 1 - slot)

    hb = _load_tiled(h_ref, tile).astype(BF16)
    acc = _dot((jax.nn.silu(_dot(hb, wsg_ref[...])) * _dot(hb, wsu_ref[...])).astype(BF16), wsd_ref[...])

    def wait(t, c):
        for k in range(TOP_K):
            row_copy(dest_ref, slot, 0, k).wait()
        return c

    lax.fori_loop(0, tile, wait, 0)
    gates = gate_ref[...]
    moe = gates[:, 0:1] * _load_tiled(buf_ref.at[slot, 0], tile)
    for k in range(1, TOP_K):
        moe = moe + gates[:, k:k + 1] * _load_tiled(buf_ref.at[slot, k], tile)
    o_ref[...] = x_ref[...] + mod_ref[0, :, 5 * d:6 * d] * _rms(moe + acc, g_ref[...])


def _combine(dest, gates, h2t, x1, mod, g_post, wsg, wsu, wsd, ys, row_map):
    t, d = x1.shape
    nc = d // LANES
    tile = COMBINE_TILE
    per_tok = TOK_TILE // tile
    n_steps = t // tile
    tok = lambda n: pl.BlockSpec((tile, n), lambda i: (i, 0))
    full = lambda a: pl.BlockSpec(a.shape, lambda i: (0,) * a.ndim)
    return pl.pallas_call(
        _combine_body,
        name="moe_combine",
        grid=(n_steps,),
        in_specs=[pl.BlockSpec((TOP_K, tile), lambda i: (0, i), memory_space=pltpu.SMEM),
                  pl.BlockSpec((TOP_K, tile), lambda i: (0, jnp.minimum(i + 1, n_steps - 1)),
                               memory_space=pltpu.SMEM),
                  tok(TOP_K), pl.BlockSpec((tile * nc, LANES), lambda i: (i, 0)), tok(d),
                  pl.BlockSpec((1, 1, mod.shape[2]), lambda i: (row_map(i // per_tok), 0, 0)),
                  full(g_post), full(wsg), full(wsu), full(wsd),
                  pl.BlockSpec(memory_space=pl.ANY)],
        out_specs=tok(d),
        out_shape=jax.ShapeDtypeStruct((t, d), F32),
        scratch_shapes=[pltpu.VMEM((2, TOP_K, tile * nc, LANES), F32), pltpu.SemaphoreType.DMA((2,))],
        compiler_params=_cparams("arbitrary"),
    )(dest, dest, gates, h2t, x1, mod, g_post, wsg, wsu, wsd, ys)


def _moe(h2t, x1, mod, p, row_map):
    t, d = x1.shape
    nc = d // LANES
    idx, gate, pos, cnt = _route(h2t, p['wr_hi'], p['wr_lo'], p['b_router'])
    counts = cnt[:, 0].astype(I32)
    padded = (counts + MOE_ROWS - 1) // MOE_ROWS * MOE_ROWS
    ends = jnp.cumsum(padded)
    starts = ends - padded
    n_blocks = (t * TOP_K + N_EXPERTS * (MOE_ROWS - 1)) // MOE_ROWS + 1
    n_rows = n_blocks * MOE_ROWS
    blk_start = jnp.arange(n_blocks, dtype=I32) * MOE_ROWS
    blk_exp = jnp.minimum(jnp.sum((ends[None, :] <= blk_start[:, None]).astype(I32), axis=1), N_EXPERTS - 1)
    pad_hi = ends.at[N_EXPERTS - 1].set(n_rows)
    dest = _dest(starts, idx, pos)
    xs = _dispatch(starts + counts, pad_hi, dest, h2t.reshape(t, nc, LANES), n_rows)
    ys = _experts(blk_exp, xs.reshape(n_rows * nc, LANES), p['w_e_gate'], p['w_e_up'], p['w_e_down'])
    return _combine(dest, gate.T, h2t, x1, mod, p['g_post_ffn'], p['w_s_gate'], p['w_s_up'], p['w_s_down'],
                    ys.reshape(n_rows, nc, LANES), row_map)


def kernel(x_prompt, x_sample, cache_k, cache_v, state_ssm_re, state_ssm_im, c, c_ctx,
           g_pre_mix, g_post_mix, g_pre_ffn, g_post_ffn, w_ada, b_ada, w_in, w_out,
           sgu_g, w_sp, b_sp, rpb, ssm_a_re, ssm_a_im, ssm_log_dt, ssm_b_re, ssm_b_im,
           ssm_c_re, ssm_c_im, ssm_d, w_glu, b_glu, w_router, b_router,
           w_e_gate, w_e_up, w_e_down, w_s_gate, w_s_up, w_s_down):
    n_pb, p_seq, d = x_prompt.shape
    n_sb, s_seq, _ = x_sample.shape
    depth = w_in.shape[0]
    d_a = sgu_g.shape[1]
    d_c = w_glu.shape[1]
    d_b = d - d_a - d_c
    n_heads = d_b // HD_B
    n_g = d_c // SSM_CH
    t_p = n_pb * p_seq
    assert p_seq % CHUNK == 0 and t_p % TOK_TILE == 0 and s_seq % TOK_TILE == 0 and t_p % s_seq == 0
    assert s_seq % GRID_W == 0 and s_seq // GRID_W >= NA_WIN_R

    mod_rows = -(-(n_sb + 1) // 8) * 8
    c_all = jnp.zeros((mod_rows, d), F32).at[:n_sb].set(c).at[n_sb].set(c_ctx)
    mod_all = _ada(c_all, w_ada, b_ada)
    p_tiles = t_p // TOK_TILE
    s_tiles = s_seq // TOK_TILE

    def row_map(i):
        return jnp.where(i < p_tiles, n_sb, (i - p_tiles) // s_tiles)

    x = jnp.concatenate([x_prompt.reshape(t_p, d), x_sample.reshape(n_sb * s_seq, d)], axis=0)
    new_k, new_v, new_re, new_im = [], [], [], []
    for l in range(depth):
        mod = mod_all[l][:, None, :]
        row = lambda a: a[l][None, :].astype(F32)
        pa, q, k32, v32, kb, vb, pc = _premix(x, mod, row(g_pre_mix), w_in[l].astype(BF16), row_map,
                                              d_a, d_b, d_c)
        bias_a = jnp.repeat(b_sp[l].T.astype(F32), d_a // NH_A, axis=1)
        ya = _chunk_mlp(pa, row(sgu_g), w_sp[l].reshape(NH_A * CHUNK, CHUNK).astype(BF16), bias_a)
        yb_p = _ctx_attn(q, k32, v32, n_pb, p_seq)
        ck = cache_k[:, l].reshape(n_sb, -1, d_b).astype(BF16)
        cv = cache_v[:, l].reshape(n_sb, -1, d_b).astype(BF16)
        yb_s = _nat_attn(q, kb, vb, ck, cv, _nat_bias(rpb[l]), n_sb, s_seq, t_p)
        s5w = _s5_weights(ssm_a_re[l], ssm_a_im[l], ssm_log_dt[l], ssm_b_re[l], ssm_b_im[l],
                          ssm_c_re[l], ssm_c_im[l], ssm_d[l])
        yc_p, fin_p = _s5(pc[:t_p], n_pb, p_seq, s5w, jnp.zeros((4, n_pb, n_g * SSM_P), F32))
        sre = state_ssm_re[:, l].astype(F32).reshape(n_sb, 2, n_g * SSM_P)
        sim = state_ssm_im[:, l].astype(F32).reshape(n_sb, 2, n_g * SSM_P)
        h0_s = jnp.stack([sre[:, 0], sim[:, 0], sre[:, 1], sim[:, 1]])
        yc_s, _ = _s5(pc[t_p:], n_sb, s_seq, s5w, h0_s)
        x1, h2 = _post(x, ya, jnp.concatenate([yb_p, yb_s], axis=0), jnp.concatenate([yc_p, yc_s], axis=0),
                       mod, row(g_post_mix), row(g_pre_ffn), w_glu[l].astype(BF16), row(b_glu),
                       w_out[l].astype(BF16), row_map)
        wr = w_router[l].astype(F32).T
        wr_hi = wr.astype(BF16)
        moe_p = {
            'wr_hi': wr_hi, 'wr_lo': (wr - wr_hi.astype(F32)).astype(BF16),
            'b_router': b_router[l].astype(F32)[:, None],
            'w_e_gate': w_e_gate[l].astype(BF16), 'w_e_up': w_e_up[l].astype(BF16),
            'w_e_down': w_e_down[l].astype(BF16),
            'w_s_gate': w_s_gate[l].astype(BF16), 'w_s_up': w_s_up[l].astype(BF16),
            'w_s_down': w_s_down[l].astype(BF16), 'g_post_ffn': row(g_post_ffn),
        }
        x = _moe(h2, x1, mod, moe_p, row_map)
        new_k.append(k32[:t_p].reshape(n_pb, p_seq, n_heads, HD_B))
        new_v.append(v32[:t_p].reshape(n_pb, p_seq, n_heads, HD_B))
        fin_p = fin_p.reshape(4, n_pb, n_g, SSM_P)
        new_re.append(jnp.stack([fin_p[0], fin_p[2]], axis=1))
        new_im.append(jnp.stack([fin_p[1], fin_p[3]], axis=1))
    return (x[:t_p].reshape(n_pb, p_seq, d), x[t_p:].reshape(n_sb, s_seq, d),
            jnp.stack(new_k, axis=1), jnp.stack(new_v, axis=1),
            jnp.stack(new_re, axis=1), jnp.stack(new_im, axis=1))
```

```python
import functools
import math

import jax
import jax.numpy as jnp
from jax import lax
from jax.experimental import pallas as pl
from jax.experimental.pallas import tpu as pltpu

F32 = jnp.float32
BF16 = jnp.bfloat16
I32 = jnp.int32

GRID_W = 64
EPS = 1e-6
NH_A = 4
CHUNK = 128
HD_B = 64
NA_WIN_R = 8
NA_WIN_C = 16
ATTN_SCALE = HD_B ** -0.5
SSM_CH = 16
SSM_P = 64
N_EXPERTS = 64
TOP_K = 8
N_EXP_GROUPS = 8
TOPK_GROUPS = 4
ROUTE_SCALE = 2.5

LANES = 128
TOK_TILE = 512
S5_CHUNK = 16
S5_OUT_STEPS = 4
MOE_ROWS = 512
COMBINE_TILE = 128
DISPATCH_TILE = 512
NAT_ROWS = 4
DMA_THREADS = 2
NEG_BIG = -1e30
VMEM_LIMIT = 48 * 1024 * 1024


def _cparams(*sem):
    return pltpu.CompilerParams(dimension_semantics=sem, vmem_limit_bytes=VMEM_LIMIT)


def _dot(a, b):
    return jnp.dot(a, b, preferred_element_type=F32)


def _dot_nt(a, b):
    return lax.dot_general(a, b, (((1,), (1,)), ((), ())), preferred_element_type=F32)


def _rms(x, g):
    return x * lax.rsqrt(jnp.mean(x * x, axis=-1, keepdims=True) + EPS) * g


def _load_tiled(ref, n_rows):
    nc = ref.shape[0] // n_rows
    return jnp.concatenate([ref[pl.ds(j, n_rows, stride=nc), :] for j in range(nc)], axis=1)


def _store_tiled(ref, val):
    n_rows = val.shape[0]
    nc = ref.shape[0] // n_rows
    for j in range(nc):
        ref[pl.ds(j, n_rows, stride=nc), :] = val[:, j * LANES:(j + 1) * LANES]


def _ada_body(c_ref, w_ref, b_ref, o_ref):
    s = jax.nn.silu(c_ref[...]).astype(BF16)
    o_ref[0] = _dot(s, w_ref[0].astype(BF16)) + b_ref[0]


def _ada(c_all, w_ada, b_ada):
    n_layers, d, n = w_ada.shape
    rows = c_all.shape[0]
    tn = 1536
    return pl.pallas_call(
        _ada_body,
        name="ada",
        grid=(n_layers, n // tn),
        in_specs=[pl.BlockSpec((rows, d), lambda l, j: (0, 0)),
                  pl.BlockSpec((1, d, tn), lambda l, j: (l, 0, j)),
                  pl.BlockSpec((1, 1, tn), lambda l, j: (l, 0, j))],
        out_specs=pl.BlockSpec((1, rows, tn), lambda l, j: (l, 0, j)),
        out_shape=jax.ShapeDtypeStruct((n_layers, rows, n), F32),
        compiler_params=_cparams("arbitrary", "arbitrary"),
    )(c_all, w_ada, b_ada.reshape(n_layers, 1, n))


def _premix_body(x_ref, mod_ref, g_ref, w_ref, pa_ref, q_ref, k_ref, v_ref, kb_ref, vb_ref, *pc_refs):
    x = x_ref[...]
    d = x.shape[1]
    h = _rms(x, g_ref[...]) * (1 + mod_ref[0, :, d:2 * d]) + mod_ref[0, :, 0:d]
    p = _dot(h.astype(BF16), w_ref[...])
    d_a2 = pa_ref.shape[1]
    d_b = q_ref.shape[1]
    o = d_a2
    pa_ref[...] = p[:, 0:o]
    q_ref[...] = p[:, o:o + d_b].astype(BF16)
    k = p[:, o + d_b:o + 2 * d_b]
    v = p[:, o + 2 * d_b:o + 3 * d_b]
    k_ref[...] = k
    v_ref[...] = v
    kb_ref[...] = k.astype(BF16)
    vb_ref[...] = v.astype(BF16)
    for h, pc_ref in enumerate(pc_refs):
        pc_ref[...] = p[:, o + 3 * d_b + h * LANES:o + 3 * d_b + (h + 1) * LANES]


def _premix(x, mod, g, w_in_b, row_map, d_a, d_b, d_c):
    t, d = x.shape
    tm = TOK_TILE
    d_in = w_in_b.shape[1]
    tok = lambda n: pl.BlockSpec((tm, n), lambda i: (i, 0))
    return pl.pallas_call(
        _premix_body,
        name="premix",
        grid=(t // tm,),
        in_specs=[tok(d),
                  pl.BlockSpec((1, 1, mod.shape[2]), lambda i: (row_map(i), 0, 0)),
                  pl.BlockSpec((1, d), lambda i: (0, 0)),
                  pl.BlockSpec((d, d_in), lambda i: (0, 0))],
        out_specs=[tok(2 * d_a), tok(d_b), tok(d_b), tok(d_b), tok(d_b), tok(d_b)] + [tok(LANES)] * (d_c // LANES),
        out_shape=[jax.ShapeDtypeStruct((t, 2 * d_a), F32),
                   jax.ShapeDtypeStruct((t, d_b), BF16),
                   jax.ShapeDtypeStruct((t, d_b), F32),
                   jax.ShapeDtypeStruct((t, d_b), F32),
                   jax.ShapeDtypeStruct((t, d_b), BF16),
                   jax.ShapeDtypeStruct((t, d_b), BF16)]
        + [jax.ShapeDtypeStruct((t, LANES), F32)] * (d_c // LANES),
        compiler_params=_cparams("arbitrary"),
    )(x, mod, g, w_in_b)


def _chunk_body(pa_ref, g_ref, w_ref, b_ref, o_ref):
    z = jax.nn.gelu(pa_ref[...])
    d_a = o_ref.shape[1]
    hd = d_a // NH_A
    u = z[:, :d_a]
    v = z[:, d_a:]
    mu = jnp.mean(v, axis=-1, keepdims=True)
    var = jnp.mean(jnp.square(v - mu), axis=-1, keepdims=True)
    vb = ((v - mu) * lax.rsqrt(var + EPS) * g_ref[...]).astype(BF16)
    head = lax.broadcasted_iota(I32, (CHUNK, d_a), 1) // hd
    for ch in range(pa_ref.shape[0] // CHUNK):
        rows = slice(ch * CHUNK, (ch + 1) * CHUNK)
        sf = _dot(w_ref[...], vb[rows])
        s = b_ref[...]
        for h in range(NH_A):
            s = s + jnp.where(head == h, sf[h * CHUNK:(h + 1) * CHUNK], 0.0)
        o_ref[rows, :] = (u[rows] * s).astype(BF16)


def _chunk_mlp(pa, sgu_g, w_sp_b, bias):
    t, d2 = pa.shape
    d_a = d2 // 2
    tm = TOK_TILE
    return pl.pallas_call(
        _chunk_body,
        name="chunk_mlp",
        grid=(t // tm,),
        in_specs=[pl.BlockSpec((tm, d2), lambda i: (i, 0)),
                  pl.BlockSpec((1, d_a), lambda i: (0, 0)),
                  pl.BlockSpec(w_sp_b.shape, lambda i: (0, 0)),
                  pl.BlockSpec(bias.shape, lambda i: (0, 0))],
        out_specs=pl.BlockSpec((tm, d_a), lambda i: (i, 0)),
        out_shape=jax.ShapeDtypeStruct((t, d_a), BF16),
        compiler_params=_cparams("arbitrary"),
    )(pa, sgu_g, w_sp_b, bias)


def _stack_pair(qg):
    lane = lax.broadcasted_iota(I32, qg.shape, 1)
    zero = jnp.zeros_like(qg)
    return jnp.concatenate([jnp.where(lane < HD_B, qg, zero), jnp.where(lane >= HD_B, qg, zero)], axis=0)


def _unstack_pair(o2):
    n = o2.shape[0] // 2
    lane = lax.broadcasted_iota(I32, (n, o2.shape[1]), 1)
    return jnp.where(lane < HD_B, o2[:n], o2[n:])


def _ctx_attn_body(q_ref, k_ref, v_ref, o_ref):
    for g in range(q_ref.shape[1] // LANES):
        cols = slice(g * LANES, (g + 1) * LANES)
        q2 = _stack_pair(q_ref[:, cols])
        s = _dot_nt(q2, k_ref[:, cols].astype(BF16)) * ATTN_SCALE
        e = jnp.exp(s - jnp.max(s, axis=-1, keepdims=True))
        p = e / jnp.sum(e, axis=-1, keepdims=True)
        o2 = _dot(p.astype(BF16), v_ref[:, cols].astype(BF16))
        o_ref[:, cols] = _unstack_pair(o2).astype(BF16)


def _ctx_attn(q, k, v, n_batch, seq):
    d_b = q.shape[1]
    blk = pl.BlockSpec((seq, d_b), lambda b: (b, 0))
    return pl.pallas_call(
        _ctx_attn_body,
        name="ctx_attn",
        grid=(n_batch,),
        in_specs=[blk, blk, blk],
        out_specs=blk,
        out_shape=jax.ShapeDtypeStruct((n_batch * seq, d_b), BF16),
        compiler_params=_cparams("arbitrary"),
    )(q, k, v)


def _nat_body(q_ref, k_ref, v_ref, ck_ref, cv_ref, bias_ref, o_ref, *, rows):
    r0 = pl.program_id(1) * NAT_ROWS
    n_win = NA_WIN_R * GRID_W
    pr = 2 * GRID_W
    starts, cases = [], []
    for i in range(NAT_ROWS):
        rs = jnp.clip(r0 + i - NA_WIN_R // 2, 0, rows - NA_WIN_R)
        starts.append(pl.multiple_of(rs * GRID_W, GRID_W))
        cases.append(r0 + i - rs)
    for g in range(q_ref.shape[1] // LANES):
        cols = slice(g * LANES, (g + 1) * LANES)
        q2 = jnp.concatenate([_stack_pair(q_ref[i * GRID_W:(i + 1) * GRID_W, cols]) for i in range(NAT_ROWS)],
                             axis=0)
        s_ctx = _dot_nt(q2, ck_ref[0, :, cols]) * ATTN_SCALE
        m_ctx = jnp.max(s_ctx, axis=-1, keepdims=True)
        e_wins, invs, ms = [], [], []
        for i in range(NAT_ROWS):
            s_win = (_dot_nt(q2[i * pr:(i + 1) * pr], k_ref[pl.ds(starts[i], n_win), cols]) * ATTN_SCALE
                     + bias_ref[cases[i], g])
            m = jnp.maximum(jnp.max(s_win, axis=-1, keepdims=True), m_ctx[i * pr:(i + 1) * pr])
            e_wins.append(jnp.exp(s_win - m))
            ms.append(m)
        e_ctx = jnp.exp(s_ctx - jnp.concatenate(ms, axis=0))
        l_ctx = jnp.sum(e_ctx, axis=-1, keepdims=True)
        for i in range(NAT_ROWS):
            invs.append(1.0 / (jnp.sum(e_wins[i], axis=-1, keepdims=True) + l_ctx[i * pr:(i + 1) * pr]))
        o_ctx = _dot((e_ctx * jnp.concatenate(invs, axis=0)).astype(BF16), cv_ref[0, :, cols])
        for i in range(NAT_ROWS):
            o2 = _dot((e_wins[i] * invs[i]).astype(BF16), v_ref[pl.ds(starts[i], n_win), cols])
            o_ref[i * GRID_W:(i + 1) * GRID_W, cols] = _unstack_pair(o2 + o_ctx[i * pr:(i + 1) * pr]).astype(BF16)


def _nat_attn(q, kb, vb, ck, cv, bias, n_batch, seq, tok0):
    d_b = q.shape[1]
    rows = seq // GRID_W
    lc = ck.shape[1]
    blk = NAT_ROWS * GRID_W
    steps = seq // blk
    q0 = tok0 // blk
    i0 = tok0 // seq
    img = pl.BlockSpec((seq, d_b), lambda b, r: (i0 + b, 0))
    ctx = pl.BlockSpec((1, lc, d_b), lambda b, r: (b, 0, 0))
    return pl.pallas_call(
        functools.partial(_nat_body, rows=rows),
        name="nat_attn",
        grid=(n_batch, steps),
        in_specs=[pl.BlockSpec((blk, d_b), lambda b, r: (q0 + b * steps + r, 0)),
                  img, img, ctx, ctx,
                  pl.BlockSpec(bias.shape, lambda b, r: (0, 0, 0, 0))],
        out_specs=pl.BlockSpec((blk, d_b), lambda b, r: (b * steps + r, 0)),
        out_shape=jax.ShapeDtypeStruct((n_batch * seq, d_b), BF16),
        compiler_params=_cparams("arbitrary", "arbitrary"),
    )(q, kb, vb, ck, cv, bias)


def _nat_bias(rpb):
    n_heads = rpb.shape[0]
    cols = jnp.arange(GRID_W)
    col_start = jnp.clip(cols - NA_WIN_C // 2, 0, GRID_W - NA_WIN_C)
    j = jnp.arange(GRID_W)
    valid = (j[None, :] >= col_start[:, None]) & (j[None, :] < col_start[:, None] + NA_WIN_C)
    col_off = jnp.clip(j[None, :] - cols[:, None] + (NA_WIN_C - 1), 0, 2 * NA_WIN_C - 2)
    toe = jnp.where(valid[None, None], rpb.astype(F32)[:, :, col_off], NEG_BIG)
    cases = jnp.stack([toe[:, NA_WIN_R - 1 - delta:2 * NA_WIN_R - 1 - delta] for delta in range(NA_WIN_R)])
    return cases.transpose(0, 1, 3, 2, 4).reshape(NA_WIN_R, n_heads // 2, 2 * GRID_W, NA_WIN_R * GRID_W)


def _s5_weights(a_re, a_im, log_dt, b_re, b_im, c_re, c_im, ssm_d):
    n_g = a_re.shape[1]
    c = S5_CHUNK
    lam = lax.complex(a_re.astype(F32), a_im.astype(F32))
    ldt = lam * jnp.exp(log_dt.astype(F32))[..., None]
    lam_bar = jnp.exp(ldt)
    b_bar = ((lam_bar - 1) / lam)[..., None] * lax.complex(b_re.astype(F32), b_im.astype(F32))
    c_mat = lax.complex(c_re.astype(F32), c_im.astype(F32))
    pw = jnp.exp(ldt[None] * jnp.arange(c + 1, dtype=F32)[:, None, None, None])
    kern = jnp.real(jnp.einsum('dgcp,kdgp,dgpe->dgkce', c_mat, pw[:c], b_bar))
    i = jnp.arange(c)
    lag = i[None, :] - i[:, None]
    tf = jnp.where((lag >= 0)[None, :, :, None, None], kern[0][:, jnp.clip(lag, 0, c - 1)], 0.0)
    tb = jnp.where((lag <= 0)[None, :, :, None, None], kern[1][:, jnp.clip(-lag, 0, c - 1)], 0.0)
    t_mat = (tf + tb).transpose(0, 1, 4, 2, 3).reshape(n_g, c * SSM_CH, c * SSM_CH)
    mf = pw[:c][::-1, 0][:, :, :, None] * b_bar[0][None]
    mb = pw[:c, 1][:, :, :, None] * b_bar[1][None]
    mf = mf.transpose(1, 0, 3, 2).reshape(n_g, c * SSM_CH, SSM_P)
    mb = mb.transpose(1, 0, 3, 2).reshape(n_g, c * SSM_CH, SSM_P)
    mq = jnp.stack([jnp.real(mf), jnp.imag(mf), jnp.real(mb), jnp.imag(mb)], axis=2)
    zf = c_mat[0][:, None] * pw[1:c + 1, 0][:, :, None, :].transpose(1, 0, 2, 3)
    zb = c_mat[1][:, None] * pw[1:c + 1, 1][::-1][:, :, None, :].transpose(1, 0, 2, 3)
    zf = zf.transpose(0, 3, 1, 2).reshape(n_g, SSM_P, c * SSM_CH)
    zb = zb.transpose(0, 3, 1, 2).reshape(n_g, SSM_P, c * SSM_CH)
    wq = jnp.stack([jnp.real(zf), -jnp.imag(zf), jnp.real(zb), -jnp.imag(zb)], axis=1)
    gh = LANES // SSM_CH
    n_half = n_g // gh
    m6 = mq.reshape(n_half, gh, c, SSM_CH, 4, SSM_P).astype(BF16)
    t6 = t_mat.reshape(n_half, gh, c, SSM_CH, c, SSM_CH).astype(BF16)
    w6 = wq.reshape(n_half, gh, 4, SSM_P, c, SSM_CH).astype(BF16)
    m_half = jnp.zeros((n_half, c, gh, SSM_CH, 4, gh, SSM_P), BF16)
    t_half = jnp.zeros((n_half, c, gh, SSM_CH, c, gh, SSM_CH), BF16)
    w_half = jnp.zeros((n_half, 4, gh, SSM_P, c, gh, SSM_CH), BF16)
    for g in range(gh):
        m_half = m_half.at[:, :, g, :, :, g, :].set(m6[:, g])
        t_half = t_half.at[:, :, g, :, :, g, :].set(t6[:, g])
        w_half = w_half.at[:, :, g, :, :, g, :].set(w6[:, g])
    m_half = m_half.reshape(n_half, c * LANES, 4 * gh * SSM_P)
    tw_half = jnp.concatenate([t_half.reshape(n_half, c * LANES, c * LANES),
                               w_half.reshape(n_half, 4 * gh * SSM_P, c * LANES)], axis=1)
    a_c = pw[c]
    a16 = jnp.stack([jnp.real(a_c[0]), jnp.imag(a_c[0]), jnp.real(a_c[1]), jnp.imag(a_c[1])])
    a16 = a16.reshape(4, 1, n_half, gh * SSM_P).transpose(2, 0, 1, 3)
    dvec = ssm_d.astype(F32).reshape(n_half, 1, LANES)
    return m_half, tw_half, a16, dvec


def _s5_state_body(u_ref, m_ref, fre_ref, fim_ref, bre_ref, bim_ref):
    nb, tn, kd = u_ref.shape
    w = fre_ref.shape[2]
    r = _dot(u_ref[...].reshape(nb * tn, kd).astype(BF16), m_ref[...])
    for q, o_ref in enumerate((fre_ref, fim_ref, bre_ref, bim_ref)):
        for b in range(nb):
            o_ref[:, b, :] = r[b * tn:(b + 1) * tn, q * w:(q + 1) * w]


def _s5_rows(nb, n):
    return max(8, min(n, 256 // nb))


def _s5_states(u3, m_half):
    nb, n, kd = u3.shape
    tn = _s5_rows(nb, n)
    w = m_half.shape[1] // 4
    out = pl.BlockSpec((tn, nb, w), lambda i: (i, 0, 0))
    return pl.pallas_call(
        _s5_state_body,
        name="s5_states",
        grid=(n // tn,),
        in_specs=[pl.BlockSpec((nb, tn, kd), lambda i: (0, i, 0)),
                  pl.BlockSpec(m_half.shape, lambda i: (0, 0))],
        out_specs=[out] * 4,
        out_shape=[jax.ShapeDtypeStruct((n, nb, w), F32)] * 4,
        compiler_params=_cparams("arbitrary"),
    )(u3, m_half)


def _s5_scan_body(sfr_ref, sfi_ref, sbr_ref, sbi_ref, a_ref, h0_ref,
                  hfr_ref, hfi_ref, hbr_ref, hbi_ref, fin_ref, st_ref):
    @pl.when(pl.program_id(0) == 0)
    def _():
        st_ref[...] = h0_ref[...]

    ks = sfr_ref.shape[0]
    afr, afi, abr, abi = a_ref[0], a_ref[1], a_ref[2], a_ref[3]

    def step(s, carry):
        fr, fi, br, bi = carry
        sb = ks - 1 - s
        hfr_ref[s] = fr
        hfi_ref[s] = fi
        hbr_ref[sb] = br
        hbi_ref[sb] = bi
        nfr = afr * fr - afi * fi + sfr_ref[s]
        nfi = afr * fi + afi * fr + sfi_ref[s]
        nbr = abr * br - abi * bi + sbr_ref[sb]
        nbi = abr * bi + abi * br + sbi_ref[sb]
        return nfr, nfi, nbr, nbi

    carry = lax.fori_loop(0, ks, step, (st_ref[0], st_ref[1], st_ref[2], st_ref[3]))
    for q in range(4):
        st_ref[q] = carry[q]
        fin_ref[q] = carry[q]


def _s5_scan(s3, a16, h0):
    n_chunks, n_batch, width = s3[0].shape
    ks = min(n_chunks, 32)
    nb = n_chunks // ks
    fwd = pl.BlockSpec((ks, n_batch, width), lambda i: (i, 0, 0))
    bwd = pl.BlockSpec((ks, n_batch, width), lambda i: (nb - 1 - i, 0, 0))
    small = lambda shape: pl.BlockSpec(shape, lambda i: (0, 0, 0))
    outs = pl.pallas_call(
        _s5_scan_body,
        name="s5_scan",
        grid=(nb,),
        in_specs=[fwd, fwd, bwd, bwd, small(a16.shape), small(h0.shape)],
        out_specs=[fwd, fwd, bwd, bwd, small(h0.shape)],
        out_shape=[jax.ShapeDtypeStruct((n_chunks, n_batch, width), F32)] * 4
        + [jax.ShapeDtypeStruct(h0.shape, F32)],
        scratch_shapes=[pltpu.VMEM(h0.shape, F32)],
        compiler_params=_cparams("arbitrary"),
    )(*s3, a16, h0)
    return outs[:4], outs[4]


def _s5_out_body(u_ref, us_ref, hfr_ref, hfi_ref, hbr_ref, hbi_ref, tw_ref, d_ref, y_ref):
    nb, tn, kd = u_ref.shape
    rows = nb * tn
    wo = us_ref.shape[2]
    y = _dot(u_ref[...].reshape(rows, kd).astype(BF16), tw_ref[0:kd, :])
    w = hfr_ref.shape[2]
    for q, h_ref in enumerate((hfr_ref, hfi_ref, hbr_ref, hbi_ref)):
        hq = jnp.concatenate([h_ref[:, b, :] for b in range(nb)], axis=0)
        y = y + _dot(hq.astype(BF16), tw_ref[kd + q * w:kd + (q + 1) * w, :])
    skip = jnp.concatenate([d_ref[...]] * (wo // LANES), axis=1) * us_ref[...].reshape(rows, wo)
    y_ref[...] = jax.nn.gelu(y + skip).reshape(nb, tn, wo)


def _s5_out(u3, h4, tw_half, dvec):
    nb, n, kd = u3.shape
    tn = _s5_rows(nb, n)
    wo = S5_OUT_STEPS * LANES
    w = h4[0].shape[2]
    hb = pl.BlockSpec((tn, nb, w), lambda i, j: (i, 0, 0))
    return pl.pallas_call(
        _s5_out_body,
        name="s5_out",
        grid=(n // tn, kd // wo),
        in_specs=[pl.BlockSpec((nb, tn, kd), lambda i, j: (0, i, 0)),
                  pl.BlockSpec((nb, tn, wo), lambda i, j: (0, i, j)),
                  hb, hb, hb, hb,
                  pl.BlockSpec((tw_half.shape[0], wo), lambda i, j: (0, j)),
                  pl.BlockSpec(dvec.shape, lambda i, j: (0, 0))],
        out_specs=pl.BlockSpec((nb, tn, wo), lambda i, j: (0, i, j)),
        out_shape=jax.ShapeDtypeStruct(u3.shape, F32),
        compiler_params=_cparams("arbitrary", "arbitrary"),
    )(u3, u3, *h4, tw_half, dvec)


def _s5(pc_halves, n_batch, seq, weights, h0):
    m_half, tw_half, a16, dvec = weights
    n_chunks = seq // S5_CHUNK
    ys, fins = [], []
    for h, pc in enumerate(pc_halves):
        w = m_half.shape[2] // 4
        u3 = pc.reshape(n_batch, n_chunks, S5_CHUNK * LANES)
        s4 = _s5_states(u3, m_half[h])
        h4, fin = _s5_scan(s4, a16[h], h0[:, :, h * w:(h + 1) * w])
        ys.append(_s5_out(u3, h4, tw_half[h], dvec[h]).reshape(n_batch * seq, LANES))
        fins.append(fin)
    return ys, jnp.concatenate(fins, axis=2)


def _post_body(x_ref, ya_ref, yb_ref, yc0_ref, yc1_ref, mod_ref, gpost_ref, gffn_ref, wglu_ref, bglu_ref, wo_ref,
               x1_ref, h2_ref):
    x = x_ref[...]
    d = x.shape[1]
    d_a = ya_ref.shape[1]
    d_b = yb_ref.shape[1]
    y = jnp.concatenate([yc0_ref[...], yc1_ref[...]], axis=1)
    glu = y * jax.nn.sigmoid(_dot(y.astype(BF16), wglu_ref[...]) + bglu_ref[...])
    mixed = (_dot(ya_ref[...], wo_ref[0:d_a, :]) + _dot(yb_ref[...], wo_ref[d_a:d_a + d_b, :])
             + _dot(glu.astype(BF16), wo_ref[d_a + d_b:, :]))
    x1 = x + mod_ref[0, :, 2 * d:3 * d] * _rms(mixed, gpost_ref[...])
    x1_ref[...] = x1
    h2 = _rms(x1, gffn_ref[...]) * (1 + mod_ref[0, :, 4 * d:5 * d]) + mod_ref[0, :, 3 * d:4 * d]
    _store_tiled(h2_ref, h2)


def _post(x, ya, yb, yc0, yc1, mod, g_post, g_ffn, w_glu_b, b_glu, w_out_b, row_map):
    t, d = x.shape
    tm = TOK_TILE
    nc = d // LANES
    tok = lambda n: pl.BlockSpec((tm, n), lambda i: (i, 0))
    full = lambda a: pl.BlockSpec(a.shape, lambda i: (0,) * a.ndim)
    return pl.pallas_call(
        _post_body,
        name="post_mix",
        grid=(t // tm,),
        in_specs=[tok(d), tok(ya.shape[1]), tok(yb.shape[1]), tok(LANES), tok(LANES),
                  pl.BlockSpec((1, 1, mod.shape[2]), lambda i: (row_map(i), 0, 0)),
                  full(g_post), full(g_ffn), full(w_glu_b), full(b_glu), full(w_out_b)],
        out_specs=[tok(d), pl.BlockSpec((tm * nc, LANES), lambda i: (i, 0))],
        out_shape=[jax.ShapeDtypeStruct((t, d), F32), jax.ShapeDtypeStruct((t * nc, LANES), F32)],
        compiler_params=_cparams("arbitrary"),
    )(x, ya, yb, yc0, yc1, mod, g_post, g_ffn, w_glu_b, b_glu, w_out_b)


def _route_body(h_ref, whi_ref, wlo_ref, b_ref, idx_ref, gate_ref, pos_ref, cnt_ref, carry_ref):
    @pl.when(pl.program_id(0) == 0)
    def _():
        carry_ref[...] = jnp.zeros_like(carry_ref)

    tm = idx_ref.shape[1]
    h = _load_tiled(h_ref, tm)
    hi = h.astype(BF16)
    lo = (h - hi.astype(F32)).astype(BF16)
    logits = _dot_nt(whi_ref[...], hi) + (_dot_nt(whi_ref[...], lo) + _dot_nt(wlo_ref[...], hi))
    scores = jax.nn.sigmoid(logits)
    sel = scores + b_ref[...]
    gsz = N_EXPERTS // N_EXP_GROUPS
    within = lax.broadcasted_iota(I32, (gsz, tm), 0).astype(F32)
    grp = []
    for g in range(N_EXP_GROUPS):
        blk = sel[g * gsz:(g + 1) * gsz]
        m1 = jnp.max(blk, axis=0, keepdims=True)
        first = jnp.min(jnp.where(blk == m1, within, float(gsz)), axis=0, keepdims=True)
        m2 = jnp.max(jnp.where(within == first, -jnp.inf, blk), axis=0, keepdims=True)
        grp.append(m1 + m2)
    blocks = []
    for g in range(N_EXP_GROUPS):
        ahead = jnp.zeros((1, tm), F32)
        for o in range(N_EXP_GROUPS):
            if o == g:
                continue
            beats = (grp[o] >= grp[g]) if o < g else (grp[o] > grp[g])
            ahead = ahead + jnp.where(beats, 1.0, 0.0)
        ahead = jnp.broadcast_to(ahead, (gsz, tm))
        blocks.append(jnp.where(ahead < TOPK_GROUPS, sel[g * gsz:(g + 1) * gsz], -jnp.inf))
    v = jnp.concatenate(blocks, axis=0)
    eidx = lax.broadcasted_iota(I32, (N_EXPERTS, tm), 0)
    rnk = jnp.zeros((N_EXPERTS, tm), F32)
    for e in range(N_EXPERTS):
        row = v[e:e + 1]
        rnk = rnk + jnp.where(eidx > e, jnp.where(row >= v, 1.0, 0.0), jnp.where(row > v, 1.0, 0.0))
    chosen = rnk < TOP_K
    w = jnp.where(chosen, scores, 0.0)
    wn = w / jnp.sum(w, axis=0, keepdims=True) * ROUTE_SCALE
    tri = (lax.broadcasted_iota(I32, (tm, tm), 0) < lax.broadcasted_iota(I32, (tm, tm), 1))
    chosen_f = jnp.where(chosen, 1.0, 0.0)
    prefix = _dot(chosen_f.astype(BF16), jnp.where(tri, 1.0, 0.0).astype(BF16)) + carry_ref[:, 0:1]
    total = carry_ref[...] + jnp.sum(chosen_f, axis=1, keepdims=True)
    carry_ref[...] = total
    cnt_ref[...] = total
    eidx_f = eidx.astype(F32)
    for k in range(TOP_K):
        one = rnk == k
        idx_ref[k:k + 1, :] = jnp.sum(jnp.where(one, eidx_f, 0.0), axis=0, keepdims=True).astype(I32)
        gate_ref[k:k + 1, :] = jnp.sum(jnp.where(one, wn, 0.0), axis=0, keepdims=True)
        pos_ref[k:k + 1, :] = jnp.sum(jnp.where(one, prefix, 0.0), axis=0, keepdims=True).astype(I32)


def _route(h2t, w_hi, w_lo, b_router):
    nc = w_hi.shape[1] // LANES
    t = h2t.shape[0] // nc
    tm = TOK_TILE
    out = pl.BlockSpec((TOP_K, tm), lambda i: (0, i))
    full = lambda a: pl.BlockSpec(a.shape, lambda i: (0, 0))
    return pl.pallas_call(
        _route_body,
        name="route",
        grid=(t // tm,),
        in_specs=[pl.BlockSpec((tm * nc, LANES), lambda i: (i, 0)), full(w_hi), full(w_lo), full(b_router)],
        out_specs=[out, out, out, pl.BlockSpec((N_EXPERTS, LANES), lambda i: (0, 0))],
        out_shape=[jax.ShapeDtypeStruct((TOP_K, t), I32), jax.ShapeDtypeStruct((TOP_K, t), F32),
                   jax.ShapeDtypeStruct((TOP_K, t), I32), jax.ShapeDtypeStruct((N_EXPERTS, LANES), F32)],
        scratch_shapes=[pltpu.VMEM((N_EXPERTS, LANES), F32)],
        compiler_params=_cparams("arbitrary"),
    )(h2t, w_hi, w_lo, b_router)


def _dest_body(starts_ref, idx_ref, pos_ref, dest_ref):
    idx = idx_ref[...]
    acc = pos_ref[...]
    for e in range(N_EXPERTS):
        acc = acc + jnp.where(idx == e, starts_ref[e], 0)
    dest_ref[...] = acc


def _dest(starts, idx, pos):
    t = idx.shape[1]
    tile = math.gcd(t, 4096)
    blk = pl.BlockSpec((TOP_K, tile), lambda i, st: (0, i))
    return pl.pallas_call(
        _dest_body,
        name="moe_dest",
        grid_spec=pltpu.PrefetchScalarGridSpec(num_scalar_prefetch=1, grid=(t // tile,),
                                               in_specs=[blk, blk], out_specs=blk),
        out_shape=jax.ShapeDtypeStruct(idx.shape, I32),
        compiler_params=_cparams("arbitrary"),
    )(starts, idx, pos)


def _dispatch_body(pad_lo_ref, pad_hi_ref, dest_ref, h_ref, xs_ref, zero_ref, sem, zsem, *, n_pad_rows):
    tile = dest_ref.shape[1]

    def zero_copy(r):
        return pltpu.make_async_copy(zero_ref, xs_ref.at[r], zsem)

    @pl.when(pl.program_id(0) == 0)
    def _():
        zero_ref[...] = jnp.zeros_like(zero_ref)

        def per_expert(e, c):
            def one(r, c2):
                zero_copy(r).start()
                return c2
            return lax.fori_loop(pad_lo_ref[e], pad_hi_ref[e], one, c)

        lax.fori_loop(0, N_EXPERTS, per_expert, 0)

    def row_copy(t, k):
        return pltpu.make_async_copy(h_ref.at[t], xs_ref.at[dest_ref[k, t]], sem)

    def start(t, c):
        for k in range(TOP_K):
            row_copy(t, k).start(priority=k % DMA_THREADS)
        return c

    def wait(t, c):
        for k in range(TOP_K):
            row_copy(t, k).wait()
        return c

    lax.fori_loop(0, tile, start, 0)
    lax.fori_loop(0, tile, wait, 0)

    @pl.when(pl.program_id(0) == 0)
    def _():
        def one(r, c):
            zero_copy(0).wait()
            return c
        lax.fori_loop(0, n_pad_rows, one, 0)


def _dispatch(pad_lo, pad_hi, dest, h2t, n_rows):
    t, nc, _ = h2t.shape
    tile = DISPATCH_TILE
    grid_spec = pltpu.PrefetchScalarGridSpec(
        num_scalar_prefetch=2,
        grid=(t // tile,),
        in_specs=[pl.BlockSpec((TOP_K, tile), lambda i, lo, hi: (0, i), memory_space=pltpu.SMEM),
                  pl.BlockSpec((tile, nc, LANES), lambda i, lo, hi: (i, 0, 0))],
        out_specs=pl.BlockSpec(memory_space=pl.ANY),
        scratch_shapes=[pltpu.VMEM((nc, LANES), F32), pltpu.SemaphoreType.DMA(()),
                        pltpu.SemaphoreType.DMA(())],
    )
    return pl.pallas_call(
        functools.partial(_dispatch_body, n_pad_rows=n_rows - t * TOP_K),
        name="moe_dispatch",
        grid_spec=grid_spec,
        out_shape=jax.ShapeDtypeStruct((n_rows, nc, LANES), F32),
        compiler_params=_cparams("arbitrary"),
    )(pad_lo, pad_hi, dest, h2t)


def _experts_body(be_ref, x_ref, wg_ref, wu_ref, wd_ref, y_ref):
    del be_ref
    xb = _load_tiled(x_ref, MOE_ROWS).astype(BF16)
    hb = jax.nn.silu(_dot(xb, wg_ref[0])) * _dot(xb, wu_ref[0])
    _store_tiled(y_ref, _dot(hb.astype(BF16), wd_ref[0]))


def _experts(blk_exp, xs, wg, wu, wd):
    d, de = wg.shape[1], wg.shape[2]
    nc = d // LANES
    bm = MOE_ROWS
    tiles = pl.BlockSpec((bm * nc, LANES), lambda i, be: (i, 0))
    grid_spec = pltpu.PrefetchScalarGridSpec(
        num_scalar_prefetch=1,
        grid=(xs.shape[0] // (bm * nc),),
        in_specs=[tiles,
                  pl.BlockSpec((1, d, de), lambda i, be: (be[i], 0, 0)),
                  pl.BlockSpec((1, d, de), lambda i, be: (be[i], 0, 0)),
                  pl.BlockSpec((1, de, d), lambda i, be: (be[i], 0, 0))],
        out_specs=tiles,
    )
    return pl.pallas_call(
        _experts_body,
        name="moe_experts",
        grid_spec=grid_spec,
        out_shape=jax.ShapeDtypeStruct(xs.shape, F32),
        compiler_params=_cparams("arbitrary"),
    )(blk_exp, xs, wg, wu, wd)


def _combine_body(dest_ref, next_ref, gate_ref, h_ref, x_ref, mod_ref, g_ref, wsg_ref, wsu_ref, wsd_ref, ys_hbm,
                  o_ref, buf_ref, sem):
    tile, d = x_ref.shape
    nc = d // LANES
    i = pl.program_id(0)
    slot = i % 2

    def row_copy(d_ref, s, t, k):
        return pltpu.make_async_copy(ys_hbm.at[d_ref[k, t]],
                                     buf_ref.at[s, k, pl.ds(pl.multiple_of(t * nc, nc), nc)], sem.at[s])

    def gather(d_ref, s):
        def start(t, c):
            for k in range(TOP_K):
                row_copy(d_ref, s, t, k).start(priority=k % DMA_THREADS)
            return c
        lax.fori_loop(0, tile, start, 0)

    @pl.when(i == 0)
    def _():
        gather(dest_ref, 0)

    @pl.when(i + 1 < pl.num_programs(0))
    def _():
        gather(next_ref, 1 - slot)

    hb = _load_tiled(h_ref, tile).astype(BF16)
    acc = _dot((jax.nn.silu(_dot(hb, wsg_ref[...])) * _dot(hb, wsu_ref[...])).astype(BF16), wsd_ref[...])

    def wait(t, c):
        for k in range(TOP_K):
            row_copy(dest_ref, slot, 0, k).wait()
        return c

    lax.fori_loop(0, tile, wait, 0)
    gates = gate_ref[...]
    moe = gates[:, 0:1] * _load_tiled(buf_ref.at[slot, 0], tile)
    for k in range(1, TOP_K):
        moe = moe + gates[:, k:k + 1] * _load_tiled(buf_ref.at[slot, k], tile)
    o_ref[...] = x_ref[...] + mod_ref[0, :, 5 * d:6 * d] * _rms(moe + acc, g_ref[...])


def _combine(dest, gates, h2t, x1, mod, g_post, wsg, wsu, wsd, ys, row_map):
    t, d = x1.shape
    nc = d // LANES
    tile = COMBINE_TILE
    per_tok = TOK_TILE // tile
    n_steps = t // tile
    tok = lambda n: pl.BlockSpec((tile, n), lambda i: (i, 0))
    full = lambda a: pl.BlockSpec(a.shape, lambda i: (0,) * a.ndim)
    return pl.pallas_call(
        _combine_body,
        name="moe_combine",
        grid=(n_steps,),
        in_specs=[pl.BlockSpec((TOP_K, tile), lambda i: (0, i), memory_space=pltpu.SMEM),
                  pl.BlockSpec((TOP_K, tile), lambda i: (0, jnp.minimum(i + 1, n_steps - 1)),
                               memory_space=pltpu.SMEM),
                  tok(TOP_K), pl.BlockSpec((tile * nc, LANES), lambda i: (i, 0)), tok(d),
                  pl.BlockSpec((1, 1, mod.shape[2]), lambda i: (row_map(i // per_tok), 0, 0)),
                  full(g_post), full(wsg), full(wsu), full(wsd),
                  pl.BlockSpec(memory_space=pl.ANY)],
        out_specs=tok(d),
        out_shape=jax.ShapeDtypeStruct((t, d), F32),
        scratch_shapes=[pltpu.VMEM((2, TOP_K, tile * nc, LANES), F32), pltpu.SemaphoreType.DMA((2,))],
        compiler_params=_cparams("arbitrary"),
    )(dest, dest, gates, h2t, x1, mod, g_post, wsg, wsu, wsd, ys)


def _moe(h2t, x1, mod, p, row_map):
    t, d = x1.shape
    nc = d // LANES
    idx, gate, pos, cnt = _route(h2t, p['wr_hi'], p['wr_lo'], p['b_router'])
    counts = cnt[:, 0].astype(I32)
    padded = (counts + MOE_ROWS - 1) // MOE_ROWS * MOE_ROWS
    ends = jnp.cumsum(padded)
    starts = ends - padded
    n_blocks = (t * TOP_K + N_EXPERTS * (MOE_ROWS - 1)) // MOE_ROWS + 1
    n_rows = n_blocks * MOE_ROWS
    blk_start = jnp.arange(n_blocks, dtype=I32) * MOE_ROWS
    blk_exp = jnp.minimum(jnp.sum((ends[None, :] <= blk_start[:, None]).astype(I32), axis=1), N_EXPERTS - 1)
    pad_hi = ends.at[N_EXPERTS - 1].set(n_rows)
    dest = _dest(starts, idx, pos)
    xs = _dispatch(starts + counts, pad_hi, dest, h2t.reshape(t, nc, LANES), n_rows)
    ys = _experts(blk_exp, xs.reshape(n_rows * nc, LANES), p['w_e_gate'], p['w_e_up'], p['w_e_down'])
    return _combine(dest, gate.T, h2t, x1, mod, p['g_post_ffn'], p['w_s_gate'], p['w_s_up'], p['w_s_down'],
                    ys.reshape(n_rows, nc, LANES), row_map)


def kernel(x_prompt, x_sample, cache_k, cache_v, state_ssm_re, state_ssm_im, c, c_ctx,
           g_pre_mix, g_post_mix, g_pre_ffn, g_post_ffn, w_ada, b_ada, w_in, w_out,
           sgu_g, w_sp, b_sp, rpb, ssm_a_re, ssm_a_im, ssm_log_dt, ssm_b_re, ssm_b_im,
           ssm_c_re, ssm_c_im, ssm_d, w_glu, b_glu, w_router, b_router,
           w_e_gate, w_e_up, w_e_down, w_s_gate, w_s_up, w_s_down):
    n_pb, p_seq, d = x_prompt.shape
    n_sb, s_seq, _ = x_sample.shape
    depth = w_in.shape[0]
    d_a = sgu_g.shape[1]
    d_c = w_glu.shape[1]
    d_b = d - d_a - d_c
    n_heads = d_b // HD_B
    n_g = d_c // SSM_CH
    t_p = n_pb * p_seq
    assert p_seq % CHUNK == 0 and t_p % TOK_TILE == 0 and s_seq % TOK_TILE == 0 and t_p % s_seq == 0
    assert s_seq % (NAT_ROWS * GRID_W) == 0 and s_seq // GRID_W >= NA_WIN_R and d_c == 2 * LANES

    mod_rows = -(-(n_sb + 1) // 8) * 8
    c_all = jnp.zeros((mod_rows, d), F32).at[:n_sb].set(c).at[n_sb].set(c_ctx)
    mod_all = _ada(c_all, w_ada, b_ada)
    p_tiles = t_p // TOK_TILE
    s_tiles = s_seq // TOK_TILE

    def row_map(i):
        return jnp.where(i < p_tiles, n_sb, (i - p_tiles) // s_tiles)

    x = jnp.concatenate([x_prompt.reshape(t_p, d), x_sample.reshape(n_sb * s_seq, d)], axis=0)
    new_k, new_v, new_re, new_im = [], [], [], []
    for l in range(depth):
        mod = mod_all[l][:, None, :]
        row = lambda a: a[l][None, :].astype(F32)
        pa, q, k32, v32, kb, vb, *pc = _premix(x, mod, row(g_pre_mix), w_in[l].astype(BF16), row_map,
                                               d_a, d_b, d_c)
        bias_a = jnp.repeat(b_sp[l].T.astype(F32), d_a // NH_A, axis=1)
        ya = _chunk_mlp(pa, row(sgu_g), w_sp[l].reshape(NH_A * CHUNK, CHUNK).astype(BF16), bias_a)
        yb_p = _ctx_attn(q, k32, v32, n_pb, p_seq)
        ck = cache_k[:, l].reshape(n_sb, -1, d_b).astype(BF16)
        cv = cache_v[:, l].reshape(n_sb, -1, d_b).astype(BF16)
        yb_s = _nat_attn(q, kb, vb, ck, cv, _nat_bias(rpb[l]), n_sb, s_seq, t_p)
        s5w = _s5_weights(ssm_a_re[l], ssm_a_im[l], ssm_log_dt[l], ssm_b_re[l], ssm_b_im[l],
                          ssm_c_re[l], ssm_c_im[l], ssm_d[l])
        yc_p, fin_p = _s5([h[:t_p] for h in pc], n_pb, p_seq, s5w, jnp.zeros((4, n_pb, n_g * SSM_P), F32))
        sre = state_ssm_re[:, l].astype(F32).reshape(n_sb, 2, n_g * SSM_P)
        sim = state_ssm_im[:, l].astype(F32).reshape(n_sb, 2, n_g * SSM_P)
        h0_s = jnp.stack([sre[:, 0], sim[:, 0], sre[:, 1], sim[:, 1]])
        yc_s, _ = _s5([h[t_p:] for h in pc], n_sb, s_seq, s5w, h0_s)
        yc = [jnp.concatenate([a, b], axis=0) for a, b in zip(yc_p, yc_s)]
        x1, h2 = _post(x, ya, jnp.concatenate([yb_p, yb_s], axis=0), yc[0], yc[1],
                       mod, row(g_post_mix), row(g_pre_ffn), w_glu[l].astype(BF16), row(b_glu),
                       w_out[l].astype(BF16), row_map)
        wr = w_router[l].astype(F32).T
        wr_hi = wr.astype(BF16)
        moe_p = {
            'wr_hi': wr_hi, 'wr_lo': (wr - wr_hi.astype(F32)).astype(BF16),
            'b_router': b_router[l].astype(F32)[:, None],
            'w_e_gate': w_e_gate[l].astype(BF16), 'w_e_up': w_e_up[l].astype(BF16),
            'w_e_down': w_e_down[l].astype(BF16),
            'w_s_gate': w_s_gate[l].astype(BF16), 'w_s_up': w_s_up[l].astype(BF16),
            'w_s_down': w_s_down[l].astype(BF16), 'g_post_ffn': row(g_post_ffn),
        }
        x = _moe(h2, x1, mod, moe_p, row_map)
        new_k.append(k32[:t_p].reshape(n_pb, p_seq, n_heads, HD_B))
        new_v.append(v32[:t_p].reshape(n_pb, p_seq, n_heads, HD_B))
        fin_p = fin_p.reshape(4, n_pb, n_g, SSM_P)
        new_re.append(jnp.stack([fin_p[0], fin_p[2]], axis=1))
        new_im.append(jnp.stack([fin_p[1], fin_p[3]], axis=1))
    return (x[:t_p].reshape(n_pb, p_seq, d), x[t_p:].reshape(n_sb, s_seq, d),
            jnp.stack(new_k, axis=1), jnp.stack(new_v, axis=1),
            jnp.stack(new_re, axis=1), jnp.stack(new_im, axis=1))
```

```python
import functools
import math

import jax
import jax.numpy as jnp
from jax import lax
from jax.experimental import pallas as pl
from jax.experimental.pallas import tpu as pltpu

F32 = jnp.float32
BF16 = jnp.bfloat16
I32 = jnp.int32

GRID_W = 64
EPS = 1e-6
NH_A = 4
CHUNK = 128
HD_B = 64
NA_WIN_R = 8
NA_WIN_C = 16
ATTN_SCALE = HD_B ** -0.5
SSM_CH = 16
SSM_P = 64
N_EXPERTS = 64
TOP_K = 8
N_EXP_GROUPS = 8
TOPK_GROUPS = 4
ROUTE_SCALE = 2.5

LANES = 128
TOK_TILE = 512
S5_CHUNK = 16
S5_OUT_STEPS = 4
MOE_ROWS = 512
COMBINE_TILE = 128
DISPATCH_TILE = 512
NAT_ROWS = 4
DMA_THREADS = 2
NEG_BIG = -1e30
VMEM_LIMIT = 48 * 1024 * 1024


def _cparams(*sem):
    return pltpu.CompilerParams(dimension_semantics=sem, vmem_limit_bytes=VMEM_LIMIT)


def _dot(a, b):
    return jnp.dot(a, b, preferred_element_type=F32)


def _dot_nt(a, b):
    return lax.dot_general(a, b, (((1,), (1,)), ((), ())), preferred_element_type=F32)


def _rms(x, g):
    return x * lax.rsqrt(jnp.mean(x * x, axis=-1, keepdims=True) + EPS) * g


def _load_tiled(ref, n_rows):
    nc = ref.shape[0] // n_rows
    return jnp.concatenate([ref[pl.ds(j, n_rows, stride=nc), :] for j in range(nc)], axis=1)


def _store_tiled(ref, val):
    n_rows = val.shape[0]
    nc = ref.shape[0] // n_rows
    for j in range(nc):
        ref[pl.ds(j, n_rows, stride=nc), :] = val[:, j * LANES:(j + 1) * LANES]


def _ada_body(c_ref, w_ref, b_ref, o_ref):
    s = jax.nn.silu(c_ref[...]).astype(BF16)
    o_ref[0] = _dot(s, w_ref[0].astype(BF16)) + b_ref[0]


def _ada(c_all, w_ada, b_ada):
    n_layers, d, n = w_ada.shape
    rows = c_all.shape[0]
    tn = 1536
    return pl.pallas_call(
        _ada_body,
        name="ada",
        grid=(n_layers, n // tn),
        in_specs=[pl.BlockSpec((rows, d), lambda l, j: (0, 0)),
                  pl.BlockSpec((1, d, tn), lambda l, j: (l, 0, j)),
                  pl.BlockSpec((1, 1, tn), lambda l, j: (l, 0, j))],
        out_specs=pl.BlockSpec((1, rows, tn), lambda l, j: (l, 0, j)),
        out_shape=jax.ShapeDtypeStruct((n_layers, rows, n), F32),
        compiler_params=_cparams("arbitrary", "arbitrary"),
    )(c_all, w_ada, b_ada.reshape(n_layers, 1, n))


def _premix_body(x_ref, mod_ref, g_ref, w_ref, pa_ref, q_ref, k_ref, v_ref, kb_ref, vb_ref, pc0_ref, pc1_ref,
                 fold_ref):
    x = x_ref[...]
    d = x.shape[1]
    h = _rms(x, g_ref[...]) * (1 + mod_ref[0, :, d:2 * d]) + mod_ref[0, :, 0:d]
    p = _dot(h.astype(BF16), w_ref[...])
    d_a2 = pa_ref.shape[1]
    d_b = q_ref.shape[1]
    o = d_a2
    pa_ref[...] = p[:, 0:o]
    q_ref[...] = p[:, o:o + d_b].astype(BF16)
    k = p[:, o + d_b:o + 2 * d_b]
    v = p[:, o + 2 * d_b:o + 3 * d_b]
    k_ref[...] = k
    v_ref[...] = v
    kb_ref[...] = k.astype(BF16)
    vb_ref[...] = v.astype(BF16)
    for h, pc_ref in enumerate((pc0_ref, pc1_ref)):
        fold_ref[...] = p[:, o + 3 * d_b + h * LANES:o + 3 * d_b + (h + 1) * LANES]
        _fold_chunks(pc_ref, fold_ref)


def _fold_chunks(dst_ref, src_ref):
    n = dst_ref.shape[0]
    for i in range(S5_CHUNK):
        dst_ref[:, i * LANES:(i + 1) * LANES] = src_ref[pl.ds(i, n, stride=S5_CHUNK), :]


def _unfold_chunks(dst_ref, src_ref):
    n = src_ref.shape[0]
    for i in range(S5_CHUNK):
        dst_ref[pl.ds(i, n, stride=S5_CHUNK), :] = src_ref[:, i * LANES:(i + 1) * LANES]


def _premix(x, mod, g, w_in_b, row_map, d_a, d_b, d_c):
    t, d = x.shape
    tm = TOK_TILE
    d_in = w_in_b.shape[1]
    tok = lambda n: pl.BlockSpec((tm, n), lambda i: (i, 0))
    fold = pl.BlockSpec((tm // S5_CHUNK, S5_CHUNK * LANES), lambda i: (i, 0))
    return pl.pallas_call(
        _premix_body,
        name="premix",
        grid=(t // tm,),
        in_specs=[tok(d),
                  pl.BlockSpec((1, 1, mod.shape[2]), lambda i: (row_map(i), 0, 0)),
                  pl.BlockSpec((1, d), lambda i: (0, 0)),
                  pl.BlockSpec((d, d_in), lambda i: (0, 0))],
        out_specs=[tok(2 * d_a), tok(d_b), tok(d_b), tok(d_b), tok(d_b), tok(d_b), fold, fold],
        out_shape=[jax.ShapeDtypeStruct((t, 2 * d_a), F32),
                   jax.ShapeDtypeStruct((t, d_b), BF16),
                   jax.ShapeDtypeStruct((t, d_b), F32),
                   jax.ShapeDtypeStruct((t, d_b), F32),
                   jax.ShapeDtypeStruct((t, d_b), BF16),
                   jax.ShapeDtypeStruct((t, d_b), BF16)]
        + [jax.ShapeDtypeStruct((t // S5_CHUNK, S5_CHUNK * LANES), F32)] * 2,
        scratch_shapes=[pltpu.VMEM((tm, LANES), F32)],
        compiler_params=_cparams("arbitrary"),
    )(x, mod, g, w_in_b)


def _chunk_body(pa_ref, g_ref, w_ref, b_ref, o_ref):
    z = jax.nn.gelu(pa_ref[...])
    d_a = o_ref.shape[1]
    hd = d_a // NH_A
    u = z[:, :d_a]
    v = z[:, d_a:]
    mu = jnp.mean(v, axis=-1, keepdims=True)
    var = jnp.mean(jnp.square(v - mu), axis=-1, keepdims=True)
    vb = ((v - mu) * lax.rsqrt(var + EPS) * g_ref[...]).astype(BF16)
    head = lax.broadcasted_iota(I32, (CHUNK, d_a), 1) // hd
    for ch in range(pa_ref.shape[0] // CHUNK):
        rows = slice(ch * CHUNK, (ch + 1) * CHUNK)
        sf = _dot(w_ref[...], vb[rows])
        s = b_ref[...]
        for h in range(NH_A):
            s = s + jnp.where(head == h, sf[h * CHUNK:(h + 1) * CHUNK], 0.0)
        o_ref[rows, :] = (u[rows] * s).astype(BF16)


def _chunk_mlp(pa, sgu_g, w_sp_b, bias):
    t, d2 = pa.shape
    d_a = d2 // 2
    tm = TOK_TILE
    return pl.pallas_call(
        _chunk_body,
        name="chunk_mlp",
        grid=(t // tm,),
        in_specs=[pl.BlockSpec((tm, d2), lambda i: (i, 0)),
                  pl.BlockSpec((1, d_a), lambda i: (0, 0)),
                  pl.BlockSpec(w_sp_b.shape, lambda i: (0, 0)),
                  pl.BlockSpec(bias.shape, lambda i: (0, 0))],
        out_specs=pl.BlockSpec((tm, d_a), lambda i: (i, 0)),
        out_shape=jax.ShapeDtypeStruct((t, d_a), BF16),
        compiler_params=_cparams("arbitrary"),
    )(pa, sgu_g, w_sp_b, bias)


def _stack_pair(qg):
    lane = lax.broadcasted_iota(I32, qg.shape, 1)
    zero = jnp.zeros_like(qg)
    return jnp.concatenate([jnp.where(lane < HD_B, qg, zero), jnp.where(lane >= HD_B, qg, zero)], axis=0)


def _unstack_pair(o2):
    n = o2.shape[0] // 2
    lane = lax.broadcasted_iota(I32, (n, o2.shape[1]), 1)
    return jnp.where(lane < HD_B, o2[:n], o2[n:])


def _ctx_attn_body(q_ref, k_ref, v_ref, o_ref):
    for g in range(q_ref.shape[1] // LANES):
        cols = slice(g * LANES, (g + 1) * LANES)
        q2 = _stack_pair(q_ref[:, cols])
        s = _dot_nt(q2, k_ref[:, cols].astype(BF16)) * ATTN_SCALE
        e = jnp.exp(s - jnp.max(s, axis=-1, keepdims=True))
        p = e / jnp.sum(e, axis=-1, keepdims=True)
        o2 = _dot(p.astype(BF16), v_ref[:, cols].astype(BF16))
        o_ref[:, cols] = _unstack_pair(o2).astype(BF16)


def _ctx_attn(q, k, v, n_batch, seq):
    d_b = q.shape[1]
    blk = pl.BlockSpec((seq, d_b), lambda b: (b, 0))
    return pl.pallas_call(
        _ctx_attn_body,
        name="ctx_attn",
        grid=(n_batch,),
        in_specs=[blk, blk, blk],
        out_specs=blk,
        out_shape=jax.ShapeDtypeStruct(q.shape, BF16),
        compiler_params=_cparams("arbitrary"),
    )(q, k, v)


def _nat_body(q_ref, k_ref, v_ref, ck_ref, cv_ref, bias_ref, yb_hbm, o_ref, *, rows):
    del yb_hbm
    r0 = pl.program_id(1) * NAT_ROWS
    n_win = NA_WIN_R * GRID_W
    pr = 2 * GRID_W
    starts, cases = [], []
    for i in range(NAT_ROWS):
        rs = jnp.clip(r0 + i - NA_WIN_R // 2, 0, rows - NA_WIN_R)
        starts.append(pl.multiple_of(rs * GRID_W, GRID_W))
        cases.append(r0 + i - rs)
    for g in range(q_ref.shape[1] // LANES):
        cols = slice(g * LANES, (g + 1) * LANES)
        q2 = jnp.concatenate([_stack_pair(q_ref[i * GRID_W:(i + 1) * GRID_W, cols]) for i in range(NAT_ROWS)],
                             axis=0)
        s_ctx = _dot_nt(q2, ck_ref[0, :, cols]) * ATTN_SCALE
        m_ctx = jnp.max(s_ctx, axis=-1, keepdims=True)
        e_wins, invs, ms = [], [], []
        for i in range(NAT_ROWS):
            s_win = (_dot_nt(q2[i * pr:(i + 1) * pr], k_ref[pl.ds(starts[i], n_win), cols]) * ATTN_SCALE
                     + bias_ref[cases[i], g])
            m = jnp.maximum(jnp.max(s_win, axis=-1, keepdims=True), m_ctx[i * pr:(i + 1) * pr])
            e_wins.append(jnp.exp(s_win - m))
            ms.append(m)
        e_ctx = jnp.exp(s_ctx - jnp.concatenate(ms, axis=0))
        l_ctx = jnp.sum(e_ctx, axis=-1, keepdims=True)
        for i in range(NAT_ROWS):
            invs.append(1.0 / (jnp.sum(e_wins[i], axis=-1, keepdims=True) + l_ctx[i * pr:(i + 1) * pr]))
        o_ctx = _dot((e_ctx * jnp.concatenate(invs, axis=0)).astype(BF16), cv_ref[0, :, cols])
        for i in range(NAT_ROWS):
            o2 = _dot((e_wins[i] * invs[i]).astype(BF16), v_ref[pl.ds(starts[i], n_win), cols])
            o_ref[i * GRID_W:(i + 1) * GRID_W, cols] = _unstack_pair(o2 + o_ctx[i * pr:(i + 1) * pr]).astype(BF16)


def _nat_attn(q, kb, vb, ck, cv, bias, yb, n_batch, seq, tok0):
    d_b = q.shape[1]
    rows = seq // GRID_W
    lc = ck.shape[1]
    blk = NAT_ROWS * GRID_W
    steps = seq // blk
    q0 = tok0 // blk
    i0 = tok0 // seq
    img = pl.BlockSpec((seq, d_b), lambda b, r: (i0 + b, 0))
    ctx = pl.BlockSpec((1, lc, d_b), lambda b, r: (b, 0, 0))
    return pl.pallas_call(
        functools.partial(_nat_body, rows=rows),
        name="nat_attn",
        grid=(n_batch, steps),
        in_specs=[pl.BlockSpec((blk, d_b), lambda b, r: (q0 + b * steps + r, 0)),
                  img, img, ctx, ctx,
                  pl.BlockSpec(bias.shape, lambda b, r: (0, 0, 0, 0)),
                  pl.BlockSpec(memory_space=pl.ANY)],
        out_specs=pl.BlockSpec((blk, d_b), lambda b, r: (q0 + b * steps + r, 0)),
        out_shape=jax.ShapeDtypeStruct(yb.shape, BF16),
        input_output_aliases={6: 0},
        compiler_params=_cparams("arbitrary", "arbitrary"),
    )(q, kb, vb, ck, cv, bias, yb)


def _nat_bias(rpb):
    n_heads = rpb.shape[0]
    cols = jnp.arange(GRID_W)
    col_start = jnp.clip(cols - NA_WIN_C // 2, 0, GRID_W - NA_WIN_C)
    j = jnp.arange(GRID_W)
    valid = (j[None, :] >= col_start[:, None]) & (j[None, :] < col_start[:, None] + NA_WIN_C)
    col_off = jnp.clip(j[None, :] - cols[:, None] + (NA_WIN_C - 1), 0, 2 * NA_WIN_C - 2)
    toe = jnp.where(valid[None, None], rpb.astype(F32)[:, :, col_off], NEG_BIG)
    cases = jnp.stack([toe[:, NA_WIN_R - 1 - delta:2 * NA_WIN_R - 1 - delta] for delta in range(NA_WIN_R)])
    return cases.transpose(0, 1, 3, 2, 4).reshape(NA_WIN_R, n_heads // 2, 2 * GRID_W, NA_WIN_R * GRID_W)


def _s5_weights(a_re, a_im, log_dt, b_re, b_im, c_re, c_im, ssm_d):
    n_g = a_re.shape[1]
    c = S5_CHUNK
    lam = lax.complex(a_re.astype(F32), a_im.astype(F32))
    ldt = lam * jnp.exp(log_dt.astype(F32))[..., None]
    lam_bar = jnp.exp(ldt)
    b_bar = ((lam_bar - 1) / lam)[..., None] * lax.complex(b_re.astype(F32), b_im.astype(F32))
    c_mat = lax.complex(c_re.astype(F32), c_im.astype(F32))
    pw = jnp.exp(ldt[None] * jnp.arange(c + 1, dtype=F32)[:, None, None, None])
    kern = jnp.real(jnp.einsum('dgcp,kdgp,dgpe->dgkce', c_mat, pw[:c], b_bar))
    i = jnp.arange(c)
    lag = i[None, :] - i[:, None]
    tf = jnp.where((lag >= 0)[None, :, :, None, None], kern[0][:, jnp.clip(lag, 0, c - 1)], 0.0)
    tb = jnp.where((lag <= 0)[None, :, :, None, None], kern[1][:, jnp.clip(-lag, 0, c - 1)], 0.0)
    t_mat = (tf + tb).transpose(0, 1, 4, 2, 3).reshape(n_g, c * SSM_CH, c * SSM_CH)
    mf = pw[:c][::-1, 0][:, :, :, None] * b_bar[0][None]
    mb = pw[:c, 1][:, :, :, None] * b_bar[1][None]
    mf = mf.transpose(1, 0, 3, 2).reshape(n_g, c * SSM_CH, SSM_P)
    mb = mb.transpose(1, 0, 3, 2).reshape(n_g, c * SSM_CH, SSM_P)
    mq = jnp.stack([jnp.real(mf), jnp.imag(mf), jnp.real(mb), jnp.imag(mb)], axis=2)
    zf = c_mat[0][:, None] * pw[1:c + 1, 0][:, :, None, :].transpose(1, 0, 2, 3)
    zb = c_mat[1][:, None] * pw[1:c + 1, 1][::-1][:, :, None, :].transpose(1, 0, 2, 3)
    zf = zf.transpose(0, 3, 1, 2).reshape(n_g, SSM_P, c * SSM_CH)
    zb = zb.transpose(0, 3, 1, 2).reshape(n_g, SSM_P, c * SSM_CH)
    wq = jnp.stack([jnp.real(zf), -jnp.imag(zf), jnp.real(zb), -jnp.imag(zb)], axis=1)
    gh = LANES // SSM_CH
    n_half = n_g // gh
    m6 = mq.reshape(n_half, gh, c, SSM_CH, 4, SSM_P).astype(BF16)
    t6 = t_mat.reshape(n_half, gh, c, SSM_CH, c, SSM_CH).astype(BF16)
    w6 = wq.reshape(n_half, gh, 4, SSM_P, c, SSM_CH).astype(BF16)
    m_half = jnp.zeros((n_half, c, gh, SSM_CH, 4, gh, SSM_P), BF16)
    t_half = jnp.zeros((n_half, c, gh, SSM_CH, c, gh, SSM_CH), BF16)
    w_half = jnp.zeros((n_half, 4, gh, SSM_P, c, gh, SSM_CH), BF16)
    for g in range(gh):
        m_half = m_half.at[:, :, g, :, :, g, :].set(m6[:, g])
        t_half = t_half.at[:, :, g, :, :, g, :].set(t6[:, g])
        w_half = w_half.at[:, :, g, :, :, g, :].set(w6[:, g])
    m_half = m_half.reshape(n_half, c * LANES, 4 * gh * SSM_P)
    tw_half = jnp.concatenate([t_half.reshape(n_half, c * LANES, c * LANES),
                               w_half.reshape(n_half, 4 * gh * SSM_P, c * LANES)], axis=1)
    a_c = pw[c]
    a16 = jnp.stack([jnp.real(a_c[0]), jnp.imag(a_c[0]), jnp.real(a_c[1]), jnp.imag(a_c[1])])
    a16 = a16.reshape(4, 1, n_half, gh * SSM_P).transpose(2, 0, 1, 3)
    dvec = ssm_d.astype(F32).reshape(n_half, 1, LANES)
    return m_half, tw_half, a16, dvec


def _s5_state_body(u_ref, m_ref, fre_ref, fim_ref, bre_ref, bim_ref):
    nb, tn, kd = u_ref.shape
    w = fre_ref.shape[2]
    r = _dot(u_ref[...].reshape(nb * tn, kd).astype(BF16), m_ref[...])
    for q, o_ref in enumerate((fre_ref, fim_ref, bre_ref, bim_ref)):
        for b in range(nb):
            o_ref[:, b, :] = r[b * tn:(b + 1) * tn, q * w:(q + 1) * w]


def _s5_rows(nb, n):
    return max(8, min(n, 256 // nb))


def _s5_states(u3, m_half):
    nb, n, kd = u3.shape
    tn = _s5_rows(nb, n)
    w = m_half.shape[1] // 4
    out = pl.BlockSpec((tn, nb, w), lambda i: (i, 0, 0))
    return pl.pallas_call(
        _s5_state_body,
        name="s5_states",
        grid=(n // tn,),
        in_specs=[pl.BlockSpec((nb, tn, kd), lambda i: (0, i, 0)),
                  pl.BlockSpec(m_half.shape, lambda i: (0, 0))],
        out_specs=[out] * 4,
        out_shape=[jax.ShapeDtypeStruct((n, nb, w), F32)] * 4,
        compiler_params=_cparams("arbitrary"),
    )(u3, m_half)


def _s5_scan_body(sfr_ref, sfi_ref, sbr_ref, sbi_ref, a_ref, h0_ref,
                  hfr_ref, hfi_ref, hbr_ref, hbi_ref, fin_ref, st_ref):
    @pl.when(pl.program_id(0) == 0)
    def _():
        st_ref[...] = h0_ref[...]

    ks = sfr_ref.shape[0]
    afr, afi, abr, abi = a_ref[0], a_ref[1], a_ref[2], a_ref[3]

    def step(s, carry):
        fr, fi, br, bi = carry
        sb = ks - 1 - s
        hfr_ref[s] = fr
        hfi_ref[s] = fi
        hbr_ref[sb] = br
        hbi_ref[sb] = bi
        nfr = afr * fr - afi * fi + sfr_ref[s]
        nfi = afr * fi + afi * fr + sfi_ref[s]
        nbr = abr * br - abi * bi + sbr_ref[sb]
        nbi = abr * bi + abi * br + sbi_ref[sb]
        return nfr, nfi, nbr, nbi

    carry = lax.fori_loop(0, ks, step, (st_ref[0], st_ref[1], st_ref[2], st_ref[3]))
    for q in range(4):
        st_ref[q] = carry[q]
        fin_ref[q] = carry[q]


def _s5_scan(s3, a16, h0):
    n_chunks, n_batch, width = s3[0].shape
    ks = min(n_chunks, 32)
    nb = n_chunks // ks
    fwd = pl.BlockSpec((ks, n_batch, width), lambda i: (i, 0, 0))
    bwd = pl.BlockSpec((ks, n_batch, width), lambda i: (nb - 1 - i, 0, 0))
    small = lambda shape: pl.BlockSpec(shape, lambda i: (0, 0, 0))
    outs = pl.pallas_call(
        _s5_scan_body,
        name="s5_scan",
        grid=(nb,),
        in_specs=[fwd, fwd, bwd, bwd, small(a16.shape), small(h0.shape)],
        out_specs=[fwd, fwd, bwd, bwd, small(h0.shape)],
        out_shape=[jax.ShapeDtypeStruct((n_chunks, n_batch, width), F32)] * 4
        + [jax.ShapeDtypeStruct(h0.shape, F32)],
        scratch_shapes=[pltpu.VMEM(h0.shape, F32)],
        compiler_params=_cparams("arbitrary"),
    )(*s3, a16, h0)
    return outs[:4], outs[4]


def _s5_out_body(u_ref, us_ref, hfr_ref, hfi_ref, hbr_ref, hbi_ref, tw_ref, d_ref, y_ref):
    nb, tn, kd = u_ref.shape
    rows = nb * tn
    wo = us_ref.shape[2]
    y = _dot(u_ref[...].reshape(rows, kd).astype(BF16), tw_ref[0:kd, :])
    w = hfr_ref.shape[2]
    for q, h_ref in enumerate((hfr_ref, hfi_ref, hbr_ref, hbi_ref)):
        hq = jnp.concatenate([h_ref[:, b, :] for b in range(nb)], axis=0)
        y = y + _dot(hq.astype(BF16), tw_ref[kd + q * w:kd + (q + 1) * w, :])
    skip = jnp.concatenate([d_ref[...]] * (wo // LANES), axis=1) * us_ref[...].reshape(rows, wo)
    y_ref[...] = jax.nn.gelu(y + skip).reshape(nb, tn, wo)


def _s5_out(u3, h4, tw_half, dvec):
    nb, n, kd = u3.shape
    tn = _s5_rows(nb, n)
    wo = S5_OUT_STEPS * LANES
    w = h4[0].shape[2]
    hb = pl.BlockSpec((tn, nb, w), lambda i, j: (i, 0, 0))
    return pl.pallas_call(
        _s5_out_body,
        name="s5_out",
        grid=(n // tn, kd // wo),
        in_specs=[pl.BlockSpec((nb, tn, kd), lambda i, j: (0, i, 0)),
                  pl.BlockSpec((nb, tn, wo), lambda i, j: (0, i, j)),
                  hb, hb, hb, hb,
                  pl.BlockSpec((tw_half.shape[0], wo), lambda i, j: (0, j)),
                  pl.BlockSpec(dvec.shape, lambda i, j: (0, 0))],
        out_specs=pl.BlockSpec((nb, tn, wo), lambda i, j: (0, i, j)),
        out_shape=jax.ShapeDtypeStruct(u3.shape, F32),
        compiler_params=_cparams("arbitrary", "arbitrary"),
    )(u3, u3, *h4, tw_half, dvec)


def _s5(pc_halves, n_batch, seq, weights, h0):
    m_half, tw_half, a16, dvec = weights
    n_chunks = seq // S5_CHUNK
    ys, fins = [], []
    for h, pc in enumerate(pc_halves):
        w = m_half.shape[2] // 4
        u3 = pc.reshape(n_batch, n_chunks, S5_CHUNK * LANES)
        s4 = _s5_states(u3, m_half[h])
        h4, fin = _s5_scan(s4, a16[h], h0[:, :, h * w:(h + 1) * w])
        ys.append(_s5_out(u3, h4, tw_half[h], dvec[h]).reshape(n_batch * n_chunks, S5_CHUNK * LANES))
        fins.append(fin)
    return ys, jnp.concatenate(fins, axis=2)


def _post_body(x_ref, ya_ref, yb_ref, yc0_ref, yc1_ref, mod_ref, gpost_ref, gffn_ref, wglu_ref, bglu_ref, wo_ref,
               x1_ref, h2_ref, unfold_ref):
    x = x_ref[...]
    d = x.shape[1]
    d_a = ya_ref.shape[1]
    d_b = yb_ref.shape[1]
    for h, yc_ref in enumerate((yc0_ref, yc1_ref)):
        _unfold_chunks(unfold_ref.at[h], yc_ref)
    y = jnp.concatenate([unfold_ref[0], unfold_ref[1]], axis=1)
    glu = y * jax.nn.sigmoid(_dot(y.astype(BF16), wglu_ref[...]) + bglu_ref[...])
    mixed = (_dot(ya_ref[...], wo_ref[0:d_a, :]) + _dot(yb_ref[...], wo_ref[d_a:d_a + d_b, :])
             + _dot(glu.astype(BF16), wo_ref[d_a + d_b:, :]))
    x1 = x + mod_ref[0, :, 2 * d:3 * d] * _rms(mixed, gpost_ref[...])
    x1_ref[...] = x1
    h2 = _rms(x1, gffn_ref[...]) * (1 + mod_ref[0, :, 4 * d:5 * d]) + mod_ref[0, :, 3 * d:4 * d]
    _store_tiled(h2_ref, h2)


def _post(x, ya, yb, yc0, yc1, mod, g_post, g_ffn, w_glu_b, b_glu, w_out_b, row_map):
    t, d = x.shape
    tm = TOK_TILE
    nc = d // LANES
    tok = lambda n: pl.BlockSpec((tm, n), lambda i: (i, 0))
    fold = pl.BlockSpec((tm // S5_CHUNK, S5_CHUNK * LANES), lambda i: (i, 0))
    full = lambda a: pl.BlockSpec(a.shape, lambda i: (0,) * a.ndim)
    return pl.pallas_call(
        _post_body,
        name="post_mix",
        grid=(t // tm,),
        in_specs=[tok(d), tok(ya.shape[1]), tok(yb.shape[1]), fold, fold,
                  pl.BlockSpec((1, 1, mod.shape[2]), lambda i: (row_map(i), 0, 0)),
                  full(g_post), full(g_ffn), full(w_glu_b), full(b_glu), full(w_out_b)],
        out_specs=[tok(d), pl.BlockSpec((tm * nc, LANES), lambda i: (i, 0))],
        out_shape=[jax.ShapeDtypeStruct((t, d), F32), jax.ShapeDtypeStruct((t * nc, LANES), F32)],
        scratch_shapes=[pltpu.VMEM((2, tm, LANES), F32)],
        compiler_params=_cparams("arbitrary"),
    )(x, ya, yb, yc0, yc1, mod, g_post, g_ffn, w_glu_b, b_glu, w_out_b)


def _route_body(h_ref, whi_ref, wlo_ref, b_ref, idx_ref, gate_ref, pos_ref, cnt_ref, carry_ref):
    @pl.when(pl.program_id(0) == 0)
    def _():
        carry_ref[...] = jnp.zeros_like(carry_ref)

    tm = idx_ref.shape[1]
    h = _load_tiled(h_ref, tm)
    hi = h.astype(BF16)
    lo = (h - hi.astype(F32)).astype(BF16)
    logits = _dot_nt(whi_ref[...], hi) + (_dot_nt(whi_ref[...], lo) + _dot_nt(wlo_ref[...], hi))
    scores = jax.nn.sigmoid(logits)
    sel = scores + b_ref[...]
    gsz = N_EXPERTS // N_EXP_GROUPS
    within = lax.broadcasted_iota(I32, (gsz, tm), 0).astype(F32)
    grp = []
    for g in range(N_EXP_GROUPS):
        blk = sel[g * gsz:(g + 1) * gsz]
        m1 = jnp.max(blk, axis=0, keepdims=True)
        first = jnp.min(jnp.where(blk == m1, within, float(gsz)), axis=0, keepdims=True)
        m2 = jnp.max(jnp.where(within == first, -jnp.inf, blk), axis=0, keepdims=True)
        grp.append(m1 + m2)
    blocks = []
    for g in range(N_EXP_GROUPS):
        ahead = jnp.zeros((1, tm), F32)
        for o in range(N_EXP_GROUPS):
            if o == g:
                continue
            beats = (grp[o] >= grp[g]) if o < g else (grp[o] > grp[g])
            ahead = ahead + jnp.where(beats, 1.0, 0.0)
        ahead = jnp.broadcast_to(ahead, (gsz, tm))
        blocks.append(jnp.where(ahead < TOPK_GROUPS, sel[g * gsz:(g + 1) * gsz], -jnp.inf))
    v = jnp.concatenate(blocks, axis=0)
    eidx = lax.broadcasted_iota(I32, (N_EXPERTS, tm), 0)
    rnk = jnp.zeros((N_EXPERTS, tm), F32)
    for e in range(N_EXPERTS):
        row = v[e:e + 1]
        rnk = rnk + jnp.where(eidx > e, jnp.where(row >= v, 1.0, 0.0), jnp.where(row > v, 1.0, 0.0))
    chosen = rnk < TOP_K
    w = jnp.where(chosen, scores, 0.0)
    wn = w / jnp.sum(w, axis=0, keepdims=True) * ROUTE_SCALE
    tri = (lax.broadcasted_iota(I32, (tm, tm), 0) < lax.broadcasted_iota(I32, (tm, tm), 1))
    chosen_f = jnp.where(chosen, 1.0, 0.0)
    prefix = _dot(chosen_f.astype(BF16), jnp.where(tri, 1.0, 0.0).astype(BF16)) + carry_ref[:, 0:1]
    total = carry_ref[...] + jnp.sum(chosen_f, axis=1, keepdims=True)
    carry_ref[...] = total
    cnt_ref[...] = total
    eidx_f = eidx.astype(F32)
    for k in range(TOP_K):
        one = rnk == k
        idx_ref[k:k + 1, :] = jnp.sum(jnp.where(one, eidx_f, 0.0), axis=0, keepdims=True).astype(I32)
        gate_ref[k:k + 1, :] = jnp.sum(jnp.where(one, wn, 0.0), axis=0, keepdims=True)
        pos_ref[k:k + 1, :] = jnp.sum(jnp.where(one, prefix, 0.0), axis=0, keepdims=True).astype(I32)


def _route(h2t, w_hi, w_lo, b_router):
    nc = w_hi.shape[1] // LANES
    t = h2t.shape[0] // nc
    tm = TOK_TILE
    out = pl.BlockSpec((TOP_K, tm), lambda i: (0, i))
    full = lambda a: pl.BlockSpec(a.shape, lambda i: (0, 0))
    return pl.pallas_call(
        _route_body,
        name="route",
        grid=(t // tm,),
        in_specs=[pl.BlockSpec((tm * nc, LANES), lambda i: (i, 0)), full(w_hi), full(w_lo), full(b_router)],
        out_specs=[out, out, out, pl.BlockSpec((N_EXPERTS, LANES), lambda i: (0, 0))],
        out_shape=[jax.ShapeDtypeStruct((TOP_K, t), I32), jax.ShapeDtypeStruct((TOP_K, t), F32),
                   jax.ShapeDtypeStruct((TOP_K, t), I32), jax.ShapeDtypeStruct((N_EXPERTS, LANES), F32)],
        scratch_shapes=[pltpu.VMEM((N_EXPERTS, LANES), F32)],
        compiler_params=_cparams("arbitrary"),
    )(h2t, w_hi, w_lo, b_router)


def _dest_body(starts_ref, idx_ref, pos_ref, dest_ref):
    idx = idx_ref[...]
    acc = pos_ref[...]
    for e in range(N_EXPERTS):
        acc = acc + jnp.where(idx == e, starts_ref[e], 0)
    dest_ref[...] = acc


def _dest(starts, idx, pos):
    t = idx.shape[1]
    tile = math.gcd(t, 4096)
    blk = pl.BlockSpec((TOP_K, tile), lambda i, st: (0, i))
    return pl.pallas_call(
        _dest_body,
        name="moe_dest",
        grid_spec=pltpu.PrefetchScalarGridSpec(num_scalar_prefetch=1, grid=(t // tile,),
                                               in_specs=[blk, blk], out_specs=blk),
        out_shape=jax.ShapeDtypeStruct(idx.shape, I32),
        compiler_params=_cparams("arbitrary"),
    )(starts, idx, pos)


def _dispatch_body(pad_lo_ref, pad_hi_ref, dest_ref, h_ref, xs_ref, zero_ref, sem, zsem, *, n_pad_rows):
    tile = dest_ref.shape[1]

    def zero_copy(r):
        return pltpu.make_async_copy(zero_ref, xs_ref.at[r], zsem)

    @pl.when(pl.program_id(0) == 0)
    def _():
        zero_ref[...] = jnp.zeros_like(zero_ref)

        def per_expert(e, c):
            def one(r, c2):
                zero_copy(r).start()
                return c2
            return lax.fori_loop(pad_lo_ref[e], pad_hi_ref[e], one, c)

        lax.fori_loop(0, N_EXPERTS, per_expert, 0)

    def row_copy(t, k):
        return pltpu.make_async_copy(h_ref.at[t], xs_ref.at[dest_ref[k, t]], sem)

    def start(t, c):
        for k in range(TOP_K):
            row_copy(t, k).start(priority=k % DMA_THREADS)
        return c

    def wait(t, c):
        for k in range(TOP_K):
            row_copy(t, k).wait()
        return c

    lax.fori_loop(0, tile, start, 0)
    lax.fori_loop(0, tile, wait, 0)

    @pl.when(pl.program_id(0) == 0)
    def _():
        def one(r, c):
            zero_copy(0).wait()
            return c
        lax.fori_loop(0, n_pad_rows, one, 0)


def _dispatch(pad_lo, pad_hi, dest, h2t, n_rows):
    t, nc, _ = h2t.shape
    tile = DISPATCH_TILE
    grid_spec = pltpu.PrefetchScalarGridSpec(
        num_scalar_prefetch=2,
        grid=(t // tile,),
        in_specs=[pl.BlockSpec((TOP_K, tile), lambda i, lo, hi: (0, i), memory_space=pltpu.SMEM),
                  pl.BlockSpec((tile, nc, LANES), lambda i, lo, hi: (i, 0, 0))],
        out_specs=pl.BlockSpec(memory_space=pl.ANY),
        scratch_shapes=[pltpu.VMEM((nc, LANES), F32), pltpu.SemaphoreType.DMA(()),
                        pltpu.SemaphoreType.DMA(())],
    )
    return pl.pallas_call(
        functools.partial(_dispatch_body, n_pad_rows=n_rows - t * TOP_K),
        name="moe_dispatch",
        grid_spec=grid_spec,
        out_shape=jax.ShapeDtypeStruct((n_rows, nc, LANES), F32),
        compiler_params=_cparams("arbitrary"),
    )(pad_lo, pad_hi, dest, h2t)


def _experts_body(be_ref, x_ref, wg_ref, wu_ref, wd_ref, y_ref, wgb_ref, wub_ref, wdb_ref):
    i = pl.program_id(0)

    @pl.when((i == 0) | (be_ref[i] != be_ref[jnp.maximum(i - 1, 0)]))
    def _():
        wgb_ref[...] = wg_ref[0, 0].astype(BF16)
        wub_ref[...] = wu_ref[0, 0].astype(BF16)
        wdb_ref[...] = wd_ref[0, 0].astype(BF16)

    xb = _load_tiled(x_ref, MOE_ROWS).astype(BF16)
    hb = jax.nn.silu(_dot(xb, wgb_ref[...])) * _dot(xb, wub_ref[...])
    _store_tiled(y_ref, _dot(hb.astype(BF16), wdb_ref[...]))


def _experts(blk_exp, xs, wg, wu, wd, layer):
    d, de = wg.shape[2], wg.shape[3]
    nc = d // LANES
    bm = MOE_ROWS
    tiles = pl.BlockSpec((bm * nc, LANES), lambda i, be: (i, 0))
    grid_spec = pltpu.PrefetchScalarGridSpec(
        num_scalar_prefetch=1,
        grid=(xs.shape[0] // (bm * nc),),
        in_specs=[tiles,
                  pl.BlockSpec((1, 1, d, de), lambda i, be: (layer, be[i], 0, 0)),
                  pl.BlockSpec((1, 1, d, de), lambda i, be: (layer, be[i], 0, 0)),
                  pl.BlockSpec((1, 1, de, d), lambda i, be: (layer, be[i], 0, 0))],
        out_specs=tiles,
        scratch_shapes=[pltpu.VMEM((d, de), BF16), pltpu.VMEM((d, de), BF16), pltpu.VMEM((de, d), BF16)],
    )
    return pl.pallas_call(
        _experts_body,
        name="moe_experts",
        grid_spec=grid_spec,
        out_shape=jax.ShapeDtypeStruct(xs.shape, F32),
        compiler_params=_cparams("arbitrary"),
    )(blk_exp, xs, wg, wu, wd)


def _combine_body(dest_ref, next_ref, gate_ref, h_ref, x_ref, mod_ref, g_ref, wsg_ref, wsu_ref, wsd_ref, ys_hbm,
                  o_ref, buf_ref, sem):
    tile, d = x_ref.shape
    nc = d // LANES
    i = pl.program_id(0)
    slot = i % 2

    def row_copy(d_ref, s, t, k):
        return pltpu.make_async_copy(ys_hbm.at[d_ref[k, t]],
                                     buf_ref.at[s, k, pl.ds(pl.multiple_of(t * nc, nc), nc)], sem.at[s])

    def gather(d_ref, s):
        def start(t, c):
            for k in range(TOP_K):
                row_copy(d_ref, s, t, k).start(priority=k % DMA_THREADS)
            return c
        lax.fori_loop(0, tile, start, 0)

    @pl.when(i == 0)
    def _():
        gather(dest_ref, 0)

    @pl.when(i + 1 < pl.num_programs(0))
    def _():
        gather(next_ref, 1 - slot)

    hb = _load_tiled(h_ref, tile).astype(BF16)
    acc = _dot((jax.nn.silu(_dot(hb, wsg_ref[...])) * _dot(hb, wsu_ref[...])).astype(BF16), wsd_ref[...])

    def wait(t, c):
        for k in range(TOP_K):
            row_copy(dest_ref, slot, 0, k).wait()
        return c

    lax.fori_loop(0, tile, wait, 0)
    gates = gate_ref[...]
    moe = gates[:, 0:1] * _load_tiled(buf_ref.at[slot, 0], tile)
    for k in range(1, TOP_K):
        moe = moe + gates[:, k:k + 1] * _load_tiled(buf_ref.at[slot, k], tile)
    o_ref[...] = x_ref[...] + mod_ref[0, :, 5 * d:6 * d] * _rms(moe + acc, g_ref[...])


def _combine(dest, gates, h2t, x1, mod, g_post, wsg, wsu, wsd, ys, row_map):
    t, d = x1.shape
    nc = d // LANES
    tile = COMBINE_TILE
    per_tok = TOK_TILE // tile
    n_steps = t // tile
    tok = lambda n: pl.BlockSpec((tile, n), lambda i: (i, 0))
    full = lambda a: pl.BlockSpec(a.shape, lambda i: (0,) * a.ndim)
    return pl.pallas_call(
        _combine_body,
        name="moe_combine",
        grid=(n_steps,),
        in_specs=[pl.BlockSpec((TOP_K, tile), lambda i: (0, i), memory_space=pltpu.SMEM),
                  pl.BlockSpec((TOP_K, tile), lambda i: (0, jnp.minimum(i + 1, n_steps - 1)),
                               memory_space=pltpu.SMEM),
                  tok(TOP_K), pl.BlockSpec((tile * nc, LANES), lambda i: (i, 0)), tok(d),
                  pl.BlockSpec((1, 1, mod.shape[2]), lambda i: (row_map(i // per_tok), 0, 0)),
                  full(g_post), full(wsg), full(wsu), full(wsd),
                  pl.BlockSpec(memory_space=pl.ANY)],
        out_specs=tok(d),
        out_shape=jax.ShapeDtypeStruct((t, d), F32),
        scratch_shapes=[pltpu.VMEM((2, TOP_K, tile * nc, LANES), F32), pltpu.SemaphoreType.DMA((2,))],
        compiler_params=_cparams("arbitrary"),
    )(dest, dest, gates, h2t, x1, mod, g_post, wsg, wsu, wsd, ys)


def _moe(h2t, x1, mod, p, row_map):
    t, d = x1.shape
    nc = d // LANES
    idx, gate, pos, cnt = _route(h2t, p['wr_hi'], p['wr_lo'], p['b_router'])
    counts = cnt[:, 0].astype(I32)
    padded = (counts + MOE_ROWS - 1) // MOE_ROWS * MOE_ROWS
    ends = jnp.cumsum(padded)
    starts = ends - padded
    n_blocks = (t * TOP_K + N_EXPERTS * (MOE_ROWS - 1)) // MOE_ROWS + 1
    n_rows = n_blocks * MOE_ROWS
    blk_start = jnp.arange(n_blocks, dtype=I32) * MOE_ROWS
    blk_exp = jnp.minimum(jnp.sum((ends[None, :] <= blk_start[:, None]).astype(I32), axis=1), N_EXPERTS - 1)
    pad_hi = ends.at[N_EXPERTS - 1].set(n_rows)
    dest = _dest(starts, idx, pos)
    xs = _dispatch(starts + counts, pad_hi, dest, h2t.reshape(t, nc, LANES), n_rows)
    ys = _experts(blk_exp, xs.reshape(n_rows * nc, LANES), p['w_e_gate'], p['w_e_up'], p['w_e_down'], p['layer'])
    return _combine(dest, gate.T, h2t, x1, mod, p['g_post_ffn'], p['w_s_gate'], p['w_s_up'], p['w_s_down'],
                    ys.reshape(n_rows, nc, LANES), row_map)


def kernel(x_prompt, x_sample, cache_k, cache_v, state_ssm_re, state_ssm_im, c, c_ctx,
           g_pre_mix, g_post_mix, g_pre_ffn, g_post_ffn, w_ada, b_ada, w_in, w_out,
           sgu_g, w_sp, b_sp, rpb, ssm_a_re, ssm_a_im, ssm_log_dt, ssm_b_re, ssm_b_im,
           ssm_c_re, ssm_c_im, ssm_d, w_glu, b_glu, w_router, b_router,
           w_e_gate, w_e_up, w_e_down, w_s_gate, w_s_up, w_s_down):
    n_pb, p_seq, d = x_prompt.shape
    n_sb, s_seq, _ = x_sample.shape
    depth = w_in.shape[0]
    d_a = sgu_g.shape[1]
    d_c = w_glu.shape[1]
    d_b = d - d_a - d_c
    n_heads = d_b // HD_B
    n_g = d_c // SSM_CH
    t_p = n_pb * p_seq
    assert p_seq % CHUNK == 0 and t_p % TOK_TILE == 0 and s_seq % TOK_TILE == 0 and t_p % s_seq == 0
    assert s_seq % (NAT_ROWS * GRID_W) == 0 and s_seq // GRID_W >= NA_WIN_R and d_c == 2 * LANES

    mod_rows = -(-(n_sb + 1) // 8) * 8
    c_all = jnp.zeros((mod_rows, d), F32).at[:n_sb].set(c).at[n_sb].set(c_ctx)
    mod_all = _ada(c_all, w_ada, b_ada)
    p_tiles = t_p // TOK_TILE
    s_tiles = s_seq // TOK_TILE

    def row_map(i):
        return jnp.where(i < p_tiles, n_sb, (i - p_tiles) // s_tiles)

    x = jnp.concatenate([x_prompt.reshape(t_p, d), x_sample.reshape(n_sb * s_seq, d)], axis=0)
    new_k, new_v, new_re, new_im = [], [], [], []
    for l in range(depth):
        mod = mod_all[l][:, None, :]
        row = lambda a: a[l][None, :].astype(F32)
        pa, q, k32, v32, kb, vb, *pc = _premix(x, mod, row(g_pre_mix), w_in[l].astype(BF16), row_map,
                                               d_a, d_b, d_c)
        bias_a = jnp.repeat(b_sp[l].T.astype(F32), d_a // NH_A, axis=1)
        ya = _chunk_mlp(pa, row(sgu_g), w_sp[l].reshape(NH_A * CHUNK, CHUNK).astype(BF16), bias_a)
        yb_p = _ctx_attn(q, k32, v32, n_pb, p_seq)
        ck = cache_k[:, l].reshape(n_sb, -1, d_b).astype(BF16)
        cv = cache_v[:, l].reshape(n_sb, -1, d_b).astype(BF16)
        yb = _nat_attn(q, kb, vb, ck, cv, _nat_bias(rpb[l]), yb_p, n_sb, s_seq, t_p)
        s5w = _s5_weights(ssm_a_re[l], ssm_a_im[l], ssm_log_dt[l], ssm_b_re[l], ssm_b_im[l],
                          ssm_c_re[l], ssm_c_im[l], ssm_d[l])
        c_p = t_p // S5_CHUNK
        yc_p, fin_p = _s5([h[:c_p] for h in pc], n_pb, p_seq, s5w, jnp.zeros((4, n_pb, n_g * SSM_P), F32))
        sre = state_ssm_re[:, l].astype(F32).reshape(n_sb, 2, n_g * SSM_P)
        sim = state_ssm_im[:, l].astype(F32).reshape(n_sb, 2, n_g * SSM_P)
        h0_s = jnp.stack([sre[:, 0], sim[:, 0], sre[:, 1], sim[:, 1]])
        yc_s, _ = _s5([h[c_p:] for h in pc], n_sb, s_seq, s5w, h0_s)
        yc = [jnp.concatenate([a, b], axis=0) for a, b in zip(yc_p, yc_s)]
        x1, h2 = _post(x, ya, yb, yc[0], yc[1],
                       mod, row(g_post_mix), row(g_pre_ffn), w_glu[l].astype(BF16), row(b_glu),
                       w_out[l].astype(BF16), row_map)
        wr = w_router[l].astype(F32).T
        wr_hi = wr.astype(BF16)
        moe_p = {
            'wr_hi': wr_hi, 'wr_lo': (wr - wr_hi.astype(F32)).astype(BF16),
            'b_router': b_router[l].astype(F32)[:, None],
            'w_e_gate': w_e_gate, 'w_e_up': w_e_up, 'w_e_down': w_e_down, 'layer': l,
            'w_s_gate': w_s_gate[l].astype(BF16), 'w_s_up': w_s_up[l].astype(BF16),
            'w_s_down': w_s_down[l].astype(BF16), 'g_post_ffn': row(g_post_ffn),
        }
        x = _moe(h2, x1, mod, moe_p, row_map)
        new_k.append(k32[:t_p].reshape(n_pb, p_seq, n_heads, HD_B))
        new_v.append(v32[:t_p].reshape(n_pb, p_seq, n_heads, HD_B))
        fin_p = fin_p.reshape(4, n_pb, n_g, SSM_P)
        new_re.append(jnp.stack([fin_p[0], fin_p[2]], axis=1))
        new_im.append(jnp.stack([fin_p[1], fin_p[3]], axis=1))
    return (x[:t_p].reshape(n_pb, p_seq, d), x[t_p:].reshape(n_sb, s_seq, d),
            jnp.stack(new_k, axis=1), jnp.stack(new_v, axis=1),
            jnp.stack(new_re, axis=1), jnp.stack(new_im, axis=1))
```

```python
import functools
import math

import jax
import jax.numpy as jnp
from jax import lax
from jax.experimental import pallas as pl
from jax.experimental.pallas import tpu as pltpu

F32 = jnp.float32
BF16 = jnp.bfloat16
I32 = jnp.int32

GRID_W = 64
EPS = 1e-6
NH_A = 4
CHUNK = 128
HD_B = 64
NA_WIN_R = 8
NA_WIN_C = 16
ATTN_SCALE = HD_B ** -0.5
SSM_CH = 16
SSM_P = 64
N_EXPERTS = 64
TOP_K = 8
N_EXP_GROUPS = 8
TOPK_GROUPS = 4
ROUTE_SCALE = 2.5

LANES = 128
TOK_TILE = 512
S5_CHUNK = 16
S5_OUT_STEPS = 4
MOE_ROWS = 512
COMBINE_TILE = 128
DISPATCH_TILE = 512
NAT_ROWS = 4
DMA_THREADS = 2
NEG_BIG = -1e30
VMEM_LIMIT = 48 * 1024 * 1024


def _cparams(*sem):
    return pltpu.CompilerParams(dimension_semantics=sem, vmem_limit_bytes=VMEM_LIMIT)


def _dot(a, b):
    return jnp.dot(a, b, preferred_element_type=F32)


def _dot_nt(a, b):
    return lax.dot_general(a, b, (((1,), (1,)), ((), ())), preferred_element_type=F32)


def _rms(x, g):
    return x * lax.rsqrt(jnp.mean(x * x, axis=-1, keepdims=True) + EPS) * g


def _load_tiled(ref, n_rows):
    nc = ref.shape[0] // n_rows
    return jnp.concatenate([ref[pl.ds(j, n_rows, stride=nc), :] for j in range(nc)], axis=1)


def _store_tiled(ref, val):
    n_rows = val.shape[0]
    nc = ref.shape[0] // n_rows
    for j in range(nc):
        ref[pl.ds(j, n_rows, stride=nc), :] = val[:, j * LANES:(j + 1) * LANES]


U32 = jnp.uint32


def _pack_halves(x):
    half = x.shape[1] // 2
    lo = lax.bitcast_convert_type(x[:, :half].astype(BF16).astype(F32), U32)
    hi = lax.bitcast_convert_type(x[:, half:].astype(BF16).astype(F32), U32)
    return (lo >> 16) | hi


def _unpack_halves(w):
    lo = lax.bitcast_convert_type(w << 16, F32)
    hi = lax.bitcast_convert_type(w & jnp.uint32(0xFFFF0000), F32)
    return jnp.concatenate([lo, hi], axis=1)


def _ada_body(c_ref, w_ref, b_ref, o_ref):
    s = jax.nn.silu(c_ref[...]).astype(BF16)
    o_ref[0] = _dot(s, w_ref[0].astype(BF16)) + b_ref[0]


def _ada(c_all, w_ada, b_ada):
    n_layers, d, n = w_ada.shape
    rows = c_all.shape[0]
    tn = 1536
    return pl.pallas_call(
        _ada_body,
        name="ada",
        grid=(n_layers, n // tn),
        in_specs=[pl.BlockSpec((rows, d), lambda l, j: (0, 0)),
                  pl.BlockSpec((1, d, tn), lambda l, j: (l, 0, j)),
                  pl.BlockSpec((1, 1, tn), lambda l, j: (l, 0, j))],
        out_specs=pl.BlockSpec((1, rows, tn), lambda l, j: (l, 0, j)),
        out_shape=jax.ShapeDtypeStruct((n_layers, rows, n), F32),
        compiler_params=_cparams("arbitrary", "arbitrary"),
    )(c_all, w_ada, b_ada.reshape(n_layers, 1, n))


def _premix_body(x_ref, mod_ref, g_ref, w_ref, pa_ref, q_ref, k_ref, v_ref, kb_ref, vb_ref, pc0_ref, pc1_ref,
                 fold_ref):
    x = x_ref[...]
    d = x.shape[1]
    h = _rms(x, g_ref[...]) * (1 + mod_ref[0, :, d:2 * d]) + mod_ref[0, :, 0:d]
    p = _dot(h.astype(BF16), w_ref[...])
    d_a2 = pa_ref.shape[1]
    d_b = q_ref.shape[1]
    o = d_a2
    pa_ref[...] = p[:, 0:o]
    q_ref[...] = p[:, o:o + d_b].astype(BF16)
    k = p[:, o + d_b:o + 2 * d_b]
    v = p[:, o + 2 * d_b:o + 3 * d_b]
    k_ref[...] = k
    v_ref[...] = v
    kb_ref[...] = k.astype(BF16)
    vb_ref[...] = v.astype(BF16)
    for h, pc_ref in enumerate((pc0_ref, pc1_ref)):
        fold_ref[...] = p[:, o + 3 * d_b + h * LANES:o + 3 * d_b + (h + 1) * LANES]
        _fold_chunks(pc_ref, fold_ref)


def _fold_chunks(dst_ref, src_ref):
    n = dst_ref.shape[0]
    for i in range(S5_CHUNK):
        dst_ref[:, i * LANES:(i + 1) * LANES] = src_ref[pl.ds(i, n, stride=S5_CHUNK), :]


def _unfold_chunks(dst_ref, src_ref):
    n = src_ref.shape[0]
    for i in range(S5_CHUNK):
        dst_ref[pl.ds(i, n, stride=S5_CHUNK), :] = src_ref[:, i * LANES:(i + 1) * LANES]


def _premix(x, mod, g, w_in_b, row_map, d_a, d_b, d_c):
    t, d = x.shape
    tm = TOK_TILE
    d_in = w_in_b.shape[1]
    tok = lambda n: pl.BlockSpec((tm, n), lambda i: (i, 0))
    fold = pl.BlockSpec((tm // S5_CHUNK, S5_CHUNK * LANES), lambda i: (i, 0))
    return pl.pallas_call(
        _premix_body,
        name="premix",
        grid=(t // tm,),
        in_specs=[tok(d),
                  pl.BlockSpec((1, 1, mod.shape[2]), lambda i: (row_map(i), 0, 0)),
                  pl.BlockSpec((1, d), lambda i: (0, 0)),
                  pl.BlockSpec((d, d_in), lambda i: (0, 0))],
        out_specs=[tok(2 * d_a), tok(d_b), tok(d_b), tok(d_b), tok(d_b), tok(d_b), fold, fold],
        out_shape=[jax.ShapeDtypeStruct((t, 2 * d_a), F32),
                   jax.ShapeDtypeStruct((t, d_b), BF16),
                   jax.ShapeDtypeStruct((t, d_b), F32),
                   jax.ShapeDtypeStruct((t, d_b), F32),
                   jax.ShapeDtypeStruct((t, d_b), BF16),
                   jax.ShapeDtypeStruct((t, d_b), BF16)]
        + [jax.ShapeDtypeStruct((t // S5_CHUNK, S5_CHUNK * LANES), F32)] * 2,
        scratch_shapes=[pltpu.VMEM((tm, LANES), F32)],
        compiler_params=_cparams("arbitrary"),
    )(x, mod, g, w_in_b)


def _chunk_body(pa_ref, g_ref, w_ref, b_ref, o_ref):
    z = jax.nn.gelu(pa_ref[...])
    d_a = o_ref.shape[1]
    hd = d_a // NH_A
    u = z[:, :d_a]
    v = z[:, d_a:]
    mu = jnp.mean(v, axis=-1, keepdims=True)
    var = jnp.mean(jnp.square(v - mu), axis=-1, keepdims=True)
    vb = ((v - mu) * lax.rsqrt(var + EPS) * g_ref[...]).astype(BF16)
    head = lax.broadcasted_iota(I32, (CHUNK, d_a), 1) // hd
    for ch in range(pa_ref.shape[0] // CHUNK):
        rows = slice(ch * CHUNK, (ch + 1) * CHUNK)
        sf = _dot(w_ref[...], vb[rows])
        s = b_ref[...]
        for h in range(NH_A):
            s = s + jnp.where(head == h, sf[h * CHUNK:(h + 1) * CHUNK], 0.0)
        o_ref[rows, :] = (u[rows] * s).astype(BF16)


def _chunk_mlp(pa, sgu_g, w_sp_b, bias):
    t, d2 = pa.shape
    d_a = d2 // 2
    tm = TOK_TILE
    return pl.pallas_call(
        _chunk_body,
        name="chunk_mlp",
        grid=(t // tm,),
        in_specs=[pl.BlockSpec((tm, d2), lambda i: (i, 0)),
                  pl.BlockSpec((1, d_a), lambda i: (0, 0)),
                  pl.BlockSpec(w_sp_b.shape, lambda i: (0, 0)),
                  pl.BlockSpec(bias.shape, lambda i: (0, 0))],
        out_specs=pl.BlockSpec((tm, d_a), lambda i: (i, 0)),
        out_shape=jax.ShapeDtypeStruct((t, d_a), BF16),
        compiler_params=_cparams("arbitrary"),
    )(pa, sgu_g, w_sp_b, bias)


def _stack_pair(qg):
    lane = lax.broadcasted_iota(I32, qg.shape, 1)
    zero = jnp.zeros_like(qg)
    return jnp.concatenate([jnp.where(lane < HD_B, qg, zero), jnp.where(lane >= HD_B, qg, zero)], axis=0)


def _unstack_pair(o2):
    n = o2.shape[0] // 2
    lane = lax.broadcasted_iota(I32, (n, o2.shape[1]), 1)
    return jnp.where(lane < HD_B, o2[:n], o2[n:])


def _ctx_attn_body(q_ref, k_ref, v_ref, o_ref):
    for g in range(q_ref.shape[1] // LANES):
        cols = slice(g * LANES, (g + 1) * LANES)
        q2 = _stack_pair(q_ref[:, cols])
        s = _dot_nt(q2, k_ref[:, cols].astype(BF16)) * ATTN_SCALE
        e = jnp.exp(s - jnp.max(s, axis=-1, keepdims=True))
        p = e / jnp.sum(e, axis=-1, keepdims=True)
        o2 = _dot(p.astype(BF16), v_ref[:, cols].astype(BF16))
        o_ref[:, cols] = _unstack_pair(o2).astype(BF16)


def _ctx_attn(q, k, v, n_batch, seq):
    d_b = q.shape[1]
    blk = pl.BlockSpec((seq, d_b), lambda b: (b, 0))
    return pl.pallas_call(
        _ctx_attn_body,
        name="ctx_attn",
        grid=(n_batch,),
        in_specs=[blk, blk, blk],
        out_specs=blk,
        out_shape=jax.ShapeDtypeStruct(q.shape, BF16),
        compiler_params=_cparams("arbitrary"),
    )(q, k, v)


def _nat_body(q_ref, k_ref, v_ref, ck_ref, cv_ref, bias_ref, yb_hbm, o_ref, *, rows):
    del yb_hbm
    r0 = pl.program_id(1) * NAT_ROWS
    n_win = NA_WIN_R * GRID_W
    pr = 2 * GRID_W
    starts, cases = [], []
    for i in range(NAT_ROWS):
        rs = jnp.clip(r0 + i - NA_WIN_R // 2, 0, rows - NA_WIN_R)
        starts.append(pl.multiple_of(rs * GRID_W, GRID_W))
        cases.append(r0 + i - rs)
    for g in range(q_ref.shape[1] // LANES):
        cols = slice(g * LANES, (g + 1) * LANES)
        q2 = jnp.concatenate([_stack_pair(q_ref[i * GRID_W:(i + 1) * GRID_W, cols]) for i in range(NAT_ROWS)],
                             axis=0)
        s_ctx = _dot_nt(q2, ck_ref[0, :, cols]) * ATTN_SCALE
        m_ctx = jnp.max(s_ctx, axis=-1, keepdims=True)
        e_wins, invs, ms = [], [], []
        for i in range(NAT_ROWS):
            s_win = (_dot_nt(q2[i * pr:(i + 1) * pr], k_ref[pl.ds(starts[i], n_win), cols]) * ATTN_SCALE
                     + bias_ref[cases[i], g])
            m = jnp.maximum(jnp.max(s_win, axis=-1, keepdims=True), m_ctx[i * pr:(i + 1) * pr])
            e_wins.append(jnp.exp(s_win - m))
            ms.append(m)
        e_ctx = jnp.exp(s_ctx - jnp.concatenate(ms, axis=0))
        l_ctx = jnp.sum(e_ctx, axis=-1, keepdims=True)
        for i in range(NAT_ROWS):
            invs.append(1.0 / (jnp.sum(e_wins[i], axis=-1, keepdims=True) + l_ctx[i * pr:(i + 1) * pr]))
        o_ctx = _dot((e_ctx * jnp.concatenate(invs, axis=0)).astype(BF16), cv_ref[0, :, cols])
        for i in range(NAT_ROWS):
            o2 = _dot((e_wins[i] * invs[i]).astype(BF16), v_ref[pl.ds(starts[i], n_win), cols])
            o_ref[i * GRID_W:(i + 1) * GRID_W, cols] = _unstack_pair(o2 + o_ctx[i * pr:(i + 1) * pr]).astype(BF16)


def _nat_attn(q, kb, vb, ck, cv, bias, yb, n_batch, seq, tok0):
    d_b = q.shape[1]
    rows = seq // GRID_W
    lc = ck.shape[1]
    blk = NAT_ROWS * GRID_W
    steps = seq // blk
    q0 = tok0 // blk
    i0 = tok0 // seq
    img = pl.BlockSpec((seq, d_b), lambda b, r: (i0 + b, 0))
    ctx = pl.BlockSpec((1, lc, d_b), lambda b, r: (b, 0, 0))
    return pl.pallas_call(
        functools.partial(_nat_body, rows=rows),
        name="nat_attn",
        grid=(n_batch, steps),
        in_specs=[pl.BlockSpec((blk, d_b), lambda b, r: (q0 + b * steps + r, 0)),
                  img, img, ctx, ctx,
                  pl.BlockSpec(bias.shape, lambda b, r: (0, 0, 0, 0)),
                  pl.BlockSpec(memory_space=pl.ANY)],
        out_specs=pl.BlockSpec((blk, d_b), lambda b, r: (q0 + b * steps + r, 0)),
        out_shape=jax.ShapeDtypeStruct(yb.shape, BF16),
        input_output_aliases={6: 0},
        compiler_params=_cparams("arbitrary", "arbitrary"),
    )(q, kb, vb, ck, cv, bias, yb)


def _nat_bias(rpb):
    n_heads = rpb.shape[0]
    cols = jnp.arange(GRID_W)
    col_start = jnp.clip(cols - NA_WIN_C // 2, 0, GRID_W - NA_WIN_C)
    j = jnp.arange(GRID_W)
    valid = (j[None, :] >= col_start[:, None]) & (j[None, :] < col_start[:, None] + NA_WIN_C)
    col_off = jnp.clip(j[None, :] - cols[:, None] + (NA_WIN_C - 1), 0, 2 * NA_WIN_C - 2)
    toe = jnp.where(valid[None, None], rpb.astype(F32)[:, :, col_off], NEG_BIG)
    cases = jnp.stack([toe[:, NA_WIN_R - 1 - delta:2 * NA_WIN_R - 1 - delta] for delta in range(NA_WIN_R)])
    return cases.transpose(0, 1, 3, 2, 4).reshape(NA_WIN_R, n_heads // 2, 2 * GRID_W, NA_WIN_R * GRID_W)


def _s5_weights(a_re, a_im, log_dt, b_re, b_im, c_re, c_im, ssm_d):
    n_g = a_re.shape[1]
    c = S5_CHUNK
    lam = lax.complex(a_re.astype(F32), a_im.astype(F32))
    ldt = lam * jnp.exp(log_dt.astype(F32))[..., None]
    lam_bar = jnp.exp(ldt)
    b_bar = ((lam_bar - 1) / lam)[..., None] * lax.complex(b_re.astype(F32), b_im.astype(F32))
    c_mat = lax.complex(c_re.astype(F32), c_im.astype(F32))
    pw = jnp.exp(ldt[None] * jnp.arange(c + 1, dtype=F32)[:, None, None, None])
    kern = jnp.real(jnp.einsum('dgcp,kdgp,dgpe->dgkce', c_mat, pw[:c], b_bar))
    i = jnp.arange(c)
    lag = i[None, :] - i[:, None]
    tf = jnp.where((lag >= 0)[None, :, :, None, None], kern[0][:, jnp.clip(lag, 0, c - 1)], 0.0)
    tb = jnp.where((lag <= 0)[None, :, :, None, None], kern[1][:, jnp.clip(-lag, 0, c - 1)], 0.0)
    t_mat = (tf + tb).transpose(0, 1, 4, 2, 3).reshape(n_g, c * SSM_CH, c * SSM_CH)
    mf = pw[:c][::-1, 0][:, :, :, None] * b_bar[0][None]
    mb = pw[:c, 1][:, :, :, None] * b_bar[1][None]
    mf = mf.transpose(1, 0, 3, 2).reshape(n_g, c * SSM_CH, SSM_P)
    mb = mb.transpose(1, 0, 3, 2).reshape(n_g, c * SSM_CH, SSM_P)
    mq = jnp.stack([jnp.real(mf), jnp.imag(mf), jnp.real(mb), jnp.imag(mb)], axis=2)
    zf = c_mat[0][:, None] * pw[1:c + 1, 0][:, :, None, :].transpose(1, 0, 2, 3)
    zb = c_mat[1][:, None] * pw[1:c + 1, 1][::-1][:, :, None, :].transpose(1, 0, 2, 3)
    zf = zf.transpose(0, 3, 1, 2).reshape(n_g, SSM_P, c * SSM_CH)
    zb = zb.transpose(0, 3, 1, 2).reshape(n_g, SSM_P, c * SSM_CH)
    wq = jnp.stack([jnp.real(zf), -jnp.imag(zf), jnp.real(zb), -jnp.imag(zb)], axis=1)
    gh = LANES // SSM_CH
    n_half = n_g // gh
    kd = c * SSM_CH

    def interleaved_block_diag(blocks, r1, c1):
        n = gh * kd
        b = blocks.astype(BF16).reshape(n_half, gh, kd, kd)
        bd = jnp.zeros((n_half, gh, kd, gh, kd), BF16)
        for g in range(gh):
            bd = bd.at[:, g, :, g, :].set(b[:, g])
        bd = bd.reshape(n_half, gh, r1, kd // r1, n).transpose(0, 2, 1, 3, 4).reshape(n_half, n, n)
        bd = jnp.swapaxes(bd, 1, 2)
        bd = bd.reshape(n_half, gh, c1, kd // c1, n).transpose(0, 2, 1, 3, 4).reshape(n_half, n, n)
        return jnp.swapaxes(bd, 1, 2)

    m_half = interleaved_block_diag(mq.reshape(n_g, kd, 4 * SSM_P), c, 4)
    t_half = interleaved_block_diag(t_mat, c, c)
    w_half = interleaved_block_diag(wq.reshape(n_g, 4 * SSM_P, kd), 4, c)
    tw_half = jnp.concatenate([t_half, w_half], axis=1)
    a_c = pw[c]
    a16 = jnp.stack([jnp.real(a_c[0]), jnp.imag(a_c[0]), jnp.real(a_c[1]), jnp.imag(a_c[1])])
    a16 = a16.reshape(4, 1, n_half, gh * SSM_P).transpose(2, 0, 1, 3)
    dvec = ssm_d.astype(F32).reshape(n_half, 1, LANES)
    return m_half, tw_half, a16, dvec


def _s5_state_body(u_ref, m_ref, fre_ref, fim_ref, bre_ref, bim_ref):
    nb, tn, kd = u_ref.shape
    w = fre_ref.shape[2]
    r = _dot(u_ref[...].reshape(nb * tn, kd).astype(BF16), m_ref[...])
    for q, o_ref in enumerate((fre_ref, fim_ref, bre_ref, bim_ref)):
        for b in range(nb):
            o_ref[:, b, :] = r[b * tn:(b + 1) * tn, q * w:(q + 1) * w]


def _s5_rows(nb, n):
    return max(8, min(n, 256 // nb))


def _s5_states(u3, m_half):
    nb, n, kd = u3.shape
    tn = _s5_rows(nb, n)
    w = m_half.shape[1] // 4
    out = pl.BlockSpec((tn, nb, w), lambda i: (i, 0, 0))
    return pl.pallas_call(
        _s5_state_body,
        name="s5_states",
        grid=(n // tn,),
        in_specs=[pl.BlockSpec((nb, tn, kd), lambda i: (0, i, 0)),
                  pl.BlockSpec(m_half.shape, lambda i: (0, 0))],
        out_specs=[out] * 4,
        out_shape=[jax.ShapeDtypeStruct((n, nb, w), F32)] * 4,
        compiler_params=_cparams("arbitrary"),
    )(u3, m_half)


def _s5_scan_body(sfr_ref, sfi_ref, sbr_ref, sbi_ref, a_ref, h0_ref,
                  hfr_ref, hfi_ref, hbr_ref, hbi_ref, fin_ref, st_ref):
    @pl.when(pl.program_id(0) == 0)
    def _():
        st_ref[...] = h0_ref[...]

    ks = sfr_ref.shape[0]
    afr, afi, abr, abi = a_ref[0], a_ref[1], a_ref[2], a_ref[3]

    def step(s, carry):
        fr, fi, br, bi = carry
        sb = ks - 1 - s
        hfr_ref[s] = fr
        hfi_ref[s] = fi
        hbr_ref[sb] = br
        hbi_ref[sb] = bi
        nfr = afr * fr - afi * fi + sfr_ref[s]
        nfi = afr * fi + afi * fr + sfi_ref[s]
        nbr = abr * br - abi * bi + sbr_ref[sb]
        nbi = abr * bi + abi * br + sbi_ref[sb]
        return nfr, nfi, nbr, nbi

    carry = lax.fori_loop(0, ks, step, (st_ref[0], st_ref[1], st_ref[2], st_ref[3]))
    for q in range(4):
        st_ref[q] = carry[q]
        fin_ref[q] = carry[q]


def _s5_scan(s3, a16, h0):
    n_chunks, n_batch, width = s3[0].shape
    ks = min(n_chunks, 32)
    nb = n_chunks // ks
    fwd = pl.BlockSpec((ks, n_batch, width), lambda i: (i, 0, 0))
    bwd = pl.BlockSpec((ks, n_batch, width), lambda i: (nb - 1 - i, 0, 0))
    small = lambda shape: pl.BlockSpec(shape, lambda i: (0, 0, 0))
    outs = pl.pallas_call(
        _s5_scan_body,
        name="s5_scan",
        grid=(nb,),
        in_specs=[fwd, fwd, bwd, bwd, small(a16.shape), small(h0.shape)],
        out_specs=[fwd, fwd, bwd, bwd, small(h0.shape)],
        out_shape=[jax.ShapeDtypeStruct((n_chunks, n_batch, width), F32)] * 4
        + [jax.ShapeDtypeStruct(h0.shape, F32)],
        scratch_shapes=[pltpu.VMEM(h0.shape, F32)],
        compiler_params=_cparams("arbitrary"),
    )(*s3, a16, h0)
    return outs[:4], outs[4]


def _s5_out_body(u_ref, us_ref, hfr_ref, hfi_ref, hbr_ref, hbi_ref, tw_ref, d_ref, y_ref):
    nb, tn, kd = u_ref.shape
    rows = nb * tn
    wo = us_ref.shape[2]
    y = _dot(u_ref[...].reshape(rows, kd).astype(BF16), tw_ref[0:kd, :])
    w = hfr_ref.shape[2]
    for q, h_ref in enumerate((hfr_ref, hfi_ref, hbr_ref, hbi_ref)):
        hq = jnp.concatenate([h_ref[:, b, :] for b in range(nb)], axis=0)
        y = y + _dot(hq.astype(BF16), tw_ref[kd + q * w:kd + (q + 1) * w, :])
    skip = jnp.concatenate([d_ref[...]] * (wo // LANES), axis=1) * us_ref[...].reshape(rows, wo)
    y_ref[...] = jax.nn.gelu(y + skip).reshape(nb, tn, wo)


def _s5_out(u3, h4, tw_half, dvec):
    nb, n, kd = u3.shape
    tn = _s5_rows(nb, n)
    wo = S5_OUT_STEPS * LANES
    w = h4[0].shape[2]
    hb = pl.BlockSpec((tn, nb, w), lambda i, j: (i, 0, 0))
    return pl.pallas_call(
        _s5_out_body,
        name="s5_out",
        grid=(n // tn, kd // wo),
        in_specs=[pl.BlockSpec((nb, tn, kd), lambda i, j: (0, i, 0)),
                  pl.BlockSpec((nb, tn, wo), lambda i, j: (0, i, j)),
                  hb, hb, hb, hb,
                  pl.BlockSpec((tw_half.shape[0], wo), lambda i, j: (0, j)),
                  pl.BlockSpec(dvec.shape, lambda i, j: (0, 0))],
        out_specs=pl.BlockSpec((nb, tn, wo), lambda i, j: (0, i, j)),
        out_shape=jax.ShapeDtypeStruct(u3.shape, F32),
        compiler_params=_cparams("arbitrary", "arbitrary"),
    )(u3, u3, *h4, tw_half, dvec)


def _s5(pc_halves, n_batch, seq, weights, h0):
    m_half, tw_half, a16, dvec = weights
    n_chunks = seq // S5_CHUNK
    ys, fins = [], []
    for h, pc in enumerate(pc_halves):
        w = m_half.shape[2] // 4
        u3 = pc.reshape(n_batch, n_chunks, S5_CHUNK * LANES)
        s4 = _s5_states(u3, m_half[h])
        h4, fin = _s5_scan(s4, a16[h], h0[:, :, h * w:(h + 1) * w])
        ys.append(_s5_out(u3, h4, tw_half[h], dvec[h]).reshape(n_batch * n_chunks, S5_CHUNK * LANES))
        fins.append(fin)
    return ys, jnp.concatenate(fins, axis=2)


def _post_body(x_ref, ya_ref, yb_ref, yc0_ref, yc1_ref, mod_ref, gpost_ref, gffn_ref, wglu_ref, bglu_ref, wo_ref,
               x1_ref, h2_ref, h2p_ref, unfold_ref):
    x = x_ref[...]
    d = x.shape[1]
    d_a = ya_ref.shape[1]
    d_b = yb_ref.shape[1]
    for h, yc_ref in enumerate((yc0_ref, yc1_ref)):
        _unfold_chunks(unfold_ref.at[h], yc_ref)
    y = jnp.concatenate([unfold_ref[0], unfold_ref[1]], axis=1)
    glu = y * jax.nn.sigmoid(_dot(y.astype(BF16), wglu_ref[...]) + bglu_ref[...])
    mixed = (_dot(ya_ref[...], wo_ref[0:d_a, :]) + _dot(yb_ref[...], wo_ref[d_a:d_a + d_b, :])
             + _dot(glu.astype(BF16), wo_ref[d_a + d_b:, :]))
    x1 = x + mod_ref[0, :, 2 * d:3 * d] * _rms(mixed, gpost_ref[...])
    x1_ref[...] = x1
    h2 = _rms(x1, gffn_ref[...]) * (1 + mod_ref[0, :, 4 * d:5 * d]) + mod_ref[0, :, 3 * d:4 * d]
    _store_tiled(h2_ref, h2)
    _store_tiled(h2p_ref, _pack_halves(h2))


def _post(x, ya, yb, yc0, yc1, mod, g_post, g_ffn, w_glu_b, b_glu, w_out_b, row_map):
    t, d = x.shape
    tm = TOK_TILE
    nc = d // LANES
    tok = lambda n: pl.BlockSpec((tm, n), lambda i: (i, 0))
    fold = pl.BlockSpec((tm // S5_CHUNK, S5_CHUNK * LANES), lambda i: (i, 0))
    full = lambda a: pl.BlockSpec(a.shape, lambda i: (0,) * a.ndim)
    return pl.pallas_call(
        _post_body,
        name="post_mix",
        grid=(t // tm,),
        in_specs=[tok(d), tok(ya.shape[1]), tok(yb.shape[1]), fold, fold,
                  pl.BlockSpec((1, 1, mod.shape[2]), lambda i: (row_map(i), 0, 0)),
                  full(g_post), full(g_ffn), full(w_glu_b), full(b_glu), full(w_out_b)],
        out_specs=[tok(d), pl.BlockSpec((tm * nc, LANES), lambda i: (i, 0)),
                   pl.BlockSpec((tm * nc // 2, LANES), lambda i: (i, 0))],
        out_shape=[jax.ShapeDtypeStruct((t, d), F32), jax.ShapeDtypeStruct((t * nc, LANES), F32),
                   jax.ShapeDtypeStruct((t * nc // 2, LANES), U32)],
        scratch_shapes=[pltpu.VMEM((2, tm, LANES), F32)],
        compiler_params=_cparams("arbitrary"),
    )(x, ya, yb, yc0, yc1, mod, g_post, g_ffn, w_glu_b, b_glu, w_out_b)


def _route_body(h_ref, whi_ref, wlo_ref, b_ref, idx_ref, gate_ref, pos_ref, cnt_ref, carry_ref):
    @pl.when(pl.program_id(0) == 0)
    def _():
        carry_ref[...] = jnp.zeros_like(carry_ref)

    tm = idx_ref.shape[1]
    h = _load_tiled(h_ref, tm)
    hi = h.astype(BF16)
    lo = (h - hi.astype(F32)).astype(BF16)
    logits = _dot_nt(whi_ref[...], hi) + (_dot_nt(whi_ref[...], lo) + _dot_nt(wlo_ref[...], hi))
    scores = jax.nn.sigmoid(logits)
    sel = scores + b_ref[...]
    gsz = N_EXPERTS // N_EXP_GROUPS
    within = lax.broadcasted_iota(I32, (gsz, tm), 0).astype(F32)
    grp = []
    for g in range(N_EXP_GROUPS):
        blk = sel[g * gsz:(g + 1) * gsz]
        m1 = jnp.max(blk, axis=0, keepdims=True)
        first = jnp.min(jnp.where(blk == m1, within, float(gsz)), axis=0, keepdims=True)
        m2 = jnp.max(jnp.where(within == first, -jnp.inf, blk), axis=0, keepdims=True)
        grp.append(m1 + m2)
    blocks = []
    for g in range(N_EXP_GROUPS):
        ahead = jnp.zeros((1, tm), F32)
        for o in range(N_EXP_GROUPS):
            if o == g:
                continue
            beats = (grp[o] >= grp[g]) if o < g else (grp[o] > grp[g])
            ahead = ahead + jnp.where(beats, 1.0, 0.0)
        ahead = jnp.broadcast_to(ahead, (gsz, tm))
        blocks.append(jnp.where(ahead < TOPK_GROUPS, sel[g * gsz:(g + 1) * gsz], -jnp.inf))
    v = jnp.concatenate(blocks, axis=0)
    eidx = lax.broadcasted_iota(I32, (N_EXPERTS, tm), 0)
    rnk = jnp.zeros((N_EXPERTS, tm), F32)
    for e in range(N_EXPERTS):
        row = v[e:e + 1]
        rnk = rnk + jnp.where(eidx > e, jnp.where(row >= v, 1.0, 0.0), jnp.where(row > v, 1.0, 0.0))
    chosen = rnk < TOP_K
    w = jnp.where(chosen, scores, 0.0)
    wn = w / jnp.sum(w, axis=0, keepdims=True) * ROUTE_SCALE
    tri = (lax.broadcasted_iota(I32, (tm, tm), 0) < lax.broadcasted_iota(I32, (tm, tm), 1))
    chosen_f = jnp.where(chosen, 1.0, 0.0)
    prefix = _dot(chosen_f.astype(BF16), jnp.where(tri, 1.0, 0.0).astype(BF16)) + carry_ref[:, 0:1]
    total = carry_ref[...] + jnp.sum(chosen_f, axis=1, keepdims=True)
    carry_ref[...] = total
    cnt_ref[...] = total
    eidx_f = eidx.astype(F32)
    for k in range(TOP_K):
        one = rnk == k
        idx_ref[k:k + 1, :] = jnp.sum(jnp.where(one, eidx_f, 0.0), axis=0, keepdims=True).astype(I32)
        gate_ref[k:k + 1, :] = jnp.sum(jnp.where(one, wn, 0.0), axis=0, keepdims=True)
        pos_ref[k:k + 1, :] = jnp.sum(jnp.where(one, prefix, 0.0), axis=0, keepdims=True).astype(I32)


def _route(h2t, w_hi, w_lo, b_router):
    nc = w_hi.shape[1] // LANES
    t = h2t.shape[0] // nc
    tm = TOK_TILE
    out = pl.BlockSpec((TOP_K, tm), lambda i: (0, i))
    full = lambda a: pl.BlockSpec(a.shape, lambda i: (0, 0))
    return pl.pallas_call(
        _route_body,
        name="route",
        grid=(t // tm,),
        in_specs=[pl.BlockSpec((tm * nc, LANES), lambda i: (i, 0)), full(w_hi), full(w_lo), full(b_router)],
        out_specs=[out, out, out, pl.BlockSpec((N_EXPERTS, LANES), lambda i: (0, 0))],
        out_shape=[jax.ShapeDtypeStruct((TOP_K, t), I32), jax.ShapeDtypeStruct((TOP_K, t), F32),
                   jax.ShapeDtypeStruct((TOP_K, t), I32), jax.ShapeDtypeStruct((N_EXPERTS, LANES), F32)],
        scratch_shapes=[pltpu.VMEM((N_EXPERTS, LANES), F32)],
        compiler_params=_cparams("arbitrary"),
    )(h2t, w_hi, w_lo, b_router)


def _dest_body(starts_ref, idx_ref, pos_ref, dest_ref):
    idx = idx_ref[...]
    acc = pos_ref[...]
    for e in range(N_EXPERTS):
        acc = acc + jnp.where(idx == e, starts_ref[e], 0)
    dest_ref[...] = acc


def _dest(starts, idx, pos):
    t = idx.shape[1]
    tile = math.gcd(t, 4096)
    blk = pl.BlockSpec((TOP_K, tile), lambda i, st: (0, i))
    return pl.pallas_call(
        _dest_body,
        name="moe_dest",
        grid_spec=pltpu.PrefetchScalarGridSpec(num_scalar_prefetch=1, grid=(t // tile,),
                                               in_specs=[blk, blk], out_specs=blk),
        out_shape=jax.ShapeDtypeStruct(idx.shape, I32),
        compiler_params=_cparams("arbitrary"),
    )(starts, idx, pos)


def _dispatch_body(pad_lo_ref, pad_hi_ref, dest_ref, h_ref, xs_ref, zero_ref, sem, zsem, *, n_pad_rows):
    tile = dest_ref.shape[1]
    nw = zero_ref.shape[0]

    def token(ref, r):
        return ref.at[pl.ds(pl.multiple_of(r * nw, nw), nw)]

    def zero_copy(r):
        return pltpu.make_async_copy(zero_ref, token(xs_ref, r), zsem)

    @pl.when(pl.program_id(0) == 0)
    def _():
        zero_ref[...] = jnp.zeros_like(zero_ref)

        def per_expert(e, c):
            def one(r, c2):
                zero_copy(r).start()
                return c2
            return lax.fori_loop(pad_lo_ref[e], pad_hi_ref[e], one, c)

        lax.fori_loop(0, N_EXPERTS, per_expert, 0)

    def row_copy(t, k):
        return pltpu.make_async_copy(token(h_ref, t), token(xs_ref, dest_ref[k, t]), sem)

    def start(t, c):
        for k in range(TOP_K):
            row_copy(t, k).start(priority=k % DMA_THREADS)
        return c

    def wait(t, c):
        for k in range(TOP_K):
            row_copy(t, k).wait()
        return c

    lax.fori_loop(0, tile, start, 0)
    lax.fori_loop(0, tile, wait, 0)

    @pl.when(pl.program_id(0) == 0)
    def _():
        def one(r, c):
            zero_copy(0).wait()
            return c
        lax.fori_loop(0, n_pad_rows, one, 0)


def _dispatch(pad_lo, pad_hi, dest, h2p, n_rows):
    t = dest.shape[1]
    nw = h2p.shape[0] // t
    tile = DISPATCH_TILE
    grid_spec = pltpu.PrefetchScalarGridSpec(
        num_scalar_prefetch=2,
        grid=(t // tile,),
        in_specs=[pl.BlockSpec((TOP_K, tile), lambda i, lo, hi: (0, i), memory_space=pltpu.SMEM),
                  pl.BlockSpec((tile * nw, LANES), lambda i, lo, hi: (i, 0))],
        out_specs=pl.BlockSpec(memory_space=pl.ANY),
        scratch_shapes=[pltpu.VMEM((nw, LANES), U32), pltpu.SemaphoreType.DMA(()),
                        pltpu.SemaphoreType.DMA(())],
    )
    return pl.pallas_call(
        functools.partial(_dispatch_body, n_pad_rows=n_rows - t * TOP_K),
        name="moe_dispatch",
        grid_spec=grid_spec,
        out_shape=jax.ShapeDtypeStruct((n_rows * nw, LANES), U32),
        compiler_params=_cparams("arbitrary"),
    )(pad_lo, pad_hi, dest, h2p)


def _experts_body(be_ref, x_ref, wg_ref, wu_ref, wd_ref, y_ref, wgb_ref, wub_ref, wdb_ref):
    i = pl.program_id(0)

    @pl.when((i == 0) | (be_ref[i] != be_ref[jnp.maximum(i - 1, 0)]))
    def _():
        wgb_ref[...] = wg_ref[0, 0].astype(BF16)
        wub_ref[...] = wu_ref[0, 0].astype(BF16)
        wdb_ref[...] = wd_ref[0, 0].astype(BF16)

    xb = _unpack_halves(_load_tiled(x_ref, MOE_ROWS)).astype(BF16)
    hb = jax.nn.silu(_dot(xb, wgb_ref[...])) * _dot(xb, wub_ref[...])
    _store_tiled(y_ref, _pack_halves(_dot(hb.astype(BF16), wdb_ref[...])))


def _experts(blk_exp, xs, wg, wu, wd, layer):
    d, de = wg.shape[2], wg.shape[3]
    nc = d // (2 * LANES)
    bm = MOE_ROWS
    tiles = pl.BlockSpec((bm * nc, LANES), lambda i, be: (i, 0))
    grid_spec = pltpu.PrefetchScalarGridSpec(
        num_scalar_prefetch=1,
        grid=(xs.shape[0] // (bm * nc),),
        in_specs=[tiles,
                  pl.BlockSpec((1, 1, d, de), lambda i, be: (layer, be[i], 0, 0)),
                  pl.BlockSpec((1, 1, d, de), lambda i, be: (layer, be[i], 0, 0)),
                  pl.BlockSpec((1, 1, de, d), lambda i, be: (layer, be[i], 0, 0))],
        out_specs=tiles,
        scratch_shapes=[pltpu.VMEM((d, de), BF16), pltpu.VMEM((d, de), BF16), pltpu.VMEM((de, d), BF16)],
    )
    return pl.pallas_call(
        _experts_body,
        name="moe_experts",
        grid_spec=grid_spec,
        out_shape=jax.ShapeDtypeStruct(xs.shape, U32),
        compiler_params=_cparams("arbitrary"),
    )(blk_exp, xs, wg, wu, wd)


def _combine_body(dest_ref, next_ref, gate_ref, h_ref, x_ref, mod_ref, g_ref, wsg_ref, wsu_ref, wsd_ref, ys_hbm,
                  o_ref, buf_ref, sem):
    tile, d = x_ref.shape
    nw = buf_ref.shape[2] // tile
    i = pl.program_id(0)
    slot = i % 2

    def row_copy(d_ref, s, t, k):
        return pltpu.make_async_copy(ys_hbm.at[pl.ds(pl.multiple_of(d_ref[k, t] * nw, nw), nw)],
                                     buf_ref.at[s, k, pl.ds(pl.multiple_of(t * nw, nw), nw)], sem.at[s])

    def gather(d_ref, s):
        def start(t, c):
            for k in range(TOP_K):
                row_copy(d_ref, s, t, k).start(priority=k % DMA_THREADS)
            return c
        lax.fori_loop(0, tile, start, 0)

    @pl.when(i == 0)
    def _():
        gather(dest_ref, 0)

    @pl.when(i + 1 < pl.num_programs(0))
    def _():
        gather(next_ref, 1 - slot)

    hb = _load_tiled(h_ref, tile).astype(BF16)
    acc = _dot((jax.nn.silu(_dot(hb, wsg_ref[...])) * _dot(hb, wsu_ref[...])).astype(BF16), wsd_ref[...])

    def wait(t, c):
        for k in range(TOP_K):
            row_copy(dest_ref, slot, 0, k).wait()
        return c

    lax.fori_loop(0, tile, wait, 0)
    gates = gate_ref[...]
    moe = gates[:, 0:1] * _unpack_halves(_load_tiled(buf_ref.at[slot, 0], tile))
    for k in range(1, TOP_K):
        moe = moe + gates[:, k:k + 1] * _unpack_halves(_load_tiled(buf_ref.at[slot, k], tile))
    o_ref[...] = x_ref[...] + mod_ref[0, :, 5 * d:6 * d] * _rms(moe + acc, g_ref[...])


def _combine(dest, gates, h2t, x1, mod, g_post, wsg, wsu, wsd, ys, row_map):
    t, d = x1.shape
    nc = d // LANES
    nw = nc // 2
    tile = COMBINE_TILE
    per_tok = TOK_TILE // tile
    n_steps = t // tile
    tok = lambda n: pl.BlockSpec((tile, n), lambda i: (i, 0))
    full = lambda a: pl.BlockSpec(a.shape, lambda i: (0,) * a.ndim)
    return pl.pallas_call(
        _combine_body,
        name="moe_combine",
        grid=(n_steps,),
        in_specs=[pl.BlockSpec((TOP_K, tile), lambda i: (0, i), memory_space=pltpu.SMEM),
                  pl.BlockSpec((TOP_K, tile), lambda i: (0, jnp.minimum(i + 1, n_steps - 1)),
                               memory_space=pltpu.SMEM),
                  tok(TOP_K), pl.BlockSpec((tile * nc, LANES), lambda i: (i, 0)), tok(d),
                  pl.BlockSpec((1, 1, mod.shape[2]), lambda i: (row_map(i // per_tok), 0, 0)),
                  full(g_post), full(wsg), full(wsu), full(wsd),
                  pl.BlockSpec(memory_space=pl.ANY)],
        out_specs=tok(d),
        out_shape=jax.ShapeDtypeStruct((t, d), F32),
        scratch_shapes=[pltpu.VMEM((2, TOP_K, tile * nw, LANES), U32), pltpu.SemaphoreType.DMA((2,))],
        compiler_params=_cparams("arbitrary"),
    )(dest, dest, gates, h2t, x1, mod, g_post, wsg, wsu, wsd, ys)


def _moe(h2t, h2p, x1, mod, p, row_map):
    t, d = x1.shape
    idx, gate, pos, cnt = _route(h2t, p['wr_hi'], p['wr_lo'], p['b_router'])
    counts = cnt[:, 0].astype(I32)
    padded = (counts + MOE_ROWS - 1) // MOE_ROWS * MOE_ROWS
    ends = jnp.cumsum(padded)
    starts = ends - padded
    n_blocks = (t * TOP_K + N_EXPERTS * (MOE_ROWS - 1)) // MOE_ROWS + 1
    n_rows = n_blocks * MOE_ROWS
    blk_start = jnp.arange(n_blocks, dtype=I32) * MOE_ROWS
    blk_exp = jnp.minimum(jnp.sum((ends[None, :] <= blk_start[:, None]).astype(I32), axis=1), N_EXPERTS - 1)
    pad_hi = ends.at[N_EXPERTS - 1].set(n_rows)
    dest = _dest(starts, idx, pos)
    xs = _dispatch(starts + counts, pad_hi, dest, h2p, n_rows)
    ys = _experts(blk_exp, xs, p['w_e_gate'], p['w_e_up'], p['w_e_down'], p['layer'])
    return _combine(dest, gate.T, h2t, x1, mod, p['g_post_ffn'], p['w_s_gate'], p['w_s_up'], p['w_s_down'],
                    ys, row_map)


def kernel(x_prompt, x_sample, cache_k, cache_v, state_ssm_re, state_ssm_im, c, c_ctx,
           g_pre_mix, g_post_mix, g_pre_ffn, g_post_ffn, w_ada, b_ada, w_in, w_out,
           sgu_g, w_sp, b_sp, rpb, ssm_a_re, ssm_a_im, ssm_log_dt, ssm_b_re, ssm_b_im,
           ssm_c_re, ssm_c_im, ssm_d, w_glu, b_glu, w_router, b_router,
           w_e_gate, w_e_up, w_e_down, w_s_gate, w_s_up, w_s_down):
    n_pb, p_seq, d = x_prompt.shape
    n_sb, s_seq, _ = x_sample.shape
    depth = w_in.shape[0]
    d_a = sgu_g.shape[1]
    d_c = w_glu.shape[1]
    d_b = d - d_a - d_c
    n_heads = d_b // HD_B
    n_g = d_c // SSM_CH
    t_p = n_pb * p_seq
    assert p_seq % CHUNK == 0 and t_p % TOK_TILE == 0 and s_seq % TOK_TILE == 0 and t_p % s_seq == 0
    assert s_seq % (NAT_ROWS * GRID_W) == 0 and s_seq // GRID_W >= NA_WIN_R and d_c == 2 * LANES

    mod_rows = -(-(n_sb + 1) // 8) * 8
    c_all = jnp.zeros((mod_rows, d), F32).at[:n_sb].set(c).at[n_sb].set(c_ctx)
    mod_all = _ada(c_all, w_ada, b_ada)
    p_tiles = t_p // TOK_TILE
    s_tiles = s_seq // TOK_TILE

    def row_map(i):
        return jnp.where(i < p_tiles, n_sb, (i - p_tiles) // s_tiles)

    x = jnp.concatenate([x_prompt.reshape(t_p, d), x_sample.reshape(n_sb * s_seq, d)], axis=0)
    new_k, new_v, new_re, new_im = [], [], [], []
    for l in range(depth):
        mod = mod_all[l][:, None, :]
        row = lambda a: a[l][None, :].astype(F32)
        pa, q, k32, v32, kb, vb, *pc = _premix(x, mod, row(g_pre_mix), w_in[l].astype(BF16), row_map,
                                               d_a, d_b, d_c)
        bias_a = jnp.repeat(b_sp[l].T.astype(F32), d_a // NH_A, axis=1)
        ya = _chunk_mlp(pa, row(sgu_g), w_sp[l].reshape(NH_A * CHUNK, CHUNK).astype(BF16), bias_a)
        yb_p = _ctx_attn(q, k32, v32, n_pb, p_seq)
        ck = cache_k[:, l].reshape(n_sb, -1, d_b).astype(BF16)
        cv = cache_v[:, l].reshape(n_sb, -1, d_b).astype(BF16)
        yb = _nat_attn(q, kb, vb, ck, cv, _nat_bias(rpb[l]), yb_p, n_sb, s_seq, t_p)
        s5w = _s5_weights(ssm_a_re[l], ssm_a_im[l], ssm_log_dt[l], ssm_b_re[l], ssm_b_im[l],
                          ssm_c_re[l], ssm_c_im[l], ssm_d[l])
        c_p = t_p // S5_CHUNK
        yc_p, fin_p = _s5([h[:c_p] for h in pc], n_pb, p_seq, s5w, jnp.zeros((4, n_pb, n_g * SSM_P), F32))
        sre = state_ssm_re[:, l].astype(F32).reshape(n_sb, 2, n_g * SSM_P)
        sim = state_ssm_im[:, l].astype(F32).reshape(n_sb, 2, n_g * SSM_P)
        h0_s = jnp.stack([sre[:, 0], sim[:, 0], sre[:, 1], sim[:, 1]])
        yc_s, _ = _s5([h[c_p:] for h in pc], n_sb, s_seq, s5w, h0_s)
        yc = [jnp.concatenate([a, b], axis=0) for a, b in zip(yc_p, yc_s)]
        x1, h2, h2p = _post(x, ya, yb, yc[0], yc[1],
                       mod, row(g_post_mix), row(g_pre_ffn), w_glu[l].astype(BF16), row(b_glu),
                       w_out[l].astype(BF16), row_map)
        wr = w_router[l].astype(F32).T
        wr_hi = wr.astype(BF16)
        moe_p = {
            'wr_hi': wr_hi, 'wr_lo': (wr - wr_hi.astype(F32)).astype(BF16),
            'b_router': b_router[l].astype(F32)[:, None],
            'w_e_gate': w_e_gate, 'w_e_up': w_e_up, 'w_e_down': w_e_down, 'layer': l,
            'w_s_gate': w_s_gate[l].astype(BF16), 'w_s_up': w_s_up[l].astype(BF16),
            'w_s_down': w_s_down[l].astype(BF16), 'g_post_ffn': row(g_post_ffn),
        }
        x = _moe(h2, h2p, x1, mod, moe_p, row_map)
        new_k.append(k32[:t_p].reshape(n_pb, p_seq, n_heads, HD_B))
        new_v.append(v32[:t_p].reshape(n_pb, p_seq, n_heads, HD_B))
        fin_p = fin_p.reshape(4, n_pb, n_g, SSM_P)
        new_re.append(jnp.stack([fin_p[0], fin_p[2]], axis=1))
        new_im.append(jnp.stack([fin_p[1], fin_p[3]], axis=1))
    return (x[:t_p].reshape(n_pb, p_seq, d), x[t_p:].reshape(n_sb, s_seq, d),
            jnp.stack(new_k, axis=1), jnp.stack(new_v, axis=1),
            jnp.stack(new_re, axis=1), jnp.stack(new_im, axis=1))
```

```python
import functools
import math

import jax
import jax.numpy as jnp
from jax import lax
from jax.experimental import pallas as pl
from jax.experimental.pallas import tpu as pltpu

F32 = jnp.float32
BF16 = jnp.bfloat16
I32 = jnp.int32

GRID_W = 64
EPS = 1e-6
NH_A = 4
CHUNK = 128
HD_B = 64
NA_WIN_R = 8
NA_WIN_C = 16
ATTN_SCALE = HD_B ** -0.5
SSM_CH = 16
SSM_P = 64
N_EXPERTS = 64
TOP_K = 8
N_EXP_GROUPS = 8
TOPK_GROUPS = 4
ROUTE_SCALE = 2.5

LANES = 128
TOK_TILE = 512
S5_CHUNK = 16
S5_OUT_STEPS = 4
MOE_ROWS = 512
COMBINE_TILE = 128
DISPATCH_TILE = 512
ZERO_RUN = 64
NAT_ROWS = 4
DMA_THREADS = 2
NEG_BIG = -1e30
VMEM_LIMIT = 48 * 1024 * 1024


def _cparams(*sem):
    return pltpu.CompilerParams(dimension_semantics=sem, vmem_limit_bytes=VMEM_LIMIT)


def _dot(a, b):
    return jnp.dot(a, b, preferred_element_type=F32)


def _dot_nt(a, b):
    return lax.dot_general(a, b, (((1,), (1,)), ((), ())), preferred_element_type=F32)


def _rms(x, g):
    return x * lax.rsqrt(jnp.mean(x * x, axis=-1, keepdims=True) + EPS) * g


def _load_tiled(ref, n_rows):
    nc = ref.shape[0] // n_rows
    return jnp.concatenate([ref[pl.ds(j, n_rows, stride=nc), :] for j in range(nc)], axis=1)


def _store_tiled(ref, val):
    n_rows = val.shape[0]
    nc = ref.shape[0] // n_rows
    for j in range(nc):
        ref[pl.ds(j, n_rows, stride=nc), :] = val[:, j * LANES:(j + 1) * LANES]


U32 = jnp.uint32


def _pack_halves(x):
    half = x.shape[1] // 2
    lo = lax.bitcast_convert_type(x[:, :half].astype(BF16).astype(F32), U32)
    hi = lax.bitcast_convert_type(x[:, half:].astype(BF16).astype(F32), U32)
    return (lo >> 16) | hi


def _unpack_halves(w):
    lo = lax.bitcast_convert_type(w << 16, F32)
    hi = lax.bitcast_convert_type(w & jnp.uint32(0xFFFF0000), F32)
    return jnp.concatenate([lo, hi], axis=1)


def _ada_body(c_ref, w_ref, b_ref, o_ref):
    s = jax.nn.silu(c_ref[...]).astype(BF16)
    o_ref[0] = _dot(s, w_ref[0].astype(BF16)) + b_ref[0]


def _ada(c_all, w_ada, b_ada):
    n_layers, d, n = w_ada.shape
    rows = c_all.shape[0]
    tn = 1536
    return pl.pallas_call(
        _ada_body,
        name="ada",
        grid=(n_layers, n // tn),
        in_specs=[pl.BlockSpec((rows, d), lambda l, j: (0, 0)),
                  pl.BlockSpec((1, d, tn), lambda l, j: (l, 0, j)),
                  pl.BlockSpec((1, 1, tn), lambda l, j: (l, 0, j))],
        out_specs=pl.BlockSpec((1, rows, tn), lambda l, j: (l, 0, j)),
        out_shape=jax.ShapeDtypeStruct((n_layers, rows, n), F32),
        compiler_params=_cparams("arbitrary", "arbitrary"),
    )(c_all, w_ada, b_ada.reshape(n_layers, 1, n))


def _premix_body(x_ref, mod_ref, g_ref, w_ref, pa_ref, q_ref, k_ref, v_ref, kb_ref, vb_ref, pc0_ref, pc1_ref,
                 fold_ref, *, ctx_tiles):
    x = x_ref[...]
    d = x.shape[1]
    h = _rms(x, g_ref[...]) * (1 + mod_ref[0, :, d:2 * d]) + mod_ref[0, :, 0:d]
    p = _dot(h.astype(BF16), w_ref[...])
    d_a2 = pa_ref.shape[1]
    d_b = q_ref.shape[1]
    o = d_a2
    pa_ref[...] = p[:, 0:o]
    q_ref[...] = p[:, o:o + d_b].astype(BF16)
    k = p[:, o + d_b:o + 2 * d_b]
    v = p[:, o + 2 * d_b:o + 3 * d_b]

    @pl.when(pl.program_id(0) < ctx_tiles)
    def _():
        k_ref[...] = k
        v_ref[...] = v

    kb_ref[...] = k.astype(BF16)
    vb_ref[...] = v.astype(BF16)
    for h, pc_ref in enumerate((pc0_ref, pc1_ref)):
        fold_ref[...] = p[:, o + 3 * d_b + h * LANES:o + 3 * d_b + (h + 1) * LANES]
        _fold_chunks(pc_ref, fold_ref)


def _fold_chunks(dst_ref, src_ref):
    n = dst_ref.shape[0]
    for i in range(S5_CHUNK):
        dst_ref[:, i * LANES:(i + 1) * LANES] = src_ref[pl.ds(i, n, stride=S5_CHUNK), :]


def _unfold_chunks(dst_ref, src_ref):
    n = src_ref.shape[0]
    for i in range(S5_CHUNK):
        dst_ref[pl.ds(i, n, stride=S5_CHUNK), :] = src_ref[:, i * LANES:(i + 1) * LANES]


def _premix(x, mod, g, w_in_b, row_map, d_a, d_b, d_c, t_ctx):
    t, d = x.shape
    tm = TOK_TILE
    d_in = w_in_b.shape[1]
    ctx_tiles = t_ctx // tm
    tok = lambda n: pl.BlockSpec((tm, n), lambda i: (i, 0))
    ctx = pl.BlockSpec((tm, d_b), lambda i: (jnp.minimum(i, ctx_tiles - 1), 0))
    fold = pl.BlockSpec((tm // S5_CHUNK, S5_CHUNK * LANES), lambda i: (i, 0))
    return pl.pallas_call(
        functools.partial(_premix_body, ctx_tiles=ctx_tiles),
        name="premix",
        grid=(t // tm,),
        in_specs=[tok(d),
                  pl.BlockSpec((1, 1, mod.shape[2]), lambda i: (row_map(i), 0, 0)),
                  pl.BlockSpec((1, d), lambda i: (0, 0)),
                  pl.BlockSpec((d, d_in), lambda i: (0, 0))],
        out_specs=[tok(2 * d_a), tok(d_b), ctx, ctx, tok(d_b), tok(d_b), fold, fold],
        out_shape=[jax.ShapeDtypeStruct((t, 2 * d_a), F32),
                   jax.ShapeDtypeStruct((t, d_b), BF16),
                   jax.ShapeDtypeStruct((t_ctx, d_b), F32),
                   jax.ShapeDtypeStruct((t_ctx, d_b), F32),
                   jax.ShapeDtypeStruct((t, d_b), BF16),
                   jax.ShapeDtypeStruct((t, d_b), BF16)]
        + [jax.ShapeDtypeStruct((t // S5_CHUNK, S5_CHUNK * LANES), F32)] * 2,
        scratch_shapes=[pltpu.VMEM((tm, LANES), F32)],
        compiler_params=_cparams("arbitrary"),
    )(x, mod, g, w_in_b)


def _chunk_body(pa_ref, g_ref, w_ref, b_ref, o_ref):
    z = jax.nn.gelu(pa_ref[...])
    d_a = o_ref.shape[1]
    hd = d_a // NH_A
    u = z[:, :d_a]
    v = z[:, d_a:]
    mu = jnp.mean(v, axis=-1, keepdims=True)
    var = jnp.mean(jnp.square(v - mu), axis=-1, keepdims=True)
    vb = ((v - mu) * lax.rsqrt(var + EPS) * g_ref[...]).astype(BF16)
    head = lax.broadcasted_iota(I32, (CHUNK, d_a), 1) // hd
    for ch in range(pa_ref.shape[0] // CHUNK):
        rows = slice(ch * CHUNK, (ch + 1) * CHUNK)
        sf = _dot(w_ref[...], vb[rows])
        s = b_ref[...]
        for h in range(NH_A):
            s = s + jnp.where(head == h, sf[h * CHUNK:(h + 1) * CHUNK], 0.0)
        o_ref[rows, :] = (u[rows] * s).astype(BF16)


def _chunk_mlp(pa, sgu_g, w_sp_b, bias):
    t, d2 = pa.shape
    d_a = d2 // 2
    tm = TOK_TILE
    return pl.pallas_call(
        _chunk_body,
        name="chunk_mlp",
        grid=(t // tm,),
        in_specs=[pl.BlockSpec((tm, d2), lambda i: (i, 0)),
                  pl.BlockSpec((1, d_a), lambda i: (0, 0)),
                  pl.BlockSpec(w_sp_b.shape, lambda i: (0, 0)),
                  pl.BlockSpec(bias.shape, lambda i: (0, 0))],
        out_specs=pl.BlockSpec((tm, d_a), lambda i: (i, 0)),
        out_shape=jax.ShapeDtypeStruct((t, d_a), BF16),
        compiler_params=_cparams("arbitrary"),
    )(pa, sgu_g, w_sp_b, bias)


def _stack_pair(qg):
    lane = lax.broadcasted_iota(I32, qg.shape, 1)
    zero = jnp.zeros_like(qg)
    return jnp.concatenate([jnp.where(lane < HD_B, qg, zero), jnp.where(lane >= HD_B, qg, zero)], axis=0)


def _unstack_pair(o2):
    n = o2.shape[0] // 2
    lane = lax.broadcasted_iota(I32, (n, o2.shape[1]), 1)
    return jnp.where(lane < HD_B, o2[:n], o2[n:])


def _ctx_attn_body(q_ref, k_ref, v_ref, o_ref):
    for g in range(q_ref.shape[1] // LANES):
        cols = slice(g * LANES, (g + 1) * LANES)
        q2 = _stack_pair(q_ref[:, cols])
        s = _dot_nt(q2, k_ref[:, cols].astype(BF16)) * ATTN_SCALE
        e = jnp.exp(s - jnp.max(s, axis=-1, keepdims=True))
        p = e / jnp.sum(e, axis=-1, keepdims=True)
        o2 = _dot(p.astype(BF16), v_ref[:, cols].astype(BF16))
        o_ref[:, cols] = _unstack_pair(o2).astype(BF16)


def _ctx_attn(q, k, v, n_batch, seq):
    d_b = q.shape[1]
    blk = pl.BlockSpec((seq, d_b), lambda b: (b, 0))
    return pl.pallas_call(
        _ctx_attn_body,
        name="ctx_attn",
        grid=(n_batch,),
        in_specs=[blk, blk, blk],
        out_specs=blk,
        out_shape=jax.ShapeDtypeStruct((n_batch * seq, d_b), BF16),
        compiler_params=_cparams("arbitrary"),
    )(q, k, v)


def _nat_body(q_ref, k_ref, v_ref, ck_ref, cv_ref, bias_ref, o_ref, *, rows):
    r0 = pl.program_id(1) * NAT_ROWS
    n_win = NA_WIN_R * GRID_W
    pr = 2 * GRID_W
    starts, cases = [], []
    for i in range(NAT_ROWS):
        rs = jnp.clip(r0 + i - NA_WIN_R // 2, 0, rows - NA_WIN_R)
        starts.append(pl.multiple_of(rs * GRID_W, GRID_W))
        cases.append(r0 + i - rs)
    for g in range(q_ref.shape[1] // LANES):
        cols = slice(g * LANES, (g + 1) * LANES)
        q2 = jnp.concatenate([_stack_pair(q_ref[i * GRID_W:(i + 1) * GRID_W, cols]) for i in range(NAT_ROWS)],
                             axis=0)
        s_ctx = _dot_nt(q2, ck_ref[0, :, cols]) * ATTN_SCALE
        m_ctx = jnp.max(s_ctx, axis=-1, keepdims=True)
        e_wins, invs, ms = [], [], []
        for i in range(NAT_ROWS):
            s_win = (_dot_nt(q2[i * pr:(i + 1) * pr], k_ref[pl.ds(starts[i], n_win), cols]) * ATTN_SCALE
                     + bias_ref[cases[i], g])
            m = jnp.maximum(jnp.max(s_win, axis=-1, keepdims=True), m_ctx[i * pr:(i + 1) * pr])
            e_wins.append(jnp.exp(s_win - m))
            ms.append(m)
        e_ctx = jnp.exp(s_ctx - jnp.concatenate(ms, axis=0))
        l_ctx = jnp.sum(e_ctx, axis=-1, keepdims=True)
        for i in range(NAT_ROWS):
            invs.append(1.0 / (jnp.sum(e_wins[i], axis=-1, keepdims=True) + l_ctx[i * pr:(i + 1) * pr]))
        o_ctx = _dot((e_ctx * jnp.concatenate(invs, axis=0)).astype(BF16), cv_ref[0, :, cols])
        for i in range(NAT_ROWS):
            o2 = _dot((e_wins[i] * invs[i]).astype(BF16), v_ref[pl.ds(starts[i], n_win), cols])
            o_ref[i * GRID_W:(i + 1) * GRID_W, cols] = _unstack_pair(o2 + o_ctx[i * pr:(i + 1) * pr]).astype(BF16)


def _nat_attn(q, kb, vb, ck, cv, bias, n_batch, seq, tok0):
    d_b = q.shape[1]
    rows = seq // GRID_W
    lc = ck.shape[1]
    blk = NAT_ROWS * GRID_W
    steps = seq // blk
    q0 = tok0 // blk
    i0 = tok0 // seq
    img = pl.BlockSpec((seq, d_b), lambda b, r: (i0 + b, 0))
    ctx = pl.BlockSpec((1, lc, d_b), lambda b, r: (b, 0, 0))
    return pl.pallas_call(
        functools.partial(_nat_body, rows=rows),
        name="nat_attn",
        grid=(n_batch, steps),
        in_specs=[pl.BlockSpec((blk, d_b), lambda b, r: (q0 + b * steps + r, 0)),
                  img, img, ctx, ctx,
                  pl.BlockSpec(bias.shape, lambda b, r: (0, 0, 0, 0))],
        out_specs=pl.BlockSpec((blk, d_b), lambda b, r: (b * steps + r, 0)),
        out_shape=jax.ShapeDtypeStruct((n_batch * seq, d_b), BF16),
        compiler_params=_cparams("arbitrary", "arbitrary"),
    )(q, kb, vb, ck, cv, bias)


def _nat_bias(rpb):
    n_heads = rpb.shape[0]
    cols = jnp.arange(GRID_W)
    col_start = jnp.clip(cols - NA_WIN_C // 2, 0, GRID_W - NA_WIN_C)
    j = jnp.arange(GRID_W)
    valid = (j[None, :] >= col_start[:, None]) & (j[None, :] < col_start[:, None] + NA_WIN_C)
    col_off = jnp.clip(j[None, :] - cols[:, None] + (NA_WIN_C - 1), 0, 2 * NA_WIN_C - 2)
    toe = jnp.where(valid[None, None], rpb.astype(F32)[:, :, col_off], NEG_BIG)
    cases = jnp.stack([toe[:, NA_WIN_R - 1 - delta:2 * NA_WIN_R - 1 - delta] for delta in range(NA_WIN_R)])
    return cases.transpose(0, 1, 3, 2, 4).reshape(NA_WIN_R, n_heads // 2, 2 * GRID_W, NA_WIN_R * GRID_W)


def _s5_weights(a_re, a_im, log_dt, b_re, b_im, c_re, c_im, ssm_d):
    n_g = a_re.shape[1]
    c = S5_CHUNK
    lam = lax.complex(a_re.astype(F32), a_im.astype(F32))
    ldt = lam * jnp.exp(log_dt.astype(F32))[..., None]
    lam_bar = jnp.exp(ldt)
    b_bar = ((lam_bar - 1) / lam)[..., None] * lax.complex(b_re.astype(F32), b_im.astype(F32))
    c_mat = lax.complex(c_re.astype(F32), c_im.astype(F32))
    pw = jnp.exp(ldt[None] * jnp.arange(c + 1, dtype=F32)[:, None, None, None])
    kern = jnp.real(jnp.einsum('dgcp,kdgp,dgpe->dgkce', c_mat, pw[:c], b_bar))
    i = jnp.arange(c)
    lag = i[None, :] - i[:, None]
    tf = jnp.where((lag >= 0)[None, :, :, None, None], kern[0][:, jnp.clip(lag, 0, c - 1)], 0.0)
    tb = jnp.where((lag <= 0)[None, :, :, None, None], kern[1][:, jnp.clip(-lag, 0, c - 1)], 0.0)
    t_mat = (tf + tb).transpose(0, 1, 4, 2, 3).reshape(n_g, c * SSM_CH, c * SSM_CH)
    mf = pw[:c][::-1, 0][:, :, :, None] * b_bar[0][None]
    mb = pw[:c, 1][:, :, :, None] * b_bar[1][None]
    mf = mf.transpose(1, 0, 3, 2).reshape(n_g, c * SSM_CH, SSM_P)
    mb = mb.transpose(1, 0, 3, 2).reshape(n_g, c * SSM_CH, SSM_P)
    mq = jnp.stack([jnp.real(mf), jnp.imag(mf), jnp.real(mb), jnp.imag(mb)], axis=2)
    zf = c_mat[0][:, None] * pw[1:c + 1, 0][:, :, None, :].transpose(1, 0, 2, 3)
    zb = c_mat[1][:, None] * pw[1:c + 1, 1][::-1][:, :, None, :].transpose(1, 0, 2, 3)
    zf = zf.transpose(0, 3, 1, 2).reshape(n_g, SSM_P, c * SSM_CH)
    zb = zb.transpose(0, 3, 1, 2).reshape(n_g, SSM_P, c * SSM_CH)
    wq = jnp.stack([jnp.real(zf), -jnp.imag(zf), jnp.real(zb), -jnp.imag(zb)], axis=1)
    gh = LANES // SSM_CH
    n_half = n_g // gh
    kd = c * SSM_CH

    def interleaved_block_diag(blocks, r1, c1):
        n = gh * kd
        b = blocks.astype(BF16).reshape(n_half, gh, kd, kd)
        bd = jnp.zeros((n_half, gh, kd, gh, kd), BF16)
        for g in range(gh):
            bd = bd.at[:, g, :, g, :].set(b[:, g])
        bd = bd.reshape(n_half, gh, r1, kd // r1, n).transpose(0, 2, 1, 3, 4).reshape(n_half, n, n)
        bd = jnp.swapaxes(bd, 1, 2)
        bd = bd.reshape(n_half, gh, c1, kd // c1, n).transpose(0, 2, 1, 3, 4).reshape(n_half, n, n)
        return jnp.swapaxes(bd, 1, 2)

    m_half = interleaved_block_diag(mq.reshape(n_g, kd, 4 * SSM_P), c, 4)
    t_half = interleaved_block_diag(t_mat, c, c)
    w_half = interleaved_block_diag(wq.reshape(n_g, 4 * SSM_P, kd), 4, c)
    tw_half = jnp.concatenate([t_half, w_half], axis=1)
    a_c = pw[c]
    a16 = jnp.stack([jnp.real(a_c[0]), jnp.imag(a_c[0]), jnp.real(a_c[1]), jnp.imag(a_c[1])])
    a16 = a16.reshape(4, 1, n_half, gh * SSM_P).transpose(2, 0, 1, 3)
    dvec = ssm_d.astype(F32).reshape(n_half, 1, LANES)
    return m_half, tw_half, a16, dvec


def _s5_state_body(u_ref, m_ref, fre_ref, fim_ref, bre_ref, bim_ref):
    nb, tn, kd = u_ref.shape
    w = fre_ref.shape[2]
    r = _dot(u_ref[...].reshape(nb * tn, kd).astype(BF16), m_ref[...])
    for q, o_ref in enumerate((fre_ref, fim_ref, bre_ref, bim_ref)):
        for b in range(nb):
            o_ref[:, b, :] = r[b * tn:(b + 1) * tn, q * w:(q + 1) * w]


def _s5_rows(nb, n):
    return max(8, min(n, 256 // nb))


def _s5_states(u3, m_half):
    nb, n, kd = u3.shape
    tn = _s5_rows(nb, n)
    w = m_half.shape[1] // 4
    out = pl.BlockSpec((tn, nb, w), lambda i: (i, 0, 0))
    return pl.pallas_call(
        _s5_state_body,
        name="s5_states",
        grid=(n // tn,),
        in_specs=[pl.BlockSpec((nb, tn, kd), lambda i: (0, i, 0)),
                  pl.BlockSpec(m_half.shape, lambda i: (0, 0))],
        out_specs=[out] * 4,
        out_shape=[jax.ShapeDtypeStruct((n, nb, w), F32)] * 4,
        compiler_params=_cparams("arbitrary"),
    )(u3, m_half)


def _s5_scan_body(sfr_ref, sfi_ref, sbr_ref, sbi_ref, a_ref, h0_ref,
                  hfr_ref, hfi_ref, hbr_ref, hbi_ref, fin_ref, st_ref):
    @pl.when(pl.program_id(0) == 0)
    def _():
        st_ref[...] = h0_ref[...]

    ks = sfr_ref.shape[0]
    afr, afi, abr, abi = a_ref[0], a_ref[1], a_ref[2], a_ref[3]

    def step(s, carry):
        fr, fi, br, bi = carry
        sb = ks - 1 - s
        hfr_ref[s] = fr
        hfi_ref[s] = fi
        hbr_ref[sb] = br
        hbi_ref[sb] = bi
        nfr = afr * fr - afi * fi + sfr_ref[s]
        nfi = afr * fi + afi * fr + sfi_ref[s]
        nbr = abr * br - abi * bi + sbr_ref[sb]
        nbi = abr * bi + abi * br + sbi_ref[sb]
        return nfr, nfi, nbr, nbi

    carry = lax.fori_loop(0, ks, step, (st_ref[0], st_ref[1], st_ref[2], st_ref[3]))
    for q in range(4):
        st_ref[q] = carry[q]
        fin_ref[q] = carry[q]


def _s5_scan(s3, a16, h0):
    n_chunks, n_batch, width = s3[0].shape
    ks = min(n_chunks, 32)
    nb = n_chunks // ks
    fwd = pl.BlockSpec((ks, n_batch, width), lambda i: (i, 0, 0))
    bwd = pl.BlockSpec((ks, n_batch, width), lambda i: (nb - 1 - i, 0, 0))
    small = lambda shape: pl.BlockSpec(shape, lambda i: (0, 0, 0))
    outs = pl.pallas_call(
        _s5_scan_body,
        name="s5_scan",
        grid=(nb,),
        in_specs=[fwd, fwd, bwd, bwd, small(a16.shape), small(h0.shape)],
        out_specs=[fwd, fwd, bwd, bwd, small(h0.shape)],
        out_shape=[jax.ShapeDtypeStruct((n_chunks, n_batch, width), F32)] * 4
        + [jax.ShapeDtypeStruct(h0.shape, F32)],
        scratch_shapes=[pltpu.VMEM(h0.shape, F32)],
        compiler_params=_cparams("arbitrary"),
    )(*s3, a16, h0)
    return outs[:4], outs[4]


def _s5_out_body(u_ref, us_ref, hfr_ref, hfi_ref, hbr_ref, hbi_ref, tw_ref, d_ref, y_ref):
    nb, tn, kd = u_ref.shape
    rows = nb * tn
    wo = us_ref.shape[2]
    y = _dot(u_ref[...].reshape(rows, kd).astype(BF16), tw_ref[0:kd, :])
    w = hfr_ref.shape[2]
    for q, h_ref in enumerate((hfr_ref, hfi_ref, hbr_ref, hbi_ref)):
        hq = jnp.concatenate([h_ref[:, b, :] for b in range(nb)], axis=0)
        y = y + _dot(hq.astype(BF16), tw_ref[kd + q * w:kd + (q + 1) * w, :])
    skip = jnp.concatenate([d_ref[...]] * (wo // LANES), axis=1) * us_ref[...].reshape(rows, wo)
    y_ref[...] = jax.nn.gelu(y + skip).reshape(nb, tn, wo)


def _s5_out(u3, h4, tw_half, dvec):
    nb, n, kd = u3.shape
    tn = _s5_rows(nb, n)
    wo = S5_OUT_STEPS * LANES
    w = h4[0].shape[2]
    hb = pl.BlockSpec((tn, nb, w), lambda i, j: (i, 0, 0))
    return pl.pallas_call(
        _s5_out_body,
        name="s5_out",
        grid=(n // tn, kd // wo),
        in_specs=[pl.BlockSpec((nb, tn, kd), lambda i, j: (0, i, 0)),
                  pl.BlockSpec((nb, tn, wo), lambda i, j: (0, i, j)),
                  hb, hb, hb, hb,
                  pl.BlockSpec((tw_half.shape[0], wo), lambda i, j: (0, j)),
                  pl.BlockSpec(dvec.shape, lambda i, j: (0, 0))],
        out_specs=pl.BlockSpec((nb, tn, wo), lambda i, j: (0, i, j)),
        out_shape=jax.ShapeDtypeStruct(u3.shape, F32),
        compiler_params=_cparams("arbitrary", "arbitrary"),
    )(u3, u3, *h4, tw_half, dvec)


def _s5(pc_halves, n_batch, seq, weights, h0):
    m_half, tw_half, a16, dvec = weights
    n_chunks = seq // S5_CHUNK
    ys, fins = [], []
    for h, pc in enumerate(pc_halves):
        w = m_half.shape[2] // 4
        u3 = pc.reshape(n_batch, n_chunks, S5_CHUNK * LANES)
        s4 = _s5_states(u3, m_half[h])
        h4, fin = _s5_scan(s4, a16[h], h0[:, :, h * w:(h + 1) * w])
        ys.append(_s5_out(u3, h4, tw_half[h], dvec[h]).reshape(n_batch * n_chunks, S5_CHUNK * LANES))
        fins.append(fin)
    return ys, jnp.concatenate(fins, axis=2)


def _post_body(x_ref, ya_ref, yb_ref, yc0_ref, yc1_ref, mod_ref, gpost_ref, gffn_ref, wglu_ref, bglu_ref, wo_ref,
               x1_ref, h2_ref, h2p_ref, unfold_ref):
    x = x_ref[...]
    d = x.shape[1]
    d_a = ya_ref.shape[1]
    d_b = yb_ref.shape[1]
    for h, yc_ref in enumerate((yc0_ref, yc1_ref)):
        _unfold_chunks(unfold_ref.at[h], yc_ref)
    y = jnp.concatenate([unfold_ref[0], unfold_ref[1]], axis=1)
    glu = y * jax.nn.sigmoid(_dot(y.astype(BF16), wglu_ref[...]) + bglu_ref[...])
    mixed = (_dot(ya_ref[...], wo_ref[0:d_a, :]) + _dot(yb_ref[...], wo_ref[d_a:d_a + d_b, :])
             + _dot(glu.astype(BF16), wo_ref[d_a + d_b:, :]))
    x1 = x + mod_ref[0, :, 2 * d:3 * d] * _rms(mixed, gpost_ref[...])
    x1_ref[...] = x1
    h2 = _rms(x1, gffn_ref[...]) * (1 + mod_ref[0, :, 4 * d:5 * d]) + mod_ref[0, :, 3 * d:4 * d]
    _store_tiled(h2_ref, h2)
    _store_tiled(h2p_ref, _pack_halves(h2))


def _post(x, ya, yb, yc0, yc1, mod, g_post, g_ffn, w_glu_b, b_glu, w_out_b, row_map):
    t, d = x.shape
    tm = TOK_TILE
    nc = d // LANES
    tok = lambda n: pl.BlockSpec((tm, n), lambda i: (i, 0))
    fold = pl.BlockSpec((tm // S5_CHUNK, S5_CHUNK * LANES), lambda i: (i, 0))
    full = lambda a: pl.BlockSpec(a.shape, lambda i: (0,) * a.ndim)
    return pl.pallas_call(
        _post_body,
        name="post_mix",
        grid=(t // tm,),
        in_specs=[tok(d), tok(ya.shape[1]), tok(yb.shape[1]), fold, fold,
                  pl.BlockSpec((1, 1, mod.shape[2]), lambda i: (row_map(i), 0, 0)),
                  full(g_post), full(g_ffn), full(w_glu_b), full(b_glu), full(w_out_b)],
        out_specs=[tok(d), pl.BlockSpec((tm * nc, LANES), lambda i: (i, 0)),
                   pl.BlockSpec((tm * nc // 2, LANES), lambda i: (i, 0))],
        out_shape=[jax.ShapeDtypeStruct((t, d), F32), jax.ShapeDtypeStruct((t * nc, LANES), F32),
                   jax.ShapeDtypeStruct((t * nc // 2, LANES), U32)],
        scratch_shapes=[pltpu.VMEM((2, tm, LANES), F32)],
        compiler_params=_cparams("arbitrary"),
    )(x, ya, yb, yc0, yc1, mod, g_post, g_ffn, w_glu_b, b_glu, w_out_b)


def _route_body(h_ref, whi_ref, wlo_ref, b_ref, idx_ref, gate_ref, pos_ref, cnt_ref, carry_ref):
    @pl.when(pl.program_id(0) == 0)
    def _():
        carry_ref[...] = jnp.zeros_like(carry_ref)

    tm = idx_ref.shape[1]
    h = _load_tiled(h_ref, tm)
    hi = h.astype(BF16)
    lo = (h - hi.astype(F32)).astype(BF16)
    logits = _dot_nt(whi_ref[...], hi) + (_dot_nt(whi_ref[...], lo) + _dot_nt(wlo_ref[...], hi))
    scores = jax.nn.sigmoid(logits)
    sel = scores + b_ref[...]
    gsz = N_EXPERTS // N_EXP_GROUPS
    within = lax.broadcasted_iota(I32, (gsz, tm), 0).astype(F32)
    grp = []
    for g in range(N_EXP_GROUPS):
        blk = sel[g * gsz:(g + 1) * gsz]
        m1 = jnp.max(blk, axis=0, keepdims=True)
        first = jnp.min(jnp.where(blk == m1, within, float(gsz)), axis=0, keepdims=True)
        m2 = jnp.max(jnp.where(within == first, -jnp.inf, blk), axis=0, keepdims=True)
        grp.append(m1 + m2)
    blocks = []
    for g in range(N_EXP_GROUPS):
        ahead = jnp.zeros((1, tm), F32)
        for o in range(N_EXP_GROUPS):
            if o == g:
                continue
            beats = (grp[o] >= grp[g]) if o < g else (grp[o] > grp[g])
            ahead = ahead + jnp.where(beats, 1.0, 0.0)
        ahead = jnp.broadcast_to(ahead, (gsz, tm))
        blocks.append(jnp.where(ahead < TOPK_GROUPS, sel[g * gsz:(g + 1) * gsz], -jnp.inf))
    v = jnp.concatenate(blocks, axis=0)
    eidx = lax.broadcasted_iota(I32, (N_EXPERTS, tm), 0)
    rnk = jnp.zeros((N_EXPERTS, tm), F32)
    for e in range(N_EXPERTS):
        row = v[e:e + 1]
        rnk = rnk + jnp.where(eidx > e, jnp.where(row >= v, 1.0, 0.0), jnp.where(row > v, 1.0, 0.0))
    chosen = rnk < TOP_K
    w = jnp.where(chosen, scores, 0.0)
    wn = w / jnp.sum(w, axis=0, keepdims=True) * ROUTE_SCALE
    tri = (lax.broadcasted_iota(I32, (tm, tm), 0) < lax.broadcasted_iota(I32, (tm, tm), 1))
    chosen_f = jnp.where(chosen, 1.0, 0.0)
    prefix = _dot(chosen_f.astype(BF16), jnp.where(tri, 1.0, 0.0).astype(BF16)) + carry_ref[:, 0:1]
    total = carry_ref[...] + jnp.sum(chosen_f, axis=1, keepdims=True)
    carry_ref[...] = total
    cnt_ref[...] = total
    eidx_f = eidx.astype(F32)
    for k in range(TOP_K):
        one = rnk == k
        idx_ref[k:k + 1, :] = jnp.sum(jnp.where(one, eidx_f, 0.0), axis=0, keepdims=True).astype(I32)
        gate_ref[k:k + 1, :] = jnp.sum(jnp.where(one, wn, 0.0), axis=0, keepdims=True)
        pos_ref[k:k + 1, :] = jnp.sum(jnp.where(one, prefix, 0.0), axis=0, keepdims=True).astype(I32)


def _route(h2t, w_hi, w_lo, b_router):
    nc = w_hi.shape[1] // LANES
    t = h2t.shape[0] // nc
    tm = TOK_TILE
    out = pl.BlockSpec((TOP_K, tm), lambda i: (0, i))
    full = lambda a: pl.BlockSpec(a.shape, lambda i: (0, 0))
    return pl.pallas_call(
        _route_body,
        name="route",
        grid=(t // tm,),
        in_specs=[pl.BlockSpec((tm * nc, LANES), lambda i: (i, 0)), full(w_hi), full(w_lo), full(b_router)],
        out_specs=[out, out, out, pl.BlockSpec((N_EXPERTS, LANES), lambda i: (0, 0))],
        out_shape=[jax.ShapeDtypeStruct((TOP_K, t), I32), jax.ShapeDtypeStruct((TOP_K, t), F32),
                   jax.ShapeDtypeStruct((TOP_K, t), I32), jax.ShapeDtypeStruct((N_EXPERTS, LANES), F32)],
        scratch_shapes=[pltpu.VMEM((N_EXPERTS, LANES), F32)],
        compiler_params=_cparams("arbitrary"),
    )(h2t, w_hi, w_lo, b_router)


def _dest_body(starts_ref, idx_ref, pos_ref, dest_ref):
    idx = idx_ref[...]
    acc = pos_ref[...]
    for e in range(N_EXPERTS):
        acc = acc + jnp.where(idx == e, starts_ref[e], 0)
    dest_ref[...] = acc


def _dest(starts, idx, pos):
    t = idx.shape[1]
    tile = math.gcd(t, 4096)
    blk = pl.BlockSpec((TOP_K, tile), lambda i, st: (0, i))
    return pl.pallas_call(
        _dest_body,
        name="moe_dest",
        grid_spec=pltpu.PrefetchScalarGridSpec(num_scalar_prefetch=1, grid=(t // tile,),
                                               in_specs=[blk, blk], out_specs=blk),
        out_shape=jax.ShapeDtypeStruct(idx.shape, I32),
        compiler_params=_cparams("arbitrary"),
    )(starts, idx, pos)


def _dispatch_body(pad_lo_ref, pad_hi_ref, dest_ref, h_ref, xs_ref, zero_ref, sem, zsem, *, n_pad_rows):
    tile = dest_ref.shape[1]
    nw = h_ref.shape[0] // tile

    def token(ref, r):
        return ref.at[pl.ds(pl.multiple_of(r * nw, nw), nw)]

    def zero_copy(r, n):
        return pltpu.make_async_copy(zero_ref.at[pl.ds(0, n * nw)],
                                     xs_ref.at[pl.ds(pl.multiple_of(r * nw, nw), n * nw)], zsem)

    @pl.when(pl.program_id(0) == 0)
    def _():
        zero_ref[...] = jnp.zeros_like(zero_ref)

        def per_expert(e, c):
            lo = pad_lo_ref[e]
            runs = (pad_hi_ref[e] - lo) // ZERO_RUN

            def run(j, c2):
                zero_copy(lo + j * ZERO_RUN, ZERO_RUN).start()
                return c2

            def one(r, c2):
                zero_copy(r, 1).start()
                return c2

            c = lax.fori_loop(0, runs, run, c)
            return lax.fori_loop(lo + runs * ZERO_RUN, pad_hi_ref[e], one, c)

        lax.fori_loop(0, N_EXPERTS, per_expert, 0)

    def row_copy(t, k):
        return pltpu.make_async_copy(token(h_ref, t), token(xs_ref, dest_ref[k, t]), sem)

    def start(t, c):
        for k in range(TOP_K):
            row_copy(t, k).start(priority=k % DMA_THREADS)
        return c

    def wait(t, c):
        for k in range(TOP_K):
            row_copy(t, k).wait()
        return c

    lax.fori_loop(0, tile, start, 0)
    lax.fori_loop(0, tile, wait, 0)

    @pl.when(pl.program_id(0) == 0)
    def _():
        def one(r, c):
            zero_copy(0, ZERO_RUN).wait()
            return c
        lax.fori_loop(0, n_pad_rows // ZERO_RUN, one, 0)


def _dispatch(pad_lo, pad_hi, dest, h2p, n_rows):
    t = dest.shape[1]
    nw = h2p.shape[0] // t
    tile = DISPATCH_TILE
    grid_spec = pltpu.PrefetchScalarGridSpec(
        num_scalar_prefetch=2,
        grid=(t // tile,),
        in_specs=[pl.BlockSpec((TOP_K, tile), lambda i, lo, hi: (0, i), memory_space=pltpu.SMEM),
                  pl.BlockSpec((tile * nw, LANES), lambda i, lo, hi: (i, 0))],
        out_specs=pl.BlockSpec(memory_space=pl.ANY),
        scratch_shapes=[pltpu.VMEM((ZERO_RUN * nw, LANES), U32), pltpu.SemaphoreType.DMA(()),
                        pltpu.SemaphoreType.DMA(())],
    )
    assert (n_rows - t * TOP_K) % ZERO_RUN == 0
    return pl.pallas_call(
        functools.partial(_dispatch_body, n_pad_rows=n_rows - t * TOP_K),
        name="moe_dispatch",
        grid_spec=grid_spec,
        out_shape=jax.ShapeDtypeStruct((n_rows * nw, LANES), U32),
        compiler_params=_cparams("arbitrary"),
    )(pad_lo, pad_hi, dest, h2p)


def _experts_body(be_ref, x_ref, wg_ref, wu_ref, wd_ref, y_ref, wgb_ref, wub_ref, wdb_ref):
    i = pl.program_id(0)

    @pl.when((i == 0) | (be_ref[i] != be_ref[jnp.maximum(i - 1, 0)]))
    def _():
        wgb_ref[...] = wg_ref[0, 0].astype(BF16)
        wub_ref[...] = wu_ref[0, 0].astype(BF16)
        wdb_ref[...] = wd_ref[0, 0].astype(BF16)

    xb = _unpack_halves(_load_tiled(x_ref, MOE_ROWS)).astype(BF16)
    hb = jax.nn.silu(_dot(xb, wgb_ref[...])) * _dot(xb, wub_ref[...])
    _store_tiled(y_ref, _pack_halves(_dot(hb.astype(BF16), wdb_ref[...])))


def _experts(blk_exp, xs, wg, wu, wd, layer):
    d, de = wg.shape[2], wg.shape[3]
    nc = d // (2 * LANES)
    bm = MOE_ROWS
    tiles = pl.BlockSpec((bm * nc, LANES), lambda i, be: (i, 0))
    grid_spec = pltpu.PrefetchScalarGridSpec(
        num_scalar_prefetch=1,
        grid=(xs.shape[0] // (bm * nc),),
        in_specs=[tiles,
                  pl.BlockSpec((1, 1, d, de), lambda i, be: (layer, be[i], 0, 0)),
                  pl.BlockSpec((1, 1, d, de), lambda i, be: (layer, be[i], 0, 0)),
                  pl.BlockSpec((1, 1, de, d), lambda i, be: (layer, be[i], 0, 0))],
        out_specs=tiles,
        scratch_shapes=[pltpu.VMEM((d, de), BF16), pltpu.VMEM((d, de), BF16), pltpu.VMEM((de, d), BF16)],
    )
    return pl.pallas_call(
        _experts_body,
        name="moe_experts",
        grid_spec=grid_spec,
        out_shape=jax.ShapeDtypeStruct(xs.shape, U32),
        compiler_params=_cparams("arbitrary"),
    )(blk_exp, xs, wg, wu, wd)


def _combine_body(dest_ref, next_ref, gate_ref, h_ref, x_ref, mod_ref, g_ref, wsg_ref, wsu_ref, wsd_ref, ys_hbm,
                  o_ref, buf_ref, sem):
    tile, d = x_ref.shape
    nw = buf_ref.shape[2] // tile
    i = pl.program_id(0)
    slot = i % 2

    def row_copy(d_ref, s, t, k):
        return pltpu.make_async_copy(ys_hbm.at[pl.ds(pl.multiple_of(d_ref[k, t] * nw, nw), nw)],
                                     buf_ref.at[s, k, pl.ds(pl.multiple_of(t * nw, nw), nw)], sem.at[s])

    def gather(d_ref, s):
        def start(t, c):
            for k in range(TOP_K):
                row_copy(d_ref, s, t, k).start(priority=k % DMA_THREADS)
            return c
        lax.fori_loop(0, tile, start, 0)

    @pl.when(i == 0)
    def _():
        gather(dest_ref, 0)

    @pl.when(i + 1 < pl.num_programs(0))
    def _():
        gather(next_ref, 1 - slot)

    hb = _load_tiled(h_ref, tile).astype(BF16)
    acc = _dot((jax.nn.silu(_dot(hb, wsg_ref[...])) * _dot(hb, wsu_ref[...])).astype(BF16), wsd_ref[...])

    def wait(t, c):
        for k in range(TOP_K):
            row_copy(dest_ref, slot, 0, k).wait()
        return c

    lax.fori_loop(0, tile, wait, 0)
    gates = gate_ref[...]
    moe = gates[:, 0:1] * _unpack_halves(_load_tiled(buf_ref.at[slot, 0], tile))
    for k in range(1, TOP_K):
        moe = moe + gates[:, k:k + 1] * _unpack_halves(_load_tiled(buf_ref.at[slot, k], tile))
    o_ref[...] = x_ref[...] + mod_ref[0, :, 5 * d:6 * d] * _rms(moe + acc, g_ref[...])


def _combine(dest, gates, h2t, x1, mod, g_post, wsg, wsu, wsd, ys, row_map):
    t, d = x1.shape
    nc = d // LANES
    nw = nc // 2
    tile = COMBINE_TILE
    per_tok = TOK_TILE // tile
    n_steps = t // tile
    tok = lambda n: pl.BlockSpec((tile, n), lambda i: (i, 0))
    full = lambda a: pl.BlockSpec(a.shape, lambda i: (0,) * a.ndim)
    return pl.pallas_call(
        _combine_body,
        name="moe_combine",
        grid=(n_steps,),
        in_specs=[pl.BlockSpec((TOP_K, tile), lambda i: (0, i), memory_space=pltpu.SMEM),
                  pl.BlockSpec((TOP_K, tile), lambda i: (0, jnp.minimum(i + 1, n_steps - 1)),
                               memory_space=pltpu.SMEM),
                  tok(TOP_K), pl.BlockSpec((tile * nc, LANES), lambda i: (i, 0)), tok(d),
                  pl.BlockSpec((1, 1, mod.shape[2]), lambda i: (row_map(i // per_tok), 0, 0)),
                  full(g_post), full(wsg), full(wsu), full(wsd),
                  pl.BlockSpec(memory_space=pl.ANY)],
        out_specs=tok(d),
        out_shape=jax.ShapeDtypeStruct((t, d), F32),
        scratch_shapes=[pltpu.VMEM((2, TOP_K, tile * nw, LANES), U32), pltpu.SemaphoreType.DMA((2,))],
        compiler_params=_cparams("arbitrary"),
    )(dest, dest, gates, h2t, x1, mod, g_post, wsg, wsu, wsd, ys)


def _moe(h2t, h2p, x1, mod, p, row_map):
    t, d = x1.shape
    idx, gate, pos, cnt = _route(h2t, p['wr_hi'], p['wr_lo'], p['b_router'])
    counts = cnt[:, 0].astype(I32)
    padded = (counts + MOE_ROWS - 1) // MOE_ROWS * MOE_ROWS
    ends = jnp.cumsum(padded)
    starts = ends - padded
    n_blocks = (t * TOP_K + N_EXPERTS * (MOE_ROWS - 1)) // MOE_ROWS + 1
    n_rows = n_blocks * MOE_ROWS
    blk_start = jnp.arange(n_blocks, dtype=I32) * MOE_ROWS
    blk_exp = jnp.minimum(jnp.sum((ends[None, :] <= blk_start[:, None]).astype(I32), axis=1), N_EXPERTS - 1)
    pad_hi = ends.at[N_EXPERTS - 1].set(n_rows)
    dest = _dest(starts, idx, pos)
    xs = _dispatch(starts + counts, pad_hi, dest, h2p, n_rows)
    ys = _experts(blk_exp, xs, p['w_e_gate'], p['w_e_up'], p['w_e_down'], p['layer'])
    return _combine(dest, gate.T, h2t, x1, mod, p['g_post_ffn'], p['w_s_gate'], p['w_s_up'], p['w_s_down'],
                    ys, row_map)


def kernel(x_prompt, x_sample, cache_k, cache_v, state_ssm_re, state_ssm_im, c, c_ctx,
           g_pre_mix, g_post_mix, g_pre_ffn, g_post_ffn, w_ada, b_ada, w_in, w_out,
           sgu_g, w_sp, b_sp, rpb, ssm_a_re, ssm_a_im, ssm_log_dt, ssm_b_re, ssm_b_im,
           ssm_c_re, ssm_c_im, ssm_d, w_glu, b_glu, w_router, b_router,
           w_e_gate, w_e_up, w_e_down, w_s_gate, w_s_up, w_s_down):
    n_pb, p_seq, d = x_prompt.shape
    n_sb, s_seq, _ = x_sample.shape
    depth = w_in.shape[0]
    d_a = sgu_g.shape[1]
    d_c = w_glu.shape[1]
    d_b = d - d_a - d_c
    n_heads = d_b // HD_B
    n_g = d_c // SSM_CH
    t_p = n_pb * p_seq
    assert p_seq % CHUNK == 0 and t_p % TOK_TILE == 0 and s_seq % TOK_TILE == 0 and t_p % s_seq == 0
    assert s_seq % (NAT_ROWS * GRID_W) == 0 and s_seq // GRID_W >= NA_WIN_R and d_c == 2 * LANES

    mod_rows = -(-(n_sb + 1) // 8) * 8
    c_all = jnp.zeros((mod_rows, d), F32).at[:n_sb].set(c).at[n_sb].set(c_ctx)
    mod_all = _ada(c_all, w_ada, b_ada)
    p_tiles = t_p // TOK_TILE
    s_tiles = s_seq // TOK_TILE

    def row_map(i):
        return jnp.where(i < p_tiles, n_sb, (i - p_tiles) // s_tiles)

    x = jnp.concatenate([x_prompt.reshape(t_p, d), x_sample.reshape(n_sb * s_seq, d)], axis=0)
    s5w_all = jax.vmap(_s5_weights)(ssm_a_re, ssm_a_im, ssm_log_dt, ssm_b_re, ssm_b_im, ssm_c_re, ssm_c_im, ssm_d)
    nat_bias_all = jax.vmap(_nat_bias)(rpb)
    new_k, new_v, new_re, new_im = [], [], [], []
    for l in range(depth):
        mod = mod_all[l][:, None, :]
        row = lambda a: a[l][None, :].astype(F32)
        pa, q, k32, v32, kb, vb, *pc = _premix(x, mod, row(g_pre_mix), w_in[l].astype(BF16), row_map,
                                               d_a, d_b, d_c, t_p)
        bias_a = jnp.repeat(b_sp[l].T.astype(F32), d_a // NH_A, axis=1)
        ya = _chunk_mlp(pa, row(sgu_g), w_sp[l].reshape(NH_A * CHUNK, CHUNK).astype(BF16), bias_a)
        yb_p = _ctx_attn(q, k32, v32, n_pb, p_seq)
        ck = cache_k[:, l].reshape(n_sb, -1, d_b).astype(BF16)
        cv = cache_v[:, l].reshape(n_sb, -1, d_b).astype(BF16)
        yb_s = _nat_attn(q, kb, vb, ck, cv, nat_bias_all[l], n_sb, s_seq, t_p)
        yb = jnp.concatenate([yb_p, yb_s], axis=0)
        s5w = [w[l] for w in s5w_all]
        c_p = t_p // S5_CHUNK
        yc_p, fin_p = _s5([h[:c_p] for h in pc], n_pb, p_seq, s5w, jnp.zeros((4, n_pb, n_g * SSM_P), F32))
        sre = state_ssm_re[:, l].astype(F32).reshape(n_sb, 2, n_g * SSM_P)
        sim = state_ssm_im[:, l].astype(F32).reshape(n_sb, 2, n_g * SSM_P)
        h0_s = jnp.stack([sre[:, 0], sim[:, 0], sre[:, 1], sim[:, 1]])
        yc_s, _ = _s5([h[c_p:] for h in pc], n_sb, s_seq, s5w, h0_s)
        yc = [jnp.concatenate([a, b], axis=0) for a, b in zip(yc_p, yc_s)]
        x1, h2, h2p = _post(x, ya, yb, yc[0], yc[1],
                       mod, row(g_post_mix), row(g_pre_ffn), w_glu[l].astype(BF16), row(b_glu),
                       w_out[l].astype(BF16), row_map)
        wr = w_router[l].astype(F32).T
        wr_hi = wr.astype(BF16)
        moe_p = {
            'wr_hi': wr_hi, 'wr_lo': (wr - wr_hi.astype(F32)).astype(BF16),
            'b_router': b_router[l].astype(F32)[:, None],
            'w_e_gate': w_e_gate, 'w_e_up': w_e_up, 'w_e_down': w_e_down, 'layer': l,
            'w_s_gate': w_s_gate[l].astype(BF16), 'w_s_up': w_s_up[l].astype(BF16),
            'w_s_down': w_s_down[l].astype(BF16), 'g_post_ffn': row(g_post_ffn),
        }
        x = _moe(h2, h2p, x1, mod, moe_p, row_map)
        new_k.append(k32[:t_p].reshape(n_pb, p_seq, n_heads, HD_B))
        new_v.append(v32[:t_p].reshape(n_pb, p_seq, n_heads, HD_B))
        fin_p = fin_p.reshape(4, n_pb, n_g, SSM_P)
        new_re.append(jnp.stack([fin_p[0], fin_p[2]], axis=1))
        new_im.append(jnp.stack([fin_p[1], fin_p[3]], axis=1))
    return (x[:t_p].reshape(n_pb, p_seq, d), x[t_p:].reshape(n_sb, s_seq, d),
            jnp.stack(new_k, axis=1), jnp.stack(new_v, axis=1),
            jnp.stack(new_re, axis=1), jnp.stack(new_im, axis=1))
```

```python
import functools
import math

import jax
import jax.numpy as jnp
from jax import lax
from jax.experimental import pallas as pl
from jax.experimental.pallas import tpu as pltpu

F32 = jnp.float32
BF16 = jnp.bfloat16
I32 = jnp.int32

GRID_W = 64
EPS = 1e-6
NH_A = 4
CHUNK = 128
HD_B = 64
NA_WIN_R = 8
NA_WIN_C = 16
ATTN_SCALE = HD_B ** -0.5
SSM_CH = 16
SSM_P = 64
N_EXPERTS = 64
TOP_K = 8
N_EXP_GROUPS = 8
TOPK_GROUPS = 4
ROUTE_SCALE = 2.5

LANES = 128
TOK_TILE = 512
S5_CHUNK = 16
S5_OUT_STEPS = 4
MOE_ROWS = 1024
COMBINE_TILE = 256
DISPATCH_TILE = 512
ZERO_RUN = 64
NAT_ROWS = 4
DMA_THREADS = 2
NEG_BIG = -1e30
VMEM_LIMIT = 48 * 1024 * 1024


def _cparams(*sem):
    return pltpu.CompilerParams(dimension_semantics=sem, vmem_limit_bytes=VMEM_LIMIT)


def _dot(a, b):
    return jnp.dot(a, b, preferred_element_type=F32)


def _dot_nt(a, b):
    return lax.dot_general(a, b, (((1,), (1,)), ((), ())), preferred_element_type=F32)


def _rms(x, g):
    return x * lax.rsqrt(jnp.mean(x * x, axis=-1, keepdims=True) + EPS) * g


def _load_tiled(ref, n_rows):
    nc = ref.shape[0] // n_rows
    return jnp.concatenate([ref[pl.ds(j, n_rows, stride=nc), :] for j in range(nc)], axis=1)


def _store_tiled(ref, val):
    n_rows = val.shape[0]
    nc = ref.shape[0] // n_rows
    for j in range(nc):
        ref[pl.ds(j, n_rows, stride=nc), :] = val[:, j * LANES:(j + 1) * LANES]


U32 = jnp.uint32


def _pack_halves(x):
    half = x.shape[1] // 2
    lo = lax.bitcast_convert_type(x[:, :half].astype(BF16).astype(F32), U32)
    hi = lax.bitcast_convert_type(x[:, half:].astype(BF16).astype(F32), U32)
    return (lo >> 16) | hi


def _unpack_halves(w):
    lo = lax.bitcast_convert_type(w << 16, F32)
    hi = lax.bitcast_convert_type(w & jnp.uint32(0xFFFF0000), F32)
    return jnp.concatenate([lo, hi], axis=1)


def _ada_body(c_ref, w_ref, b_ref, o_ref):
    s = jax.nn.silu(c_ref[...]).astype(BF16)
    o_ref[0] = _dot(s, w_ref[0].astype(BF16)) + b_ref[0]


def _ada(c_all, w_ada, b_ada):
    n_layers, d, n = w_ada.shape
    rows = c_all.shape[0]
    tn = 1536
    return pl.pallas_call(
        _ada_body,
        name="ada",
        grid=(n_layers, n // tn),
        in_specs=[pl.BlockSpec((rows, d), lambda l, j: (0, 0)),
                  pl.BlockSpec((1, d, tn), lambda l, j: (l, 0, j)),
                  pl.BlockSpec((1, 1, tn), lambda l, j: (l, 0, j))],
        out_specs=pl.BlockSpec((1, rows, tn), lambda l, j: (l, 0, j)),
        out_shape=jax.ShapeDtypeStruct((n_layers, rows, n), F32),
        compiler_params=_cparams("arbitrary", "arbitrary"),
    )(c_all, w_ada, b_ada.reshape(n_layers, 1, n))


def _stream_tile(xc_ref, xl_ref, ctx_tiles):
    return jnp.where(pl.program_id(0) < ctx_tiles, xc_ref[...], xl_ref[...])


def _stream_specs(tm, d, ctx_tiles):
    return [pl.BlockSpec((tm, d), lambda i: (jnp.minimum(i, ctx_tiles - 1), 0)),
            pl.BlockSpec((tm, d), lambda i: (jnp.maximum(i - ctx_tiles, 0), 0))]


def _premix_body(xc_ref, xl_ref, mod_ref, g_ref, w_ref, pa_ref, q_ref, k_ref, v_ref, kb_ref, vb_ref, pc0_ref,
                 pc1_ref, fold_ref, *, ctx_tiles):
    x = _stream_tile(xc_ref, xl_ref, ctx_tiles)
    d = x.shape[1]
    h = _rms(x, g_ref[...]) * (1 + mod_ref[0, :, d:2 * d]) + mod_ref[0, :, 0:d]
    p = _dot(h.astype(BF16), w_ref[...])
    d_a2 = pa_ref.shape[1]
    d_b = q_ref.shape[1]
    o = d_a2
    pa_ref[...] = p[:, 0:o]
    q_ref[...] = p[:, o:o + d_b].astype(BF16)
    k = p[:, o + d_b:o + 2 * d_b]
    v = p[:, o + 2 * d_b:o + 3 * d_b]

    @pl.when(pl.program_id(0) < ctx_tiles)
    def _():
        k_ref[...] = k
        v_ref[...] = v

    kb_ref[...] = k.astype(BF16)
    vb_ref[...] = v.astype(BF16)
    for h, pc_ref in enumerate((pc0_ref, pc1_ref)):
        fold_ref[...] = p[:, o + 3 * d_b + h * LANES:o + 3 * d_b + (h + 1) * LANES]
        _fold_chunks(pc_ref, fold_ref)


def _fold_chunks(dst_ref, src_ref):
    n = dst_ref.shape[0]
    for i in range(S5_CHUNK):
        dst_ref[:, i * LANES:(i + 1) * LANES] = src_ref[pl.ds(i, n, stride=S5_CHUNK), :]


def _unfold_chunks(dst_ref, src_ref):
    n = src_ref.shape[0]
    for i in range(S5_CHUNK):
        dst_ref[pl.ds(i, n, stride=S5_CHUNK), :] = src_ref[:, i * LANES:(i + 1) * LANES]


def _premix(xc, xl, mod, g, w_in_b, row_map, d_a, d_b, d_c):
    t_ctx, d = xc.shape
    t = t_ctx + xl.shape[0]
    tm = TOK_TILE
    d_in = w_in_b.shape[1]
    ctx_tiles = t_ctx // tm
    tok = lambda n: pl.BlockSpec((tm, n), lambda i: (i, 0))
    ctx = pl.BlockSpec((tm, d_b), lambda i: (jnp.minimum(i, ctx_tiles - 1), 0))
    fold = pl.BlockSpec((tm // S5_CHUNK, S5_CHUNK * LANES), lambda i: (i, 0))
    return pl.pallas_call(
        functools.partial(_premix_body, ctx_tiles=ctx_tiles),
        name="premix",
        grid=(t // tm,),
        in_specs=_stream_specs(tm, d, ctx_tiles) + [
            pl.BlockSpec((1, 1, mod.shape[2]), lambda i: (row_map(i), 0, 0)),
            pl.BlockSpec((1, d), lambda i: (0, 0)),
            pl.BlockSpec((d, d_in), lambda i: (0, 0))],
        out_specs=[tok(2 * d_a), tok(d_b), ctx, ctx, tok(d_b), tok(d_b), fold, fold],
        out_shape=[jax.ShapeDtypeStruct((t, 2 * d_a), F32),
                   jax.ShapeDtypeStruct((t, d_b), BF16),
                   jax.ShapeDtypeStruct((t_ctx, d_b), F32),
                   jax.ShapeDtypeStruct((t_ctx, d_b), F32),
                   jax.ShapeDtypeStruct((t, d_b), BF16),
                   jax.ShapeDtypeStruct((t, d_b), BF16)]
        + [jax.ShapeDtypeStruct((t // S5_CHUNK, S5_CHUNK * LANES), F32)] * 2,
        scratch_shapes=[pltpu.VMEM((tm, LANES), F32)],
        compiler_params=_cparams("arbitrary"),
    )(xc, xl, mod, g, w_in_b)


def _chunk_body(pa_ref, g_ref, w_ref, b_ref, o_ref):
    z = jax.nn.gelu(pa_ref[...])
    d_a = o_ref.shape[1]
    hd = d_a // NH_A
    u = z[:, :d_a]
    v = z[:, d_a:]
    mu = jnp.mean(v, axis=-1, keepdims=True)
    var = jnp.mean(jnp.square(v - mu), axis=-1, keepdims=True)
    vb = ((v - mu) * lax.rsqrt(var + EPS) * g_ref[...]).astype(BF16)
    head = lax.broadcasted_iota(I32, (CHUNK, d_a), 1) // hd
    for ch in range(pa_ref.shape[0] // CHUNK):
        rows = slice(ch * CHUNK, (ch + 1) * CHUNK)
        sf = _dot(w_ref[...], vb[rows])
        s = b_ref[...]
        for h in range(NH_A):
            s = s + jnp.where(head == h, sf[h * CHUNK:(h + 1) * CHUNK], 0.0)
        o_ref[rows, :] = (u[rows] * s).astype(BF16)


def _chunk_mlp(pa, sgu_g, w_sp_b, bias):
    t, d2 = pa.shape
    d_a = d2 // 2
    tm = TOK_TILE
    return pl.pallas_call(
        _chunk_body,
        name="chunk_mlp",
        grid=(t // tm,),
        in_specs=[pl.BlockSpec((tm, d2), lambda i: (i, 0)),
                  pl.BlockSpec((1, d_a), lambda i: (0, 0)),
                  pl.BlockSpec(w_sp_b.shape, lambda i: (0, 0)),
                  pl.BlockSpec(bias.shape, lambda i: (0, 0))],
        out_specs=pl.BlockSpec((tm, d_a), lambda i: (i, 0)),
        out_shape=jax.ShapeDtypeStruct((t, d_a), BF16),
        compiler_params=_cparams("arbitrary"),
    )(pa, sgu_g, w_sp_b, bias)


def _stack_pair(qg):
    lane = lax.broadcasted_iota(I32, qg.shape, 1)
    zero = jnp.zeros_like(qg)
    return jnp.concatenate([jnp.where(lane < HD_B, qg, zero), jnp.where(lane >= HD_B, qg, zero)], axis=0)


def _unstack_pair(o2):
    n = o2.shape[0] // 2
    lane = lax.broadcasted_iota(I32, (n, o2.shape[1]), 1)
    return jnp.where(lane < HD_B, o2[:n], o2[n:])


def _ctx_attn_body(q_ref, k_ref, v_ref, o_ref):
    for g in range(q_ref.shape[1] // LANES):
        cols = slice(g * LANES, (g + 1) * LANES)
        q2 = _stack_pair(q_ref[:, cols])
        s = _dot_nt(q2, k_ref[:, cols].astype(BF16)) * ATTN_SCALE
        e = jnp.exp(s - jnp.max(s, axis=-1, keepdims=True))
        p = e / jnp.sum(e, axis=-1, keepdims=True)
        o2 = _dot(p.astype(BF16), v_ref[:, cols].astype(BF16))
        o_ref[:, cols] = _unstack_pair(o2).astype(BF16)


def _ctx_attn(q, k, v, n_batch, seq):
    d_b = q.shape[1]
    blk = pl.BlockSpec((seq, d_b), lambda b: (b, 0))
    return pl.pallas_call(
        _ctx_attn_body,
        name="ctx_attn",
        grid=(n_batch,),
        in_specs=[blk, blk, blk],
        out_specs=blk,
        out_shape=jax.ShapeDtypeStruct((n_batch * seq, d_b), BF16),
        compiler_params=_cparams("arbitrary"),
    )(q, k, v)


def _nat_body(q_ref, k_ref, v_ref, ck_ref, cv_ref, bias_ref, o_ref, *, rows):
    r0 = pl.program_id(1) * NAT_ROWS
    n_win = NA_WIN_R * GRID_W
    pr = 2 * GRID_W
    starts, cases = [], []
    for i in range(NAT_ROWS):
        rs = jnp.clip(r0 + i - NA_WIN_R // 2, 0, rows - NA_WIN_R)
        starts.append(pl.multiple_of(rs * GRID_W, GRID_W))
        cases.append(r0 + i - rs)
    for g in range(q_ref.shape[1] // LANES):
        cols = slice(g * LANES, (g + 1) * LANES)
        q2 = jnp.concatenate([_stack_pair(q_ref[i * GRID_W:(i + 1) * GRID_W, cols]) for i in range(NAT_ROWS)],
                             axis=0)
        s_ctx = _dot_nt(q2, ck_ref[0, :, cols]) * ATTN_SCALE
        m_ctx = jnp.max(s_ctx, axis=-1, keepdims=True)
        e_wins, invs, ms = [], [], []
        for i in range(NAT_ROWS):
            s_win = (_dot_nt(q2[i * pr:(i + 1) * pr], k_ref[pl.ds(starts[i], n_win), cols]) * ATTN_SCALE
                     + bias_ref[cases[i], g])
            m = jnp.maximum(jnp.max(s_win, axis=-1, keepdims=True), m_ctx[i * pr:(i + 1) * pr])
            e_wins.append(jnp.exp(s_win - m))
            ms.append(m)
        e_ctx = jnp.exp(s_ctx - jnp.concatenate(ms, axis=0))
        l_ctx = jnp.sum(e_ctx, axis=-1, keepdims=True)
        for i in range(NAT_ROWS):
            invs.append(1.0 / (jnp.sum(e_wins[i], axis=-1, keepdims=True) + l_ctx[i * pr:(i + 1) * pr]))
        o_ctx = _dot((e_ctx * jnp.concatenate(invs, axis=0)).astype(BF16), cv_ref[0, :, cols])
        for i in range(NAT_ROWS):
            o2 = _dot((e_wins[i] * invs[i]).astype(BF16), v_ref[pl.ds(starts[i], n_win), cols])
            o_ref[i * GRID_W:(i + 1) * GRID_W, cols] = _unstack_pair(o2 + o_ctx[i * pr:(i + 1) * pr]).astype(BF16)


def _nat_attn(q, kb, vb, ck, cv, bias, n_batch, seq, tok0):
    d_b = q.shape[1]
    rows = seq // GRID_W
    lc = ck.shape[1]
    blk = NAT_ROWS * GRID_W
    steps = seq // blk
    q0 = tok0 // blk
    i0 = tok0 // seq
    img = pl.BlockSpec((seq, d_b), lambda b, r: (i0 + b, 0))
    ctx = pl.BlockSpec((1, lc, d_b), lambda b, r: (b, 0, 0))
    return pl.pallas_call(
        functools.partial(_nat_body, rows=rows),
        name="nat_attn",
        grid=(n_batch, steps),
        in_specs=[pl.BlockSpec((blk, d_b), lambda b, r: (q0 + b * steps + r, 0)),
                  img, img, ctx, ctx,
                  pl.BlockSpec(bias.shape, lambda b, r: (0, 0, 0, 0))],
        out_specs=pl.BlockSpec((blk, d_b), lambda b, r: (b * steps + r, 0)),
        out_shape=jax.ShapeDtypeStruct((n_batch * seq, d_b), BF16),
        compiler_params=_cparams("arbitrary", "arbitrary"),
    )(q, kb, vb, ck, cv, bias)


def _nat_bias(rpb):
    n_heads = rpb.shape[0]
    cols = jnp.arange(GRID_W)
    col_start = jnp.clip(cols - NA_WIN_C // 2, 0, GRID_W - NA_WIN_C)
    j = jnp.arange(GRID_W)
    valid = (j[None, :] >= col_start[:, None]) & (j[None, :] < col_start[:, None] + NA_WIN_C)
    col_off = jnp.clip(j[None, :] - cols[:, None] + (NA_WIN_C - 1), 0, 2 * NA_WIN_C - 2)
    toe = jnp.where(valid[None, None], rpb.astype(F32)[:, :, col_off], NEG_BIG)
    cases = jnp.stack([toe[:, NA_WIN_R - 1 - delta:2 * NA_WIN_R - 1 - delta] for delta in range(NA_WIN_R)])
    return cases.transpose(0, 1, 3, 2, 4).reshape(NA_WIN_R, n_heads // 2, 2 * GRID_W, NA_WIN_R * GRID_W)


def _s5_weights(a_re, a_im, log_dt, b_re, b_im, c_re, c_im, ssm_d):
    n_g = a_re.shape[1]
    c = S5_CHUNK
    lam = lax.complex(a_re.astype(F32), a_im.astype(F32))
    ldt = lam * jnp.exp(log_dt.astype(F32))[..., None]
    lam_bar = jnp.exp(ldt)
    b_bar = ((lam_bar - 1) / lam)[..., None] * lax.complex(b_re.astype(F32), b_im.astype(F32))
    c_mat = lax.complex(c_re.astype(F32), c_im.astype(F32))
    pw = jnp.exp(ldt[None] * jnp.arange(c + 1, dtype=F32)[:, None, None, None])
    kern = jnp.real(jnp.einsum('dgcp,kdgp,dgpe->dgkce', c_mat, pw[:c], b_bar))
    i = jnp.arange(c)
    lag = i[None, :] - i[:, None]
    tf = jnp.where((lag >= 0)[None, :, :, None, None], kern[0][:, jnp.clip(lag, 0, c - 1)], 0.0)
    tb = jnp.where((lag <= 0)[None, :, :, None, None], kern[1][:, jnp.clip(-lag, 0, c - 1)], 0.0)
    t_mat = (tf + tb).transpose(0, 1, 4, 2, 3).reshape(n_g, c * SSM_CH, c * SSM_CH)
    mf = pw[:c][::-1, 0][:, :, :, None] * b_bar[0][None]
    mb = pw[:c, 1][:, :, :, None] * b_bar[1][None]
    mf = mf.transpose(1, 0, 3, 2).reshape(n_g, c * SSM_CH, SSM_P)
    mb = mb.transpose(1, 0, 3, 2).reshape(n_g, c * SSM_CH, SSM_P)
    mq = jnp.stack([jnp.real(mf), jnp.imag(mf), jnp.real(mb), jnp.imag(mb)], axis=2)
    zf = c_mat[0][:, None] * pw[1:c + 1, 0][:, :, None, :].transpose(1, 0, 2, 3)
    zb = c_mat[1][:, None] * pw[1:c + 1, 1][::-1][:, :, None, :].transpose(1, 0, 2, 3)
    zf = zf.transpose(0, 3, 1, 2).reshape(n_g, SSM_P, c * SSM_CH)
    zb = zb.transpose(0, 3, 1, 2).reshape(n_g, SSM_P, c * SSM_CH)
    wq = jnp.stack([jnp.real(zf), -jnp.imag(zf), jnp.real(zb), -jnp.imag(zb)], axis=1)
    gh = LANES // SSM_CH
    n_half = n_g // gh
    kd = c * SSM_CH

    def interleaved_block_diag(blocks, r1, c1):
        n = gh * kd
        b = blocks.astype(BF16).reshape(n_half, gh, kd, kd)
        bd = jnp.zeros((n_half, gh, kd, gh, kd), BF16)
        for g in range(gh):
            bd = bd.at[:, g, :, g, :].set(b[:, g])
        bd = bd.reshape(n_half, gh, r1, kd // r1, n).transpose(0, 2, 1, 3, 4).reshape(n_half, n, n)
        bd = jnp.swapaxes(bd, 1, 2)
        bd = bd.reshape(n_half, gh, c1, kd // c1, n).transpose(0, 2, 1, 3, 4).reshape(n_half, n, n)
        return jnp.swapaxes(bd, 1, 2)

    m_half = interleaved_block_diag(mq.reshape(n_g, kd, 4 * SSM_P), c, 4)
    t_half = interleaved_block_diag(t_mat, c, c)
    w_half = interleaved_block_diag(wq.reshape(n_g, 4 * SSM_P, kd), 4, c)
    tw_half = jnp.concatenate([t_half, w_half], axis=1)
    a_c = pw[c]
    a16 = jnp.stack([jnp.real(a_c[0]), jnp.imag(a_c[0]), jnp.real(a_c[1]), jnp.imag(a_c[1])])
    a16 = a16.reshape(4, 1, n_half, gh * SSM_P).transpose(2, 0, 1, 3)
    dvec = ssm_d.astype(F32).reshape(n_half, 1, LANES)
    return m_half, tw_half, a16, dvec


def _s5_state_body(u_ref, m_ref, fre_ref, fim_ref, bre_ref, bim_ref):
    nb, tn, kd = u_ref.shape
    w = fre_ref.shape[2]
    r = _dot(u_ref[...].reshape(nb * tn, kd).astype(BF16), m_ref[...])
    for q, o_ref in enumerate((fre_ref, fim_ref, bre_ref, bim_ref)):
        for b in range(nb):
            o_ref[:, b, :] = r[b * tn:(b + 1) * tn, q * w:(q + 1) * w]


def _s5_rows(nb, n):
    return max(8, min(n, 256 // nb))


def _s5_states(u3, m_half):
    nb, n, kd = u3.shape
    tn = _s5_rows(nb, n)
    w = m_half.shape[1] // 4
    out = pl.BlockSpec((tn, nb, w), lambda i: (i, 0, 0))
    return pl.pallas_call(
        _s5_state_body,
        name="s5_states",
        grid=(n // tn,),
        in_specs=[pl.BlockSpec((nb, tn, kd), lambda i: (0, i, 0)),
                  pl.BlockSpec(m_half.shape, lambda i: (0, 0))],
        out_specs=[out] * 4,
        out_shape=[jax.ShapeDtypeStruct((n, nb, w), F32)] * 4,
        compiler_params=_cparams("arbitrary"),
    )(u3, m_half)


def _s5_scan_body(sfr_ref, sfi_ref, sbr_ref, sbi_ref, a_ref, h0_ref,
                  hfr_ref, hfi_ref, hbr_ref, hbi_ref, fin_ref, st_ref):
    @pl.when(pl.program_id(0) == 0)
    def _():
        st_ref[...] = h0_ref[...]

    ks = sfr_ref.shape[0]
    afr, afi, abr, abi = a_ref[0], a_ref[1], a_ref[2], a_ref[3]

    def step(s, carry):
        fr, fi, br, bi = carry
        sb = ks - 1 - s
        hfr_ref[s] = fr
        hfi_ref[s] = fi
        hbr_ref[sb] = br
        hbi_ref[sb] = bi
        nfr = afr * fr - afi * fi + sfr_ref[s]
        nfi = afr * fi + afi * fr + sfi_ref[s]
        nbr = abr * br - abi * bi + sbr_ref[sb]
        nbi = abr * bi + abi * br + sbi_ref[sb]
        return nfr, nfi, nbr, nbi

    carry = lax.fori_loop(0, ks, step, (st_ref[0], st_ref[1], st_ref[2], st_ref[3]))
    for q in range(4):
        st_ref[q] = carry[q]
        fin_ref[q] = carry[q]


def _s5_scan(s3, a16, h0):
    n_chunks, n_batch, width = s3[0].shape
    ks = min(n_chunks, 32)
    nb = n_chunks // ks
    fwd = pl.BlockSpec((ks, n_batch, width), lambda i: (i, 0, 0))
    bwd = pl.BlockSpec((ks, n_batch, width), lambda i: (nb - 1 - i, 0, 0))
    small = lambda shape: pl.BlockSpec(shape, lambda i: (0, 0, 0))
    outs = pl.pallas_call(
        _s5_scan_body,
        name="s5_scan",
        grid=(nb,),
        in_specs=[fwd, fwd, bwd, bwd, small(a16.shape), small(h0.shape)],
        out_specs=[fwd, fwd, bwd, bwd, small(h0.shape)],
        out_shape=[jax.ShapeDtypeStruct((n_chunks, n_batch, width), F32)] * 4
        + [jax.ShapeDtypeStruct(h0.shape, F32)],
        scratch_shapes=[pltpu.VMEM(h0.shape, F32)],
        compiler_params=_cparams("arbitrary"),
    )(*s3, a16, h0)
    return outs[:4], outs[4]


def _s5_out_body(u_ref, us_ref, hfr_ref, hfi_ref, hbr_ref, hbi_ref, tw_ref, d_ref, y_ref):
    nb, tn, kd = u_ref.shape
    rows = nb * tn
    wo = us_ref.shape[2]
    y = _dot(u_ref[...].reshape(rows, kd).astype(BF16), tw_ref[0:kd, :])
    w = hfr_ref.shape[2]
    for q, h_ref in enumerate((hfr_ref, hfi_ref, hbr_ref, hbi_ref)):
        hq = jnp.concatenate([h_ref[:, b, :] for b in range(nb)], axis=0)
        y = y + _dot(hq.astype(BF16), tw_ref[kd + q * w:kd + (q + 1) * w, :])
    skip = jnp.concatenate([d_ref[...]] * (wo // LANES), axis=1) * us_ref[...].reshape(rows, wo)
    y_ref[...] = jax.nn.gelu(y + skip).reshape(nb, tn, wo)


def _s5_out(u3, h4, tw_half, dvec):
    nb, n, kd = u3.shape
    tn = _s5_rows(nb, n)
    wo = S5_OUT_STEPS * LANES
    w = h4[0].shape[2]
    hb = pl.BlockSpec((tn, nb, w), lambda i, j: (i, 0, 0))
    return pl.pallas_call(
        _s5_out_body,
        name="s5_out",
        grid=(n // tn, kd // wo),
        in_specs=[pl.BlockSpec((nb, tn, kd), lambda i, j: (0, i, 0)),
                  pl.BlockSpec((nb, tn, wo), lambda i, j: (0, i, j)),
                  hb, hb, hb, hb,
                  pl.BlockSpec((tw_half.shape[0], wo), lambda i, j: (0, j)),
                  pl.BlockSpec(dvec.shape, lambda i, j: (0, 0))],
        out_specs=pl.BlockSpec((nb, tn, wo), lambda i, j: (0, i, j)),
        out_shape=jax.ShapeDtypeStruct(u3.shape, F32),
        compiler_params=_cparams("arbitrary", "arbitrary"),
    )(u3, u3, *h4, tw_half, dvec)


def _s5(pc_halves, n_batch, seq, weights, h0):
    m_half, tw_half, a16, dvec = weights
    n_chunks = seq // S5_CHUNK
    ys, fins = [], []
    for h, pc in enumerate(pc_halves):
        w = m_half.shape[2] // 4
        u3 = pc.reshape(n_batch, n_chunks, S5_CHUNK * LANES)
        s4 = _s5_states(u3, m_half[h])
        h4, fin = _s5_scan(s4, a16[h], h0[:, :, h * w:(h + 1) * w])
        ys.append(_s5_out(u3, h4, tw_half[h], dvec[h]).reshape(n_batch * n_chunks, S5_CHUNK * LANES))
        fins.append(fin)
    return ys, jnp.concatenate(fins, axis=2)


def _post_body(xc_ref, xl_ref, ya_ref, yb_ref, yc0_ref, yc1_ref, mod_ref, gpost_ref, gffn_ref, wglu_ref, bglu_ref,
               wo_ref, x1_ref, h2_ref, h2p_ref, unfold_ref, *, ctx_tiles):
    x = _stream_tile(xc_ref, xl_ref, ctx_tiles)
    d = x.shape[1]
    d_a = ya_ref.shape[1]
    d_b = yb_ref.shape[1]
    for h, yc_ref in enumerate((yc0_ref, yc1_ref)):
        _unfold_chunks(unfold_ref.at[h], yc_ref)
    y = jnp.concatenate([unfold_ref[0], unfold_ref[1]], axis=1)
    glu = y * jax.nn.sigmoid(_dot(y.astype(BF16), wglu_ref[...]) + bglu_ref[...])
    mixed = (_dot(ya_ref[...], wo_ref[0:d_a, :]) + _dot(yb_ref[...], wo_ref[d_a:d_a + d_b, :])
             + _dot(glu.astype(BF16), wo_ref[d_a + d_b:, :]))
    x1 = x + mod_ref[0, :, 2 * d:3 * d] * _rms(mixed, gpost_ref[...])
    x1_ref[...] = x1
    h2 = _rms(x1, gffn_ref[...]) * (1 + mod_ref[0, :, 4 * d:5 * d]) + mod_ref[0, :, 3 * d:4 * d]
    _store_tiled(h2_ref, h2)
    _store_tiled(h2p_ref, _pack_halves(h2))


def _post(xc, xl, ya, yb, yc0, yc1, mod, g_post, g_ffn, w_glu_b, b_glu, w_out_b, row_map):
    t_ctx, d = xc.shape
    t = t_ctx + xl.shape[0]
    tm = TOK_TILE
    nc = d // LANES
    ctx_tiles = t_ctx // tm
    tok = lambda n: pl.BlockSpec((tm, n), lambda i: (i, 0))
    fold = pl.BlockSpec((tm // S5_CHUNK, S5_CHUNK * LANES), lambda i: (i, 0))
    full = lambda a: pl.BlockSpec(a.shape, lambda i: (0,) * a.ndim)
    return pl.pallas_call(
        functools.partial(_post_body, ctx_tiles=ctx_tiles),
        name="post_mix",
        grid=(t // tm,),
        in_specs=_stream_specs(tm, d, ctx_tiles) + [
            tok(ya.shape[1]), tok(yb.shape[1]), fold, fold,
            pl.BlockSpec((1, 1, mod.shape[2]), lambda i: (row_map(i), 0, 0)),
            full(g_post), full(g_ffn), full(w_glu_b), full(b_glu), full(w_out_b)],
        out_specs=[tok(d), pl.BlockSpec((tm * nc, LANES), lambda i: (i, 0)),
                   pl.BlockSpec((tm * nc // 2, LANES), lambda i: (i, 0))],
        out_shape=[jax.ShapeDtypeStruct((t, d), F32), jax.ShapeDtypeStruct((t * nc, LANES), F32),
                   jax.ShapeDtypeStruct((t * nc // 2, LANES), U32)],
        scratch_shapes=[pltpu.VMEM((2, tm, LANES), F32)],
        compiler_params=_cparams("arbitrary"),
    )(xc, xl, ya, yb, yc0, yc1, mod, g_post, g_ffn, w_glu_b, b_glu, w_out_b)


def _route_body(h_ref, whi_ref, wlo_ref, b_ref, idx_ref, gate_ref, pos_ref, cnt_ref, carry_ref):
    @pl.when(pl.program_id(0) == 0)
    def _():
        carry_ref[...] = jnp.zeros_like(carry_ref)

    tm = idx_ref.shape[1]
    h = _load_tiled(h_ref, tm)
    hi = h.astype(BF16)
    lo = (h - hi.astype(F32)).astype(BF16)
    logits = _dot_nt(whi_ref[...], hi) + (_dot_nt(whi_ref[...], lo) + _dot_nt(wlo_ref[...], hi))
    scores = jax.nn.sigmoid(logits)
    sel = scores + b_ref[...]
    gsz = N_EXPERTS // N_EXP_GROUPS
    within = lax.broadcasted_iota(I32, (gsz, tm), 0).astype(F32)
    grp = []
    for g in range(N_EXP_GROUPS):
        blk = sel[g * gsz:(g + 1) * gsz]
        m1 = jnp.max(blk, axis=0, keepdims=True)
        first = jnp.min(jnp.where(blk == m1, within, float(gsz)), axis=0, keepdims=True)
        m2 = jnp.max(jnp.where(within == first, -jnp.inf, blk), axis=0, keepdims=True)
        grp.append(m1 + m2)
    blocks = []
    for g in range(N_EXP_GROUPS):
        ahead = jnp.zeros((1, tm), F32)
        for o in range(N_EXP_GROUPS):
            if o == g:
                continue
            beats = (grp[o] >= grp[g]) if o < g else (grp[o] > grp[g])
            ahead = ahead + jnp.where(beats, 1.0, 0.0)
        ahead = jnp.broadcast_to(ahead, (gsz, tm))
        blocks.append(jnp.where(ahead < TOPK_GROUPS, sel[g * gsz:(g + 1) * gsz], -jnp.inf))
    v = jnp.concatenate(blocks, axis=0)
    eidx = lax.broadcasted_iota(I32, (N_EXPERTS, tm), 0)
    rnk = jnp.zeros((N_EXPERTS, tm), F32)
    for e in range(N_EXPERTS):
        row = v[e:e + 1]
        rnk = rnk + jnp.where(eidx > e, jnp.where(row >= v, 1.0, 0.0), jnp.where(row > v, 1.0, 0.0))
    chosen = rnk < TOP_K
    w = jnp.where(chosen, scores, 0.0)
    wn = w / jnp.sum(w, axis=0, keepdims=True) * ROUTE_SCALE
    tri = (lax.broadcasted_iota(I32, (tm, tm), 0) < lax.broadcasted_iota(I32, (tm, tm), 1))
    chosen_f = jnp.where(chosen, 1.0, 0.0)
    prefix = _dot(chosen_f.astype(BF16), jnp.where(tri, 1.0, 0.0).astype(BF16)) + carry_ref[:, 0:1]
    total = carry_ref[...] + jnp.sum(chosen_f, axis=1, keepdims=True)
    carry_ref[...] = total
    cnt_ref[...] = total
    eidx_f = eidx.astype(F32)
    for k in range(TOP_K):
        one = rnk == k
        idx_ref[k:k + 1, :] = jnp.sum(jnp.where(one, eidx_f, 0.0), axis=0, keepdims=True).astype(I32)
        gate_ref[k:k + 1, :] = jnp.sum(jnp.where(one, wn, 0.0), axis=0, keepdims=True)
        pos_ref[k:k + 1, :] = jnp.sum(jnp.where(one, prefix, 0.0), axis=0, keepdims=True).astype(I32)


def _route(h2t, w_hi, w_lo, b_router):
    nc = w_hi.shape[1] // LANES
    t = h2t.shape[0] // nc
    tm = TOK_TILE
    out = pl.BlockSpec((TOP_K, tm), lambda i: (0, i))
    full = lambda a: pl.BlockSpec(a.shape, lambda i: (0, 0))
    return pl.pallas_call(
        _route_body,
        name="route",
        grid=(t // tm,),
        in_specs=[pl.BlockSpec((tm * nc, LANES), lambda i: (i, 0)), full(w_hi), full(w_lo), full(b_router)],
        out_specs=[out, out, out, pl.BlockSpec((N_EXPERTS, LANES), lambda i: (0, 0))],
        out_shape=[jax.ShapeDtypeStruct((TOP_K, t), I32), jax.ShapeDtypeStruct((TOP_K, t), F32),
                   jax.ShapeDtypeStruct((TOP_K, t), I32), jax.ShapeDtypeStruct((N_EXPERTS, LANES), F32)],
        scratch_shapes=[pltpu.VMEM((N_EXPERTS, LANES), F32)],
        compiler_params=_cparams("arbitrary"),
    )(h2t, w_hi, w_lo, b_router)


def _dest_body(starts_ref, idx_ref, pos_ref, dest_ref):
    idx = idx_ref[...]
    acc = pos_ref[...]
    for e in range(N_EXPERTS):
        acc = acc + jnp.where(idx == e, starts_ref[e], 0)
    dest_ref[...] = acc


def _dest(starts, idx, pos):
    t = idx.shape[1]
    tile = math.gcd(t, 4096)
    blk = pl.BlockSpec((TOP_K, tile), lambda i, st: (0, i))
    return pl.pallas_call(
        _dest_body,
        name="moe_dest",
        grid_spec=pltpu.PrefetchScalarGridSpec(num_scalar_prefetch=1, grid=(t // tile,),
                                               in_specs=[blk, blk], out_specs=blk),
        out_shape=jax.ShapeDtypeStruct(idx.shape, I32),
        compiler_params=_cparams("arbitrary"),
    )(starts, idx, pos)


def _dispatch_body(pad_lo_ref, pad_hi_ref, dest_ref, h_ref, xs_ref, zero_ref, sem, zsem, *, n_pad_rows):
    tile = dest_ref.shape[1]
    nw = h_ref.shape[0] // tile

    def token(ref, r):
        return ref.at[pl.ds(pl.multiple_of(r * nw, nw), nw)]

    def zero_copy(r, n):
        return pltpu.make_async_copy(zero_ref.at[pl.ds(0, n * nw)],
                                     xs_ref.at[pl.ds(pl.multiple_of(r * nw, nw), n * nw)], zsem)

    @pl.when(pl.program_id(0) == 0)
    def _():
        zero_ref[...] = jnp.zeros_like(zero_ref)

        def per_expert(e, c):
            lo = pad_lo_ref[e]
            runs = (pad_hi_ref[e] - lo) // ZERO_RUN

            def run(j, c2):
                zero_copy(lo + j * ZERO_RUN, ZERO_RUN).start()
                return c2

            def one(r, c2):
                zero_copy(r, 1).start()
                return c2

            c = lax.fori_loop(0, runs, run, c)
            return lax.fori_loop(lo + runs * ZERO_RUN, pad_hi_ref[e], one, c)

        lax.fori_loop(0, N_EXPERTS, per_expert, 0)

    def row_copy(t, k):
        return pltpu.make_async_copy(token(h_ref, t), token(xs_ref, dest_ref[k, t]), sem)

    def start(t, c):
        for k in range(TOP_K):
            row_copy(t, k).start(priority=k % DMA_THREADS)
        return c

    def wait(t, c):
        for k in range(TOP_K):
            row_copy(t, k).wait()
        return c

    lax.fori_loop(0, tile, start, 0)
    lax.fori_loop(0, tile, wait, 0)

    @pl.when(pl.program_id(0) == 0)
    def _():
        def one(r, c):
            zero_copy(0, ZERO_RUN).wait()
            return c
        lax.fori_loop(0, n_pad_rows // ZERO_RUN, one, 0)


def _dispatch(pad_lo, pad_hi, dest, h2p, n_rows):
    t = dest.shape[1]
    nw = h2p.shape[0] // t
    tile = DISPATCH_TILE
    grid_spec = pltpu.PrefetchScalarGridSpec(
        num_scalar_prefetch=2,
        grid=(t // tile,),
        in_specs=[pl.BlockSpec((TOP_K, tile), lambda i, lo, hi: (0, i), memory_space=pltpu.SMEM),
                  pl.BlockSpec((tile * nw, LANES), lambda i, lo, hi: (i, 0))],
        out_specs=pl.BlockSpec(memory_space=pl.ANY),
        scratch_shapes=[pltpu.VMEM((ZERO_RUN * nw, LANES), U32), pltpu.SemaphoreType.DMA(()),
                        pltpu.SemaphoreType.DMA(())],
    )
    assert (n_rows - t * TOP_K) % ZERO_RUN == 0
    return pl.pallas_call(
        functools.partial(_dispatch_body, n_pad_rows=n_rows - t * TOP_K),
        name="moe_dispatch",
        grid_spec=grid_spec,
        out_shape=jax.ShapeDtypeStruct((n_rows * nw, LANES), U32),
        compiler_params=_cparams("arbitrary"),
    )(pad_lo, pad_hi, dest, h2p)


def _experts_body(be_ref, x_ref, wg_ref, wu_ref, wd_ref, y_ref, wgb_ref, wub_ref, wdb_ref):
    i = pl.program_id(0)

    @pl.when((i == 0) | (be_ref[i] != be_ref[jnp.maximum(i - 1, 0)]))
    def _():
        wgb_ref[...] = wg_ref[0, 0].astype(BF16)
        wub_ref[...] = wu_ref[0, 0].astype(BF16)
        wdb_ref[...] = wd_ref[0, 0].astype(BF16)

    xb = _unpack_halves(_load_tiled(x_ref, MOE_ROWS)).astype(BF16)
    hb = jax.nn.silu(_dot(xb, wgb_ref[...])) * _dot(xb, wub_ref[...])
    _store_tiled(y_ref, _pack_halves(_dot(hb.astype(BF16), wdb_ref[...])))


def _experts(blk_exp, xs, wg, wu, wd, layer):
    d, de = wg.shape[2], wg.shape[3]
    nc = d // (2 * LANES)
    bm = MOE_ROWS
    tiles = pl.BlockSpec((bm * nc, LANES), lambda i, be: (i, 0))
    grid_spec = pltpu.PrefetchScalarGridSpec(
        num_scalar_prefetch=1,
        grid=(xs.shape[0] // (bm * nc),),
        in_specs=[tiles,
                  pl.BlockSpec((1, 1, d, de), lambda i, be: (layer, be[i], 0, 0)),
                  pl.BlockSpec((1, 1, d, de), lambda i, be: (layer, be[i], 0, 0)),
                  pl.BlockSpec((1, 1, de, d), lambda i, be: (layer, be[i], 0, 0))],
        out_specs=tiles,
        scratch_shapes=[pltpu.VMEM((d, de), BF16), pltpu.VMEM((d, de), BF16), pltpu.VMEM((de, d), BF16)],
    )
    return pl.pallas_call(
        _experts_body,
        name="moe_experts",
        grid_spec=grid_spec,
        out_shape=jax.ShapeDtypeStruct(xs.shape, U32),
        compiler_params=_cparams("arbitrary"),
    )(blk_exp, xs, wg, wu, wd)


def _combine_body(dest_ref, next_ref, gate_ref, h_ref, x_ref, mod_ref, g_ref, wsg_ref, wsu_ref, wsd_ref, ys_hbm,
                  oc_ref, ol_ref, buf_ref, sem, *, ctx_steps):
    tile, d = x_ref.shape
    nw = buf_ref.shape[2] // tile
    i = pl.program_id(0)
    slot = i % 2

    def row_copy(d_ref, s, t, k):
        return pltpu.make_async_copy(ys_hbm.at[pl.ds(pl.multiple_of(d_ref[k, t] * nw, nw), nw)],
                                     buf_ref.at[s, k, pl.ds(pl.multiple_of(t * nw, nw), nw)], sem.at[s])

    def gather(d_ref, s):
        def start(t, c):
            for k in range(TOP_K):
                row_copy(d_ref, s, t, k).start(priority=k % DMA_THREADS)
            return c
        lax.fori_loop(0, tile, start, 0)

    @pl.when(i == 0)
    def _():
        gather(dest_ref, 0)

    @pl.when(i + 1 < pl.num_programs(0))
    def _():
        gather(next_ref, 1 - slot)

    hb = _load_tiled(h_ref, tile).astype(BF16)
    acc = _dot((jax.nn.silu(_dot(hb, wsg_ref[...])) * _dot(hb, wsu_ref[...])).astype(BF16), wsd_ref[...])

    def wait(t, c):
        for k in range(TOP_K):
            row_copy(dest_ref, slot, 0, k).wait()
        return c

    lax.fori_loop(0, tile, wait, 0)
    gates = gate_ref[...]
    moe = gates[:, 0:1] * _unpack_halves(_load_tiled(buf_ref.at[slot, 0], tile))
    for k in range(1, TOP_K):
        moe = moe + gates[:, k:k + 1] * _unpack_halves(_load_tiled(buf_ref.at[slot, k], tile))
    out = x_ref[...] + mod_ref[0, :, 5 * d:6 * d] * _rms(moe + acc, g_ref[...])

    @pl.when(i < ctx_steps)
    def _():
        oc_ref[...] = out

    @pl.when(i >= ctx_steps)
    def _():
        ol_ref[...] = out


def _combine(dest, gates, h2t, x1, mod, g_post, wsg, wsu, wsd, ys, row_map, t_ctx):
    t, d = x1.shape
    nc = d // LANES
    nw = nc // 2
    tile = COMBINE_TILE
    per_tok = TOK_TILE // tile
    n_steps = t // tile
    tok = lambda n: pl.BlockSpec((tile, n), lambda i: (i, 0))
    full = lambda a: pl.BlockSpec(a.shape, lambda i: (0,) * a.ndim)
    ctx_steps = t_ctx // tile
    return pl.pallas_call(
        functools.partial(_combine_body, ctx_steps=ctx_steps),
        name="moe_combine",
        grid=(n_steps,),
        in_specs=[pl.BlockSpec((TOP_K, tile), lambda i: (0, i), memory_space=pltpu.SMEM),
                  pl.BlockSpec((TOP_K, tile), lambda i: (0, jnp.minimum(i + 1, n_steps - 1)),
                               memory_space=pltpu.SMEM),
                  tok(TOP_K), pl.BlockSpec((tile * nc, LANES), lambda i: (i, 0)), tok(d),
                  pl.BlockSpec((1, 1, mod.shape[2]), lambda i: (row_map(i // per_tok), 0, 0)),
                  full(g_post), full(wsg), full(wsu), full(wsd),
                  pl.BlockSpec(memory_space=pl.ANY)],
        out_specs=[pl.BlockSpec((tile, d), lambda i: (jnp.minimum(i, ctx_steps - 1), 0)),
                   pl.BlockSpec((tile, d), lambda i: (jnp.maximum(i - ctx_steps, 0), 0))],
        out_shape=[jax.ShapeDtypeStruct((t_ctx, d), F32), jax.ShapeDtypeStruct((t - t_ctx, d), F32)],
        scratch_shapes=[pltpu.VMEM((2, TOP_K, tile * nw, LANES), U32), pltpu.SemaphoreType.DMA((2,))],
        compiler_params=_cparams("arbitrary"),
    )(dest, dest, gates, h2t, x1, mod, g_post, wsg, wsu, wsd, ys)


def _moe(h2t, h2p, x1, mod, p, row_map, t_ctx):
    t, d = x1.shape
    idx, gate, pos, cnt = _route(h2t, p['wr_hi'], p['wr_lo'], p['b_router'])
    counts = cnt[:, 0].astype(I32)
    padded = (counts + MOE_ROWS - 1) // MOE_ROWS * MOE_ROWS
    ends = jnp.cumsum(padded)
    starts = ends - padded
    n_blocks = (t * TOP_K + N_EXPERTS * (MOE_ROWS - 1)) // MOE_ROWS + 1
    n_rows = n_blocks * MOE_ROWS
    blk_start = jnp.arange(n_blocks, dtype=I32) * MOE_ROWS
    blk_exp = jnp.minimum(jnp.sum((ends[None, :] <= blk_start[:, None]).astype(I32), axis=1), N_EXPERTS - 1)
    pad_hi = ends.at[N_EXPERTS - 1].set(n_rows)
    dest = _dest(starts, idx, pos)
    xs = _dispatch(starts + counts, pad_hi, dest, h2p, n_rows)
    ys = _experts(blk_exp, xs, p['w_e_gate'], p['w_e_up'], p['w_e_down'], p['layer'])
    return _combine(dest, gate.T, h2t, x1, mod, p['g_post_ffn'], p['w_s_gate'], p['w_s_up'], p['w_s_down'],
                    ys, row_map, t_ctx)


def kernel(x_prompt, x_sample, cache_k, cache_v, state_ssm_re, state_ssm_im, c, c_ctx,
           g_pre_mix, g_post_mix, g_pre_ffn, g_post_ffn, w_ada, b_ada, w_in, w_out,
           sgu_g, w_sp, b_sp, rpb, ssm_a_re, ssm_a_im, ssm_log_dt, ssm_b_re, ssm_b_im,
           ssm_c_re, ssm_c_im, ssm_d, w_glu, b_glu, w_router, b_router,
           w_e_gate, w_e_up, w_e_down, w_s_gate, w_s_up, w_s_down):
    n_pb, p_seq, d = x_prompt.shape
    n_sb, s_seq, _ = x_sample.shape
    depth = w_in.shape[0]
    d_a = sgu_g.shape[1]
    d_c = w_glu.shape[1]
    d_b = d - d_a - d_c
    n_heads = d_b // HD_B
    n_g = d_c // SSM_CH
    t_p = n_pb * p_seq
    assert p_seq % CHUNK == 0 and t_p % TOK_TILE == 0 and s_seq % TOK_TILE == 0 and t_p % s_seq == 0
    assert s_seq % (NAT_ROWS * GRID_W) == 0 and s_seq // GRID_W >= NA_WIN_R and d_c == 2 * LANES

    mod_rows = -(-(n_sb + 1) // 8) * 8
    c_all = jnp.zeros((mod_rows, d), F32).at[:n_sb].set(c).at[n_sb].set(c_ctx)
    mod_all = _ada(c_all, w_ada, b_ada)
    p_tiles = t_p // TOK_TILE
    s_tiles = s_seq // TOK_TILE

    def row_map(i):
        return jnp.where(i < p_tiles, n_sb, (i - p_tiles) // s_tiles)

    xc, xl = x_prompt.reshape(t_p, d), x_sample.reshape(n_sb * s_seq, d)
    s5w_all = jax.vmap(_s5_weights)(ssm_a_re, ssm_a_im, ssm_log_dt, ssm_b_re, ssm_b_im, ssm_c_re, ssm_c_im, ssm_d)
    nat_bias_all = jax.vmap(_nat_bias)(rpb)
    new_k, new_v, new_re, new_im = [], [], [], []
    for l in range(depth):
        mod = mod_all[l][:, None, :]
        row = lambda a: a[l][None, :].astype(F32)
        pa, q, k32, v32, kb, vb, *pc = _premix(xc, xl, mod, row(g_pre_mix), w_in[l].astype(BF16), row_map,
                                               d_a, d_b, d_c)
        bias_a = jnp.repeat(b_sp[l].T.astype(F32), d_a // NH_A, axis=1)
        ya = _chunk_mlp(pa, row(sgu_g), w_sp[l].reshape(NH_A * CHUNK, CHUNK).astype(BF16), bias_a)
        yb_p = _ctx_attn(q, k32, v32, n_pb, p_seq)
        ck = cache_k[:, l].reshape(n_sb, -1, d_b).astype(BF16)
        cv = cache_v[:, l].reshape(n_sb, -1, d_b).astype(BF16)
        yb_s = _nat_attn(q, kb, vb, ck, cv, nat_bias_all[l], n_sb, s_seq, t_p)
        yb = jnp.concatenate([yb_p, yb_s], axis=0)
        s5w = [w[l] for w in s5w_all]
        c_p = t_p // S5_CHUNK
        yc_p, fin_p = _s5([h[:c_p] for h in pc], n_pb, p_seq, s5w, jnp.zeros((4, n_pb, n_g * SSM_P), F32))
        sre = state_ssm_re[:, l].astype(F32).reshape(n_sb, 2, n_g * SSM_P)
        sim = state_ssm_im[:, l].astype(F32).reshape(n_sb, 2, n_g * SSM_P)
        h0_s = jnp.stack([sre[:, 0], sim[:, 0], sre[:, 1], sim[:, 1]])
        yc_s, _ = _s5([h[c_p:] for h in pc], n_sb, s_seq, s5w, h0_s)
        yc = [jnp.concatenate([a, b], axis=0) for a, b in zip(yc_p, yc_s)]
        x1, h2, h2p = _post(xc, xl, ya, yb, yc[0], yc[1],
                       mod, row(g_post_mix), row(g_pre_ffn), w_glu[l].astype(BF16), row(b_glu),
                       w_out[l].astype(BF16), row_map)
        wr = w_router[l].astype(F32).T
        wr_hi = wr.astype(BF16)
        moe_p = {
            'wr_hi': wr_hi, 'wr_lo': (wr - wr_hi.astype(F32)).astype(BF16),
            'b_router': b_router[l].astype(F32)[:, None],
            'w_e_gate': w_e_gate, 'w_e_up': w_e_up, 'w_e_down': w_e_down, 'layer': l,
            'w_s_gate': w_s_gate[l].astype(BF16), 'w_s_up': w_s_up[l].astype(BF16),
            'w_s_down': w_s_down[l].astype(BF16), 'g_post_ffn': row(g_post_ffn),
        }
        xc, xl = _moe(h2, h2p, x1, mod, moe_p, row_map, t_p)
        new_k.append(k32[:t_p].reshape(n_pb, p_seq, n_heads, HD_B))
        new_v.append(v32[:t_p].reshape(n_pb, p_seq, n_heads, HD_B))
        fin_p = fin_p.reshape(4, n_pb, n_g, SSM_P)
        new_re.append(jnp.stack([fin_p[0], fin_p[2]], axis=1))
        new_im.append(jnp.stack([fin_p[1], fin_p[3]], axis=1))
    return (xc.reshape(n_pb, p_seq, d), xl.reshape(n_sb, s_seq, d),
            jnp.stack(new_k, axis=1), jnp.stack(new_v, axis=1),
            jnp.stack(new_re, axis=1), jnp.stack(new_im, axis=1))
```

```python
import functools
import math

import jax
import jax.numpy as jnp
from jax import lax
from jax.experimental import pallas as pl
from jax.experimental.pallas import tpu as pltpu

F32 = jnp.float32
BF16 = jnp.bfloat16
I32 = jnp.int32

GRID_W = 64
EPS = 1e-6
NH_A = 4
CHUNK = 128
HD_B = 64
NA_WIN_R = 8
NA_WIN_C = 16
ATTN_SCALE = HD_B ** -0.5
SSM_CH = 16
SSM_P = 64
N_EXPERTS = 64
TOP_K = 8
N_EXP_GROUPS = 8
TOPK_GROUPS = 4
ROUTE_SCALE = 2.5

LANES = 128
TOK_TILE = 512
S5_CHUNK = 16
S5_OUT_STEPS = 4
MOE_ROWS = 1024
COMBINE_TILE = 256
DISPATCH_TILE = 512
ZERO_RUN = 64
NAT_ROWS = 4
DMA_THREADS = 2
NEG_BIG = -1e30
VMEM_LIMIT = 48 * 1024 * 1024


def _cparams(*sem):
    return pltpu.CompilerParams(dimension_semantics=sem, vmem_limit_bytes=VMEM_LIMIT)


def _dot(a, b):
    return jnp.dot(a, b, preferred_element_type=F32)


def _dot_nt(a, b):
    return lax.dot_general(a, b, (((1,), (1,)), ((), ())), preferred_element_type=F32)


def _rms(x, g):
    return x * lax.rsqrt(jnp.mean(x * x, axis=-1, keepdims=True) + EPS) * g


def _load_tiled(ref, n_rows):
    nc = ref.shape[0] // n_rows
    return jnp.concatenate([ref[pl.ds(j, n_rows, stride=nc), :] for j in range(nc)], axis=1)


def _store_tiled(ref, val):
    n_rows = val.shape[0]
    nc = ref.shape[0] // n_rows
    for j in range(nc):
        ref[pl.ds(j, n_rows, stride=nc), :] = val[:, j * LANES:(j + 1) * LANES]


U32 = jnp.uint32


def _pack_halves(x):
    half = x.shape[1] // 2
    lo = lax.bitcast_convert_type(x[:, :half].astype(BF16).astype(F32), U32)
    hi = lax.bitcast_convert_type(x[:, half:].astype(BF16).astype(F32), U32)
    return (lo >> 16) | hi


def _unpack_halves(w):
    lo = lax.bitcast_convert_type(w << 16, F32)
    hi = lax.bitcast_convert_type(w & jnp.uint32(0xFFFF0000), F32)
    return jnp.concatenate([lo, hi], axis=1)


def _ada_body(c_ref, w_ref, b_ref, o_ref):
    s = jax.nn.silu(c_ref[...]).astype(BF16)
    o_ref[0] = _dot(s, w_ref[0].astype(BF16)) + b_ref[0]


def _ada(c_all, w_ada, b_ada):
    n_layers, d, n = w_ada.shape
    rows = c_all.shape[0]
    tn = 1536
    return pl.pallas_call(
        _ada_body,
        name="ada",
        grid=(n_layers, n // tn),
        in_specs=[pl.BlockSpec((rows, d), lambda l, j: (0, 0)),
                  pl.BlockSpec((1, d, tn), lambda l, j: (l, 0, j)),
                  pl.BlockSpec((1, 1, tn), lambda l, j: (l, 0, j))],
        out_specs=pl.BlockSpec((1, rows, tn), lambda l, j: (l, 0, j)),
        out_shape=jax.ShapeDtypeStruct((n_layers, rows, n), F32),
        compiler_params=_cparams("arbitrary", "arbitrary"),
    )(c_all, w_ada, b_ada.reshape(n_layers, 1, n))


def _stream_tile(xc_ref, xl_ref, ctx_tiles):
    return jnp.where(pl.program_id(0) < ctx_tiles, xc_ref[...], xl_ref[...])


def _stream_specs(tm, d, ctx_tiles):
    return [pl.BlockSpec((tm, d), lambda i: (jnp.minimum(i, ctx_tiles - 1), 0)),
            pl.BlockSpec((tm, d), lambda i: (jnp.maximum(i - ctx_tiles, 0), 0))]


def _premix_body(xc_ref, xl_ref, mod_ref, g_ref, w_ref, pa_ref, q_ref, k_ref, v_ref, kb_ref, vb_ref, pc0_ref,
                 pc1_ref, fold_ref, *, ctx_tiles):
    x = _stream_tile(xc_ref, xl_ref, ctx_tiles)
    d = x.shape[1]
    h = _rms(x, g_ref[...]) * (1 + mod_ref[0, :, d:2 * d]) + mod_ref[0, :, 0:d]
    p = _dot(h.astype(BF16), w_ref[...])
    d_a2 = pa_ref.shape[1]
    d_b = q_ref.shape[1]
    o = d_a2
    pa_ref[...] = p[:, 0:o]
    q_ref[...] = p[:, o:o + d_b].astype(BF16)
    k = p[:, o + d_b:o + 2 * d_b]
    v = p[:, o + 2 * d_b:o + 3 * d_b]

    @pl.when(pl.program_id(0) < ctx_tiles)
    def _():
        k_ref[...] = k
        v_ref[...] = v

    kb_ref[...] = k.astype(BF16)
    vb_ref[...] = v.astype(BF16)
    for h, pc_ref in enumerate((pc0_ref, pc1_ref)):
        fold_ref[...] = p[:, o + 3 * d_b + h * LANES:o + 3 * d_b + (h + 1) * LANES]
        _fold_chunks(pc_ref, fold_ref)


def _fold_chunks(dst_ref, src_ref):
    n = dst_ref.shape[0]
    for i in range(S5_CHUNK):
        dst_ref[:, i * LANES:(i + 1) * LANES] = src_ref[pl.ds(i, n, stride=S5_CHUNK), :]


def _unfold_chunks(dst_ref, src_ref):
    n = src_ref.shape[0]
    for i in range(S5_CHUNK):
        dst_ref[pl.ds(i, n, stride=S5_CHUNK), :] = src_ref[:, i * LANES:(i + 1) * LANES]


def _premix(xc, xl, mod, g, w_in_b, row_map, d_a, d_b, d_c):
    t_ctx, d = xc.shape
    t = t_ctx + xl.shape[0]
    tm = TOK_TILE
    d_in = w_in_b.shape[1]
    ctx_tiles = t_ctx // tm
    tok = lambda n: pl.BlockSpec((tm, n), lambda i: (i, 0))
    ctx = pl.BlockSpec((tm, d_b), lambda i: (jnp.minimum(i, ctx_tiles - 1), 0))
    fold = pl.BlockSpec((tm // S5_CHUNK, S5_CHUNK * LANES), lambda i: (i, 0))
    return pl.pallas_call(
        functools.partial(_premix_body, ctx_tiles=ctx_tiles),
        name="premix",
        grid=(t // tm,),
        in_specs=_stream_specs(tm, d, ctx_tiles) + [
            pl.BlockSpec((1, 1, mod.shape[2]), lambda i: (row_map(i), 0, 0)),
            pl.BlockSpec((1, d), lambda i: (0, 0)),
            pl.BlockSpec((d, d_in), lambda i: (0, 0))],
        out_specs=[tok(2 * d_a), tok(d_b), ctx, ctx, tok(d_b), tok(d_b), fold, fold],
        out_shape=[jax.ShapeDtypeStruct((t, 2 * d_a), F32),
                   jax.ShapeDtypeStruct((t, d_b), BF16),
                   jax.ShapeDtypeStruct((t_ctx, d_b), F32),
                   jax.ShapeDtypeStruct((t_ctx, d_b), F32),
                   jax.ShapeDtypeStruct((t, d_b), BF16),
                   jax.ShapeDtypeStruct((t, d_b), BF16)]
        + [jax.ShapeDtypeStruct((t // S5_CHUNK, S5_CHUNK * LANES), F32)] * 2,
        scratch_shapes=[pltpu.VMEM((tm, LANES), F32)],
        compiler_params=_cparams("arbitrary"),
    )(xc, xl, mod, g, w_in_b)


def _chunk_body(pa_ref, g_ref, w_ref, b_ref, o_ref):
    z = jax.nn.gelu(pa_ref[...])
    d_a = o_ref.shape[1]
    hd = d_a // NH_A
    u = z[:, :d_a]
    v = z[:, d_a:]
    mu = jnp.mean(v, axis=-1, keepdims=True)
    var = jnp.mean(jnp.square(v - mu), axis=-1, keepdims=True)
    vb = ((v - mu) * lax.rsqrt(var + EPS) * g_ref[...]).astype(BF16)
    head = lax.broadcasted_iota(I32, (CHUNK, d_a), 1) // hd
    for ch in range(pa_ref.shape[0] // CHUNK):
        rows = slice(ch * CHUNK, (ch + 1) * CHUNK)
        sf = _dot(w_ref[...], vb[rows])
        s = b_ref[...]
        for h in range(NH_A):
            s = s + jnp.where(head == h, sf[h * CHUNK:(h + 1) * CHUNK], 0.0)
        o_ref[rows, :] = (u[rows] * s).astype(BF16)


def _chunk_mlp(pa, sgu_g, w_sp_b, bias):
    t, d2 = pa.shape
    d_a = d2 // 2
    tm = TOK_TILE
    return pl.pallas_call(
        _chunk_body,
        name="chunk_mlp",
        grid=(t // tm,),
        in_specs=[pl.BlockSpec((tm, d2), lambda i: (i, 0)),
                  pl.BlockSpec((1, d_a), lambda i: (0, 0)),
                  pl.BlockSpec(w_sp_b.shape, lambda i: (0, 0)),
                  pl.BlockSpec(bias.shape, lambda i: (0, 0))],
        out_specs=pl.BlockSpec((tm, d_a), lambda i: (i, 0)),
        out_shape=jax.ShapeDtypeStruct((t, d_a), BF16),
        compiler_params=_cparams("arbitrary"),
    )(pa, sgu_g, w_sp_b, bias)


def _stack_pair(qg):
    lane = lax.broadcasted_iota(I32, qg.shape, 1)
    zero = jnp.zeros_like(qg)
    return jnp.concatenate([jnp.where(lane < HD_B, qg, zero), jnp.where(lane >= HD_B, qg, zero)], axis=0)


def _unstack_pair(o2):
    n = o2.shape[0] // 2
    lane = lax.broadcasted_iota(I32, (n, o2.shape[1]), 1)
    return jnp.where(lane < HD_B, o2[:n], o2[n:])


def _ctx_attn_body(q_ref, k_ref, v_ref, o_ref):
    for g in range(q_ref.shape[1] // LANES):
        cols = slice(g * LANES, (g + 1) * LANES)
        q2 = _stack_pair(q_ref[:, cols])
        s = _dot_nt(q2, k_ref[:, cols].astype(BF16)) * ATTN_SCALE
        e = jnp.exp(s - jnp.max(s, axis=-1, keepdims=True))
        p = e / jnp.sum(e, axis=-1, keepdims=True)
        o2 = _dot(p.astype(BF16), v_ref[:, cols].astype(BF16))
        o_ref[:, cols] = _unstack_pair(o2).astype(BF16)


def _ctx_attn(q, k, v, n_batch, seq):
    d_b = q.shape[1]
    blk = pl.BlockSpec((seq, d_b), lambda b: (b, 0))
    return pl.pallas_call(
        _ctx_attn_body,
        name="ctx_attn",
        grid=(n_batch,),
        in_specs=[blk, blk, blk],
        out_specs=blk,
        out_shape=jax.ShapeDtypeStruct((n_batch * seq, d_b), BF16),
        compiler_params=_cparams("arbitrary"),
    )(q, k, v)


def _nat_body(q_ref, k_ref, v_ref, ck_ref, cv_ref, bias_ref, o_ref, *, rows):
    r0 = pl.program_id(1) * NAT_ROWS
    n_win = NA_WIN_R * GRID_W
    pr = 2 * GRID_W
    starts, cases = [], []
    for i in range(NAT_ROWS):
        rs = jnp.clip(r0 + i - NA_WIN_R // 2, 0, rows - NA_WIN_R)
        starts.append(pl.multiple_of(rs * GRID_W, GRID_W))
        cases.append(r0 + i - rs)
    for g in range(q_ref.shape[1] // LANES):
        cols = slice(g * LANES, (g + 1) * LANES)
        q2 = jnp.concatenate([_stack_pair(q_ref[i * GRID_W:(i + 1) * GRID_W, cols]) for i in range(NAT_ROWS)],
                             axis=0)
        s_ctx = _dot_nt(q2, ck_ref[0, :, cols]) * ATTN_SCALE
        m_ctx = jnp.max(s_ctx, axis=-1, keepdims=True)
        e_wins, invs, ms = [], [], []
        for i in range(NAT_ROWS):
            s_win = (_dot_nt(q2[i * pr:(i + 1) * pr], k_ref[pl.ds(starts[i], n_win), cols]) * ATTN_SCALE
                     + bias_ref[cases[i], g])
            m = jnp.maximum(jnp.max(s_win, axis=-1, keepdims=True), m_ctx[i * pr:(i + 1) * pr])
            e_wins.append(jnp.exp(s_win - m))
            ms.append(m)
        e_ctx = jnp.exp(s_ctx - jnp.concatenate(ms, axis=0))
        l_ctx = jnp.sum(e_ctx, axis=-1, keepdims=True)
        for i in range(NAT_ROWS):
            invs.append(1.0 / (jnp.sum(e_wins[i], axis=-1, keepdims=True) + l_ctx[i * pr:(i + 1) * pr]))
        o_ctx = _dot((e_ctx * jnp.concatenate(invs, axis=0)).astype(BF16), cv_ref[0, :, cols])
        for i in range(NAT_ROWS):
            o2 = _dot((e_wins[i] * invs[i]).astype(BF16), v_ref[pl.ds(starts[i], n_win), cols])
            o_ref[i * GRID_W:(i + 1) * GRID_W, cols] = _unstack_pair(o2 + o_ctx[i * pr:(i + 1) * pr]).astype(BF16)


def _nat_attn(q, kb, vb, ck, cv, bias, n_batch, seq, tok0):
    d_b = q.shape[1]
    rows = seq // GRID_W
    lc = ck.shape[1]
    blk = NAT_ROWS * GRID_W
    steps = seq // blk
    q0 = tok0 // blk
    i0 = tok0 // seq
    img = pl.BlockSpec((seq, d_b), lambda b, r: (i0 + b, 0))
    ctx = pl.BlockSpec((1, lc, d_b), lambda b, r: (b, 0, 0))
    return pl.pallas_call(
        functools.partial(_nat_body, rows=rows),
        name="nat_attn",
        grid=(n_batch, steps),
        in_specs=[pl.BlockSpec((blk, d_b), lambda b, r: (q0 + b * steps + r, 0)),
                  img, img, ctx, ctx,
                  pl.BlockSpec(bias.shape, lambda b, r: (0, 0, 0, 0))],
        out_specs=pl.BlockSpec((blk, d_b), lambda b, r: (b * steps + r, 0)),
        out_shape=jax.ShapeDtypeStruct((n_batch * seq, d_b), BF16),
        compiler_params=_cparams("arbitrary", "arbitrary"),
    )(q, kb, vb, ck, cv, bias)


def _nat_bias(rpb):
    n_heads = rpb.shape[0]
    cols = jnp.arange(GRID_W)
    col_start = jnp.clip(cols - NA_WIN_C // 2, 0, GRID_W - NA_WIN_C)
    j = jnp.arange(GRID_W)
    valid = (j[None, :] >= col_start[:, None]) & (j[None, :] < col_start[:, None] + NA_WIN_C)
    col_off = jnp.clip(j[None, :] - cols[:, None] + (NA_WIN_C - 1), 0, 2 * NA_WIN_C - 2)
    toe = jnp.where(valid[None, None], rpb.astype(F32)[:, :, col_off], NEG_BIG)
    cases = jnp.stack([toe[:, NA_WIN_R - 1 - delta:2 * NA_WIN_R - 1 - delta] for delta in range(NA_WIN_R)])
    return cases.transpose(0, 1, 3, 2, 4).reshape(NA_WIN_R, n_heads // 2, 2 * GRID_W, NA_WIN_R * GRID_W)


def _s5_weights(a_re, a_im, log_dt, b_re, b_im, c_re, c_im, ssm_d):
    n_g = a_re.shape[1]
    c = S5_CHUNK
    lam = lax.complex(a_re.astype(F32), a_im.astype(F32))
    ldt = lam * jnp.exp(log_dt.astype(F32))[..., None]
    lam_bar = jnp.exp(ldt)
    b_bar = ((lam_bar - 1) / lam)[..., None] * lax.complex(b_re.astype(F32), b_im.astype(F32))
    c_mat = lax.complex(c_re.astype(F32), c_im.astype(F32))
    pw = jnp.exp(ldt[None] * jnp.arange(c + 1, dtype=F32)[:, None, None, None])
    kern = jnp.real(jnp.einsum('dgcp,kdgp,dgpe->dgkce', c_mat, pw[:c], b_bar))
    i = jnp.arange(c)
    lag = i[None, :] - i[:, None]
    tf = jnp.where((lag >= 0)[None, :, :, None, None], kern[0][:, jnp.clip(lag, 0, c - 1)], 0.0)
    tb = jnp.where((lag <= 0)[None, :, :, None, None], kern[1][:, jnp.clip(-lag, 0, c - 1)], 0.0)
    t_mat = (tf + tb).transpose(0, 1, 4, 2, 3).reshape(n_g, c * SSM_CH, c * SSM_CH)
    mf = pw[:c][::-1, 0][:, :, :, None] * b_bar[0][None]
    mb = pw[:c, 1][:, :, :, None] * b_bar[1][None]
    mf = mf.transpose(1, 0, 3, 2).reshape(n_g, c * SSM_CH, SSM_P)
    mb = mb.transpose(1, 0, 3, 2).reshape(n_g, c * SSM_CH, SSM_P)
    mq = jnp.stack([jnp.real(mf), jnp.imag(mf), jnp.real(mb), jnp.imag(mb)], axis=2)
    zf = c_mat[0][:, None] * pw[1:c + 1, 0][:, :, None, :].transpose(1, 0, 2, 3)
    zb = c_mat[1][:, None] * pw[1:c + 1, 1][::-1][:, :, None, :].transpose(1, 0, 2, 3)
    zf = zf.transpose(0, 3, 1, 2).reshape(n_g, SSM_P, c * SSM_CH)
    zb = zb.transpose(0, 3, 1, 2).reshape(n_g, SSM_P, c * SSM_CH)
    wq = jnp.stack([jnp.real(zf), -jnp.imag(zf), jnp.real(zb), -jnp.imag(zb)], axis=1)
    gh = LANES // SSM_CH
    n_half = n_g // gh
    kd = c * SSM_CH

    def interleaved_block_diag(blocks, r1, c1):
        n = gh * kd
        b = blocks.astype(BF16).reshape(n_half, gh, kd, kd)
        bd = jnp.zeros((n_half, gh, kd, gh, kd), BF16)
        for g in range(gh):
            bd = bd.at[:, g, :, g, :].set(b[:, g])
        bd = bd.reshape(n_half, gh, r1, kd // r1, n).transpose(0, 2, 1, 3, 4).reshape(n_half, n, n)
        bd = jnp.swapaxes(bd, 1, 2)
        bd = bd.reshape(n_half, gh, c1, kd // c1, n).transpose(0, 2, 1, 3, 4).reshape(n_half, n, n)
        return jnp.swapaxes(bd, 1, 2)

    m_half = interleaved_block_diag(mq.reshape(n_g, kd, 4 * SSM_P), c, 4)
    t_half = interleaved_block_diag(t_mat, c, c)
    w_half = interleaved_block_diag(wq.reshape(n_g, 4 * SSM_P, kd), 4, c)
    tw_half = jnp.concatenate([t_half, w_half], axis=1)
    a_c = pw[c]
    a16 = jnp.stack([jnp.real(a_c[0]), jnp.imag(a_c[0]), jnp.real(a_c[1]), jnp.imag(a_c[1])])
    a16 = a16.reshape(4, 1, n_half, gh * SSM_P).transpose(2, 0, 1, 3)
    dvec = ssm_d.astype(F32).reshape(n_half, 1, LANES)
    return m_half, tw_half, a16, dvec


def _s5_state_body(u_ref, m_ref, fre_ref, fim_ref, bre_ref, bim_ref):
    nb, tn, kd = u_ref.shape
    w = fre_ref.shape[2]
    r = _dot(u_ref[...].reshape(nb * tn, kd).astype(BF16), m_ref[...])
    for q, o_ref in enumerate((fre_ref, fim_ref, bre_ref, bim_ref)):
        for b in range(nb):
            o_ref[:, b, :] = r[b * tn:(b + 1) * tn, q * w:(q + 1) * w]


def _s5_rows(nb, n):
    return max(8, min(n, 256 // nb))


def _s5_states(u3, m_half):
    nb, n, kd = u3.shape
    tn = _s5_rows(nb, n)
    w = m_half.shape[1] // 4
    out = pl.BlockSpec((tn, nb, w), lambda i: (i, 0, 0))
    return pl.pallas_call(
        _s5_state_body,
        name="s5_states",
        grid=(n // tn,),
        in_specs=[pl.BlockSpec((nb, tn, kd), lambda i: (0, i, 0)),
                  pl.BlockSpec(m_half.shape, lambda i: (0, 0))],
        out_specs=[out] * 4,
        out_shape=[jax.ShapeDtypeStruct((n, nb, w), F32)] * 4,
        compiler_params=_cparams("arbitrary"),
    )(u3, m_half)


def _s5_scan_body(sfr_ref, sfi_ref, sbr_ref, sbi_ref, a_ref, h0_ref,
                  hfr_ref, hfi_ref, hbr_ref, hbi_ref, fin_ref, st_ref):
    @pl.when(pl.program_id(0) == 0)
    def _():
        st_ref[...] = h0_ref[...]

    ks = sfr_ref.shape[0]
    afr, afi, abr, abi = a_ref[0], a_ref[1], a_ref[2], a_ref[3]

    def step(s, carry):
        fr, fi, br, bi = carry
        sb = ks - 1 - s
        hfr_ref[s] = fr
        hfi_ref[s] = fi
        hbr_ref[sb] = br
        hbi_ref[sb] = bi
        nfr = afr * fr - afi * fi + sfr_ref[s]
        nfi = afr * fi + afi * fr + sfi_ref[s]
        nbr = abr * br - abi * bi + sbr_ref[sb]
        nbi = abr * bi + abi * br + sbi_ref[sb]
        return nfr, nfi, nbr, nbi

    carry = lax.fori_loop(0, ks, step, (st_ref[0], st_ref[1], st_ref[2], st_ref[3]))
    for q in range(4):
        st_ref[q] = carry[q]
        fin_ref[q] = carry[q]


def _s5_scan(s3, a16, h0):
    n_chunks, n_batch, width = s3[0].shape
    ks = min(n_chunks, 32)
    nb = n_chunks // ks
    fwd = pl.BlockSpec((ks, n_batch, width), lambda i: (i, 0, 0))
    bwd = pl.BlockSpec((ks, n_batch, width), lambda i: (nb - 1 - i, 0, 0))
    small = lambda shape: pl.BlockSpec(shape, lambda i: (0, 0, 0))
    outs = pl.pallas_call(
        _s5_scan_body,
        name="s5_scan",
        grid=(nb,),
        in_specs=[fwd, fwd, bwd, bwd, small(a16.shape), small(h0.shape)],
        out_specs=[fwd, fwd, bwd, bwd, small(h0.shape)],
        out_shape=[jax.ShapeDtypeStruct((n_chunks, n_batch, width), F32)] * 4
        + [jax.ShapeDtypeStruct(h0.shape, F32)],
        scratch_shapes=[pltpu.VMEM(h0.shape, F32)],
        compiler_params=_cparams("arbitrary"),
    )(*s3, a16, h0)
    return outs[:4], outs[4]


def _s5_out_body(u_ref, us_ref, hfr_ref, hfi_ref, hbr_ref, hbi_ref, tw_ref, d_ref, y_ref):
    nb, tn, kd = u_ref.shape
    rows = nb * tn
    wo = us_ref.shape[2]
    y = _dot(u_ref[...].reshape(rows, kd).astype(BF16), tw_ref[0:kd, :])
    w = hfr_ref.shape[2]
    for q, h_ref in enumerate((hfr_ref, hfi_ref, hbr_ref, hbi_ref)):
        hq = jnp.concatenate([h_ref[:, b, :] for b in range(nb)], axis=0)
        y = y + _dot(hq.astype(BF16), tw_ref[kd + q * w:kd + (q + 1) * w, :])
    skip = jnp.concatenate([d_ref[...]] * (wo // LANES), axis=1) * us_ref[...].reshape(rows, wo)
    y_ref[...] = jax.nn.gelu(y + skip).reshape(nb, tn, wo)


def _s5_out(u3, h4, tw_half, dvec):
    nb, n, kd = u3.shape
    tn = _s5_rows(nb, n)
    wo = S5_OUT_STEPS * LANES
    w = h4[0].shape[2]
    hb = pl.BlockSpec((tn, nb, w), lambda i, j: (i, 0, 0))
    return pl.pallas_call(
        _s5_out_body,
        name="s5_out",
        grid=(n // tn, kd // wo),
        in_specs=[pl.BlockSpec((nb, tn, kd), lambda i, j: (0, i, 0)),
                  pl.BlockSpec((nb, tn, wo), lambda i, j: (0, i, j)),
                  hb, hb, hb, hb,
                  pl.BlockSpec((tw_half.shape[0], wo), lambda i, j: (0, j)),
                  pl.BlockSpec(dvec.shape, lambda i, j: (0, 0))],
        out_specs=pl.BlockSpec((nb, tn, wo), lambda i, j: (0, i, j)),
        out_shape=jax.ShapeDtypeStruct(u3.shape, F32),
        compiler_params=_cparams("arbitrary", "arbitrary"),
    )(u3, u3, *h4, tw_half, dvec)


def _s5(pc_halves, n_batch, seq, weights, h0):
    m_half, tw_half, a16, dvec = weights
    n_chunks = seq // S5_CHUNK
    ys, fins = [], []
    for h, pc in enumerate(pc_halves):
        w = m_half.shape[2] // 4
        u3 = pc.reshape(n_batch, n_chunks, S5_CHUNK * LANES)
        s4 = _s5_states(u3, m_half[h])
        h4, fin = _s5_scan(s4, a16[h], h0[:, :, h * w:(h + 1) * w])
        ys.append(_s5_out(u3, h4, tw_half[h], dvec[h]).reshape(n_batch * n_chunks, S5_CHUNK * LANES))
        fins.append(fin)
    return ys, jnp.concatenate(fins, axis=2)


def _post_body(xc_ref, xl_ref, ya_ref, yb_ref, yc0_ref, yc1_ref, mod_ref, gpost_ref, gffn_ref, wglu_ref, bglu_ref,
               wo_ref, whi_ref, wlo_ref, x1_ref, logit_ref, h2p_ref, unfold_ref, *, ctx_tiles):
    x = _stream_tile(xc_ref, xl_ref, ctx_tiles)
    d = x.shape[1]
    d_a = ya_ref.shape[1]
    d_b = yb_ref.shape[1]
    for h, yc_ref in enumerate((yc0_ref, yc1_ref)):
        _unfold_chunks(unfold_ref.at[h], yc_ref)
    y = jnp.concatenate([unfold_ref[0], unfold_ref[1]], axis=1)
    glu = y * jax.nn.sigmoid(_dot(y.astype(BF16), wglu_ref[...]) + bglu_ref[...])
    mixed = (_dot(ya_ref[...], wo_ref[0:d_a, :]) + _dot(yb_ref[...], wo_ref[d_a:d_a + d_b, :])
             + _dot(glu.astype(BF16), wo_ref[d_a + d_b:, :]))
    x1 = x + mod_ref[0, :, 2 * d:3 * d] * _rms(mixed, gpost_ref[...])
    x1_ref[...] = x1
    h2 = _rms(x1, gffn_ref[...]) * (1 + mod_ref[0, :, 4 * d:5 * d]) + mod_ref[0, :, 3 * d:4 * d]
    logit_ref[...] = _router_logits(h2, whi_ref, wlo_ref)
    _store_tiled(h2p_ref, _pack_halves(h2))


def _post(xc, xl, ya, yb, yc0, yc1, mod, g_post, g_ffn, w_glu_b, b_glu, w_out_b, w_hi, w_lo, row_map):
    t_ctx, d = xc.shape
    t = t_ctx + xl.shape[0]
    tm = TOK_TILE
    nc = d // LANES
    ctx_tiles = t_ctx // tm
    tok = lambda n: pl.BlockSpec((tm, n), lambda i: (i, 0))
    fold = pl.BlockSpec((tm // S5_CHUNK, S5_CHUNK * LANES), lambda i: (i, 0))
    full = lambda a: pl.BlockSpec(a.shape, lambda i: (0,) * a.ndim)
    return pl.pallas_call(
        functools.partial(_post_body, ctx_tiles=ctx_tiles),
        name="post_mix",
        grid=(t // tm,),
        in_specs=_stream_specs(tm, d, ctx_tiles) + [
            tok(ya.shape[1]), tok(yb.shape[1]), fold, fold,
            pl.BlockSpec((1, 1, mod.shape[2]), lambda i: (row_map(i), 0, 0)),
            full(g_post), full(g_ffn), full(w_glu_b), full(b_glu), full(w_out_b), full(w_hi), full(w_lo)],
        out_specs=[tok(d), pl.BlockSpec((N_EXPERTS, tm), lambda i: (0, i)),
                   pl.BlockSpec((tm * nc // 2, LANES), lambda i: (i, 0))],
        out_shape=[jax.ShapeDtypeStruct((t, d), F32), jax.ShapeDtypeStruct((N_EXPERTS, t), F32),
                   jax.ShapeDtypeStruct((t * nc // 2, LANES), U32)],
        scratch_shapes=[pltpu.VMEM((2, tm, LANES), F32)],
        compiler_params=_cparams("arbitrary"),
    )(xc, xl, ya, yb, yc0, yc1, mod, g_post, g_ffn, w_glu_b, b_glu, w_out_b, w_hi, w_lo)


def _router_logits(h, whi_ref, wlo_ref):
    hi = h.astype(BF16)
    lo = (h - hi.astype(F32)).astype(BF16)
    return _dot_nt(whi_ref[...], hi) + (_dot_nt(whi_ref[...], lo) + _dot_nt(wlo_ref[...], hi))


def _route_body(logit_ref, b_ref, idx_ref, gate_ref, pos_ref, cnt_ref, carry_ref):
    @pl.when(pl.program_id(0) == 0)
    def _():
        carry_ref[...] = jnp.zeros_like(carry_ref)

    tm = idx_ref.shape[1]
    scores = jax.nn.sigmoid(logit_ref[...])
    sel = scores + b_ref[...]
    gsz = N_EXPERTS // N_EXP_GROUPS
    within = lax.broadcasted_iota(I32, (gsz, tm), 0).astype(F32)
    grp = []
    for g in range(N_EXP_GROUPS):
        blk = sel[g * gsz:(g + 1) * gsz]
        m1 = jnp.max(blk, axis=0, keepdims=True)
        first = jnp.min(jnp.where(blk == m1, within, float(gsz)), axis=0, keepdims=True)
        m2 = jnp.max(jnp.where(within == first, -jnp.inf, blk), axis=0, keepdims=True)
        grp.append(m1 + m2)
    blocks = []
    for g in range(N_EXP_GROUPS):
        ahead = jnp.zeros((1, tm), F32)
        for o in range(N_EXP_GROUPS):
            if o == g:
                continue
            beats = (grp[o] >= grp[g]) if o < g else (grp[o] > grp[g])
            ahead = ahead + jnp.where(beats, 1.0, 0.0)
        ahead = jnp.broadcast_to(ahead, (gsz, tm))
        blocks.append(jnp.where(ahead < TOPK_GROUPS, sel[g * gsz:(g + 1) * gsz], -jnp.inf))
    v = jnp.concatenate(blocks, axis=0)
    eidx = lax.broadcasted_iota(I32, (N_EXPERTS, tm), 0)
    rnk = jnp.zeros((N_EXPERTS, tm), F32)
    for e in range(N_EXPERTS):
        row = v[e:e + 1]
        rnk = rnk + jnp.where(eidx > e, jnp.where(row >= v, 1.0, 0.0), jnp.where(row > v, 1.0, 0.0))
    chosen = rnk < TOP_K
    w = jnp.where(chosen, scores, 0.0)
    wn = w / jnp.sum(w, axis=0, keepdims=True) * ROUTE_SCALE
    tri = (lax.broadcasted_iota(I32, (tm, tm), 0) < lax.broadcasted_iota(I32, (tm, tm), 1))
    chosen_f = jnp.where(chosen, 1.0, 0.0)
    prefix = _dot(chosen_f.astype(BF16), jnp.where(tri, 1.0, 0.0).astype(BF16)) + carry_ref[:, 0:1]
    total = carry_ref[...] + jnp.sum(chosen_f, axis=1, keepdims=True)
    carry_ref[...] = total
    cnt_ref[...] = total
    eidx_f = eidx.astype(F32)
    for k in range(TOP_K):
        one = rnk == k
        idx_ref[k:k + 1, :] = jnp.sum(jnp.where(one, eidx_f, 0.0), axis=0, keepdims=True).astype(I32)
        gate_ref[k:k + 1, :] = jnp.sum(jnp.where(one, wn, 0.0), axis=0, keepdims=True)
        pos_ref[k:k + 1, :] = jnp.sum(jnp.where(one, prefix, 0.0), axis=0, keepdims=True).astype(I32)


def _route(logits, b_router):
    t = logits.shape[1]
    tm = TOK_TILE
    out = pl.BlockSpec((TOP_K, tm), lambda i: (0, i))
    full = lambda a: pl.BlockSpec(a.shape, lambda i: (0, 0))
    return pl.pallas_call(
        _route_body,
        name="route",
        grid=(t // tm,),
        in_specs=[pl.BlockSpec((N_EXPERTS, tm), lambda i: (0, i)), full(b_router)],
        out_specs=[out, out, out, pl.BlockSpec((N_EXPERTS, LANES), lambda i: (0, 0))],
        out_shape=[jax.ShapeDtypeStruct((TOP_K, t), I32), jax.ShapeDtypeStruct((TOP_K, t), F32),
                   jax.ShapeDtypeStruct((TOP_K, t), I32), jax.ShapeDtypeStruct((N_EXPERTS, LANES), F32)],
        scratch_shapes=[pltpu.VMEM((N_EXPERTS, LANES), F32)],
        compiler_params=_cparams("arbitrary"),
    )(logits, b_router)


def _dest_body(starts_ref, idx_ref, pos_ref, dest_ref):
    idx = idx_ref[...]
    acc = pos_ref[...]
    for e in range(N_EXPERTS):
        acc = acc + jnp.where(idx == e, starts_ref[e], 0)
    dest_ref[...] = acc


def _dest(starts, idx, pos):
    t = idx.shape[1]
    tile = math.gcd(t, 4096)
    blk = pl.BlockSpec((TOP_K, tile), lambda i, st: (0, i))
    return pl.pallas_call(
        _dest_body,
        name="moe_dest",
        grid_spec=pltpu.PrefetchScalarGridSpec(num_scalar_prefetch=1, grid=(t // tile,),
                                               in_specs=[blk, blk], out_specs=blk),
        out_shape=jax.ShapeDtypeStruct(idx.shape, I32),
        compiler_params=_cparams("arbitrary"),
    )(starts, idx, pos)


def _dispatch_body(pad_lo_ref, pad_hi_ref, dest_ref, h_ref, xs_ref, zero_ref, sem, zsem, *, n_pad_rows):
    tile = dest_ref.shape[1]
    nw = h_ref.shape[0] // tile

    def token(ref, r):
        return ref.at[pl.ds(pl.multiple_of(r * nw, nw), nw)]

    def zero_copy(r, n):
        return pltpu.make_async_copy(zero_ref.at[pl.ds(0, n * nw)],
                                     xs_ref.at[pl.ds(pl.multiple_of(r * nw, nw), n * nw)], zsem)

    @pl.when(pl.program_id(0) == 0)
    def _():
        zero_ref[...] = jnp.zeros_like(zero_ref)

        def per_expert(e, c):
            lo = pad_lo_ref[e]
            runs = (pad_hi_ref[e] - lo) // ZERO_RUN

            def run(j, c2):
                zero_copy(lo + j * ZERO_RUN, ZERO_RUN).start()
                return c2

            def one(r, c2):
                zero_copy(r, 1).start()
                return c2

            c = lax.fori_loop(0, runs, run, c)
            return lax.fori_loop(lo + runs * ZERO_RUN, pad_hi_ref[e], one, c)

        lax.fori_loop(0, N_EXPERTS, per_expert, 0)

    def row_copy(t, k):
        return pltpu.make_async_copy(token(h_ref, t), token(xs_ref, dest_ref[k, t]), sem)

    def start(t, c):
        for k in range(TOP_K):
            row_copy(t, k).start(priority=k % DMA_THREADS)
        return c

    def wait(t, c):
        for k in range(TOP_K):
            row_copy(t, k).wait()
        return c

    lax.fori_loop(0, tile, start, 0)
    lax.fori_loop(0, tile, wait, 0)

    @pl.when(pl.program_id(0) == 0)
    def _():
        def one(r, c):
            zero_copy(0, ZERO_RUN).wait()
            return c
        lax.fori_loop(0, n_pad_rows // ZERO_RUN, one, 0)


def _dispatch(pad_lo, pad_hi, dest, h2p, n_rows):
    t = dest.shape[1]
    nw = h2p.shape[0] // t
    tile = DISPATCH_TILE
    grid_spec = pltpu.PrefetchScalarGridSpec(
        num_scalar_prefetch=2,
        grid=(t // tile,),
        in_specs=[pl.BlockSpec((TOP_K, tile), lambda i, lo, hi: (0, i), memory_space=pltpu.SMEM),
                  pl.BlockSpec((tile * nw, LANES), lambda i, lo, hi: (i, 0))],
        out_specs=pl.BlockSpec(memory_space=pl.ANY),
        scratch_shapes=[pltpu.VMEM((ZERO_RUN * nw, LANES), U32), pltpu.SemaphoreType.DMA(()),
                        pltpu.SemaphoreType.DMA(())],
    )
    assert (n_rows - t * TOP_K) % ZERO_RUN == 0
    return pl.pallas_call(
        functools.partial(_dispatch_body, n_pad_rows=n_rows - t * TOP_K),
        name="moe_dispatch",
        grid_spec=grid_spec,
        out_shape=jax.ShapeDtypeStruct((n_rows * nw, LANES), U32),
        compiler_params=_cparams("arbitrary"),
    )(pad_lo, pad_hi, dest, h2p)


def _experts_body(be_ref, x_ref, wg_ref, wu_ref, wd_ref, y_ref, wgb_ref, wub_ref, wdb_ref):
    i = pl.program_id(0)

    @pl.when((i == 0) | (be_ref[i] != be_ref[jnp.maximum(i - 1, 0)]))
    def _():
        wgb_ref[...] = wg_ref[0, 0].astype(BF16)
        wub_ref[...] = wu_ref[0, 0].astype(BF16)
        wdb_ref[...] = wd_ref[0, 0].astype(BF16)

    xb = _unpack_halves(_load_tiled(x_ref, MOE_ROWS)).astype(BF16)
    hb = jax.nn.silu(_dot(xb, wgb_ref[...])) * _dot(xb, wub_ref[...])
    _store_tiled(y_ref, _pack_halves(_dot(hb.astype(BF16), wdb_ref[...])))


def _experts(blk_exp, xs, wg, wu, wd, layer):
    d, de = wg.shape[2], wg.shape[3]
    nc = d // (2 * LANES)
    bm = MOE_ROWS
    tiles = pl.BlockSpec((bm * nc, LANES), lambda i, be: (i, 0))
    grid_spec = pltpu.PrefetchScalarGridSpec(
        num_scalar_prefetch=1,
        grid=(xs.shape[0] // (bm * nc),),
        in_specs=[tiles,
                  pl.BlockSpec((1, 1, d, de), lambda i, be: (layer, be[i], 0, 0)),
                  pl.BlockSpec((1, 1, d, de), lambda i, be: (layer, be[i], 0, 0)),
                  pl.BlockSpec((1, 1, de, d), lambda i, be: (layer, be[i], 0, 0))],
        out_specs=tiles,
        scratch_shapes=[pltpu.VMEM((d, de), BF16), pltpu.VMEM((d, de), BF16), pltpu.VMEM((de, d), BF16)],
    )
    return pl.pallas_call(
        _experts_body,
        name="moe_experts",
        grid_spec=grid_spec,
        out_shape=jax.ShapeDtypeStruct(xs.shape, U32),
        compiler_params=_cparams("arbitrary"),
    )(blk_exp, xs, wg, wu, wd)


def _combine_body(dest_ref, next_ref, gate_ref, h_ref, x_ref, mod_ref, g_ref, wsg_ref, wsu_ref, wsd_ref, ys_hbm,
                  oc_ref, ol_ref, buf_ref, sem, *, ctx_steps):
    tile, d = x_ref.shape
    nw = buf_ref.shape[2] // tile
    i = pl.program_id(0)
    slot = i % 2

    def row_copy(d_ref, s, t, k):
        return pltpu.make_async_copy(ys_hbm.at[pl.ds(pl.multiple_of(d_ref[k, t] * nw, nw), nw)],
                                     buf_ref.at[s, k, pl.ds(pl.multiple_of(t * nw, nw), nw)], sem.at[s])

    def gather(d_ref, s):
        def start(t, c):
            for k in range(TOP_K):
                row_copy(d_ref, s, t, k).start(priority=k % DMA_THREADS)
            return c
        lax.fori_loop(0, tile, start, 0)

    @pl.when(i == 0)
    def _():
        gather(dest_ref, 0)

    @pl.when(i + 1 < pl.num_programs(0))
    def _():
        gather(next_ref, 1 - slot)

    hb = _unpack_halves(_load_tiled(h_ref, tile)).astype(BF16)
    acc = _dot((jax.nn.silu(_dot(hb, wsg_ref[...])) * _dot(hb, wsu_ref[...])).astype(BF16), wsd_ref[...])

    def wait(t, c):
        for k in range(TOP_K):
            row_copy(dest_ref, slot, 0, k).wait()
        return c

    lax.fori_loop(0, tile, wait, 0)
    gates = gate_ref[...]
    moe = gates[:, 0:1] * _unpack_halves(_load_tiled(buf_ref.at[slot, 0], tile))
    for k in range(1, TOP_K):
        moe = moe + gates[:, k:k + 1] * _unpack_halves(_load_tiled(buf_ref.at[slot, k], tile))
    out = x_ref[...] + mod_ref[0, :, 5 * d:6 * d] * _rms(moe + acc, g_ref[...])

    @pl.when(i < ctx_steps)
    def _():
        oc_ref[...] = out

    @pl.when(i >= ctx_steps)
    def _():
        ol_ref[...] = out


def _combine(dest, gates, h2p, x1, mod, g_post, wsg, wsu, wsd, ys, row_map, t_ctx):
    t, d = x1.shape
    nc = d // LANES
    nw = nc // 2
    tile = COMBINE_TILE
    per_tok = TOK_TILE // tile
    n_steps = t // tile
    tok = lambda n: pl.BlockSpec((tile, n), lambda i: (i, 0))
    full = lambda a: pl.BlockSpec(a.shape, lambda i: (0,) * a.ndim)
    ctx_steps = t_ctx // tile
    return pl.pallas_call(
        functools.partial(_combine_body, ctx_steps=ctx_steps),
        name="moe_combine",
        grid=(n_steps,),
        in_specs=[pl.BlockSpec((TOP_K, tile), lambda i: (0, i), memory_space=pltpu.SMEM),
                  pl.BlockSpec((TOP_K, tile), lambda i: (0, jnp.minimum(i + 1, n_steps - 1)),
                               memory_space=pltpu.SMEM),
                  tok(TOP_K), pl.BlockSpec((tile * nw, LANES), lambda i: (i, 0)), tok(d),
                  pl.BlockSpec((1, 1, mod.shape[2]), lambda i: (row_map(i // per_tok), 0, 0)),
                  full(g_post), full(wsg), full(wsu), full(wsd),
                  pl.BlockSpec(memory_space=pl.ANY)],
        out_specs=[pl.BlockSpec((tile, d), lambda i: (jnp.minimum(i, ctx_steps - 1), 0)),
                   pl.BlockSpec((tile, d), lambda i: (jnp.maximum(i - ctx_steps, 0), 0))],
        out_shape=[jax.ShapeDtypeStruct((t_ctx, d), F32), jax.ShapeDtypeStruct((t - t_ctx, d), F32)],
        scratch_shapes=[pltpu.VMEM((2, TOP_K, tile * nw, LANES), U32), pltpu.SemaphoreType.DMA((2,))],
        compiler_params=_cparams("arbitrary"),
    )(dest, dest, gates, h2p, x1, mod, g_post, wsg, wsu, wsd, ys)


def _moe(logits, h2p, x1, mod, p, row_map, t_ctx):
    t, d = x1.shape
    idx, gate, pos, cnt = _route(logits, p['b_router'])
    counts = cnt[:, 0].astype(I32)
    padded = (counts + MOE_ROWS - 1) // MOE_ROWS * MOE_ROWS
    ends = jnp.cumsum(padded)
    starts = ends - padded
    n_blocks = (t * TOP_K + N_EXPERTS * (MOE_ROWS - 1)) // MOE_ROWS + 1
    n_rows = n_blocks * MOE_ROWS
    blk_start = jnp.arange(n_blocks, dtype=I32) * MOE_ROWS
    blk_exp = jnp.minimum(jnp.sum((ends[None, :] <= blk_start[:, None]).astype(I32), axis=1), N_EXPERTS - 1)
    pad_hi = ends.at[N_EXPERTS - 1].set(n_rows)
    dest = _dest(starts, idx, pos)
    xs = _dispatch(starts + counts, pad_hi, dest, h2p, n_rows)
    ys = _experts(blk_exp, xs, p['w_e_gate'], p['w_e_up'], p['w_e_down'], p['layer'])
    return _combine(dest, gate.T, h2p, x1, mod, p['g_post_ffn'], p['w_s_gate'], p['w_s_up'], p['w_s_down'],
                    ys, row_map, t_ctx)


def kernel(x_prompt, x_sample, cache_k, cache_v, state_ssm_re, state_ssm_im, c, c_ctx,
           g_pre_mix, g_post_mix, g_pre_ffn, g_post_ffn, w_ada, b_ada, w_in, w_out,
           sgu_g, w_sp, b_sp, rpb, ssm_a_re, ssm_a_im, ssm_log_dt, ssm_b_re, ssm_b_im,
           ssm_c_re, ssm_c_im, ssm_d, w_glu, b_glu, w_router, b_router,
           w_e_gate, w_e_up, w_e_down, w_s_gate, w_s_up, w_s_down):
    n_pb, p_seq, d = x_prompt.shape
    n_sb, s_seq, _ = x_sample.shape
    depth = w_in.shape[0]
    d_a = sgu_g.shape[1]
    d_c = w_glu.shape[1]
    d_b = d - d_a - d_c
    n_heads = d_b // HD_B
    n_g = d_c // SSM_CH
    t_p = n_pb * p_seq
    assert p_seq % CHUNK == 0 and t_p % TOK_TILE == 0 and s_seq % TOK_TILE == 0 and t_p % s_seq == 0
    assert s_seq % (NAT_ROWS * GRID_W) == 0 and s_seq // GRID_W >= NA_WIN_R and d_c == 2 * LANES

    mod_rows = -(-(n_sb + 1) // 8) * 8
    c_all = jnp.zeros((mod_rows, d), F32).at[:n_sb].set(c).at[n_sb].set(c_ctx)
    mod_all = _ada(c_all, w_ada, b_ada)
    p_tiles = t_p // TOK_TILE
    s_tiles = s_seq // TOK_TILE

    def row_map(i):
        return jnp.where(i < p_tiles, n_sb, (i - p_tiles) // s_tiles)

    xc, xl = x_prompt.reshape(t_p, d), x_sample.reshape(n_sb * s_seq, d)
    s5w_all = jax.vmap(_s5_weights)(ssm_a_re, ssm_a_im, ssm_log_dt, ssm_b_re, ssm_b_im, ssm_c_re, ssm_c_im, ssm_d)
    nat_bias_all = jax.vmap(_nat_bias)(rpb)
    new_k, new_v, new_re, new_im = [], [], [], []
    for l in range(depth):
        mod = mod_all[l][:, None, :]
        row = lambda a: a[l][None, :].astype(F32)
        pa, q, k32, v32, kb, vb, *pc = _premix(xc, xl, mod, row(g_pre_mix), w_in[l].astype(BF16), row_map,
                                               d_a, d_b, d_c)
        bias_a = jnp.repeat(b_sp[l].T.astype(F32), d_a // NH_A, axis=1)
        ya = _chunk_mlp(pa, row(sgu_g), w_sp[l].reshape(NH_A * CHUNK, CHUNK).astype(BF16), bias_a)
        yb_p = _ctx_attn(q, k32, v32, n_pb, p_seq)
        ck = cache_k[:, l].reshape(n_sb, -1, d_b).astype(BF16)
        cv = cache_v[:, l].reshape(n_sb, -1, d_b).astype(BF16)
        yb_s = _nat_attn(q, kb, vb, ck, cv, nat_bias_all[l], n_sb, s_seq, t_p)
        yb = jnp.concatenate([yb_p, yb_s], axis=0)
        s5w = [w[l] for w in s5w_all]
        c_p = t_p // S5_CHUNK
        yc_p, fin_p = _s5([h[:c_p] for h in pc], n_pb, p_seq, s5w, jnp.zeros((4, n_pb, n_g * SSM_P), F32))
        sre = state_ssm_re[:, l].astype(F32).reshape(n_sb, 2, n_g * SSM_P)
        sim = state_ssm_im[:, l].astype(F32).reshape(n_sb, 2, n_g * SSM_P)
        h0_s = jnp.stack([sre[:, 0], sim[:, 0], sre[:, 1], sim[:, 1]])
        yc_s, _ = _s5([h[c_p:] for h in pc], n_sb, s_seq, s5w, h0_s)
        yc = [jnp.concatenate([a, b], axis=0) for a, b in zip(yc_p, yc_s)]
        wr = w_router[l].astype(F32).T
        wr_hi = wr.astype(BF16)
        wr_lo = (wr - wr_hi.astype(F32)).astype(BF16)
        x1, logits, h2p = _post(xc, xl, ya, yb, yc[0], yc[1],
                                mod, row(g_post_mix), row(g_pre_ffn), w_glu[l].astype(BF16), row(b_glu),
                                w_out[l].astype(BF16), wr_hi, wr_lo, row_map)
        moe_p = {
            'b_router': b_router[l].astype(F32)[:, None],
            'w_e_gate': w_e_gate, 'w_e_up': w_e_up, 'w_e_down': w_e_down, 'layer': l,
            'w_s_gate': w_s_gate[l].astype(BF16), 'w_s_up': w_s_up[l].astype(BF16),
            'w_s_down': w_s_down[l].astype(BF16), 'g_post_ffn': row(g_post_ffn),
        }
        xc, xl = _moe(logits, h2p, x1, mod, moe_p, row_map, t_p)
        new_k.append(k32[:t_p].reshape(n_pb, p_seq, n_heads, HD_B))
        new_v.append(v32[:t_p].reshape(n_pb, p_seq, n_heads, HD_B))
        fin_p = fin_p.reshape(4, n_pb, n_g, SSM_P)
        new_re.append(jnp.stack([fin_p[0], fin_p[2]], axis=1))
        new_im.append(jnp.stack([fin_p[1], fin_p[3]], axis=1))
    return (xc.reshape(n_pb, p_seq, d), xl.reshape(n_sb, s_seq, d),
            jnp.stack(new_k, axis=1), jnp.stack(new_v, axis=1),
            jnp.stack(new_re, axis=1), jnp.stack(new_im, axis=1))
```

```python
import functools
import math

import jax
import jax.numpy as jnp
from jax import lax
from jax.experimental import pallas as pl
from jax.experimental.pallas import tpu as pltpu

F32 = jnp.float32
BF16 = jnp.bfloat16
I32 = jnp.int32

GRID_W = 64
EPS = 1e-6
NH_A = 4
CHUNK = 128
HD_B = 64
NA_WIN_R = 8
NA_WIN_C = 16
ATTN_SCALE = HD_B ** -0.5
SSM_CH = 16
SSM_P = 64
N_EXPERTS = 64
TOP_K = 8
N_EXP_GROUPS = 8
TOPK_GROUPS = 4
ROUTE_SCALE = 2.5

LANES = 128
TOK_TILE = 512
S5_CHUNK = 16
S5_OUT_STEPS = 4
MOE_ROWS = 1024
COMBINE_TILE = 256
DISPATCH_TILE = 512
ZERO_RUN = 64
NAT_ROWS = 4
DMA_THREADS = 2
NEG_BIG = -1e30
VMEM_LIMIT = 48 * 1024 * 1024


def _cparams(*sem):
    return pltpu.CompilerParams(dimension_semantics=sem, vmem_limit_bytes=VMEM_LIMIT)


def _dot(a, b):
    return jnp.dot(a, b, preferred_element_type=F32)


def _dot_nt(a, b):
    return lax.dot_general(a, b, (((1,), (1,)), ((), ())), preferred_element_type=F32)


def _rms(x, g):
    return x * lax.rsqrt(jnp.mean(x * x, axis=-1, keepdims=True) + EPS) * g


def _load_tiled(ref, n_rows):
    nc = ref.shape[0] // n_rows
    return jnp.concatenate([ref[pl.ds(j, n_rows, stride=nc), :] for j in range(nc)], axis=1)


def _store_tiled(ref, val):
    n_rows = val.shape[0]
    nc = ref.shape[0] // n_rows
    for j in range(nc):
        ref[pl.ds(j, n_rows, stride=nc), :] = val[:, j * LANES:(j + 1) * LANES]


U32 = jnp.uint32


def _pack_halves(x):
    half = x.shape[1] // 2
    lo = lax.bitcast_convert_type(x[:, :half].astype(BF16).astype(F32), U32)
    hi = lax.bitcast_convert_type(x[:, half:].astype(BF16).astype(F32), U32)
    return (lo >> 16) | hi


def _unpack_halves(w):
    lo = lax.bitcast_convert_type(w << 16, F32)
    hi = lax.bitcast_convert_type(w & jnp.uint32(0xFFFF0000), F32)
    return jnp.concatenate([lo, hi], axis=1)


def _ada_body(c_ref, w_ref, b_ref, o_ref):
    s = jax.nn.silu(c_ref[...]).astype(BF16)
    o_ref[0] = _dot(s, w_ref[0].astype(BF16)) + b_ref[0]


def _ada(c_all, w_ada, b_ada):
    n_layers, d, n = w_ada.shape
    rows = c_all.shape[0]
    tn = 1536
    return pl.pallas_call(
        _ada_body,
        name="ada",
        grid=(n_layers, n // tn),
        in_specs=[pl.BlockSpec((rows, d), lambda l, j: (0, 0)),
                  pl.BlockSpec((1, d, tn), lambda l, j: (l, 0, j)),
                  pl.BlockSpec((1, 1, tn), lambda l, j: (l, 0, j))],
        out_specs=pl.BlockSpec((1, rows, tn), lambda l, j: (l, 0, j)),
        out_shape=jax.ShapeDtypeStruct((n_layers, rows, n), F32),
        compiler_params=_cparams("arbitrary", "arbitrary"),
    )(c_all, w_ada, b_ada.reshape(n_layers, 1, n))


def _stream_tile(xc_ref, xl_ref, ctx_tiles):
    return jnp.where(pl.program_id(0) < ctx_tiles, xc_ref[...], xl_ref[...])


def _stream_specs(tm, d, ctx_tiles):
    return [pl.BlockSpec((tm, d), lambda i: (jnp.minimum(i, ctx_tiles - 1), 0)),
            pl.BlockSpec((tm, d), lambda i: (jnp.maximum(i - ctx_tiles, 0), 0))]


def _premix_body(xc_ref, xl_ref, mod_ref, g_ref, w_ref, pa_ref, q_ref, k_ref, v_ref, kb_ref, vb_ref, pc0_ref,
                 pc1_ref, fold_ref, *, ctx_tiles):
    x = _stream_tile(xc_ref, xl_ref, ctx_tiles)
    d = x.shape[1]
    h = _rms(x, g_ref[...]) * (1 + mod_ref[0, :, d:2 * d]) + mod_ref[0, :, 0:d]
    p = _dot(h.astype(BF16), w_ref[...])
    d_a2 = pa_ref.shape[1]
    d_b = q_ref.shape[1]
    o = d_a2
    pa_ref[...] = p[:, 0:o]
    q_ref[...] = p[:, o:o + d_b].astype(BF16)
    k = p[:, o + d_b:o + 2 * d_b]
    v = p[:, o + 2 * d_b:o + 3 * d_b]

    @pl.when(pl.program_id(0) < ctx_tiles)
    def _():
        k_ref[...] = k
        v_ref[...] = v

    kb_ref[...] = k.astype(BF16)
    vb_ref[...] = v.astype(BF16)
    for h, pc_ref in enumerate((pc0_ref, pc1_ref)):
        fold_ref[...] = p[:, o + 3 * d_b + h * LANES:o + 3 * d_b + (h + 1) * LANES]
        _fold_chunks(pc_ref, fold_ref)


def _fold_chunks(dst_ref, src_ref):
    n = dst_ref.shape[0]
    for i in range(S5_CHUNK):
        dst_ref[:, i * LANES:(i + 1) * LANES] = src_ref[pl.ds(i, n, stride=S5_CHUNK), :]


def _unfold_chunks(dst_ref, src_ref):
    n = src_ref.shape[0]
    for i in range(S5_CHUNK):
        dst_ref[pl.ds(i, n, stride=S5_CHUNK), :] = src_ref[:, i * LANES:(i + 1) * LANES]


def _premix(xc, xl, mod, g, w_in_b, row_map, d_a, d_b, d_c):
    t_ctx, d = xc.shape
    t = t_ctx + xl.shape[0]
    tm = TOK_TILE
    d_in = w_in_b.shape[1]
    ctx_tiles = t_ctx // tm
    tok = lambda n: pl.BlockSpec((tm, n), lambda i: (i, 0))
    ctx = pl.BlockSpec((tm, d_b), lambda i: (jnp.minimum(i, ctx_tiles - 1), 0))
    fold = pl.BlockSpec((tm // S5_CHUNK, S5_CHUNK * LANES), lambda i: (i, 0))
    return pl.pallas_call(
        functools.partial(_premix_body, ctx_tiles=ctx_tiles),
        name="premix",
        grid=(t // tm,),
        in_specs=_stream_specs(tm, d, ctx_tiles) + [
            pl.BlockSpec((1, 1, mod.shape[2]), lambda i: (row_map(i), 0, 0)),
            pl.BlockSpec((1, d), lambda i: (0, 0)),
            pl.BlockSpec((d, d_in), lambda i: (0, 0))],
        out_specs=[tok(2 * d_a), tok(d_b), ctx, ctx, tok(d_b), tok(d_b), fold, fold],
        out_shape=[jax.ShapeDtypeStruct((t, 2 * d_a), F32),
                   jax.ShapeDtypeStruct((t, d_b), BF16),
                   jax.ShapeDtypeStruct((t_ctx, d_b), F32),
                   jax.ShapeDtypeStruct((t_ctx, d_b), F32),
                   jax.ShapeDtypeStruct((t, d_b), BF16),
                   jax.ShapeDtypeStruct((t, d_b), BF16)]
        + [jax.ShapeDtypeStruct((t // S5_CHUNK, S5_CHUNK * LANES), F32)] * 2,
        scratch_shapes=[pltpu.VMEM((tm, LANES), F32)],
        compiler_params=_cparams("arbitrary"),
    )(xc, xl, mod, g, w_in_b)


def _chunk_body(pa_ref, g_ref, w_ref, b_ref, o_ref):
    z = jax.nn.gelu(pa_ref[...])
    d_a = o_ref.shape[1]
    hd = d_a // NH_A
    u = z[:, :d_a]
    v = z[:, d_a:]
    mu = jnp.mean(v, axis=-1, keepdims=True)
    var = jnp.mean(jnp.square(v - mu), axis=-1, keepdims=True)
    vb = ((v - mu) * lax.rsqrt(var + EPS) * g_ref[...]).astype(BF16)
    head = lax.broadcasted_iota(I32, (CHUNK, d_a), 1) // hd
    for ch in range(pa_ref.shape[0] // CHUNK):
        rows = slice(ch * CHUNK, (ch + 1) * CHUNK)
        sf = _dot(w_ref[...], vb[rows])
        s = b_ref[...]
        for h in range(NH_A):
            s = s + jnp.where(head == h, sf[h * CHUNK:(h + 1) * CHUNK], 0.0)
        o_ref[rows, :] = (u[rows] * s).astype(BF16)


def _chunk_mlp(pa, sgu_g, w_sp_b, bias):
    t, d2 = pa.shape
    d_a = d2 // 2
    tm = TOK_TILE
    return pl.pallas_call(
        _chunk_body,
        name="chunk_mlp",
        grid=(t // tm,),
        in_specs=[pl.BlockSpec((tm, d2), lambda i: (i, 0)),
                  pl.BlockSpec((1, d_a), lambda i: (0, 0)),
                  pl.BlockSpec(w_sp_b.shape, lambda i: (0, 0)),
                  pl.BlockSpec(bias.shape, lambda i: (0, 0))],
        out_specs=pl.BlockSpec((tm, d_a), lambda i: (i, 0)),
        out_shape=jax.ShapeDtypeStruct((t, d_a), BF16),
        compiler_params=_cparams("arbitrary"),
    )(pa, sgu_g, w_sp_b, bias)


def _stack_pair(qg):
    lane = lax.broadcasted_iota(I32, qg.shape, 1)
    zero = jnp.zeros_like(qg)
    return jnp.concatenate([jnp.where(lane < HD_B, qg, zero), jnp.where(lane >= HD_B, qg, zero)], axis=0)


def _unstack_pair(o2):
    n = o2.shape[0] // 2
    lane = lax.broadcasted_iota(I32, (n, o2.shape[1]), 1)
    return jnp.where(lane < HD_B, o2[:n], o2[n:])


def _ctx_attn_body(q_ref, k_ref, v_ref, o_ref):
    for g in range(q_ref.shape[1] // LANES):
        cols = slice(g * LANES, (g + 1) * LANES)
        q2 = _stack_pair(q_ref[:, cols])
        s = _dot_nt(q2, k_ref[:, cols].astype(BF16)) * ATTN_SCALE
        e = jnp.exp(s - jnp.max(s, axis=-1, keepdims=True))
        p = e / jnp.sum(e, axis=-1, keepdims=True)
        o2 = _dot(p.astype(BF16), v_ref[:, cols].astype(BF16))
        o_ref[:, cols] = _unstack_pair(o2).astype(BF16)


def _ctx_attn(q, k, v, n_batch, seq):
    d_b = q.shape[1]
    blk = pl.BlockSpec((seq, d_b), lambda b: (b, 0))
    return pl.pallas_call(
        _ctx_attn_body,
        name="ctx_attn",
        grid=(n_batch,),
        in_specs=[blk, blk, blk],
        out_specs=blk,
        out_shape=jax.ShapeDtypeStruct((n_batch * seq, d_b), BF16),
        compiler_params=_cparams("arbitrary"),
    )(q, k, v)


def _nat_body(q_ref, k_ref, v_ref, ck_ref, cv_ref, bias_ref, o_ref, *, rows):
    r0 = pl.program_id(1) * NAT_ROWS
    n_win = NA_WIN_R * GRID_W
    pr = 2 * GRID_W
    starts, cases = [], []
    for i in range(NAT_ROWS):
        rs = jnp.clip(r0 + i - NA_WIN_R // 2, 0, rows - NA_WIN_R)
        starts.append(pl.multiple_of(rs * GRID_W, GRID_W))
        cases.append(r0 + i - rs)
    for g in range(q_ref.shape[1] // LANES):
        cols = slice(g * LANES, (g + 1) * LANES)
        q2 = jnp.concatenate([_stack_pair(q_ref[i * GRID_W:(i + 1) * GRID_W, cols]) for i in range(NAT_ROWS)],
                             axis=0)
        s_ctx = _dot_nt(q2, ck_ref[0, :, cols]) * ATTN_SCALE
        m_ctx = jnp.max(s_ctx, axis=-1, keepdims=True)
        e_wins, invs, ms = [], [], []
        for i in range(NAT_ROWS):
            s_win = (_dot_nt(q2[i * pr:(i + 1) * pr], k_ref[pl.ds(starts[i], n_win), cols]) * ATTN_SCALE
                     + bias_ref[cases[i], g])
            m = jnp.maximum(jnp.max(s_win, axis=-1, keepdims=True), m_ctx[i * pr:(i + 1) * pr])
            e_wins.append(jnp.exp(s_win - m))
            ms.append(m)
        e_ctx = jnp.exp(s_ctx - jnp.concatenate(ms, axis=0))
        l_ctx = jnp.sum(e_ctx, axis=-1, keepdims=True)
        for i in range(NAT_ROWS):
            invs.append(1.0 / (jnp.sum(e_wins[i], axis=-1, keepdims=True) + l_ctx[i * pr:(i + 1) * pr]))
        o_ctx = _dot((e_ctx * jnp.concatenate(invs, axis=0)).astype(BF16), cv_ref[0, :, cols])
        for i in range(NAT_ROWS):
            o2 = _dot((e_wins[i] * invs[i]).astype(BF16), v_ref[pl.ds(starts[i], n_win), cols])
            o_ref[i * GRID_W:(i + 1) * GRID_W, cols] = _unstack_pair(o2 + o_ctx[i * pr:(i + 1) * pr]).astype(BF16)


def _nat_attn(q, kb, vb, ck, cv, bias, n_batch, seq, tok0):
    d_b = q.shape[1]
    rows = seq // GRID_W
    lc = ck.shape[1]
    blk = NAT_ROWS * GRID_W
    steps = seq // blk
    q0 = tok0 // blk
    i0 = tok0 // seq
    img = pl.BlockSpec((seq, d_b), lambda b, r: (i0 + b, 0))
    ctx = pl.BlockSpec((1, lc, d_b), lambda b, r: (b, 0, 0))
    return pl.pallas_call(
        functools.partial(_nat_body, rows=rows),
        name="nat_attn",
        grid=(n_batch, steps),
        in_specs=[pl.BlockSpec((blk, d_b), lambda b, r: (q0 + b * steps + r, 0)),
                  img, img, ctx, ctx,
                  pl.BlockSpec(bias.shape, lambda b, r: (0, 0, 0, 0))],
        out_specs=pl.BlockSpec((blk, d_b), lambda b, r: (b * steps + r, 0)),
        out_shape=jax.ShapeDtypeStruct((n_batch * seq, d_b), BF16),
        compiler_params=_cparams("arbitrary", "arbitrary"),
    )(q, kb, vb, ck, cv, bias)


def _nat_bias(rpb):
    n_heads = rpb.shape[0]
    cols = jnp.arange(GRID_W)
    col_start = jnp.clip(cols - NA_WIN_C // 2, 0, GRID_W - NA_WIN_C)
    j = jnp.arange(GRID_W)
    valid = (j[None, :] >= col_start[:, None]) & (j[None, :] < col_start[:, None] + NA_WIN_C)
    col_off = jnp.clip(j[None, :] - cols[:, None] + (NA_WIN_C - 1), 0, 2 * NA_WIN_C - 2)
    toe = jnp.where(valid[None, None], rpb.astype(F32)[:, :, col_off], NEG_BIG)
    cases = jnp.stack([toe[:, NA_WIN_R - 1 - delta:2 * NA_WIN_R - 1 - delta] for delta in range(NA_WIN_R)])
    return cases.transpose(0, 1, 3, 2, 4).reshape(NA_WIN_R, n_heads // 2, 2 * GRID_W, NA_WIN_R * GRID_W)


def _s5_weights(a_re, a_im, log_dt, b_re, b_im, c_re, c_im, ssm_d):
    n_g = a_re.shape[1]
    c = S5_CHUNK
    lam = lax.complex(a_re.astype(F32), a_im.astype(F32))
    ldt = lam * jnp.exp(log_dt.astype(F32))[..., None]
    lam_bar = jnp.exp(ldt)
    b_bar = ((lam_bar - 1) / lam)[..., None] * lax.complex(b_re.astype(F32), b_im.astype(F32))
    c_mat = lax.complex(c_re.astype(F32), c_im.astype(F32))
    pw = jnp.exp(ldt[None] * jnp.arange(c + 1, dtype=F32)[:, None, None, None])
    kern = jnp.real(jnp.einsum('dgcp,kdgp,dgpe->dgkce', c_mat, pw[:c], b_bar))
    i = jnp.arange(c)
    lag = i[None, :] - i[:, None]
    tf = jnp.where((lag >= 0)[None, :, :, None, None], kern[0][:, jnp.clip(lag, 0, c - 1)], 0.0)
    tb = jnp.where((lag <= 0)[None, :, :, None, None], kern[1][:, jnp.clip(-lag, 0, c - 1)], 0.0)
    t_mat = (tf + tb).transpose(0, 1, 4, 2, 3).reshape(n_g, c * SSM_CH, c * SSM_CH)
    mf = pw[:c][::-1, 0][:, :, :, None] * b_bar[0][None]
    mb = pw[:c, 1][:, :, :, None] * b_bar[1][None]
    mf = mf.transpose(1, 0, 3, 2).reshape(n_g, c * SSM_CH, SSM_P)
    mb = mb.transpose(1, 0, 3, 2).reshape(n_g, c * SSM_CH, SSM_P)
    mq = jnp.stack([jnp.real(mf), jnp.imag(mf), jnp.real(mb), jnp.imag(mb)], axis=2)
    zf = c_mat[0][:, None] * pw[1:c + 1, 0][:, :, None, :].transpose(1, 0, 2, 3)
    zb = c_mat[1][:, None] * pw[1:c + 1, 1][::-1][:, :, None, :].transpose(1, 0, 2, 3)
    zf = zf.transpose(0, 3, 1, 2).reshape(n_g, SSM_P, c * SSM_CH)
    zb = zb.transpose(0, 3, 1, 2).reshape(n_g, SSM_P, c * SSM_CH)
    wq = jnp.stack([jnp.real(zf), -jnp.imag(zf), jnp.real(zb), -jnp.imag(zb)], axis=1)
    gh = LANES // SSM_CH
    n_half = n_g // gh
    kd = c * SSM_CH

    def stack_rows(blocks, r1):
        b = blocks.astype(BF16).reshape(n_half, gh, r1, kd // r1, kd)
        return b.transpose(0, 2, 1, 3, 4).reshape(n_half, gh * kd, kd)

    s_m = stack_rows(mq.reshape(n_g, kd, 4 * SSM_P), c)
    s_t = stack_rows(t_mat, c)
    s_w = stack_rows(wq.reshape(n_g, 4 * SSM_P, kd), 4)
    a_c = pw[c]
    a16 = jnp.stack([jnp.real(a_c[0]), jnp.imag(a_c[0]), jnp.real(a_c[1]), jnp.imag(a_c[1])])
    a16 = a16.reshape(4, 1, n_half, gh * SSM_P).transpose(2, 0, 1, 3)
    dvec = ssm_d.astype(F32).reshape(n_half, 1, LANES)
    return s_m, s_t, s_w, a16, dvec


def _expand_body(s_ref, rep_ref, o_ref, *, r2w, c2w):
    tr, n = o_ref.shape[1], o_ref.shape[2]
    gh = n // s_ref.shape[2]
    full = _dot(s_ref[0], rep_ref[...])
    row = pl.program_id(1) * tr + lax.broadcasted_iota(I32, (tr, n), 0)
    col = lax.broadcasted_iota(I32, (tr, n), 1)
    o_ref[0] = jnp.where((row // r2w) % gh == (col // c2w) % gh, full, 0.0).astype(BF16)


def _expand_block_diag(s, r2w, c2w):
    m, n, kd = s.shape
    gh = n // kd
    rep = jnp.broadcast_to(jnp.eye(kd, dtype=BF16).reshape(kd, kd // c2w, 1, c2w), (kd, kd // c2w, gh, c2w))
    rep = rep.reshape(kd, n)
    tr = 256
    return pl.pallas_call(
        functools.partial(_expand_body, r2w=r2w, c2w=c2w),
        name="s5_expand",
        grid=(m, n // tr),
        in_specs=[pl.BlockSpec((1, tr, kd), lambda a, i: (a, i, 0)),
                  pl.BlockSpec((kd, n), lambda a, i: (0, 0))],
        out_specs=pl.BlockSpec((1, tr, n), lambda a, i: (a, i, 0)),
        out_shape=jax.ShapeDtypeStruct((m, n, n), BF16),
        compiler_params=_cparams("arbitrary", "arbitrary"),
    )(s, rep)


def _s5_state_body(u_ref, m_ref, fre_ref, fim_ref, bre_ref, bim_ref):
    nb, tn, kd = u_ref.shape
    w = fre_ref.shape[2]
    r = _dot(u_ref[...].reshape(nb * tn, kd).astype(BF16), m_ref[0, 0])
    for q, o_ref in enumerate((fre_ref, fim_ref, bre_ref, bim_ref)):
        for b in range(nb):
            o_ref[:, b, :] = r[b * tn:(b + 1) * tn, q * w:(q + 1) * w]


def _s5_rows(nb, n):
    return max(8, min(n, 256 // nb))


def _s5_states(u3, m_all, layer, half):
    nb, n, kd = u3.shape
    tn = _s5_rows(nb, n)
    w = m_all.shape[3] // 4
    out = pl.BlockSpec((tn, nb, w), lambda i: (i, 0, 0))
    return pl.pallas_call(
        _s5_state_body,
        name="s5_states",
        grid=(n // tn,),
        in_specs=[pl.BlockSpec((nb, tn, kd), lambda i: (0, i, 0)),
                  pl.BlockSpec((1, 1) + m_all.shape[2:], lambda i: (layer, half, 0, 0))],
        out_specs=[out] * 4,
        out_shape=[jax.ShapeDtypeStruct((n, nb, w), F32)] * 4,
        compiler_params=_cparams("arbitrary"),
    )(u3, m_all)


def _s5_scan_body(sfr_ref, sfi_ref, sbr_ref, sbi_ref, a_ref, h0_ref,
                  hfr_ref, hfi_ref, hbr_ref, hbi_ref, fin_ref, st_ref):
    @pl.when(pl.program_id(0) == 0)
    def _():
        st_ref[...] = h0_ref[...]

    ks = sfr_ref.shape[0]
    afr, afi, abr, abi = a_ref[0], a_ref[1], a_ref[2], a_ref[3]

    def step(s, carry):
        fr, fi, br, bi = carry
        sb = ks - 1 - s
        hfr_ref[s] = fr
        hfi_ref[s] = fi
        hbr_ref[sb] = br
        hbi_ref[sb] = bi
        nfr = afr * fr - afi * fi + sfr_ref[s]
        nfi = afr * fi + afi * fr + sfi_ref[s]
        nbr = abr * br - abi * bi + sbr_ref[sb]
        nbi = abr * bi + abi * br + sbi_ref[sb]
        return nfr, nfi, nbr, nbi

    carry = lax.fori_loop(0, ks, step, (st_ref[0], st_ref[1], st_ref[2], st_ref[3]))
    for q in range(4):
        st_ref[q] = carry[q]
        fin_ref[q] = carry[q]


def _s5_scan(s3, a16, h0):
    n_chunks, n_batch, width = s3[0].shape
    ks = min(n_chunks, 32)
    nb = n_chunks // ks
    fwd = pl.BlockSpec((ks, n_batch, width), lambda i: (i, 0, 0))
    bwd = pl.BlockSpec((ks, n_batch, width), lambda i: (nb - 1 - i, 0, 0))
    small = lambda shape: pl.BlockSpec(shape, lambda i: (0, 0, 0))
    outs = pl.pallas_call(
        _s5_scan_body,
        name="s5_scan",
        grid=(nb,),
        in_specs=[fwd, fwd, bwd, bwd, small(a16.shape), small(h0.shape)],
        out_specs=[fwd, fwd, bwd, bwd, small(h0.shape)],
        out_shape=[jax.ShapeDtypeStruct((n_chunks, n_batch, width), F32)] * 4
        + [jax.ShapeDtypeStruct(h0.shape, F32)],
        scratch_shapes=[pltpu.VMEM(h0.shape, F32)],
        compiler_params=_cparams("arbitrary"),
    )(*s3, a16, h0)
    return outs[:4], outs[4]


def _s5_out_body(u_ref, us_ref, hfr_ref, hfi_ref, hbr_ref, hbi_ref, t_ref, w_ref, d_ref, y_ref):
    nb, tn, kd = u_ref.shape
    rows = nb * tn
    wo = us_ref.shape[2]
    y = _dot(u_ref[...].reshape(rows, kd).astype(BF16), t_ref[0, 0])
    w = hfr_ref.shape[2]
    for q, h_ref in enumerate((hfr_ref, hfi_ref, hbr_ref, hbi_ref)):
        hq = jnp.concatenate([h_ref[:, b, :] for b in range(nb)], axis=0)
        y = y + _dot(hq.astype(BF16), w_ref[0, 0, q * w:(q + 1) * w, :])
    skip = jnp.concatenate([d_ref[...]] * (wo // LANES), axis=1) * us_ref[...].reshape(rows, wo)
    y_ref[...] = jax.nn.gelu(y + skip).reshape(nb, tn, wo)


def _s5_out(u3, h4, t_all, w_all, dvec, layer, half):
    nb, n, kd = u3.shape
    tn = _s5_rows(nb, n)
    wo = S5_OUT_STEPS * LANES
    w = h4[0].shape[2]
    hb = pl.BlockSpec((tn, nb, w), lambda i, j: (i, 0, 0))
    mat = lambda a: pl.BlockSpec((1, 1, a.shape[2], wo), lambda i, j: (layer, half, 0, j))
    return pl.pallas_call(
        _s5_out_body,
        name="s5_out",
        grid=(n // tn, kd // wo),
        in_specs=[pl.BlockSpec((nb, tn, kd), lambda i, j: (0, i, 0)),
                  pl.BlockSpec((nb, tn, wo), lambda i, j: (0, i, j)),
                  hb, hb, hb, hb, mat(t_all), mat(w_all),
                  pl.BlockSpec(dvec.shape, lambda i, j: (0, 0))],
        out_specs=pl.BlockSpec((nb, tn, wo), lambda i, j: (0, i, j)),
        out_shape=jax.ShapeDtypeStruct(u3.shape, F32),
        compiler_params=_cparams("arbitrary", "arbitrary"),
    )(u3, u3, *h4, t_all, w_all, dvec)


def _s5(pc_halves, n_batch, seq, weights, layer, h0):
    m_all, t_all, w_all, a16, dvec = weights
    n_chunks = seq // S5_CHUNK
    ys, fins = [], []
    for h, pc in enumerate(pc_halves):
        w = m_all.shape[3] // 4
        u3 = pc.reshape(n_batch, n_chunks, S5_CHUNK * LANES)
        s4 = _s5_states(u3, m_all, layer, h)
        h4, fin = _s5_scan(s4, a16[layer, h], h0[:, :, h * w:(h + 1) * w])
        y = _s5_out(u3, h4, t_all, w_all, dvec[layer, h], layer, h)
        ys.append(y.reshape(n_batch * n_chunks, S5_CHUNK * LANES))
        fins.append(fin)
    return ys, jnp.concatenate(fins, axis=2)


def _post_body(xc_ref, xl_ref, ya_ref, yb_ref, yc0_ref, yc1_ref, mod_ref, gpost_ref, gffn_ref, wglu_ref, bglu_ref,
               wo_ref, whi_ref, wlo_ref, x1_ref, logit_ref, h2p_ref, unfold_ref, *, ctx_tiles):
    x = _stream_tile(xc_ref, xl_ref, ctx_tiles)
    d = x.shape[1]
    d_a = ya_ref.shape[1]
    d_b = yb_ref.shape[1]
    for h, yc_ref in enumerate((yc0_ref, yc1_ref)):
        _unfold_chunks(unfold_ref.at[h], yc_ref)
    y = jnp.concatenate([unfold_ref[0], unfold_ref[1]], axis=1)
    glu = y * jax.nn.sigmoid(_dot(y.astype(BF16), wglu_ref[...]) + bglu_ref[...])
    mixed = (_dot(ya_ref[...], wo_ref[0:d_a, :]) + _dot(yb_ref[...], wo_ref[d_a:d_a + d_b, :])
             + _dot(glu.astype(BF16), wo_ref[d_a + d_b:, :]))
    x1 = x + mod_ref[0, :, 2 * d:3 * d] * _rms(mixed, gpost_ref[...])
    x1_ref[...] = x1
    h2 = _rms(x1, gffn_ref[...]) * (1 + mod_ref[0, :, 4 * d:5 * d]) + mod_ref[0, :, 3 * d:4 * d]
    logit_ref[...] = _router_logits(h2, whi_ref, wlo_ref)
    _store_tiled(h2p_ref, _pack_halves(h2))


def _post(xc, xl, ya, yb, yc0, yc1, mod, g_post, g_ffn, w_glu_b, b_glu, w_out_b, w_hi, w_lo, row_map):
    t_ctx, d = xc.shape
    t = t_ctx + xl.shape[0]
    tm = TOK_TILE
    nc = d // LANES
    ctx_tiles = t_ctx // tm
    tok = lambda n: pl.BlockSpec((tm, n), lambda i: (i, 0))
    fold = pl.BlockSpec((tm // S5_CHUNK, S5_CHUNK * LANES), lambda i: (i, 0))
    full = lambda a: pl.BlockSpec(a.shape, lambda i: (0,) * a.ndim)
    return pl.pallas_call(
        functools.partial(_post_body, ctx_tiles=ctx_tiles),
        name="post_mix",
        grid=(t // tm,),
        in_specs=_stream_specs(tm, d, ctx_tiles) + [
            tok(ya.shape[1]), tok(yb.shape[1]), fold, fold,
            pl.BlockSpec((1, 1, mod.shape[2]), lambda i: (row_map(i), 0, 0)),
            full(g_post), full(g_ffn), full(w_glu_b), full(b_glu), full(w_out_b), full(w_hi), full(w_lo)],
        out_specs=[tok(d), pl.BlockSpec((N_EXPERTS, tm), lambda i: (0, i)),
                   pl.BlockSpec((tm * nc // 2, LANES), lambda i: (i, 0))],
        out_shape=[jax.ShapeDtypeStruct((t, d), F32), jax.ShapeDtypeStruct((N_EXPERTS, t), F32),
                   jax.ShapeDtypeStruct((t * nc // 2, LANES), U32)],
        scratch_shapes=[pltpu.VMEM((2, tm, LANES), F32)],
        compiler_params=_cparams("arbitrary"),
    )(xc, xl, ya, yb, yc0, yc1, mod, g_post, g_ffn, w_glu_b, b_glu, w_out_b, w_hi, w_lo)


def _router_logits(h, whi_ref, wlo_ref):
    hi = h.astype(BF16)
    lo = (h - hi.astype(F32)).astype(BF16)
    return _dot_nt(whi_ref[...], hi) + (_dot_nt(whi_ref[...], lo) + _dot_nt(wlo_ref[...], hi))


def _route_body(logit_ref, b_ref, idx_ref, gate_ref, pos_ref, cnt_ref, carry_ref):
    @pl.when(pl.program_id(0) == 0)
    def _():
        carry_ref[...] = jnp.zeros_like(carry_ref)

    tm = idx_ref.shape[1]
    scores = jax.nn.sigmoid(logit_ref[...])
    sel = scores + b_ref[...]
    gsz = N_EXPERTS // N_EXP_GROUPS
    within = lax.broadcasted_iota(I32, (gsz, tm), 0).astype(F32)
    grp = []
    for g in range(N_EXP_GROUPS):
        blk = sel[g * gsz:(g + 1) * gsz]
        m1 = jnp.max(blk, axis=0, keepdims=True)
        first = jnp.min(jnp.where(blk == m1, within, float(gsz)), axis=0, keepdims=True)
        m2 = jnp.max(jnp.where(within == first, -jnp.inf, blk), axis=0, keepdims=True)
        grp.append(m1 + m2)
    blocks = []
    for g in range(N_EXP_GROUPS):
        ahead = jnp.zeros((1, tm), F32)
        for o in range(N_EXP_GROUPS):
            if o == g:
                continue
            beats = (grp[o] >= grp[g]) if o < g else (grp[o] > grp[g])
            ahead = ahead + jnp.where(beats, 1.0, 0.0)
        ahead = jnp.broadcast_to(ahead, (gsz, tm))
        blocks.append(jnp.where(ahead < TOPK_GROUPS, sel[g * gsz:(g + 1) * gsz], -jnp.inf))
    v = jnp.concatenate(blocks, axis=0)
    eidx = lax.broadcasted_iota(I32, (N_EXPERTS, tm), 0)
    rnk = jnp.zeros((N_EXPERTS, tm), F32)
    for e in range(N_EXPERTS):
        row = v[e:e + 1]
        rnk = rnk + jnp.where(eidx > e, jnp.where(row >= v, 1.0, 0.0), jnp.where(row > v, 1.0, 0.0))
    chosen = rnk < TOP_K
    w = jnp.where(chosen, scores, 0.0)
    wn = w / jnp.sum(w, axis=0, keepdims=True) * ROUTE_SCALE
    tri = (lax.broadcasted_iota(I32, (tm, tm), 0) < lax.broadcasted_iota(I32, (tm, tm), 1))
    chosen_f = jnp.where(chosen, 1.0, 0.0)
    prefix = _dot(chosen_f.astype(BF16), jnp.where(tri, 1.0, 0.0).astype(BF16)) + carry_ref[:, 0:1]
    total = carry_ref[...] + jnp.sum(chosen_f, axis=1, keepdims=True)
    carry_ref[...] = total
    cnt_ref[...] = total
    eidx_f = eidx.astype(F32)
    for k in range(TOP_K):
        one = rnk == k
        idx_ref[k:k + 1, :] = jnp.sum(jnp.where(one, eidx_f, 0.0), axis=0, keepdims=True).astype(I32)
        gate_ref[k:k + 1, :] = jnp.sum(jnp.where(one, wn, 0.0), axis=0, keepdims=True)
        pos_ref[k:k + 1, :] = jnp.sum(jnp.where(one, prefix, 0.0), axis=0, keepdims=True).astype(I32)


def _route(logits, b_router):
    t = logits.shape[1]
    tm = TOK_TILE
    out = pl.BlockSpec((TOP_K, tm), lambda i: (0, i))
    full = lambda a: pl.BlockSpec(a.shape, lambda i: (0, 0))
    return pl.pallas_call(
        _route_body,
        name="route",
        grid=(t // tm,),
        in_specs=[pl.BlockSpec((N_EXPERTS, tm), lambda i: (0, i)), full(b_router)],
        out_specs=[out, out, out, pl.BlockSpec((N_EXPERTS, LANES), lambda i: (0, 0))],
        out_shape=[jax.ShapeDtypeStruct((TOP_K, t), I32), jax.ShapeDtypeStruct((TOP_K, t), F32),
                   jax.ShapeDtypeStruct((TOP_K, t), I32), jax.ShapeDtypeStruct((N_EXPERTS, LANES), F32)],
        scratch_shapes=[pltpu.VMEM((N_EXPERTS, LANES), F32)],
        compiler_params=_cparams("arbitrary"),
    )(logits, b_router)


def _dest_body(starts_ref, idx_ref, pos_ref, dest_ref):
    idx = idx_ref[...]
    acc = pos_ref[...]
    for e in range(N_EXPERTS):
        acc = acc + jnp.where(idx == e, starts_ref[e], 0)
    dest_ref[...] = acc


def _dest(starts, idx, pos):
    t = idx.shape[1]
    tile = math.gcd(t, 4096)
    blk = pl.BlockSpec((TOP_K, tile), lambda i, st: (0, i))
    return pl.pallas_call(
        _dest_body,
        name="moe_dest",
        grid_spec=pltpu.PrefetchScalarGridSpec(num_scalar_prefetch=1, grid=(t // tile,),
                                               in_specs=[blk, blk], out_specs=blk),
        out_shape=jax.ShapeDtypeStruct(idx.shape, I32),
        compiler_params=_cparams("arbitrary"),
    )(starts, idx, pos)


def _dispatch_body(pad_lo_ref, pad_hi_ref, dest_ref, h_ref, xs_ref, zero_ref, sem, zsem, *, n_pad_rows):
    tile = dest_ref.shape[1]
    nw = h_ref.shape[0] // tile

    def token(ref, r):
        return ref.at[pl.ds(pl.multiple_of(r * nw, nw), nw)]

    def zero_copy(r, n):
        return pltpu.make_async_copy(zero_ref.at[pl.ds(0, n * nw)],
                                     xs_ref.at[pl.ds(pl.multiple_of(r * nw, nw), n * nw)], zsem)

    @pl.when(pl.program_id(0) == 0)
    def _():
        zero_ref[...] = jnp.zeros_like(zero_ref)

        def per_expert(e, c):
            lo = pad_lo_ref[e]
            runs = (pad_hi_ref[e] - lo) // ZERO_RUN

            def run(j, c2):
                zero_copy(lo + j * ZERO_RUN, ZERO_RUN).start()
                return c2

            def one(r, c2):
                zero_copy(r, 1).start()
                return c2

            c = lax.fori_loop(0, runs, run, c)
            return lax.fori_loop(lo + runs * ZERO_RUN, pad_hi_ref[e], one, c)

        lax.fori_loop(0, N_EXPERTS, per_expert, 0)

    def row_copy(t, k):
        return pltpu.make_async_copy(token(h_ref, t), token(xs_ref, dest_ref[k, t]), sem)

    def start(t, c):
        for k in range(TOP_K):
            row_copy(t, k).start(priority=k % DMA_THREADS)
        return c

    def wait(t, c):
        for k in range(TOP_K):
            row_copy(t, k).wait()
        return c

    lax.fori_loop(0, tile, start, 0)
    lax.fori_loop(0, tile, wait, 0)

    @pl.when(pl.program_id(0) == 0)
    def _():
        def one(r, c):
            zero_copy(0, ZERO_RUN).wait()
            return c
        lax.fori_loop(0, n_pad_rows // ZERO_RUN, one, 0)


def _dispatch(pad_lo, pad_hi, dest, h2p, n_rows):
    t = dest.shape[1]
    nw = h2p.shape[0] // t
    tile = DISPATCH_TILE
    grid_spec = pltpu.PrefetchScalarGridSpec(
        num_scalar_prefetch=2,
        grid=(t // tile,),
        in_specs=[pl.BlockSpec((TOP_K, tile), lambda i, lo, hi: (0, i), memory_space=pltpu.SMEM),
                  pl.BlockSpec((tile * nw, LANES), lambda i, lo, hi: (i, 0))],
        out_specs=pl.BlockSpec(memory_space=pl.ANY),
        scratch_shapes=[pltpu.VMEM((ZERO_RUN * nw, LANES), U32), pltpu.SemaphoreType.DMA(()),
                        pltpu.SemaphoreType.DMA(())],
    )
    assert (n_rows - t * TOP_K) % ZERO_RUN == 0
    return pl.pallas_call(
        functools.partial(_dispatch_body, n_pad_rows=n_rows - t * TOP_K),
        name="moe_dispatch",
        grid_spec=grid_spec,
        out_shape=jax.ShapeDtypeStruct((n_rows * nw, LANES), U32),
        compiler_params=_cparams("arbitrary"),
    )(pad_lo, pad_hi, dest, h2p)


def _experts_body(be_ref, x_ref, wg_ref, wu_ref, wd_ref, y_ref, wgb_ref, wub_ref, wdb_ref):
    i = pl.program_id(0)

    @pl.when((i == 0) | (be_ref[i] != be_ref[jnp.maximum(i - 1, 0)]))
    def _():
        wgb_ref[...] = wg_ref[0, 0].astype(BF16)
        wub_ref[...] = wu_ref[0, 0].astype(BF16)
        wdb_ref[...] = wd_ref[0, 0].astype(BF16)

    xb = _unpack_halves(_load_tiled(x_ref, MOE_ROWS)).astype(BF16)
    hb = jax.nn.silu(_dot(xb, wgb_ref[...])) * _dot(xb, wub_ref[...])
    _store_tiled(y_ref, _pack_halves(_dot(hb.astype(BF16), wdb_ref[...])))


def _experts(blk_exp, xs, wg, wu, wd, layer):
    d, de = wg.shape[2], wg.shape[3]
    nc = d // (2 * LANES)
    bm = MOE_ROWS
    tiles = pl.BlockSpec((bm * nc, LANES), lambda i, be: (i, 0))
    grid_spec = pltpu.PrefetchScalarGridSpec(
        num_scalar_prefetch=1,
        grid=(xs.shape[0] // (bm * nc),),
        in_specs=[tiles,
                  pl.BlockSpec((1, 1, d, de), lambda i, be: (layer, be[i], 0, 0)),
                  pl.BlockSpec((1, 1, d, de), lambda i, be: (layer, be[i], 0, 0)),
                  pl.BlockSpec((1, 1, de, d), lambda i, be: (layer, be[i], 0, 0))],
        out_specs=tiles,
        scratch_shapes=[pltpu.VMEM((d, de), BF16), pltpu.VMEM((d, de), BF16), pltpu.VMEM((de, d), BF16)],
    )
    return pl.pallas_call(
        _experts_body,
        name="moe_experts",
        grid_spec=grid_spec,
        out_shape=jax.ShapeDtypeStruct(xs.shape, U32),
        compiler_params=_cparams("arbitrary"),
    )(blk_exp, xs, wg, wu, wd)


def _combine_body(dest_ref, next_ref, gate_ref, h_ref, x_ref, mod_ref, g_ref, wsg_ref, wsu_ref, wsd_ref, ys_hbm,
                  oc_ref, ol_ref, buf_ref, sem, *, ctx_steps):
    tile, d = x_ref.shape
    nw = buf_ref.shape[2] // tile
    i = pl.program_id(0)
    slot = i % 2

    def row_copy(d_ref, s, t, k):
        return pltpu.make_async_copy(ys_hbm.at[pl.ds(pl.multiple_of(d_ref[k, t] * nw, nw), nw)],
                                     buf_ref.at[s, k, pl.ds(pl.multiple_of(t * nw, nw), nw)], sem.at[s])

    def gather(d_ref, s):
        def start(t, c):
            for k in range(TOP_K):
                row_copy(d_ref, s, t, k).start(priority=k % DMA_THREADS)
            return c
        lax.fori_loop(0, tile, start, 0)

    @pl.when(i == 0)
    def _():
        gather(dest_ref, 0)

    @pl.when(i + 1 < pl.num_programs(0))
    def _():
        gather(next_ref, 1 - slot)

    hb = _unpack_halves(_load_tiled(h_ref, tile)).astype(BF16)
    acc = _dot((jax.nn.silu(_dot(hb, wsg_ref[...])) * _dot(hb, wsu_ref[...])).astype(BF16), wsd_ref[...])

    def wait(t, c):
        for k in range(TOP_K):
            row_copy(dest_ref, slot, 0, k).wait()
        return c

    lax.fori_loop(0, tile, wait, 0)
    gates = gate_ref[...]
    moe = gates[:, 0:1] * _unpack_halves(_load_tiled(buf_ref.at[slot, 0], tile))
    for k in range(1, TOP_K):
        moe = moe + gates[:, k:k + 1] * _unpack_halves(_load_tiled(buf_ref.at[slot, k], tile))
    out = x_ref[...] + mod_ref[0, :, 5 * d:6 * d] * _rms(moe + acc, g_ref[...])

    @pl.when(i < ctx_steps)
    def _():
        oc_ref[...] = out

    @pl.when(i >= ctx_steps)
    def _():
        ol_ref[...] = out


def _combine(dest, gates, h2p, x1, mod, g_post, wsg, wsu, wsd, ys, row_map, t_ctx):
    t, d = x1.shape
    nc = d // LANES
    nw = nc // 2
    tile = COMBINE_TILE
    per_tok = TOK_TILE // tile
    n_steps = t // tile
    tok = lambda n: pl.BlockSpec((tile, n), lambda i: (i, 0))
    full = lambda a: pl.BlockSpec(a.shape, lambda i: (0,) * a.ndim)
    ctx_steps = t_ctx // tile
    return pl.pallas_call(
        functools.partial(_combine_body, ctx_steps=ctx_steps),
        name="moe_combine",
        grid=(n_steps,),
        in_specs=[pl.BlockSpec((TOP_K, tile), lambda i: (0, i), memory_space=pltpu.SMEM),
                  pl.BlockSpec((TOP_K, tile), lambda i: (0, jnp.minimum(i + 1, n_steps - 1)),
                               memory_space=pltpu.SMEM),
                  tok(TOP_K), pl.BlockSpec((tile * nw, LANES), lambda i: (i, 0)), tok(d),
                  pl.BlockSpec((1, 1, mod.shape[2]), lambda i: (row_map(i // per_tok), 0, 0)),
                  full(g_post), full(wsg), full(wsu), full(wsd),
                  pl.BlockSpec(memory_space=pl.ANY)],
        out_specs=[pl.BlockSpec((tile, d), lambda i: (jnp.minimum(i, ctx_steps - 1), 0)),
                   pl.BlockSpec((tile, d), lambda i: (jnp.maximum(i - ctx_steps, 0), 0))],
        out_shape=[jax.ShapeDtypeStruct((t_ctx, d), F32), jax.ShapeDtypeStruct((t - t_ctx, d), F32)],
        scratch_shapes=[pltpu.VMEM((2, TOP_K, tile * nw, LANES), U32), pltpu.SemaphoreType.DMA((2,))],
        compiler_params=_cparams("arbitrary"),
    )(dest, dest, gates, h2p, x1, mod, g_post, wsg, wsu, wsd, ys)


def _moe(logits, h2p, x1, mod, p, row_map, t_ctx):
    t, d = x1.shape
    idx, gate, pos, cnt = _route(logits, p['b_router'])
    counts = cnt[:, 0].astype(I32)
    padded = (counts + MOE_ROWS - 1) // MOE_ROWS * MOE_ROWS
    ends = jnp.cumsum(padded)
    starts = ends - padded
    n_blocks = (t * TOP_K + N_EXPERTS * (MOE_ROWS - 1)) // MOE_ROWS + 1
    n_rows = n_blocks * MOE_ROWS
    blk_start = jnp.arange(n_blocks, dtype=I32) * MOE_ROWS
    blk_exp = jnp.minimum(jnp.sum((ends[None, :] <= blk_start[:, None]).astype(I32), axis=1), N_EXPERTS - 1)
    pad_hi = ends.at[N_EXPERTS - 1].set(n_rows)
    dest = _dest(starts, idx, pos)
    xs = _dispatch(starts + counts, pad_hi, dest, h2p, n_rows)
    ys = _experts(blk_exp, xs, p['w_e_gate'], p['w_e_up'], p['w_e_down'], p['layer'])
    return _combine(dest, gate.T, h2p, x1, mod, p['g_post_ffn'], p['w_s_gate'], p['w_s_up'], p['w_s_down'],
                    ys, row_map, t_ctx)


def kernel(x_prompt, x_sample, cache_k, cache_v, state_ssm_re, state_ssm_im, c, c_ctx,
           g_pre_mix, g_post_mix, g_pre_ffn, g_post_ffn, w_ada, b_ada, w_in, w_out,
           sgu_g, w_sp, b_sp, rpb, ssm_a_re, ssm_a_im, ssm_log_dt, ssm_b_re, ssm_b_im,
           ssm_c_re, ssm_c_im, ssm_d, w_glu, b_glu, w_router, b_router,
           w_e_gate, w_e_up, w_e_down, w_s_gate, w_s_up, w_s_down):
    n_pb, p_seq, d = x_prompt.shape
    n_sb, s_seq, _ = x_sample.shape
    depth = w_in.shape[0]
    d_a = sgu_g.shape[1]
    d_c = w_glu.shape[1]
    d_b = d - d_a - d_c
    n_heads = d_b // HD_B
    n_g = d_c // SSM_CH
    t_p = n_pb * p_seq
    assert p_seq % CHUNK == 0 and t_p % TOK_TILE == 0 and s_seq % TOK_TILE == 0 and t_p % s_seq == 0
    assert s_seq % (NAT_ROWS * GRID_W) == 0 and s_seq // GRID_W >= NA_WIN_R and d_c == 2 * LANES

    mod_rows = -(-(n_sb + 1) // 8) * 8
    c_all = jnp.zeros((mod_rows, d), F32).at[:n_sb].set(c).at[n_sb].set(c_ctx)
    mod_all = _ada(c_all, w_ada, b_ada)
    p_tiles = t_p // TOK_TILE
    s_tiles = s_seq // TOK_TILE

    def row_map(i):
        return jnp.where(i < p_tiles, n_sb, (i - p_tiles) // s_tiles)

    xc, xl = x_prompt.reshape(t_p, d), x_sample.reshape(n_sb * s_seq, d)
    s_m, s_t, s_w, a16, dvec = jax.vmap(_s5_weights)(ssm_a_re, ssm_a_im, ssm_log_dt, ssm_b_re, ssm_b_im,
                                                     ssm_c_re, ssm_c_im, ssm_d)

    def expand(s, r2w, c2w):
        full = _expand_block_diag(s.reshape((-1,) + s.shape[2:]), r2w, c2w)
        return full.reshape(s.shape[:2] + full.shape[1:])

    s5w = (expand(s_m, SSM_CH, SSM_P), expand(s_t, SSM_CH, SSM_CH), expand(s_w, SSM_P, SSM_CH), a16, dvec)
    nat_bias_all = jax.vmap(_nat_bias)(rpb)
    new_k, new_v, new_re, new_im = [], [], [], []
    for l in range(depth):
        mod = mod_all[l][:, None, :]
        row = lambda a: a[l][None, :].astype(F32)
        pa, q, k32, v32, kb, vb, *pc = _premix(xc, xl, mod, row(g_pre_mix), w_in[l].astype(BF16), row_map,
                                               d_a, d_b, d_c)
        bias_a = jnp.repeat(b_sp[l].T.astype(F32), d_a // NH_A, axis=1)
        ya = _chunk_mlp(pa, row(sgu_g), w_sp[l].reshape(NH_A * CHUNK, CHUNK).astype(BF16), bias_a)
        yb_p = _ctx_attn(q, k32, v32, n_pb, p_seq)
        ck = cache_k[:, l].reshape(n_sb, -1, d_b).astype(BF16)
        cv = cache_v[:, l].reshape(n_sb, -1, d_b).astype(BF16)
        yb_s = _nat_attn(q, kb, vb, ck, cv, nat_bias_all[l], n_sb, s_seq, t_p)
        yb = jnp.concatenate([yb_p, yb_s], axis=0)
        c_p = t_p // S5_CHUNK
        yc_p, fin_p = _s5([h[:c_p] for h in pc], n_pb, p_seq, s5w, l, jnp.zeros((4, n_pb, n_g * SSM_P), F32))
        sre = state_ssm_re[:, l].astype(F32).reshape(n_sb, 2, n_g * SSM_P)
        sim = state_ssm_im[:, l].astype(F32).reshape(n_sb, 2, n_g * SSM_P)
        h0_s = jnp.stack([sre[:, 0], sim[:, 0], sre[:, 1], sim[:, 1]])
        yc_s, _ = _s5([h[c_p:] for h in pc], n_sb, s_seq, s5w, l, h0_s)
        yc = [jnp.concatenate([a, b], axis=0) for a, b in zip(yc_p, yc_s)]
        wr = w_router[l].astype(F32).T
        wr_hi = wr.astype(BF16)
        wr_lo = (wr - wr_hi.astype(F32)).astype(BF16)
        x1, logits, h2p = _post(xc, xl, ya, yb, yc[0], yc[1],
                                mod, row(g_post_mix), row(g_pre_ffn), w_glu[l].astype(BF16), row(b_glu),
                                w_out[l].astype(BF16), wr_hi, wr_lo, row_map)
        moe_p = {
            'b_router': b_router[l].astype(F32)[:, None],
            'w_e_gate': w_e_gate, 'w_e_up': w_e_up, 'w_e_down': w_e_down, 'layer': l,
            'w_s_gate': w_s_gate[l].astype(BF16), 'w_s_up': w_s_up[l].astype(BF16),
            'w_s_down': w_s_down[l].astype(BF16), 'g_post_ffn': row(g_post_ffn),
        }
        xc, xl = _moe(logits, h2p, x1, mod, moe_p, row_map, t_p)
        new_k.append(k32[:t_p].reshape(n_pb, p_seq, n_heads, HD_B))
        new_v.append(v32[:t_p].reshape(n_pb, p_seq, n_heads, HD_B))
        fin_p = fin_p.reshape(4, n_pb, n_g, SSM_P)
        new_re.append(jnp.stack([fin_p[0], fin_p[2]], axis=1))
        new_im.append(jnp.stack([fin_p[1], fin_p[3]], axis=1))
    return (xc.reshape(n_pb, p_seq, d), xl.reshape(n_sb, s_seq, d),
            jnp.stack(new_k, axis=1), jnp.stack(new_v, axis=1),
            jnp.stack(new_re, axis=1), jnp.stack(new_im, axis=1))
```

```python
import functools
import math

import jax
import jax.numpy as jnp
from jax import lax
from jax.experimental import pallas as pl
from jax.experimental.pallas import tpu as pltpu

F32 = jnp.float32
BF16 = jnp.bfloat16
I32 = jnp.int32

GRID_W = 64
EPS = 1e-6
NH_A = 4
CHUNK = 128
HD_B = 64
NA_WIN_R = 8
NA_WIN_C = 16
ATTN_SCALE = HD_B ** -0.5
SSM_CH = 16
SSM_P = 64
N_EXPERTS = 64
TOP_K = 8
N_EXP_GROUPS = 8
TOPK_GROUPS = 4
ROUTE_SCALE = 2.5

LANES = 128
TOK_TILE = 512
S5_CHUNK = 16
S5_OUT_STEPS = 4
MOE_ROWS = 1024
COMBINE_TILE = 256
DISPATCH_TILE = 512
ZERO_RUN = 64
NAT_ROWS = 4
DMA_THREADS = 2
NEG_BIG = -1e30
VMEM_LIMIT = 48 * 1024 * 1024


def _cparams(*sem):
    return pltpu.CompilerParams(dimension_semantics=sem, vmem_limit_bytes=VMEM_LIMIT)


def _dot(a, b):
    return jnp.dot(a, b, preferred_element_type=F32)


def _dot_nt(a, b):
    return lax.dot_general(a, b, (((1,), (1,)), ((), ())), preferred_element_type=F32)


def _rms(x, g):
    return x * lax.rsqrt(jnp.mean(x * x, axis=-1, keepdims=True) + EPS) * g


def _load_tiled(ref, n_rows):
    nc = ref.shape[0] // n_rows
    return jnp.concatenate([ref[pl.ds(j, n_rows, stride=nc), :] for j in range(nc)], axis=1)


def _store_tiled(ref, val):
    n_rows = val.shape[0]
    nc = ref.shape[0] // n_rows
    for j in range(nc):
        ref[pl.ds(j, n_rows, stride=nc), :] = val[:, j * LANES:(j + 1) * LANES]


U32 = jnp.uint32


def _pack_halves(x):
    half = x.shape[1] // 2
    lo = lax.bitcast_convert_type(x[:, :half].astype(BF16).astype(F32), U32)
    hi = lax.bitcast_convert_type(x[:, half:].astype(BF16).astype(F32), U32)
    return (lo >> 16) | hi


def _unpack_halves(w):
    lo = lax.bitcast_convert_type(w << 16, F32)
    hi = lax.bitcast_convert_type(w & jnp.uint32(0xFFFF0000), F32)
    return jnp.concatenate([lo, hi], axis=1)


def _ada_body(c_ref, w_ref, b_ref, o_ref):
    s = jax.nn.silu(c_ref[...]).astype(BF16)
    o_ref[0] = _dot(s, w_ref[0].astype(BF16)) + b_ref[0]


def _ada(c_all, w_ada, b_ada):
    n_layers, d, n = w_ada.shape
    rows = c_all.shape[0]
    tn = 1536
    return pl.pallas_call(
        _ada_body,
        name="ada",
        grid=(n_layers, n // tn),
        in_specs=[pl.BlockSpec((rows, d), lambda l, j: (0, 0)),
                  pl.BlockSpec((1, d, tn), lambda l, j: (l, 0, j)),
                  pl.BlockSpec((1, 1, tn), lambda l, j: (l, 0, j))],
        out_specs=pl.BlockSpec((1, rows, tn), lambda l, j: (l, 0, j)),
        out_shape=jax.ShapeDtypeStruct((n_layers, rows, n), F32),
        compiler_params=_cparams("arbitrary", "arbitrary"),
    )(c_all, w_ada, b_ada.reshape(n_layers, 1, n))


def _stream_tile(xc_ref, xl_ref, ctx_tiles):
    return jnp.where(pl.program_id(0) < ctx_tiles, xc_ref[...], xl_ref[...])


def _stream_specs(tm, d, ctx_tiles):
    return [pl.BlockSpec((tm, d), lambda i: (jnp.minimum(i, ctx_tiles - 1), 0)),
            pl.BlockSpec((tm, d), lambda i: (jnp.maximum(i - ctx_tiles, 0), 0))]


def _premix_body(xc_ref, xl_ref, mod_ref, g_ref, w_ref, pa_ref, q_ref, k_ref, v_ref, kb_ref, vb_ref, pc0_ref,
                 pc1_ref, fold_ref, *, ctx_tiles):
    x = _stream_tile(xc_ref, xl_ref, ctx_tiles)
    d = x.shape[1]
    h = _rms(x, g_ref[...]) * (1 + mod_ref[0, :, d:2 * d]) + mod_ref[0, :, 0:d]
    p = _dot(h.astype(BF16), w_ref[...])
    d_a2 = pa_ref.shape[1]
    d_b = q_ref.shape[1]
    o = d_a2
    pa_ref[...] = p[:, 0:o]
    q_ref[...] = p[:, o:o + d_b].astype(BF16)
    k = p[:, o + d_b:o + 2 * d_b]
    v = p[:, o + 2 * d_b:o + 3 * d_b]

    @pl.when(pl.program_id(0) < ctx_tiles)
    def _():
        k_ref[...] = k
        v_ref[...] = v

    kb_ref[...] = k.astype(BF16)
    vb_ref[...] = v.astype(BF16)
    for h, pc_ref in enumerate((pc0_ref, pc1_ref)):
        fold_ref[...] = p[:, o + 3 * d_b + h * LANES:o + 3 * d_b + (h + 1) * LANES]
        _fold_chunks(pc_ref, fold_ref)


def _fold_chunks(dst_ref, src_ref):
    n = dst_ref.shape[0]
    for i in range(S5_CHUNK):
        dst_ref[:, i * LANES:(i + 1) * LANES] = src_ref[pl.ds(i, n, stride=S5_CHUNK), :]


def _unfold_chunks(dst_ref, src_ref):
    n = src_ref.shape[0]
    for i in range(S5_CHUNK):
        dst_ref[pl.ds(i, n, stride=S5_CHUNK), :] = src_ref[:, i * LANES:(i + 1) * LANES]


def _premix(xc, xl, mod, g, w_in_b, row_map, d_a, d_b, d_c):
    t_ctx, d = xc.shape
    t = t_ctx + xl.shape[0]
    tm = TOK_TILE
    d_in = w_in_b.shape[1]
    ctx_tiles = t_ctx // tm
    tok = lambda n: pl.BlockSpec((tm, n), lambda i: (i, 0))
    ctx = pl.BlockSpec((tm, d_b), lambda i: (jnp.minimum(i, ctx_tiles - 1), 0))
    fold = pl.BlockSpec((tm // S5_CHUNK, S5_CHUNK * LANES), lambda i: (i, 0))
    return pl.pallas_call(
        functools.partial(_premix_body, ctx_tiles=ctx_tiles),
        name="premix",
        grid=(t // tm,),
        in_specs=_stream_specs(tm, d, ctx_tiles) + [
            pl.BlockSpec((1, 1, mod.shape[2]), lambda i: (row_map(i), 0, 0)),
            pl.BlockSpec((1, d), lambda i: (0, 0)),
            pl.BlockSpec((d, d_in), lambda i: (0, 0))],
        out_specs=[tok(2 * d_a), tok(d_b), ctx, ctx, tok(d_b), tok(d_b), fold, fold],
        out_shape=[jax.ShapeDtypeStruct((t, 2 * d_a), F32),
                   jax.ShapeDtypeStruct((t, d_b), BF16),
                   jax.ShapeDtypeStruct((t_ctx, d_b), F32),
                   jax.ShapeDtypeStruct((t_ctx, d_b), F32),
                   jax.ShapeDtypeStruct((t, d_b), BF16),
                   jax.ShapeDtypeStruct((t, d_b), BF16)]
        + [jax.ShapeDtypeStruct((t // S5_CHUNK, S5_CHUNK * LANES), F32)] * 2,
        scratch_shapes=[pltpu.VMEM((tm, LANES), F32)],
        compiler_params=_cparams("arbitrary"),
    )(xc, xl, mod, g, w_in_b)


def _chunk_body(pa_ref, g_ref, w_ref, b_ref, o_ref):
    z = jax.nn.gelu(pa_ref[...])
    d_a = o_ref.shape[1]
    hd = d_a // NH_A
    u = z[:, :d_a]
    v = z[:, d_a:]
    mu = jnp.mean(v, axis=-1, keepdims=True)
    var = jnp.mean(jnp.square(v - mu), axis=-1, keepdims=True)
    vb = ((v - mu) * lax.rsqrt(var + EPS) * g_ref[...]).astype(BF16)
    head = lax.broadcasted_iota(I32, (CHUNK, d_a), 1) // hd
    for ch in range(pa_ref.shape[0] // CHUNK):
        rows = slice(ch * CHUNK, (ch + 1) * CHUNK)
        sf = _dot(w_ref[...], vb[rows])
        s = b_ref[...]
        for h in range(NH_A):
            s = s + jnp.where(head == h, sf[h * CHUNK:(h + 1) * CHUNK], 0.0)
        o_ref[rows, :] = (u[rows] * s).astype(BF16)


def _chunk_mlp(pa, sgu_g, w_sp_b, bias):
    t, d2 = pa.shape
    d_a = d2 // 2
    tm = TOK_TILE
    return pl.pallas_call(
        _chunk_body,
        name="chunk_mlp",
        grid=(t // tm,),
        in_specs=[pl.BlockSpec((tm, d2), lambda i: (i, 0)),
                  pl.BlockSpec((1, d_a), lambda i: (0, 0)),
                  pl.BlockSpec(w_sp_b.shape, lambda i: (0, 0)),
                  pl.BlockSpec(bias.shape, lambda i: (0, 0))],
        out_specs=pl.BlockSpec((tm, d_a), lambda i: (i, 0)),
        out_shape=jax.ShapeDtypeStruct((t, d_a), BF16),
        compiler_params=_cparams("arbitrary"),
    )(pa, sgu_g, w_sp_b, bias)


def _stack_pair(qg):
    lane = lax.broadcasted_iota(I32, qg.shape, 1)
    zero = jnp.zeros_like(qg)
    return jnp.concatenate([jnp.where(lane < HD_B, qg, zero), jnp.where(lane >= HD_B, qg, zero)], axis=0)


def _unstack_pair(o2):
    n = o2.shape[0] // 2
    lane = lax.broadcasted_iota(I32, (n, o2.shape[1]), 1)
    return jnp.where(lane < HD_B, o2[:n], o2[n:])


def _ctx_attn_body(q_ref, k_ref, v_ref, o_ref):
    for g in range(q_ref.shape[1] // LANES):
        cols = slice(g * LANES, (g + 1) * LANES)
        q2 = _stack_pair(q_ref[:, cols])
        s = _dot_nt(q2, k_ref[:, cols].astype(BF16)) * ATTN_SCALE
        e = jnp.exp(s - jnp.max(s, axis=-1, keepdims=True))
        p = e / jnp.sum(e, axis=-1, keepdims=True)
        o2 = _dot(p.astype(BF16), v_ref[:, cols].astype(BF16))
        o_ref[:, cols] = _unstack_pair(o2).astype(BF16)


def _ctx_attn(q, k, v, n_batch, seq):
    d_b = q.shape[1]
    blk = pl.BlockSpec((seq, d_b), lambda b: (b, 0))
    return pl.pallas_call(
        _ctx_attn_body,
        name="ctx_attn",
        grid=(n_batch,),
        in_specs=[blk, blk, blk],
        out_specs=blk,
        out_shape=jax.ShapeDtypeStruct((n_batch * seq, d_b), BF16),
        compiler_params=_cparams("arbitrary"),
    )(q, k, v)


def _nat_body(q_ref, k_ref, v_ref, ck_ref, cv_ref, bias_ref, o_ref, *, rows):
    r0 = pl.program_id(1) * NAT_ROWS
    n_win = NA_WIN_R * GRID_W
    pr = 2 * GRID_W
    starts, cases = [], []
    for i in range(NAT_ROWS):
        rs = jnp.clip(r0 + i - NA_WIN_R // 2, 0, rows - NA_WIN_R)
        starts.append(pl.multiple_of(rs * GRID_W, GRID_W))
        cases.append(r0 + i - rs)
    for g in range(q_ref.shape[1] // LANES):
        cols = slice(g * LANES, (g + 1) * LANES)
        q2 = jnp.concatenate([_stack_pair(q_ref[i * GRID_W:(i + 1) * GRID_W, cols]) for i in range(NAT_ROWS)],
                             axis=0)
        s_ctx = _dot_nt(q2, ck_ref[0, :, cols]) * ATTN_SCALE
        m_ctx = jnp.max(s_ctx, axis=-1, keepdims=True)
        e_wins, invs, ms = [], [], []
        for i in range(NAT_ROWS):
            s_win = (_dot_nt(q2[i * pr:(i + 1) * pr], k_ref[pl.ds(starts[i], n_win), cols]) * ATTN_SCALE
                     + bias_ref[cases[i], g])
            m = jnp.maximum(jnp.max(s_win, axis=-1, keepdims=True), m_ctx[i * pr:(i + 1) * pr])
            e_wins.append(jnp.exp(s_win - m))
            ms.append(m)
        e_ctx = jnp.exp(s_ctx - jnp.concatenate(ms, axis=0))
        l_ctx = jnp.sum(e_ctx, axis=-1, keepdims=True)
        for i in range(NAT_ROWS):
            invs.append(1.0 / (jnp.sum(e_wins[i], axis=-1, keepdims=True) + l_ctx[i * pr:(i + 1) * pr]))
        o_ctx = _dot((e_ctx * jnp.concatenate(invs, axis=0)).astype(BF16), cv_ref[0, :, cols])
        for i in range(NAT_ROWS):
            o2 = _dot((e_wins[i] * invs[i]).astype(BF16), v_ref[pl.ds(starts[i], n_win), cols])
            o_ref[i * GRID_W:(i + 1) * GRID_W, cols] = _unstack_pair(o2 + o_ctx[i * pr:(i + 1) * pr]).astype(BF16)


def _nat_attn(q, kb, vb, ck, cv, bias, n_batch, seq, tok0):
    d_b = q.shape[1]
    rows = seq // GRID_W
    lc = ck.shape[1]
    blk = NAT_ROWS * GRID_W
    steps = seq // blk
    q0 = tok0 // blk
    i0 = tok0 // seq
    img = pl.BlockSpec((seq, d_b), lambda b, r: (i0 + b, 0))
    ctx = pl.BlockSpec((1, lc, d_b), lambda b, r: (b, 0, 0))
    return pl.pallas_call(
        functools.partial(_nat_body, rows=rows),
        name="nat_attn",
        grid=(n_batch, steps),
        in_specs=[pl.BlockSpec((blk, d_b), lambda b, r: (q0 + b * steps + r, 0)),
                  img, img, ctx, ctx,
                  pl.BlockSpec(bias.shape, lambda b, r: (0, 0, 0, 0))],
        out_specs=pl.BlockSpec((blk, d_b), lambda b, r: (b * steps + r, 0)),
        out_shape=jax.ShapeDtypeStruct((n_batch * seq, d_b), BF16),
        compiler_params=_cparams("arbitrary", "arbitrary"),
    )(q, kb, vb, ck, cv, bias)


def _nat_bias(rpb):
    n_heads = rpb.shape[0]
    cols = jnp.arange(GRID_W)
    col_start = jnp.clip(cols - NA_WIN_C // 2, 0, GRID_W - NA_WIN_C)
    j = jnp.arange(GRID_W)
    valid = (j[None, :] >= col_start[:, None]) & (j[None, :] < col_start[:, None] + NA_WIN_C)
    col_off = jnp.clip(j[None, :] - cols[:, None] + (NA_WIN_C - 1), 0, 2 * NA_WIN_C - 2)
    toe = jnp.where(valid[None, None], rpb.astype(F32)[:, :, col_off], NEG_BIG)
    cases = jnp.stack([toe[:, NA_WIN_R - 1 - delta:2 * NA_WIN_R - 1 - delta] for delta in range(NA_WIN_R)])
    return cases.transpose(0, 1, 3, 2, 4).reshape(NA_WIN_R, n_heads // 2, 2 * GRID_W, NA_WIN_R * GRID_W)


def _s5_weights(a_re, a_im, log_dt, b_re, b_im, c_re, c_im, ssm_d):
    n_g = a_re.shape[1]
    c = S5_CHUNK
    lam = lax.complex(a_re.astype(F32), a_im.astype(F32))
    ldt = lam * jnp.exp(log_dt.astype(F32))[..., None]
    lam_bar = jnp.exp(ldt)
    b_bar = ((lam_bar - 1) / lam)[..., None] * lax.complex(b_re.astype(F32), b_im.astype(F32))
    c_mat = lax.complex(c_re.astype(F32), c_im.astype(F32))
    pw = jnp.exp(ldt[None] * jnp.arange(c + 1, dtype=F32)[:, None, None, None])
    kern = jnp.real(jnp.einsum('dgcp,kdgp,dgpe->dgkce', c_mat, pw[:c], b_bar))
    i = jnp.arange(c)
    lag = i[None, :] - i[:, None]
    tf = jnp.where((lag >= 0)[None, :, :, None, None], kern[0][:, jnp.clip(lag, 0, c - 1)], 0.0)
    tb = jnp.where((lag <= 0)[None, :, :, None, None], kern[1][:, jnp.clip(-lag, 0, c - 1)], 0.0)
    t_mat = (tf + tb).transpose(0, 1, 4, 2, 3).reshape(n_g, c * SSM_CH, c * SSM_CH)
    mf = pw[:c][::-1, 0][:, :, :, None] * b_bar[0][None]
    mb = pw[:c, 1][:, :, :, None] * b_bar[1][None]
    mf = mf.transpose(1, 0, 3, 2).reshape(n_g, c * SSM_CH, SSM_P)
    mb = mb.transpose(1, 0, 3, 2).reshape(n_g, c * SSM_CH, SSM_P)
    mq = jnp.stack([jnp.real(mf), jnp.imag(mf), jnp.real(mb), jnp.imag(mb)], axis=2)
    zf = c_mat[0][:, None] * pw[1:c + 1, 0][:, :, None, :].transpose(1, 0, 2, 3)
    zb = c_mat[1][:, None] * pw[1:c + 1, 1][::-1][:, :, None, :].transpose(1, 0, 2, 3)
    zf = zf.transpose(0, 3, 1, 2).reshape(n_g, SSM_P, c * SSM_CH)
    zb = zb.transpose(0, 3, 1, 2).reshape(n_g, SSM_P, c * SSM_CH)
    wq = jnp.stack([jnp.real(zf), -jnp.imag(zf), jnp.real(zb), -jnp.imag(zb)], axis=1)
    gh = LANES // SSM_CH
    n_half = n_g // gh
    kd = c * SSM_CH

    def stack_rows(blocks, r1):
        b = blocks.astype(BF16).reshape(n_half, gh, r1, kd // r1, kd)
        return b.transpose(0, 2, 1, 3, 4).reshape(n_half, gh * kd, kd)

    s_m = stack_rows(mq.reshape(n_g, kd, 4 * SSM_P), c)
    s_t = stack_rows(t_mat, c)
    s_w = stack_rows(wq.reshape(n_g, 4 * SSM_P, kd), 4)
    a_c = pw[c]
    a16 = jnp.stack([jnp.real(a_c[0]), jnp.imag(a_c[0]), jnp.real(a_c[1]), jnp.imag(a_c[1])])
    a16 = a16.reshape(4, 1, n_half, gh * SSM_P).transpose(2, 0, 1, 3)
    dvec = ssm_d.astype(F32).reshape(n_half, 1, LANES)
    return s_m, s_t, s_w, a16, dvec


def _expand_body(s_ref, rep_ref, o_ref, *, r2w, c2w):
    tr, n = o_ref.shape[1], o_ref.shape[2]
    gh = n // s_ref.shape[2]
    full = _dot(s_ref[0], rep_ref[...])
    row = pl.program_id(1) * tr + lax.broadcasted_iota(I32, (tr, n), 0)
    col = lax.broadcasted_iota(I32, (tr, n), 1)
    o_ref[0] = jnp.where((row // r2w) % gh == (col // c2w) % gh, full, 0.0).astype(BF16)


def _expand_block_diag(s, r2w, c2w):
    m, n, kd = s.shape
    gh = n // kd
    rep = jnp.broadcast_to(jnp.eye(kd, dtype=BF16).reshape(kd, kd // c2w, 1, c2w), (kd, kd // c2w, gh, c2w))
    rep = rep.reshape(kd, n)
    tr = 256
    return pl.pallas_call(
        functools.partial(_expand_body, r2w=r2w, c2w=c2w),
        name="s5_expand",
        grid=(m, n // tr),
        in_specs=[pl.BlockSpec((1, tr, kd), lambda a, i: (a, i, 0)),
                  pl.BlockSpec((kd, n), lambda a, i: (0, 0))],
        out_specs=pl.BlockSpec((1, tr, n), lambda a, i: (a, i, 0)),
        out_shape=jax.ShapeDtypeStruct((m, n, n), BF16),
        compiler_params=_cparams("arbitrary", "arbitrary"),
    )(s, rep)


def _s5_state_body(u_ref, m_ref, fre_ref, fim_ref, bre_ref, bim_ref):
    nb, tn, kd = u_ref.shape
    w = fre_ref.shape[2]
    r = _dot(u_ref[...].reshape(nb * tn, kd).astype(BF16), m_ref[0, 0])
    for q, o_ref in enumerate((fre_ref, fim_ref, bre_ref, bim_ref)):
        for b in range(nb):
            o_ref[:, b, :] = r[b * tn:(b + 1) * tn, q * w:(q + 1) * w]


def _s5_rows(nb, n):
    return max(8, min(n, 256 // nb))


def _s5_states(u3, m_all, layer, half):
    nb, n, kd = u3.shape
    tn = _s5_rows(nb, n)
    w = m_all.shape[3] // 4
    out = pl.BlockSpec((tn, nb, w), lambda i: (i, 0, 0))
    return pl.pallas_call(
        _s5_state_body,
        name="s5_states",
        grid=(n // tn,),
        in_specs=[pl.BlockSpec((nb, tn, kd), lambda i: (0, i, 0)),
                  pl.BlockSpec((1, 1) + m_all.shape[2:], lambda i: (layer, half, 0, 0))],
        out_specs=[out] * 4,
        out_shape=[jax.ShapeDtypeStruct((n, nb, w), F32)] * 4,
        compiler_params=_cparams("arbitrary"),
    )(u3, m_all)


def _s5_scan_body(sfr_ref, sfi_ref, sbr_ref, sbi_ref, a_ref, h0_ref,
                  hfr_ref, hfi_ref, hbr_ref, hbi_ref, fin_ref, st_ref):
    @pl.when(pl.program_id(0) == 0)
    def _():
        st_ref[...] = h0_ref[...]

    ks = sfr_ref.shape[0]
    afr, afi, abr, abi = a_ref[0], a_ref[1], a_ref[2], a_ref[3]

    def step(s, carry):
        fr, fi, br, bi = carry
        sb = ks - 1 - s
        hfr_ref[s] = fr
        hfi_ref[s] = fi
        hbr_ref[sb] = br
        hbi_ref[sb] = bi
        nfr = afr * fr - afi * fi + sfr_ref[s]
        nfi = afr * fi + afi * fr + sfi_ref[s]
        nbr = abr * br - abi * bi + sbr_ref[sb]
        nbi = abr * bi + abi * br + sbi_ref[sb]
        return nfr, nfi, nbr, nbi

    carry = lax.fori_loop(0, ks, step, (st_ref[0], st_ref[1], st_ref[2], st_ref[3]))
    for q in range(4):
        st_ref[q] = carry[q]
        fin_ref[q] = carry[q]


def _s5_scan(s3, a16, h0):
    n_chunks, n_batch, width = s3[0].shape
    ks = min(n_chunks, 32)
    nb = n_chunks // ks
    fwd = pl.BlockSpec((ks, n_batch, width), lambda i: (i, 0, 0))
    bwd = pl.BlockSpec((ks, n_batch, width), lambda i: (nb - 1 - i, 0, 0))
    small = lambda shape: pl.BlockSpec(shape, lambda i: (0, 0, 0))
    outs = pl.pallas_call(
        _s5_scan_body,
        name="s5_scan",
        grid=(nb,),
        in_specs=[fwd, fwd, bwd, bwd, small(a16.shape), small(h0.shape)],
        out_specs=[fwd, fwd, bwd, bwd, small(h0.shape)],
        out_shape=[jax.ShapeDtypeStruct((n_chunks, n_batch, width), F32)] * 4
        + [jax.ShapeDtypeStruct(h0.shape, F32)],
        scratch_shapes=[pltpu.VMEM(h0.shape, F32)],
        compiler_params=_cparams("arbitrary"),
    )(*s3, a16, h0)
    return outs[:4], outs[4]


def _s5_out_body(u_ref, us_ref, hfr_ref, hfi_ref, hbr_ref, hbi_ref, t_ref, w_ref, d_ref, y_ref):
    nb, tn, kd = u_ref.shape
    rows = nb * tn
    wo = us_ref.shape[2]
    y = _dot(u_ref[...].reshape(rows, kd).astype(BF16), t_ref[0, 0])
    w = hfr_ref.shape[2]
    for q, h_ref in enumerate((hfr_ref, hfi_ref, hbr_ref, hbi_ref)):
        hq = jnp.concatenate([h_ref[:, b, :] for b in range(nb)], axis=0)
        y = y + _dot(hq.astype(BF16), w_ref[0, 0, q * w:(q + 1) * w, :])
    skip = jnp.concatenate([d_ref[...]] * (wo // LANES), axis=1) * us_ref[...].reshape(rows, wo)
    y_ref[...] = jax.nn.gelu(y + skip).reshape(nb, tn, wo)


def _s5_out(u3, h4, t_all, w_all, dvec, layer, half):
    nb, n, kd = u3.shape
    tn = _s5_rows(nb, n)
    wo = S5_OUT_STEPS * LANES
    w = h4[0].shape[2]
    hb = pl.BlockSpec((tn, nb, w), lambda i, j: (i, 0, 0))
    mat = lambda a: pl.BlockSpec((1, 1, a.shape[2], wo), lambda i, j: (layer, half, 0, j))
    return pl.pallas_call(
        _s5_out_body,
        name="s5_out",
        grid=(n // tn, kd // wo),
        in_specs=[pl.BlockSpec((nb, tn, kd), lambda i, j: (0, i, 0)),
                  pl.BlockSpec((nb, tn, wo), lambda i, j: (0, i, j)),
                  hb, hb, hb, hb, mat(t_all), mat(w_all),
                  pl.BlockSpec(dvec.shape, lambda i, j: (0, 0))],
        out_specs=pl.BlockSpec((nb, tn, wo), lambda i, j: (0, i, j)),
        out_shape=jax.ShapeDtypeStruct(u3.shape, F32),
        compiler_params=_cparams("arbitrary", "arbitrary"),
    )(u3, u3, *h4, t_all, w_all, dvec)


def _s5(pc_halves, n_batch, seq, weights, layer, h0):
    m_all, t_all, w_all, a16, dvec = weights
    n_chunks = seq // S5_CHUNK
    ys, fins = [], []
    for h, pc in enumerate(pc_halves):
        w = m_all.shape[3] // 4
        u3 = pc.reshape(n_batch, n_chunks, S5_CHUNK * LANES)
        s4 = _s5_states(u3, m_all, layer, h)
        h4, fin = _s5_scan(s4, a16[layer, h], h0[:, :, h * w:(h + 1) * w])
        y = _s5_out(u3, h4, t_all, w_all, dvec[layer, h], layer, h)
        ys.append(y.reshape(n_batch * n_chunks, S5_CHUNK * LANES))
        fins.append(fin)
    return ys, jnp.concatenate(fins, axis=2)


def _post_body(xc_ref, xl_ref, ya_ref, ybc_ref, ybl_ref, yc0c_ref, yc0l_ref, yc1c_ref, yc1l_ref, mod_ref, gpost_ref,
               gffn_ref, wglu_ref, bglu_ref, wo_ref, whi_ref, wlo_ref, x1_ref, logit_ref, h2p_ref, unfold_ref, *,
               ctx_tiles):
    x = _stream_tile(xc_ref, xl_ref, ctx_tiles)
    d = x.shape[1]
    d_a = ya_ref.shape[1]
    d_b = ybc_ref.shape[1]
    in_ctx = pl.program_id(0) < ctx_tiles
    for h, (c_ref, l_ref) in enumerate(((yc0c_ref, yc0l_ref), (yc1c_ref, yc1l_ref))):
        @pl.when(in_ctx)
        def _():
            _unfold_chunks(unfold_ref.at[h], c_ref)

        @pl.when(jnp.logical_not(in_ctx))
        def _():
            _unfold_chunks(unfold_ref.at[h], l_ref)
    y = jnp.concatenate([unfold_ref[0], unfold_ref[1]], axis=1)
    glu = y * jax.nn.sigmoid(_dot(y.astype(BF16), wglu_ref[...]) + bglu_ref[...])
    yb = _stream_tile(ybc_ref, ybl_ref, ctx_tiles)
    mixed = (_dot(ya_ref[...], wo_ref[0:d_a, :]) + _dot(yb, wo_ref[d_a:d_a + d_b, :])
             + _dot(glu.astype(BF16), wo_ref[d_a + d_b:, :]))
    x1 = x + mod_ref[0, :, 2 * d:3 * d] * _rms(mixed, gpost_ref[...])
    x1_ref[...] = x1
    h2 = _rms(x1, gffn_ref[...]) * (1 + mod_ref[0, :, 4 * d:5 * d]) + mod_ref[0, :, 3 * d:4 * d]
    logit_ref[...] = _router_logits(h2, whi_ref, wlo_ref)
    _store_tiled(h2p_ref, _pack_halves(h2))


def _post(xc, xl, ya, yb, yc0, yc1, mod, g_post, g_ffn, w_glu_b, b_glu, w_out_b, w_hi, w_lo, row_map):
    t_ctx, d = xc.shape
    t = t_ctx + xl.shape[0]
    tm = TOK_TILE
    nc = d // LANES
    ctx_tiles = t_ctx // tm
    tok = lambda n: pl.BlockSpec((tm, n), lambda i: (i, 0))
    full = lambda a: pl.BlockSpec(a.shape, lambda i: (0,) * a.ndim)
    fold = _stream_specs(tm // S5_CHUNK, S5_CHUNK * LANES, ctx_tiles)
    return pl.pallas_call(
        functools.partial(_post_body, ctx_tiles=ctx_tiles),
        name="post_mix",
        grid=(t // tm,),
        in_specs=_stream_specs(tm, d, ctx_tiles) + [tok(ya.shape[1])]
        + _stream_specs(tm, yb[0].shape[1], ctx_tiles) + fold + fold + [
            pl.BlockSpec((1, 1, mod.shape[2]), lambda i: (row_map(i), 0, 0)),
            full(g_post), full(g_ffn), full(w_glu_b), full(b_glu), full(w_out_b), full(w_hi), full(w_lo)],
        out_specs=[tok(d), pl.BlockSpec((N_EXPERTS, tm), lambda i: (0, i)),
                   pl.BlockSpec((tm * nc // 2, LANES), lambda i: (i, 0))],
        out_shape=[jax.ShapeDtypeStruct((t, d), F32), jax.ShapeDtypeStruct((N_EXPERTS, t), F32),
                   jax.ShapeDtypeStruct((t * nc // 2, LANES), U32)],
        scratch_shapes=[pltpu.VMEM((2, tm, LANES), F32)],
        compiler_params=_cparams("arbitrary"),
    )(xc, xl, ya, *yb, *yc0, *yc1, mod, g_post, g_ffn, w_glu_b, b_glu, w_out_b, w_hi, w_lo)


def _router_logits(h, whi_ref, wlo_ref):
    hi = h.astype(BF16)
    lo = (h - hi.astype(F32)).astype(BF16)
    return _dot_nt(whi_ref[...], hi) + (_dot_nt(whi_ref[...], lo) + _dot_nt(wlo_ref[...], hi))


def _route_body(logit_ref, b_ref, idx_ref, gate_ref, pos_ref, cnt_ref, carry_ref):
    @pl.when(pl.program_id(0) == 0)
    def _():
        carry_ref[...] = jnp.zeros_like(carry_ref)

    tm = idx_ref.shape[1]
    scores = jax.nn.sigmoid(logit_ref[...])
    sel = scores + b_ref[...]
    gsz = N_EXPERTS // N_EXP_GROUPS
    within = lax.broadcasted_iota(I32, (gsz, tm), 0).astype(F32)
    grp = []
    for g in range(N_EXP_GROUPS):
        blk = sel[g * gsz:(g + 1) * gsz]
        m1 = jnp.max(blk, axis=0, keepdims=True)
        first = jnp.min(jnp.where(blk == m1, within, float(gsz)), axis=0, keepdims=True)
        m2 = jnp.max(jnp.where(within == first, -jnp.inf, blk), axis=0, keepdims=True)
        grp.append(m1 + m2)
    blocks = []
    for g in range(N_EXP_GROUPS):
        ahead = jnp.zeros((1, tm), F32)
        for o in range(N_EXP_GROUPS):
            if o == g:
                continue
            beats = (grp[o] >= grp[g]) if o < g else (grp[o] > grp[g])
            ahead = ahead + jnp.where(beats, 1.0, 0.0)
        ahead = jnp.broadcast_to(ahead, (gsz, tm))
        blocks.append(jnp.where(ahead < TOPK_GROUPS, sel[g * gsz:(g + 1) * gsz], -jnp.inf))
    v = jnp.concatenate(blocks, axis=0)
    eidx = lax.broadcasted_iota(I32, (N_EXPERTS, tm), 0)
    rnk = jnp.zeros((N_EXPERTS, tm), F32)
    for e in range(N_EXPERTS):
        row = v[e:e + 1]
        rnk = rnk + jnp.where(eidx > e, jnp.where(row >= v, 1.0, 0.0), jnp.where(row > v, 1.0, 0.0))
    chosen = rnk < TOP_K
    w = jnp.where(chosen, scores, 0.0)
    wn = w / jnp.sum(w, axis=0, keepdims=True) * ROUTE_SCALE
    tri = (lax.broadcasted_iota(I32, (tm, tm), 0) < lax.broadcasted_iota(I32, (tm, tm), 1))
    chosen_f = jnp.where(chosen, 1.0, 0.0)
    prefix = _dot(chosen_f.astype(BF16), jnp.where(tri, 1.0, 0.0).astype(BF16)) + carry_ref[:, 0:1]
    total = carry_ref[...] + jnp.sum(chosen_f, axis=1, keepdims=True)
    carry_ref[...] = total
    cnt_ref[...] = total
    eidx_f = eidx.astype(F32)
    for k in range(TOP_K):
        one = rnk == k
        idx_ref[k:k + 1, :] = jnp.sum(jnp.where(one, eidx_f, 0.0), axis=0, keepdims=True).astype(I32)
        gate_ref[k:k + 1, :] = jnp.sum(jnp.where(one, wn, 0.0), axis=0, keepdims=True)
        pos_ref[k:k + 1, :] = jnp.sum(jnp.where(one, prefix, 0.0), axis=0, keepdims=True).astype(I32)


def _route(logits, b_router):
    t = logits.shape[1]
    tm = TOK_TILE
    out = pl.BlockSpec((TOP_K, tm), lambda i: (0, i))
    full = lambda a: pl.BlockSpec(a.shape, lambda i: (0, 0))
    return pl.pallas_call(
        _route_body,
        name="route",
        grid=(t // tm,),
        in_specs=[pl.BlockSpec((N_EXPERTS, tm), lambda i: (0, i)), full(b_router)],
        out_specs=[out, out, out, pl.BlockSpec((N_EXPERTS, LANES), lambda i: (0, 0))],
        out_shape=[jax.ShapeDtypeStruct((TOP_K, t), I32), jax.ShapeDtypeStruct((TOP_K, t), F32),
                   jax.ShapeDtypeStruct((TOP_K, t), I32), jax.ShapeDtypeStruct((N_EXPERTS, LANES), F32)],
        scratch_shapes=[pltpu.VMEM((N_EXPERTS, LANES), F32)],
        compiler_params=_cparams("arbitrary"),
    )(logits, b_router)


def _dest_body(starts_ref, idx_ref, pos_ref, dest_ref):
    idx = idx_ref[...]
    acc = pos_ref[...]
    for e in range(N_EXPERTS):
        acc = acc + jnp.where(idx == e, starts_ref[e], 0)
    dest_ref[...] = acc


def _dest(starts, idx, pos):
    t = idx.shape[1]
    tile = math.gcd(t, 4096)
    blk = pl.BlockSpec((TOP_K, tile), lambda i, st: (0, i))
    return pl.pallas_call(
        _dest_body,
        name="moe_dest",
        grid_spec=pltpu.PrefetchScalarGridSpec(num_scalar_prefetch=1, grid=(t // tile,),
                                               in_specs=[blk, blk], out_specs=blk),
        out_shape=jax.ShapeDtypeStruct(idx.shape, I32),
        compiler_params=_cparams("arbitrary"),
    )(starts, idx, pos)


def _dispatch_body(pad_lo_ref, pad_hi_ref, dest_ref, h_ref, xs_ref, zero_ref, sem, zsem, *, n_pad_rows):
    tile = dest_ref.shape[1]
    nw = h_ref.shape[0] // tile

    def token(ref, r):
        return ref.at[pl.ds(pl.multiple_of(r * nw, nw), nw)]

    def zero_copy(r, n):
        return pltpu.make_async_copy(zero_ref.at[pl.ds(0, n * nw)],
                                     xs_ref.at[pl.ds(pl.multiple_of(r * nw, nw), n * nw)], zsem)

    @pl.when(pl.program_id(0) == 0)
    def _():
        zero_ref[...] = jnp.zeros_like(zero_ref)

        def per_expert(e, c):
            lo = pad_lo_ref[e]
            runs = (pad_hi_ref[e] - lo) // ZERO_RUN

            def run(j, c2):
                zero_copy(lo + j * ZERO_RUN, ZERO_RUN).start()
                return c2

            def one(r, c2):
                zero_copy(r, 1).start()
                return c2

            c = lax.fori_loop(0, runs, run, c)
            return lax.fori_loop(lo + runs * ZERO_RUN, pad_hi_ref[e], one, c)

        lax.fori_loop(0, N_EXPERTS, per_expert, 0)

    def row_copy(t, k):
        return pltpu.make_async_copy(token(h_ref, t), token(xs_ref, dest_ref[k, t]), sem)

    def start(t, c):
        for k in range(TOP_K):
            row_copy(t, k).start(priority=k % DMA_THREADS)
        return c

    def wait(t, c):
        for k in range(TOP_K):
            row_copy(t, k).wait()
        return c

    lax.fori_loop(0, tile, start, 0)
    lax.fori_loop(0, tile, wait, 0)

    @pl.when(pl.program_id(0) == 0)
    def _():
        def one(r, c):
            zero_copy(0, ZERO_RUN).wait()
            return c
        lax.fori_loop(0, n_pad_rows // ZERO_RUN, one, 0)


def _dispatch(pad_lo, pad_hi, dest, h2p, n_rows):
    t = dest.shape[1]
    nw = h2p.shape[0] // t
    tile = DISPATCH_TILE
    grid_spec = pltpu.PrefetchScalarGridSpec(
        num_scalar_prefetch=2,
        grid=(t // tile,),
        in_specs=[pl.BlockSpec((TOP_K, tile), lambda i, lo, hi: (0, i), memory_space=pltpu.SMEM),
                  pl.BlockSpec((tile * nw, LANES), lambda i, lo, hi: (i, 0))],
        out_specs=pl.BlockSpec(memory_space=pl.ANY),
        scratch_shapes=[pltpu.VMEM((ZERO_RUN * nw, LANES), U32), pltpu.SemaphoreType.DMA(()),
                        pltpu.SemaphoreType.DMA(())],
    )
    assert (n_rows - t * TOP_K) % ZERO_RUN == 0
    return pl.pallas_call(
        functools.partial(_dispatch_body, n_pad_rows=n_rows - t * TOP_K),
        name="moe_dispatch",
        grid_spec=grid_spec,
        out_shape=jax.ShapeDtypeStruct((n_rows * nw, LANES), U32),
        compiler_params=_cparams("arbitrary"),
    )(pad_lo, pad_hi, dest, h2p)


def _experts_body(be_ref, x_ref, wg_ref, wu_ref, wd_ref, y_ref, wgb_ref, wub_ref, wdb_ref):
    i = pl.program_id(0)

    @pl.when((i == 0) | (be_ref[i] != be_ref[jnp.maximum(i - 1, 0)]))
    def _():
        wgb_ref[...] = wg_ref[0, 0].astype(BF16)
        wub_ref[...] = wu_ref[0, 0].astype(BF16)
        wdb_ref[...] = wd_ref[0, 0].astype(BF16)

    xb = _unpack_halves(_load_tiled(x_ref, MOE_ROWS)).astype(BF16)
    hb = jax.nn.silu(_dot(xb, wgb_ref[...])) * _dot(xb, wub_ref[...])
    _store_tiled(y_ref, _pack_halves(_dot(hb.astype(BF16), wdb_ref[...])))


def _experts(blk_exp, xs, wg, wu, wd, layer):
    d, de = wg.shape[2], wg.shape[3]
    nc = d // (2 * LANES)
    bm = MOE_ROWS
    tiles = pl.BlockSpec((bm * nc, LANES), lambda i, be: (i, 0))
    grid_spec = pltpu.PrefetchScalarGridSpec(
        num_scalar_prefetch=1,
        grid=(xs.shape[0] // (bm * nc),),
        in_specs=[tiles,
                  pl.BlockSpec((1, 1, d, de), lambda i, be: (layer, be[i], 0, 0)),
                  pl.BlockSpec((1, 1, d, de), lambda i, be: (layer, be[i], 0, 0)),
                  pl.BlockSpec((1, 1, de, d), lambda i, be: (layer, be[i], 0, 0))],
        out_specs=tiles,
        scratch_shapes=[pltpu.VMEM((d, de), BF16), pltpu.VMEM((d, de), BF16), pltpu.VMEM((de, d), BF16)],
    )
    return pl.pallas_call(
        _experts_body,
        name="moe_experts",
        grid_spec=grid_spec,
        out_shape=jax.ShapeDtypeStruct(xs.shape, U32),
        compiler_params=_cparams("arbitrary"),
    )(blk_exp, xs, wg, wu, wd)


def _combine_body(dest_ref, next_ref, gate_ref, h_ref, x_ref, mod_ref, g_ref, wsg_ref, wsu_ref, wsd_ref, ys_hbm,
                  oc_ref, ol_ref, buf_ref, sem, *, ctx_steps):
    tile, d = x_ref.shape
    nw = buf_ref.shape[2] // tile
    i = pl.program_id(0)
    slot = i % 2

    def row_copy(d_ref, s, t, k):
        return pltpu.make_async_copy(ys_hbm.at[pl.ds(pl.multiple_of(d_ref[k, t] * nw, nw), nw)],
                                     buf_ref.at[s, k, pl.ds(pl.multiple_of(t * nw, nw), nw)], sem.at[s])

    def gather(d_ref, s):
        def start(t, c):
            for k in range(TOP_K):
                row_copy(d_ref, s, t, k).start(priority=k % DMA_THREADS)
            return c
        lax.fori_loop(0, tile, start, 0)

    @pl.when(i == 0)
    def _():
        gather(dest_ref, 0)

    @pl.when(i + 1 < pl.num_programs(0))
    def _():
        gather(next_ref, 1 - slot)

    hb = _unpack_halves(_load_tiled(h_ref, tile)).astype(BF16)
    acc = _dot((jax.nn.silu(_dot(hb, wsg_ref[...])) * _dot(hb, wsu_ref[...])).astype(BF16), wsd_ref[...])

    def wait(t, c):
        for k in range(TOP_K):
            row_copy(dest_ref, slot, 0, k).wait()
        return c

    lax.fori_loop(0, tile, wait, 0)
    gates = gate_ref[...]
    moe = gates[:, 0:1] * _unpack_halves(_load_tiled(buf_ref.at[slot, 0], tile))
    for k in range(1, TOP_K):
        moe = moe + gates[:, k:k + 1] * _unpack_halves(_load_tiled(buf_ref.at[slot, k], tile))
    out = x_ref[...] + mod_ref[0, :, 5 * d:6 * d] * _rms(moe + acc, g_ref[...])

    @pl.when(i < ctx_steps)
    def _():
        oc_ref[...] = out

    @pl.when(i >= ctx_steps)
    def _():
        ol_ref[...] = out


def _combine(dest, gates, h2p, x1, mod, g_post, wsg, wsu, wsd, ys, row_map, t_ctx):
    t, d = x1.shape
    nc = d // LANES
    nw = nc // 2
    tile = COMBINE_TILE
    per_tok = TOK_TILE // tile
    n_steps = t // tile
    tok = lambda n: pl.BlockSpec((tile, n), lambda i: (i, 0))
    full = lambda a: pl.BlockSpec(a.shape, lambda i: (0,) * a.ndim)
    ctx_steps = t_ctx // tile
    return pl.pallas_call(
        functools.partial(_combine_body, ctx_steps=ctx_steps),
        name="moe_combine",
        grid=(n_steps,),
        in_specs=[pl.BlockSpec((TOP_K, tile), lambda i: (0, i), memory_space=pltpu.SMEM),
                  pl.BlockSpec((TOP_K, tile), lambda i: (0, jnp.minimum(i + 1, n_steps - 1)),
                               memory_space=pltpu.SMEM),
                  tok(TOP_K), pl.BlockSpec((tile * nw, LANES), lambda i: (i, 0)), tok(d),
                  pl.BlockSpec((1, 1, mod.shape[2]), lambda i: (row_map(i // per_tok), 0, 0)),
                  full(g_post), full(wsg), full(wsu), full(wsd),
                  pl.BlockSpec(memory_space=pl.ANY)],
        out_specs=[pl.BlockSpec((tile, d), lambda i: (jnp.minimum(i, ctx_steps - 1), 0)),
                   pl.BlockSpec((tile, d), lambda i: (jnp.maximum(i - ctx_steps, 0), 0))],
        out_shape=[jax.ShapeDtypeStruct((t_ctx, d), F32), jax.ShapeDtypeStruct((t - t_ctx, d), F32)],
        scratch_shapes=[pltpu.VMEM((2, TOP_K, tile * nw, LANES), U32), pltpu.SemaphoreType.DMA((2,))],
        compiler_params=_cparams("arbitrary"),
    )(dest, dest, gates, h2p, x1, mod, g_post, wsg, wsu, wsd, ys)


def _moe(logits, h2p, x1, mod, p, row_map, t_ctx):
    t, d = x1.shape
    idx, gate, pos, cnt = _route(logits, p['b_router'])
    counts = cnt[:, 0].astype(I32)
    padded = (counts + MOE_ROWS - 1) // MOE_ROWS * MOE_ROWS
    ends = jnp.cumsum(padded)
    starts = ends - padded
    n_blocks = (t * TOP_K + N_EXPERTS * (MOE_ROWS - 1)) // MOE_ROWS + 1
    n_rows = n_blocks * MOE_ROWS
    blk_start = jnp.arange(n_blocks, dtype=I32) * MOE_ROWS
    blk_exp = jnp.minimum(jnp.sum((ends[None, :] <= blk_start[:, None]).astype(I32), axis=1), N_EXPERTS - 1)
    pad_hi = ends.at[N_EXPERTS - 1].set(n_rows)
    dest = _dest(starts, idx, pos)
    xs = _dispatch(starts + counts, pad_hi, dest, h2p, n_rows)
    ys = _experts(blk_exp, xs, p['w_e_gate'], p['w_e_up'], p['w_e_down'], p['layer'])
    return _combine(dest, gate.T, h2p, x1, mod, p['g_post_ffn'], p['w_s_gate'], p['w_s_up'], p['w_s_down'],
                    ys, row_map, t_ctx)


def kernel(x_prompt, x_sample, cache_k, cache_v, state_ssm_re, state_ssm_im, c, c_ctx,
           g_pre_mix, g_post_mix, g_pre_ffn, g_post_ffn, w_ada, b_ada, w_in, w_out,
           sgu_g, w_sp, b_sp, rpb, ssm_a_re, ssm_a_im, ssm_log_dt, ssm_b_re, ssm_b_im,
           ssm_c_re, ssm_c_im, ssm_d, w_glu, b_glu, w_router, b_router,
           w_e_gate, w_e_up, w_e_down, w_s_gate, w_s_up, w_s_down):
    n_pb, p_seq, d = x_prompt.shape
    n_sb, s_seq, _ = x_sample.shape
    depth = w_in.shape[0]
    d_a = sgu_g.shape[1]
    d_c = w_glu.shape[1]
    d_b = d - d_a - d_c
    n_heads = d_b // HD_B
    n_g = d_c // SSM_CH
    t_p = n_pb * p_seq
    assert p_seq % CHUNK == 0 and t_p % TOK_TILE == 0 and s_seq % TOK_TILE == 0 and t_p % s_seq == 0
    assert s_seq % (NAT_ROWS * GRID_W) == 0 and s_seq // GRID_W >= NA_WIN_R and d_c == 2 * LANES

    mod_rows = -(-(n_sb + 1) // 8) * 8
    c_all = jnp.zeros((mod_rows, d), F32).at[:n_sb].set(c).at[n_sb].set(c_ctx)
    mod_all = _ada(c_all, w_ada, b_ada)
    p_tiles = t_p // TOK_TILE
    s_tiles = s_seq // TOK_TILE

    def row_map(i):
        return jnp.where(i < p_tiles, n_sb, (i - p_tiles) // s_tiles)

    xc, xl = x_prompt.reshape(t_p, d), x_sample.reshape(n_sb * s_seq, d)
    s_m, s_t, s_w, a16, dvec = jax.vmap(_s5_weights)(ssm_a_re, ssm_a_im, ssm_log_dt, ssm_b_re, ssm_b_im,
                                                     ssm_c_re, ssm_c_im, ssm_d)

    def expand(s, r2w, c2w):
        full = _expand_block_diag(s.reshape((-1,) + s.shape[2:]), r2w, c2w)
        return full.reshape(s.shape[:2] + full.shape[1:])

    s5w = (expand(s_m, SSM_CH, SSM_P), expand(s_t, SSM_CH, SSM_CH), expand(s_w, SSM_P, SSM_CH), a16, dvec)
    nat_bias_all = jax.vmap(_nat_bias)(rpb)
    new_k, new_v, new_re, new_im = [], [], [], []
    for l in range(depth):
        mod = mod_all[l][:, None, :]
        row = lambda a: a[l][None, :].astype(F32)
        pa, q, k32, v32, kb, vb, *pc = _premix(xc, xl, mod, row(g_pre_mix), w_in[l].astype(BF16), row_map,
                                               d_a, d_b, d_c)
        bias_a = jnp.repeat(b_sp[l].T.astype(F32), d_a // NH_A, axis=1)
        ya = _chunk_mlp(pa, row(sgu_g), w_sp[l].reshape(NH_A * CHUNK, CHUNK).astype(BF16), bias_a)
        yb_p = _ctx_attn(q, k32, v32, n_pb, p_seq)
        ck = cache_k[:, l].reshape(n_sb, -1, d_b).astype(BF16)
        cv = cache_v[:, l].reshape(n_sb, -1, d_b).astype(BF16)
        yb_s = _nat_attn(q, kb, vb, ck, cv, nat_bias_all[l], n_sb, s_seq, t_p)
        c_p = t_p // S5_CHUNK
        yc_p, fin_p = _s5([h[:c_p] for h in pc], n_pb, p_seq, s5w, l, jnp.zeros((4, n_pb, n_g * SSM_P), F32))
        sre = state_ssm_re[:, l].astype(F32).reshape(n_sb, 2, n_g * SSM_P)
        sim = state_ssm_im[:, l].astype(F32).reshape(n_sb, 2, n_g * SSM_P)
        h0_s = jnp.stack([sre[:, 0], sim[:, 0], sre[:, 1], sim[:, 1]])
        yc_s, _ = _s5([h[c_p:] for h in pc], n_sb, s_seq, s5w, l, h0_s)
        wr = w_router[l].astype(F32).T
        wr_hi = wr.astype(BF16)
        wr_lo = (wr - wr_hi.astype(F32)).astype(BF16)
        x1, logits, h2p = _post(xc, xl, ya, (yb_p, yb_s), (yc_p[0], yc_s[0]), (yc_p[1], yc_s[1]),
                                mod, row(g_post_mix), row(g_pre_ffn), w_glu[l].astype(BF16), row(b_glu),
                                w_out[l].astype(BF16), wr_hi, wr_lo, row_map)
        moe_p = {
            'b_router': b_router[l].astype(F32)[:, None],
            'w_e_gate': w_e_gate, 'w_e_up': w_e_up, 'w_e_down': w_e_down, 'layer': l,
            'w_s_gate': w_s_gate[l].astype(BF16), 'w_s_up': w_s_up[l].astype(BF16),
            'w_s_down': w_s_down[l].astype(BF16), 'g_post_ffn': row(g_post_ffn),
        }
        xc, xl = _moe(logits, h2p, x1, mod, moe_p, row_map, t_p)
        new_k.append(k32[:t_p].reshape(n_pb, p_seq, n_heads, HD_B))
        new_v.append(v32[:t_p].reshape(n_pb, p_seq, n_heads, HD_B))
        fin_p = fin_p.reshape(4, n_pb, n_g, SSM_P)
        new_re.append(jnp.stack([fin_p[0], fin_p[2]], axis=1))
        new_im.append(jnp.stack([fin_p[1], fin_p[3]], axis=1))
    return (xc.reshape(n_pb, p_seq, d), xl.reshape(n_sb, s_seq, d),
            jnp.stack(new_k, axis=1), jnp.stack(new_v, axis=1),
            jnp.stack(new_re, axis=1), jnp.stack(new_im, axis=1))
```

```python
import functools
import math

import jax
import jax.numpy as jnp
from jax import lax
from jax.experimental import pallas as pl
from jax.experimental.pallas import tpu as pltpu

F32 = jnp.float32
BF16 = jnp.bfloat16
I32 = jnp.int32

GRID_W = 64
EPS = 1e-6
NH_A = 4
CHUNK = 128
HD_B = 64
NA_WIN_R = 8
NA_WIN_C = 16
ATTN_SCALE = HD_B ** -0.5
SSM_CH = 16
SSM_P = 64
N_EXPERTS = 64
TOP_K = 8
N_EXP_GROUPS = 8
TOPK_GROUPS = 4
ROUTE_SCALE = 2.5

LANES = 128
TOK_TILE = 512
S5_CHUNK = 16
S5_OUT_STEPS = 4
MOE_ROWS = 1024
COMBINE_TILE = 256
DISPATCH_TILE = 1024
ZERO_RUN = 64
NAT_ROWS = 8
DMA_THREADS = 2
NEG_BIG = -1e30
VMEM_LIMIT = 48 * 1024 * 1024


def _cparams(*sem):
    return pltpu.CompilerParams(dimension_semantics=sem, vmem_limit_bytes=VMEM_LIMIT)


def _dot(a, b):
    return jnp.dot(a, b, preferred_element_type=F32)


def _dot_nt(a, b):
    return lax.dot_general(a, b, (((1,), (1,)), ((), ())), preferred_element_type=F32)


def _rms(x, g):
    return x * lax.rsqrt(jnp.mean(x * x, axis=-1, keepdims=True) + EPS) * g


def _load_tiled(ref, n_rows):
    nc = ref.shape[0] // n_rows
    return jnp.concatenate([ref[pl.ds(j, n_rows, stride=nc), :] for j in range(nc)], axis=1)


def _store_tiled(ref, val):
    n_rows = val.shape[0]
    nc = ref.shape[0] // n_rows
    for j in range(nc):
        ref[pl.ds(j, n_rows, stride=nc), :] = val[:, j * LANES:(j + 1) * LANES]


U32 = jnp.uint32


def _pack_halves(x):
    half = x.shape[1] // 2
    lo = lax.bitcast_convert_type(x[:, :half].astype(BF16).astype(F32), U32)
    hi = lax.bitcast_convert_type(x[:, half:].astype(BF16).astype(F32), U32)
    return (lo >> 16) | hi


def _unpack_halves(w):
    lo = lax.bitcast_convert_type(w << 16, F32)
    hi = lax.bitcast_convert_type(w & jnp.uint32(0xFFFF0000), F32)
    return jnp.concatenate([lo, hi], axis=1)


def _ada_body(c_ref, w_ref, b_ref, o_ref):
    s = jax.nn.silu(c_ref[...]).astype(BF16)
    o_ref[0] = _dot(s, w_ref[0].astype(BF16)) + b_ref[0]


def _ada(c_all, w_ada, b_ada):
    n_layers, d, n = w_ada.shape
    rows = c_all.shape[0]
    tn = 1536
    return pl.pallas_call(
        _ada_body,
        name="ada",
        grid=(n_layers, n // tn),
        in_specs=[pl.BlockSpec((rows, d), lambda l, j: (0, 0)),
                  pl.BlockSpec((1, d, tn), lambda l, j: (l, 0, j)),
                  pl.BlockSpec((1, 1, tn), lambda l, j: (l, 0, j))],
        out_specs=pl.BlockSpec((1, rows, tn), lambda l, j: (l, 0, j)),
        out_shape=jax.ShapeDtypeStruct((n_layers, rows, n), F32),
        compiler_params=_cparams("arbitrary", "arbitrary"),
    )(c_all, w_ada, b_ada.reshape(n_layers, 1, n))


def _stream_tile(xc_ref, xl_ref, ctx_tiles):
    return jnp.where(pl.program_id(0) < ctx_tiles, xc_ref[...], xl_ref[...])


def _stream_specs(tm, d, ctx_tiles):
    return [pl.BlockSpec((tm, d), lambda i: (jnp.minimum(i, ctx_tiles - 1), 0)),
            pl.BlockSpec((tm, d), lambda i: (jnp.maximum(i - ctx_tiles, 0), 0))]


def _premix_body(xc_ref, xl_ref, mod_ref, g_ref, w_ref, pa_ref, q_ref, k_ref, v_ref, kb_ref, vb_ref, pc0_ref,
                 pc1_ref, fold_ref, *, ctx_tiles):
    x = _stream_tile(xc_ref, xl_ref, ctx_tiles)
    d = x.shape[1]
    h = _rms(x, g_ref[...]) * (1 + mod_ref[0, :, d:2 * d]) + mod_ref[0, :, 0:d]
    p = _dot(h.astype(BF16), w_ref[...])
    d_a2 = pa_ref.shape[1]
    d_b = q_ref.shape[1]
    o = d_a2
    pa_ref[...] = p[:, 0:o]
    q_ref[...] = p[:, o:o + d_b].astype(BF16)
    k = p[:, o + d_b:o + 2 * d_b]
    v = p[:, o + 2 * d_b:o + 3 * d_b]

    @pl.when(pl.program_id(0) < ctx_tiles)
    def _():
        k_ref[...] = k
        v_ref[...] = v

    kb_ref[...] = k.astype(BF16)
    vb_ref[...] = v.astype(BF16)
    for h, pc_ref in enumerate((pc0_ref, pc1_ref)):
        fold_ref[...] = p[:, o + 3 * d_b + h * LANES:o + 3 * d_b + (h + 1) * LANES]
        _fold_chunks(pc_ref, fold_ref)


def _fold_chunks(dst_ref, src_ref):
    n = dst_ref.shape[0]
    for i in range(S5_CHUNK):
        dst_ref[:, i * LANES:(i + 1) * LANES] = src_ref[pl.ds(i, n, stride=S5_CHUNK), :]


def _unfold_chunks(dst_ref, src_ref):
    n = src_ref.shape[0]
    for i in range(S5_CHUNK):
        dst_ref[pl.ds(i, n, stride=S5_CHUNK), :] = src_ref[:, i * LANES:(i + 1) * LANES]


def _premix(xc, xl, mod, g, w_in_b, row_map, d_a, d_b, d_c):
    t_ctx, d = xc.shape
    t = t_ctx + xl.shape[0]
    tm = TOK_TILE
    d_in = w_in_b.shape[1]
    ctx_tiles = t_ctx // tm
    tok = lambda n: pl.BlockSpec((tm, n), lambda i: (i, 0))
    ctx = pl.BlockSpec((tm, d_b), lambda i: (jnp.minimum(i, ctx_tiles - 1), 0))
    fold = pl.BlockSpec((tm // S5_CHUNK, S5_CHUNK * LANES), lambda i: (i, 0))
    return pl.pallas_call(
        functools.partial(_premix_body, ctx_tiles=ctx_tiles),
        name="premix",
        grid=(t // tm,),
        in_specs=_stream_specs(tm, d, ctx_tiles) + [
            pl.BlockSpec((1, 1, mod.shape[2]), lambda i: (row_map(i), 0, 0)),
            pl.BlockSpec((1, d), lambda i: (0, 0)),
            pl.BlockSpec((d, d_in), lambda i: (0, 0))],
        out_specs=[tok(2 * d_a), tok(d_b), ctx, ctx, tok(d_b), tok(d_b), fold, fold],
        out_shape=[jax.ShapeDtypeStruct((t, 2 * d_a), F32),
                   jax.ShapeDtypeStruct((t, d_b), BF16),
                   jax.ShapeDtypeStruct((t_ctx, d_b), F32),
                   jax.ShapeDtypeStruct((t_ctx, d_b), F32),
                   jax.ShapeDtypeStruct((t, d_b), BF16),
                   jax.ShapeDtypeStruct((t, d_b), BF16)]
        + [jax.ShapeDtypeStruct((t // S5_CHUNK, S5_CHUNK * LANES), F32)] * 2,
        scratch_shapes=[pltpu.VMEM((tm, LANES), F32)],
        compiler_params=_cparams("arbitrary"),
    )(xc, xl, mod, g, w_in_b)


def _chunk_body(pa_ref, g_ref, w_ref, b_ref, o_ref):
    z = jax.nn.gelu(pa_ref[...])
    d_a = o_ref.shape[1]
    hd = d_a // NH_A
    u = z[:, :d_a]
    v = z[:, d_a:]
    mu = jnp.mean(v, axis=-1, keepdims=True)
    var = jnp.mean(jnp.square(v - mu), axis=-1, keepdims=True)
    vb = ((v - mu) * lax.rsqrt(var + EPS) * g_ref[...]).astype(BF16)
    head = lax.broadcasted_iota(I32, (CHUNK, d_a), 1) // hd
    for ch in range(pa_ref.shape[0] // CHUNK):
        rows = slice(ch * CHUNK, (ch + 1) * CHUNK)
        sf = _dot(w_ref[...], vb[rows])
        s = b_ref[...]
        for h in range(NH_A):
            s = s + jnp.where(head == h, sf[h * CHUNK:(h + 1) * CHUNK], 0.0)
        o_ref[rows, :] = (u[rows] * s).astype(BF16)


def _chunk_mlp(pa, sgu_g, w_sp_b, bias):
    t, d2 = pa.shape
    d_a = d2 // 2
    tm = TOK_TILE
    return pl.pallas_call(
        _chunk_body,
        name="chunk_mlp",
        grid=(t // tm,),
        in_specs=[pl.BlockSpec((tm, d2), lambda i: (i, 0)),
                  pl.BlockSpec((1, d_a), lambda i: (0, 0)),
                  pl.BlockSpec(w_sp_b.shape, lambda i: (0, 0)),
                  pl.BlockSpec(bias.shape, lambda i: (0, 0))],
        out_specs=pl.BlockSpec((tm, d_a), lambda i: (i, 0)),
        out_shape=jax.ShapeDtypeStruct((t, d_a), BF16),
        compiler_params=_cparams("arbitrary"),
    )(pa, sgu_g, w_sp_b, bias)


def _stack_pair(qg):
    lane = lax.broadcasted_iota(I32, qg.shape, 1)
    zero = jnp.zeros_like(qg)
    return jnp.concatenate([jnp.where(lane < HD_B, qg, zero), jnp.where(lane >= HD_B, qg, zero)], axis=0)


def _unstack_pair(o2):
    n = o2.shape[0] // 2
    lane = lax.broadcasted_iota(I32, (n, o2.shape[1]), 1)
    return jnp.where(lane < HD_B, o2[:n], o2[n:])


def _ctx_attn_body(q_ref, k_ref, v_ref, o_ref):
    for g in range(q_ref.shape[1] // LANES):
        cols = slice(g * LANES, (g + 1) * LANES)
        q2 = _stack_pair(q_ref[:, cols])
        s = _dot_nt(q2, k_ref[:, cols].astype(BF16)) * ATTN_SCALE
        e = jnp.exp(s - jnp.max(s, axis=-1, keepdims=True))
        p = e / jnp.sum(e, axis=-1, keepdims=True)
        o2 = _dot(p.astype(BF16), v_ref[:, cols].astype(BF16))
        o_ref[:, cols] = _unstack_pair(o2).astype(BF16)


def _ctx_attn(q, k, v, n_batch, seq):
    d_b = q.shape[1]
    blk = pl.BlockSpec((seq, d_b), lambda b: (b, 0))
    return pl.pallas_call(
        _ctx_attn_body,
        name="ctx_attn",
        grid=(n_batch,),
        in_specs=[blk, blk, blk],
        out_specs=blk,
        out_shape=jax.ShapeDtypeStruct((n_batch * seq, d_b), BF16),
        compiler_params=_cparams("arbitrary"),
    )(q, k, v)


def _nat_body(q_ref, k_ref, v_ref, ck_ref, cv_ref, bias_ref, o_ref, *, rows):
    r0 = pl.program_id(1) * NAT_ROWS
    n_win = NA_WIN_R * GRID_W
    pr = 2 * GRID_W
    starts, cases = [], []
    for i in range(NAT_ROWS):
        rs = jnp.clip(r0 + i - NA_WIN_R // 2, 0, rows - NA_WIN_R)
        starts.append(pl.multiple_of(rs * GRID_W, GRID_W))
        cases.append(r0 + i - rs)
    for g in range(q_ref.shape[1] // LANES):
        cols = slice(g * LANES, (g + 1) * LANES)
        q2 = jnp.concatenate([_stack_pair(q_ref[i * GRID_W:(i + 1) * GRID_W, cols]) for i in range(NAT_ROWS)],
                             axis=0)
        s_ctx = _dot_nt(q2, ck_ref[0, :, cols]) * ATTN_SCALE
        m_ctx = jnp.max(s_ctx, axis=-1, keepdims=True)
        e_wins, invs, ms = [], [], []
        for i in range(NAT_ROWS):
            s_win = (_dot_nt(q2[i * pr:(i + 1) * pr], k_ref[pl.ds(starts[i], n_win), cols]) * ATTN_SCALE
                     + bias_ref[cases[i], g])
            m = jnp.maximum(jnp.max(s_win, axis=-1, keepdims=True), m_ctx[i * pr:(i + 1) * pr])
            e_wins.append(jnp.exp(s_win - m))
            ms.append(m)
        e_ctx = jnp.exp(s_ctx - jnp.concatenate(ms, axis=0))
        l_ctx = jnp.sum(e_ctx, axis=-1, keepdims=True)
        for i in range(NAT_ROWS):
            invs.append(1.0 / (jnp.sum(e_wins[i], axis=-1, keepdims=True) + l_ctx[i * pr:(i + 1) * pr]))
        o_ctx = _dot((e_ctx * jnp.concatenate(invs, axis=0)).astype(BF16), cv_ref[0, :, cols])
        for i in range(NAT_ROWS):
            o2 = _dot((e_wins[i] * invs[i]).astype(BF16), v_ref[pl.ds(starts[i], n_win), cols])
            o_ref[i * GRID_W:(i + 1) * GRID_W, cols] = _unstack_pair(o2 + o_ctx[i * pr:(i + 1) * pr]).astype(BF16)


def _nat_attn(q, kb, vb, ck, cv, bias, n_batch, seq, tok0):
    d_b = q.shape[1]
    rows = seq // GRID_W
    lc = ck.shape[1]
    blk = NAT_ROWS * GRID_W
    steps = seq // blk
    q0 = tok0 // blk
    i0 = tok0 // seq
    img = pl.BlockSpec((seq, d_b), lambda b, r: (i0 + b, 0))
    ctx = pl.BlockSpec((1, lc, d_b), lambda b, r: (b, 0, 0))
    return pl.pallas_call(
        functools.partial(_nat_body, rows=rows),
        name="nat_attn",
        grid=(n_batch, steps),
        in_specs=[pl.BlockSpec((blk, d_b), lambda b, r: (q0 + b * steps + r, 0)),
                  img, img, ctx, ctx,
                  pl.BlockSpec(bias.shape, lambda b, r: (0, 0, 0, 0))],
        out_specs=pl.BlockSpec((blk, d_b), lambda b, r: (b * steps + r, 0)),
        out_shape=jax.ShapeDtypeStruct((n_batch * seq, d_b), BF16),
        compiler_params=_cparams("arbitrary", "arbitrary"),
    )(q, kb, vb, ck, cv, bias)


def _nat_bias(rpb):
    n_heads = rpb.shape[0]
    cols = jnp.arange(GRID_W)
    col_start = jnp.clip(cols - NA_WIN_C // 2, 0, GRID_W - NA_WIN_C)
    j = jnp.arange(GRID_W)
    valid = (j[None, :] >= col_start[:, None]) & (j[None, :] < col_start[:, None] + NA_WIN_C)
    col_off = jnp.clip(j[None, :] - cols[:, None] + (NA_WIN_C - 1), 0, 2 * NA_WIN_C - 2)
    toe = jnp.where(valid[None, None], rpb.astype(F32)[:, :, col_off], NEG_BIG)
    cases = jnp.stack([toe[:, NA_WIN_R - 1 - delta:2 * NA_WIN_R - 1 - delta] for delta in range(NA_WIN_R)])
    return cases.transpose(0, 1, 3, 2, 4).reshape(NA_WIN_R, n_heads // 2, 2 * GRID_W, NA_WIN_R * GRID_W)


def _s5_weights(a_re, a_im, log_dt, b_re, b_im, c_re, c_im, ssm_d):
    n_g = a_re.shape[1]
    c = S5_CHUNK
    lam = lax.complex(a_re.astype(F32), a_im.astype(F32))
    ldt = lam * jnp.exp(log_dt.astype(F32))[..., None]
    lam_bar = jnp.exp(ldt)
    b_bar = ((lam_bar - 1) / lam)[..., None] * lax.complex(b_re.astype(F32), b_im.astype(F32))
    c_mat = lax.complex(c_re.astype(F32), c_im.astype(F32))
    pw = jnp.exp(ldt[None] * jnp.arange(c + 1, dtype=F32)[:, None, None, None])
    kern = jnp.real(jnp.einsum('dgcp,kdgp,dgpe->dgkce', c_mat, pw[:c], b_bar))
    i = jnp.arange(c)
    lag = i[None, :] - i[:, None]
    tf = jnp.where((lag >= 0)[None, :, :, None, None], kern[0][:, jnp.clip(lag, 0, c - 1)], 0.0)
    tb = jnp.where((lag <= 0)[None, :, :, None, None], kern[1][:, jnp.clip(-lag, 0, c - 1)], 0.0)
    t_mat = (tf + tb).transpose(0, 1, 4, 2, 3).reshape(n_g, c * SSM_CH, c * SSM_CH)
    mf = pw[:c][::-1, 0][:, :, :, None] * b_bar[0][None]
    mb = pw[:c, 1][:, :, :, None] * b_bar[1][None]
    mf = mf.transpose(1, 0, 3, 2).reshape(n_g, c * SSM_CH, SSM_P)
    mb = mb.transpose(1, 0, 3, 2).reshape(n_g, c * SSM_CH, SSM_P)
    mq = jnp.stack([jnp.real(mf), jnp.imag(mf), jnp.real(mb), jnp.imag(mb)], axis=2)
    zf = c_mat[0][:, None] * pw[1:c + 1, 0][:, :, None, :].transpose(1, 0, 2, 3)
    zb = c_mat[1][:, None] * pw[1:c + 1, 1][::-1][:, :, None, :].transpose(1, 0, 2, 3)
    zf = zf.transpose(0, 3, 1, 2).reshape(n_g, SSM_P, c * SSM_CH)
    zb = zb.transpose(0, 3, 1, 2).reshape(n_g, SSM_P, c * SSM_CH)
    wq = jnp.stack([jnp.real(zf), -jnp.imag(zf), jnp.real(zb), -jnp.imag(zb)], axis=1)
    gh = LANES // SSM_CH
    n_half = n_g // gh
    kd = c * SSM_CH

    def stack_rows(blocks, r1):
        b = blocks.astype(BF16).reshape(n_half, gh, r1, kd // r1, kd)
        return b.transpose(0, 2, 1, 3, 4).reshape(n_half, gh * kd, kd)

    s_m = stack_rows(mq.reshape(n_g, kd, 4 * SSM_P), c)
    s_t = stack_rows(t_mat, c)
    s_w = stack_rows(wq.reshape(n_g, 4 * SSM_P, kd), 4)
    a_c = pw[c]
    a16 = jnp.stack([jnp.real(a_c[0]), jnp.imag(a_c[0]), jnp.real(a_c[1]), jnp.imag(a_c[1])])
    a16 = a16.reshape(4, 1, n_half, gh * SSM_P).transpose(2, 0, 1, 3)
    dvec = ssm_d.astype(F32).reshape(n_half, 1, LANES)
    return s_m, s_t, s_w, a16, dvec


def _expand_body(s_ref, rep_ref, o_ref, *, r2w, c2w):
    tr, n = o_ref.shape[1], o_ref.shape[2]
    gh = n // s_ref.shape[2]
    full = _dot(s_ref[0], rep_ref[...])
    row = pl.program_id(1) * tr + lax.broadcasted_iota(I32, (tr, n), 0)
    col = lax.broadcasted_iota(I32, (tr, n), 1)
    o_ref[0] = jnp.where((row // r2w) % gh == (col // c2w) % gh, full, 0.0).astype(BF16)


def _expand_block_diag(s, r2w, c2w):
    m, n, kd = s.shape
    gh = n // kd
    rep = jnp.broadcast_to(jnp.eye(kd, dtype=BF16).reshape(kd, kd // c2w, 1, c2w), (kd, kd // c2w, gh, c2w))
    rep = rep.reshape(kd, n)
    tr = 256
    return pl.pallas_call(
        functools.partial(_expand_body, r2w=r2w, c2w=c2w),
        name="s5_expand",
        grid=(m, n // tr),
        in_specs=[pl.BlockSpec((1, tr, kd), lambda a, i: (a, i, 0)),
                  pl.BlockSpec((kd, n), lambda a, i: (0, 0))],
        out_specs=pl.BlockSpec((1, tr, n), lambda a, i: (a, i, 0)),
        out_shape=jax.ShapeDtypeStruct((m, n, n), BF16),
        compiler_params=_cparams("arbitrary", "arbitrary"),
    )(s, rep)


def _s5_state_body(u_ref, m_ref, fre_ref, fim_ref, bre_ref, bim_ref):
    nb, tn, kd = u_ref.shape
    w = fre_ref.shape[2]
    r = _dot(u_ref[...].reshape(nb * tn, kd).astype(BF16), m_ref[0, 0])
    for q, o_ref in enumerate((fre_ref, fim_ref, bre_ref, bim_ref)):
        for b in range(nb):
            o_ref[:, b, :] = r[b * tn:(b + 1) * tn, q * w:(q + 1) * w]


def _s5_rows(nb, n):
    return max(8, min(n, 256 // nb))


def _s5_states(u3, m_all, layer, half):
    nb, n, kd = u3.shape
    tn = _s5_rows(nb, n)
    w = m_all.shape[3] // 4
    out = pl.BlockSpec((tn, nb, w), lambda i: (i, 0, 0))
    return pl.pallas_call(
        _s5_state_body,
        name="s5_states",
        grid=(n // tn,),
        in_specs=[pl.BlockSpec((nb, tn, kd), lambda i: (0, i, 0)),
                  pl.BlockSpec((1, 1) + m_all.shape[2:], lambda i: (layer, half, 0, 0))],
        out_specs=[out] * 4,
        out_shape=[jax.ShapeDtypeStruct((n, nb, w), F32)] * 4,
        compiler_params=_cparams("arbitrary"),
    )(u3, m_all)


def _s5_scan_body(sfr_ref, sfi_ref, sbr_ref, sbi_ref, a_ref, h0_ref,
                  hfr_ref, hfi_ref, hbr_ref, hbi_ref, fin_ref, st_ref):
    @pl.when(pl.program_id(0) == 0)
    def _():
        st_ref[...] = h0_ref[...]

    ks = sfr_ref.shape[0]
    afr, afi, abr, abi = a_ref[0], a_ref[1], a_ref[2], a_ref[3]

    def step(s, carry):
        fr, fi, br, bi = carry
        sb = ks - 1 - s
        hfr_ref[s] = fr
        hfi_ref[s] = fi
        hbr_ref[sb] = br
        hbi_ref[sb] = bi
        nfr = afr * fr - afi * fi + sfr_ref[s]
        nfi = afr * fi + afi * fr + sfi_ref[s]
        nbr = abr * br - abi * bi + sbr_ref[sb]
        nbi = abr * bi + abi * br + sbi_ref[sb]
        return nfr, nfi, nbr, nbi

    carry = lax.fori_loop(0, ks, step, (st_ref[0], st_ref[1], st_ref[2], st_ref[3]))
    for q in range(4):
        st_ref[q] = carry[q]
        fin_ref[q] = carry[q]


def _s5_scan(s3, a16, h0):
    n_chunks, n_batch, width = s3[0].shape
    ks = min(n_chunks, 32)
    nb = n_chunks // ks
    fwd = pl.BlockSpec((ks, n_batch, width), lambda i: (i, 0, 0))
    bwd = pl.BlockSpec((ks, n_batch, width), lambda i: (nb - 1 - i, 0, 0))
    small = lambda shape: pl.BlockSpec(shape, lambda i: (0, 0, 0))
    outs = pl.pallas_call(
        _s5_scan_body,
        name="s5_scan",
        grid=(nb,),
        in_specs=[fwd, fwd, bwd, bwd, small(a16.shape), small(h0.shape)],
        out_specs=[fwd, fwd, bwd, bwd, small(h0.shape)],
        out_shape=[jax.ShapeDtypeStruct((n_chunks, n_batch, width), F32)] * 4
        + [jax.ShapeDtypeStruct(h0.shape, F32)],
        scratch_shapes=[pltpu.VMEM(h0.shape, F32)],
        compiler_params=_cparams("arbitrary"),
    )(*s3, a16, h0)
    return outs[:4], outs[4]


def _s5_out_body(u_ref, us_ref, hfr_ref, hfi_ref, hbr_ref, hbi_ref, t_ref, w_ref, d_ref, y_ref):
    nb, tn, kd = u_ref.shape
    rows = nb * tn
    wo = us_ref.shape[2]
    y = _dot(u_ref[...].reshape(rows, kd).astype(BF16), t_ref[0, 0])
    w = hfr_ref.shape[2]
    for q, h_ref in enumerate((hfr_ref, hfi_ref, hbr_ref, hbi_ref)):
        hq = jnp.concatenate([h_ref[:, b, :] for b in range(nb)], axis=0)
        y = y + _dot(hq.astype(BF16), w_ref[0, 0, q * w:(q + 1) * w, :])
    skip = jnp.concatenate([d_ref[...]] * (wo // LANES), axis=1) * us_ref[...].reshape(rows, wo)
    y_ref[...] = jax.nn.gelu(y + skip).reshape(nb, tn, wo)


def _s5_out(u3, h4, t_all, w_all, dvec, layer, half):
    nb, n, kd = u3.shape
    tn = _s5_rows(nb, n)
    wo = S5_OUT_STEPS * LANES
    w = h4[0].shape[2]
    hb = pl.BlockSpec((tn, nb, w), lambda i, j: (i, 0, 0))
    mat = lambda a: pl.BlockSpec((1, 1, a.shape[2], wo), lambda i, j: (layer, half, 0, j))
    return pl.pallas_call(
        _s5_out_body,
        name="s5_out",
        grid=(n // tn, kd // wo),
        in_specs=[pl.BlockSpec((nb, tn, kd), lambda i, j: (0, i, 0)),
                  pl.BlockSpec((nb, tn, wo), lambda i, j: (0, i, j)),
                  hb, hb, hb, hb, mat(t_all), mat(w_all),
                  pl.BlockSpec(dvec.shape, lambda i, j: (0, 0))],
        out_specs=pl.BlockSpec((nb, tn, wo), lambda i, j: (0, i, j)),
        out_shape=jax.ShapeDtypeStruct(u3.shape, F32),
        compiler_params=_cparams("arbitrary", "arbitrary"),
    )(u3, u3, *h4, t_all, w_all, dvec)


def _s5(pc_halves, n_batch, seq, weights, layer, h0):
    m_all, t_all, w_all, a16, dvec = weights
    n_chunks = seq // S5_CHUNK
    ys, fins = [], []
    for h, pc in enumerate(pc_halves):
        w = m_all.shape[3] // 4
        u3 = pc.reshape(n_batch, n_chunks, S5_CHUNK * LANES)
        s4 = _s5_states(u3, m_all, layer, h)
        h4, fin = _s5_scan(s4, a16[layer, h], h0[:, :, h * w:(h + 1) * w])
        y = _s5_out(u3, h4, t_all, w_all, dvec[layer, h], layer, h)
        ys.append(y.reshape(n_batch * n_chunks, S5_CHUNK * LANES))
        fins.append(fin)
    return ys, jnp.concatenate(fins, axis=2)


def _post_body(xc_ref, xl_ref, ya_ref, ybc_ref, ybl_ref, yc0c_ref, yc0l_ref, yc1c_ref, yc1l_ref, mod_ref, gpost_ref,
               gffn_ref, wglu_ref, bglu_ref, wo_ref, whi_ref, wlo_ref, x1_ref, logit_ref, h2p_ref, unfold_ref, *,
               ctx_tiles):
    x = _stream_tile(xc_ref, xl_ref, ctx_tiles)
    d = x.shape[1]
    d_a = ya_ref.shape[1]
    d_b = ybc_ref.shape[1]
    in_ctx = pl.program_id(0) < ctx_tiles
    for h, (c_ref, l_ref) in enumerate(((yc0c_ref, yc0l_ref), (yc1c_ref, yc1l_ref))):
        @pl.when(in_ctx)
        def _():
            _unfold_chunks(unfold_ref.at[h], c_ref)

        @pl.when(jnp.logical_not(in_ctx))
        def _():
            _unfold_chunks(unfold_ref.at[h], l_ref)
    y = jnp.concatenate([unfold_ref[0], unfold_ref[1]], axis=1)
    glu = y * jax.nn.sigmoid(_dot(y.astype(BF16), wglu_ref[...]) + bglu_ref[...])
    yb = _stream_tile(ybc_ref, ybl_ref, ctx_tiles)
    mixed = (_dot(ya_ref[...], wo_ref[0:d_a, :]) + _dot(yb, wo_ref[d_a:d_a + d_b, :])
             + _dot(glu.astype(BF16), wo_ref[d_a + d_b:, :]))
    x1 = x + mod_ref[0, :, 2 * d:3 * d] * _rms(mixed, gpost_ref[...])
    x1_ref[...] = x1
    h2 = _rms(x1, gffn_ref[...]) * (1 + mod_ref[0, :, 4 * d:5 * d]) + mod_ref[0, :, 3 * d:4 * d]
    logit_ref[...] = _router_logits(h2, whi_ref, wlo_ref)
    _store_tiled(h2p_ref, _pack_halves(h2))


def _post(xc, xl, ya, yb, yc0, yc1, mod, g_post, g_ffn, w_glu_b, b_glu, w_out_b, w_hi, w_lo, row_map):
    t_ctx, d = xc.shape
    t = t_ctx + xl.shape[0]
    tm = TOK_TILE
    nc = d // LANES
    ctx_tiles = t_ctx // tm
    tok = lambda n: pl.BlockSpec((tm, n), lambda i: (i, 0))
    full = lambda a: pl.BlockSpec(a.shape, lambda i: (0,) * a.ndim)
    fold = _stream_specs(tm // S5_CHUNK, S5_CHUNK * LANES, ctx_tiles)
    return pl.pallas_call(
        functools.partial(_post_body, ctx_tiles=ctx_tiles),
        name="post_mix",
        grid=(t // tm,),
        in_specs=_stream_specs(tm, d, ctx_tiles) + [tok(ya.shape[1])]
        + _stream_specs(tm, yb[0].shape[1], ctx_tiles) + fold + fold + [
            pl.BlockSpec((1, 1, mod.shape[2]), lambda i: (row_map(i), 0, 0)),
            full(g_post), full(g_ffn), full(w_glu_b), full(b_glu), full(w_out_b), full(w_hi), full(w_lo)],
        out_specs=[tok(d), pl.BlockSpec((N_EXPERTS, tm), lambda i: (0, i)),
                   pl.BlockSpec((tm * nc // 2, LANES), lambda i: (i, 0))],
        out_shape=[jax.ShapeDtypeStruct((t, d), F32), jax.ShapeDtypeStruct((N_EXPERTS, t), F32),
                   jax.ShapeDtypeStruct((t * nc // 2, LANES), U32)],
        scratch_shapes=[pltpu.VMEM((2, tm, LANES), F32)],
        compiler_params=_cparams("arbitrary"),
    )(xc, xl, ya, *yb, *yc0, *yc1, mod, g_post, g_ffn, w_glu_b, b_glu, w_out_b, w_hi, w_lo)


def _router_logits(h, whi_ref, wlo_ref):
    hi = h.astype(BF16)
    lo = (h - hi.astype(F32)).astype(BF16)
    return _dot_nt(whi_ref[...], hi) + (_dot_nt(whi_ref[...], lo) + _dot_nt(wlo_ref[...], hi))


def _route_body(logit_ref, b_ref, idx_ref, gate_ref, pos_ref, cnt_ref, carry_ref):
    @pl.when(pl.program_id(0) == 0)
    def _():
        carry_ref[...] = jnp.zeros_like(carry_ref)

    tm = idx_ref.shape[1]
    scores = jax.nn.sigmoid(logit_ref[...])
    sel = scores + b_ref[...]
    gsz = N_EXPERTS // N_EXP_GROUPS
    within = lax.broadcasted_iota(I32, (gsz, tm), 0).astype(F32)
    grp = []
    for g in range(N_EXP_GROUPS):
        blk = sel[g * gsz:(g + 1) * gsz]
        m1 = jnp.max(blk, axis=0, keepdims=True)
        first = jnp.min(jnp.where(blk == m1, within, float(gsz)), axis=0, keepdims=True)
        m2 = jnp.max(jnp.where(within == first, -jnp.inf, blk), axis=0, keepdims=True)
        grp.append(m1 + m2)
    blocks = []
    for g in range(N_EXP_GROUPS):
        ahead = jnp.zeros((1, tm), F32)
        for o in range(N_EXP_GROUPS):
            if o == g:
                continue
            beats = (grp[o] >= grp[g]) if o < g else (grp[o] > grp[g])
            ahead = ahead + jnp.where(beats, 1.0, 0.0)
        ahead = jnp.broadcast_to(ahead, (gsz, tm))
        blocks.append(jnp.where(ahead < TOPK_GROUPS, sel[g * gsz:(g + 1) * gsz], -jnp.inf))
    v = jnp.concatenate(blocks, axis=0)
    eidx = lax.broadcasted_iota(I32, (N_EXPERTS, tm), 0)
    rnk = jnp.zeros((N_EXPERTS, tm), F32)
    for e in range(N_EXPERTS):
        row = v[e:e + 1]
        rnk = rnk + jnp.where(eidx > e, jnp.where(row >= v, 1.0, 0.0), jnp.where(row > v, 1.0, 0.0))
    chosen = rnk < TOP_K
    w = jnp.where(chosen, scores, 0.0)
    wn = w / jnp.sum(w, axis=0, keepdims=True) * ROUTE_SCALE
    tri = (lax.broadcasted_iota(I32, (tm, tm), 0) < lax.broadcasted_iota(I32, (tm, tm), 1))
    chosen_f = jnp.where(chosen, 1.0, 0.0)
    prefix = _dot(chosen_f.astype(BF16), jnp.where(tri, 1.0, 0.0).astype(BF16)) + carry_ref[:, 0:1]
    total = carry_ref[...] + jnp.sum(chosen_f, axis=1, keepdims=True)
    carry_ref[...] = total
    cnt_ref[...] = total
    eidx_f = eidx.astype(F32)
    for k in range(TOP_K):
        one = rnk == k
        idx_ref[k:k + 1, :] = jnp.sum(jnp.where(one, eidx_f, 0.0), axis=0, keepdims=True).astype(I32)
        gate_ref[k:k + 1, :] = jnp.sum(jnp.where(one, wn, 0.0), axis=0, keepdims=True)
        pos_ref[k:k + 1, :] = jnp.sum(jnp.where(one, prefix, 0.0), axis=0, keepdims=True).astype(I32)


def _route(logits, b_router):
    t = logits.shape[1]
    tm = TOK_TILE
    out = pl.BlockSpec((TOP_K, tm), lambda i: (0, i))
    full = lambda a: pl.BlockSpec(a.shape, lambda i: (0, 0))
    return pl.pallas_call(
        _route_body,
        name="route",
        grid=(t // tm,),
        in_specs=[pl.BlockSpec((N_EXPERTS, tm), lambda i: (0, i)), full(b_router)],
        out_specs=[out, out, out, pl.BlockSpec((N_EXPERTS, LANES), lambda i: (0, 0))],
        out_shape=[jax.ShapeDtypeStruct((TOP_K, t), I32), jax.ShapeDtypeStruct((TOP_K, t), F32),
                   jax.ShapeDtypeStruct((TOP_K, t), I32), jax.ShapeDtypeStruct((N_EXPERTS, LANES), F32)],
        scratch_shapes=[pltpu.VMEM((N_EXPERTS, LANES), F32)],
        compiler_params=_cparams("arbitrary"),
    )(logits, b_router)


def _dest_body(starts_ref, idx_ref, pos_ref, dest_ref):
    idx = idx_ref[...]
    acc = pos_ref[...]
    for e in range(N_EXPERTS):
        acc = acc + jnp.where(idx == e, starts_ref[e], 0)
    dest_ref[...] = acc


def _dest(starts, idx, pos):
    t = idx.shape[1]
    tile = math.gcd(t, 4096)
    blk = pl.BlockSpec((TOP_K, tile), lambda i, st: (0, i))
    return pl.pallas_call(
        _dest_body,
        name="moe_dest",
        grid_spec=pltpu.PrefetchScalarGridSpec(num_scalar_prefetch=1, grid=(t // tile,),
                                               in_specs=[blk, blk], out_specs=blk),
        out_shape=jax.ShapeDtypeStruct(idx.shape, I32),
        compiler_params=_cparams("arbitrary"),
    )(starts, idx, pos)


def _dispatch_body(pad_lo_ref, pad_hi_ref, dest_ref, h_ref, xs_ref, zero_ref, sem, zsem, *, n_pad_rows):
    tile = dest_ref.shape[1]
    nw = h_ref.shape[0] // tile

    def token(ref, r):
        return ref.at[pl.ds(pl.multiple_of(r * nw, nw), nw)]

    def zero_copy(r, n):
        return pltpu.make_async_copy(zero_ref.at[pl.ds(0, n * nw)],
                                     xs_ref.at[pl.ds(pl.multiple_of(r * nw, nw), n * nw)], zsem)

    @pl.when(pl.program_id(0) == 0)
    def _():
        zero_ref[...] = jnp.zeros_like(zero_ref)

        def per_expert(e, c):
            lo = pad_lo_ref[e]
            runs = (pad_hi_ref[e] - lo) // ZERO_RUN

            def run(j, c2):
                zero_copy(lo + j * ZERO_RUN, ZERO_RUN).start()
                return c2

            def one(r, c2):
                zero_copy(r, 1).start()
                return c2

            c = lax.fori_loop(0, runs, run, c)
            return lax.fori_loop(lo + runs * ZERO_RUN, pad_hi_ref[e], one, c)

        lax.fori_loop(0, N_EXPERTS, per_expert, 0)

    def row_copy(t, k):
        return pltpu.make_async_copy(token(h_ref, t), token(xs_ref, dest_ref[k, t]), sem)

    def start(t, c):
        for k in range(TOP_K):
            row_copy(t, k).start(priority=k % DMA_THREADS)
        return c

    def wait(t, c):
        for k in range(TOP_K):
            row_copy(t, k).wait()
        return c

    lax.fori_loop(0, tile, start, 0)
    lax.fori_loop(0, tile, wait, 0)

    @pl.when(pl.program_id(0) == 0)
    def _():
        def one(r, c):
            zero_copy(0, ZERO_RUN).wait()
            return c
        lax.fori_loop(0, n_pad_rows // ZERO_RUN, one, 0)


def _dispatch(pad_lo, pad_hi, dest, h2p, n_rows):
    t = dest.shape[1]
    nw = h2p.shape[0] // t
    tile = DISPATCH_TILE
    grid_spec = pltpu.PrefetchScalarGridSpec(
        num_scalar_prefetch=2,
        grid=(t // tile,),
        in_specs=[pl.BlockSpec((TOP_K, tile), lambda i, lo, hi: (0, i), memory_space=pltpu.SMEM),
                  pl.BlockSpec((tile * nw, LANES), lambda i, lo, hi: (i, 0))],
        out_specs=pl.BlockSpec(memory_space=pl.ANY),
        scratch_shapes=[pltpu.VMEM((ZERO_RUN * nw, LANES), U32), pltpu.SemaphoreType.DMA(()),
                        pltpu.SemaphoreType.DMA(())],
    )
    assert (n_rows - t * TOP_K) % ZERO_RUN == 0
    return pl.pallas_call(
        functools.partial(_dispatch_body, n_pad_rows=n_rows - t * TOP_K),
        name="moe_dispatch",
        grid_spec=grid_spec,
        out_shape=jax.ShapeDtypeStruct((n_rows * nw, LANES), U32),
        compiler_params=_cparams("arbitrary"),
    )(pad_lo, pad_hi, dest, h2p)


def _experts_body(be_ref, x_ref, wg_ref, wu_ref, wd_ref, y_ref, wgb_ref, wub_ref, wdb_ref):
    i = pl.program_id(0)

    @pl.when((i == 0) | (be_ref[i] != be_ref[jnp.maximum(i - 1, 0)]))
    def _():
        wgb_ref[...] = wg_ref[0, 0].astype(BF16)
        wub_ref[...] = wu_ref[0, 0].astype(BF16)
        wdb_ref[...] = wd_ref[0, 0].astype(BF16)

    xb = _unpack_halves(_load_tiled(x_ref, MOE_ROWS)).astype(BF16)
    hb = jax.nn.silu(_dot(xb, wgb_ref[...])) * _dot(xb, wub_ref[...])
    _store_tiled(y_ref, _pack_halves(_dot(hb.astype(BF16), wdb_ref[...])))


def _experts(blk_exp, xs, wg, wu, wd, layer):
    d, de = wg.shape[2], wg.shape[3]
    nc = d // (2 * LANES)
    bm = MOE_ROWS
    tiles = pl.BlockSpec((bm * nc, LANES), lambda i, be: (i, 0))
    grid_spec = pltpu.PrefetchScalarGridSpec(
        num_scalar_prefetch=1,
        grid=(xs.shape[0] // (bm * nc),),
        in_specs=[tiles,
                  pl.BlockSpec((1, 1, d, de), lambda i, be: (layer, be[i], 0, 0)),
                  pl.BlockSpec((1, 1, d, de), lambda i, be: (layer, be[i], 0, 0)),
                  pl.BlockSpec((1, 1, de, d), lambda i, be: (layer, be[i], 0, 0))],
        out_specs=tiles,
        scratch_shapes=[pltpu.VMEM((d, de), BF16), pltpu.VMEM((d, de), BF16), pltpu.VMEM((de, d), BF16)],
    )
    return pl.pallas_call(
        _experts_body,
        name="moe_experts",
        grid_spec=grid_spec,
        out_shape=jax.ShapeDtypeStruct(xs.shape, U32),
        compiler_params=_cparams("arbitrary"),
    )(blk_exp, xs, wg, wu, wd)


def _combine_body(dest_ref, next_ref, gate_ref, h_ref, x_ref, mod_ref, g_ref, wsg_ref, wsu_ref, wsd_ref, ys_hbm,
                  oc_ref, ol_ref, buf_ref, sem, *, ctx_steps):
    tile, d = x_ref.shape
    nw = buf_ref.shape[2] // tile
    i = pl.program_id(0)
    slot = i % 2

    def row_copy(d_ref, s, t, k):
        return pltpu.make_async_copy(ys_hbm.at[pl.ds(pl.multiple_of(d_ref[k, t] * nw, nw), nw)],
                                     buf_ref.at[s, k, pl.ds(pl.multiple_of(t * nw, nw), nw)], sem.at[s])

    def gather(d_ref, s):
        def start(t, c):
            for k in range(TOP_K):
                row_copy(d_ref, s, t, k).start(priority=k % DMA_THREADS)
            return c
        lax.fori_loop(0, tile, start, 0)

    @pl.when(i == 0)
    def _():
        gather(dest_ref, 0)

    @pl.when(i + 1 < pl.num_programs(0))
    def _():
        gather(next_ref, 1 - slot)

    hb = _unpack_halves(_load_tiled(h_ref, tile)).astype(BF16)
    acc = _dot((jax.nn.silu(_dot(hb, wsg_ref[...])) * _dot(hb, wsu_ref[...])).astype(BF16), wsd_ref[...])

    def wait(t, c):
        for k in range(TOP_K):
            row_copy(dest_ref, slot, 0, k).wait()
        return c

    lax.fori_loop(0, tile, wait, 0)
    gates = gate_ref[...]
    moe = gates[:, 0:1] * _unpack_halves(_load_tiled(buf_ref.at[slot, 0], tile))
    for k in range(1, TOP_K):
        moe = moe + gates[:, k:k + 1] * _unpack_halves(_load_tiled(buf_ref.at[slot, k], tile))
    out = x_ref[...] + mod_ref[0, :, 5 * d:6 * d] * _rms(moe + acc, g_ref[...])

    @pl.when(i < ctx_steps)
    def _():
        oc_ref[...] = out

    @pl.when(i >= ctx_steps)
    def _():
        ol_ref[...] = out


def _combine(dest, gates, h2p, x1, mod, g_post, wsg, wsu, wsd, ys, row_map, t_ctx):
    t, d = x1.shape
    nc = d // LANES
    nw = nc // 2
    tile = COMBINE_TILE
    per_tok = TOK_TILE // tile
    n_steps = t // tile
    tok = lambda n: pl.BlockSpec((tile, n), lambda i: (i, 0))
    full = lambda a: pl.BlockSpec(a.shape, lambda i: (0,) * a.ndim)
    ctx_steps = t_ctx // tile
    return pl.pallas_call(
        functools.partial(_combine_body, ctx_steps=ctx_steps),
        name="moe_combine",
        grid=(n_steps,),
        in_specs=[pl.BlockSpec((TOP_K, tile), lambda i: (0, i), memory_space=pltpu.SMEM),
                  pl.BlockSpec((TOP_K, tile), lambda i: (0, jnp.minimum(i + 1, n_steps - 1)),
                               memory_space=pltpu.SMEM),
                  tok(TOP_K), pl.BlockSpec((tile * nw, LANES), lambda i: (i, 0)), tok(d),
                  pl.BlockSpec((1, 1, mod.shape[2]), lambda i: (row_map(i // per_tok), 0, 0)),
                  full(g_post), full(wsg), full(wsu), full(wsd),
                  pl.BlockSpec(memory_space=pl.ANY)],
        out_specs=[pl.BlockSpec((tile, d), lambda i: (jnp.minimum(i, ctx_steps - 1), 0)),
                   pl.BlockSpec((tile, d), lambda i: (jnp.maximum(i - ctx_steps, 0), 0))],
        out_shape=[jax.ShapeDtypeStruct((t_ctx, d), F32), jax.ShapeDtypeStruct((t - t_ctx, d), F32)],
        scratch_shapes=[pltpu.VMEM((2, TOP_K, tile * nw, LANES), U32), pltpu.SemaphoreType.DMA((2,))],
        compiler_params=_cparams("arbitrary"),
    )(dest, dest, gates, h2p, x1, mod, g_post, wsg, wsu, wsd, ys)


def _moe(logits, h2p, x1, mod, p, row_map, t_ctx):
    t, d = x1.shape
    idx, gate, pos, cnt = _route(logits, p['b_router'])
    counts = cnt[:, 0].astype(I32)
    padded = (counts + MOE_ROWS - 1) // MOE_ROWS * MOE_ROWS
    ends = jnp.cumsum(padded)
    starts = ends - padded
    n_blocks = (t * TOP_K + N_EXPERTS * (MOE_ROWS - 1)) // MOE_ROWS + 1
    n_rows = n_blocks * MOE_ROWS
    blk_start = jnp.arange(n_blocks, dtype=I32) * MOE_ROWS
    blk_exp = jnp.minimum(jnp.sum((ends[None, :] <= blk_start[:, None]).astype(I32), axis=1), N_EXPERTS - 1)
    pad_hi = ends.at[N_EXPERTS - 1].set(n_rows)
    dest = _dest(starts, idx, pos)
    xs = _dispatch(starts + counts, pad_hi, dest, h2p, n_rows)
    ys = _experts(blk_exp, xs, p['w_e_gate'], p['w_e_up'], p['w_e_down'], p['layer'])
    return _combine(dest, gate.T, h2p, x1, mod, p['g_post_ffn'], p['w_s_gate'], p['w_s_up'], p['w_s_down'],
                    ys, row_map, t_ctx)


def kernel(x_prompt, x_sample, cache_k, cache_v, state_ssm_re, state_ssm_im, c, c_ctx,
           g_pre_mix, g_post_mix, g_pre_ffn, g_post_ffn, w_ada, b_ada, w_in, w_out,
           sgu_g, w_sp, b_sp, rpb, ssm_a_re, ssm_a_im, ssm_log_dt, ssm_b_re, ssm_b_im,
           ssm_c_re, ssm_c_im, ssm_d, w_glu, b_glu, w_router, b_router,
           w_e_gate, w_e_up, w_e_down, w_s_gate, w_s_up, w_s_down):
    n_pb, p_seq, d = x_prompt.shape
    n_sb, s_seq, _ = x_sample.shape
    depth = w_in.shape[0]
    d_a = sgu_g.shape[1]
    d_c = w_glu.shape[1]
    d_b = d - d_a - d_c
    n_heads = d_b // HD_B
    n_g = d_c // SSM_CH
    t_p = n_pb * p_seq
    assert p_seq % CHUNK == 0 and t_p % TOK_TILE == 0 and s_seq % TOK_TILE == 0 and t_p % s_seq == 0
    assert s_seq % (NAT_ROWS * GRID_W) == 0 and s_seq // GRID_W >= NA_WIN_R and d_c == 2 * LANES

    mod_rows = -(-(n_sb + 1) // 8) * 8
    c_all = jnp.zeros((mod_rows, d), F32).at[:n_sb].set(c).at[n_sb].set(c_ctx)
    mod_all = _ada(c_all, w_ada, b_ada)
    p_tiles = t_p // TOK_TILE
    s_tiles = s_seq // TOK_TILE

    def row_map(i):
        return jnp.where(i < p_tiles, n_sb, (i - p_tiles) // s_tiles)

    xc, xl = x_prompt.reshape(t_p, d), x_sample.reshape(n_sb * s_seq, d)
    s_m, s_t, s_w, a16, dvec = jax.vmap(_s5_weights)(ssm_a_re, ssm_a_im, ssm_log_dt, ssm_b_re, ssm_b_im,
                                                     ssm_c_re, ssm_c_im, ssm_d)

    def expand(s, r2w, c2w):
        full = _expand_block_diag(s.reshape((-1,) + s.shape[2:]), r2w, c2w)
        return full.reshape(s.shape[:2] + full.shape[1:])

    s5w = (expand(s_m, SSM_CH, SSM_P), expand(s_t, SSM_CH, SSM_CH), expand(s_w, SSM_P, SSM_CH), a16, dvec)
    nat_bias_all = jax.vmap(_nat_bias)(rpb)
    new_k, new_v, new_re, new_im = [], [], [], []
    for l in range(depth):
        mod = mod_all[l][:, None, :]
        row = lambda a: a[l][None, :].astype(F32)
        pa, q, k32, v32, kb, vb, *pc = _premix(xc, xl, mod, row(g_pre_mix), w_in[l].astype(BF16), row_map,
                                               d_a, d_b, d_c)
        bias_a = jnp.repeat(b_sp[l].T.astype(F32), d_a // NH_A, axis=1)
        ya = _chunk_mlp(pa, row(sgu_g), w_sp[l].reshape(NH_A * CHUNK, CHUNK).astype(BF16), bias_a)
        yb_p = _ctx_attn(q, k32, v32, n_pb, p_seq)
        ck = cache_k[:, l].reshape(n_sb, -1, d_b).astype(BF16)
        cv = cache_v[:, l].reshape(n_sb, -1, d_b).astype(BF16)
        yb_s = _nat_attn(q, kb, vb, ck, cv, nat_bias_all[l], n_sb, s_seq, t_p)
        c_p = t_p // S5_CHUNK
        yc_p, fin_p = _s5([h[:c_p] for h in pc], n_pb, p_seq, s5w, l, jnp.zeros((4, n_pb, n_g * SSM_P), F32))
        sre = state_ssm_re[:, l].astype(F32).reshape(n_sb, 2, n_g * SSM_P)
        sim = state_ssm_im[:, l].astype(F32).reshape(n_sb, 2, n_g * SSM_P)
        h0_s = jnp.stack([sre[:, 0], sim[:, 0], sre[:, 1], sim[:, 1]])
        yc_s, _ = _s5([h[c_p:] for h in pc], n_sb, s_seq, s5w, l, h0_s)
        wr = w_router[l].astype(F32).T
        wr_hi = wr.astype(BF16)
        wr_lo = (wr - wr_hi.astype(F32)).astype(BF16)
        x1, logits, h2p = _post(xc, xl, ya, (yb_p, yb_s), (yc_p[0], yc_s[0]), (yc_p[1], yc_s[1]),
                                mod, row(g_post_mix), row(g_pre_ffn), w_glu[l].astype(BF16), row(b_glu),
                                w_out[l].astype(BF16), wr_hi, wr_lo, row_map)
        moe_p = {
            'b_router': b_router[l].astype(F32)[:, None],
            'w_e_gate': w_e_gate, 'w_e_up': w_e_up, 'w_e_down': w_e_down, 'layer': l,
            'w_s_gate': w_s_gate[l].astype(BF16), 'w_s_up': w_s_up[l].astype(BF16),
            'w_s_down': w_s_down[l].astype(BF16), 'g_post_ffn': row(g_post_ffn),
        }
        xc, xl = _moe(logits, h2p, x1, mod, moe_p, row_map, t_p)
        new_k.append(k32[:t_p].reshape(n_pb, p_seq, n_heads, HD_B))
        new_v.append(v32[:t_p].reshape(n_pb, p_seq, n_heads, HD_B))
        fin_p = fin_p.reshape(4, n_pb, n_g, SSM_P)
        new_re.append(jnp.stack([fin_p[0], fin_p[2]], axis=1))
        new_im.append(jnp.stack([fin_p[1], fin_p[3]], axis=1))
    return (xc.reshape(n_pb, p_seq, d), xl.reshape(n_sb, s_seq, d),
            jnp.stack(new_k, axis=1), jnp.stack(new_v, axis=1),
            jnp.stack(new_re, axis=1), jnp.stack(new_im, axis=1))
```
